```python
import math
import jax
import jax.numpy as jnp
from jax import lax
import numpy as np

D_MODEL = 1024
BATCH = 16
SEQ = 4096
DEPTH = 2

HEAD_DIM = 64
Q_BLOCK = 128
RMS_EPS = 1e-6
NEG_INF = -1e30
FORCE_SCORE = 1e9

N_BUCKETS = 32
BUCKET_EXACT = 16
BUCKET_MAX_DIST = 2048

A_HEADS = 4
A_CMP_LEN = 32
A_CMP_STRIDE = 16
A_PHI_HIDDEN = 128
A_SEL_BLOCK = 64
A_SEL_TOPK = 16
A_WINDOW = 512

B_GROUPS = ((128, 1), (512, 4), (2048, 16))
B_HEADS_PER_GROUP = 2
B_HEADS = B_HEADS_PER_GROUP * len(B_GROUPS)

C_HEADS = 4

D_HEADS = 4
D_Q_LORA = 256
D_KV_LORA = 128
D_NOPE = 64
D_ROPE = 32
D_VDIM = 64
ROPE_THETA = 10000.0

D_FF = 2816
FFN_CONV_WIDTH = 3

N_BRANCHES = 4
IN_SPLITS = (
    A_HEADS * HEAD_DIM,
    6 * HEAD_DIM,
    3 * A_HEADS,
    3 * B_HEADS * HEAD_DIM,
    3 * C_HEADS * HEAD_DIM,
    C_HEADS,
    D_Q_LORA,
    D_KV_LORA,
    D_ROPE,
)
D_IN = sum(IN_SPLITS)
A_OUT = A_HEADS * HEAD_DIM
B_OUT = B_HEADS_PER_GROUP * HEAD_DIM
C_OUT = C_HEADS * HEAD_DIM
D_OUT = D_HEADS * D_VDIM

kernel_name = 'hybrid_nsa_dilated_fox_mla_block'


def rms_norm(x, gain):
    xf = x.astype(jnp.float32)
    y = xf * lax.rsqrt(jnp.mean(xf * xf, axis=-1, keepdims=True) + RMS_EPS)
    return (y * gain.astype(jnp.float32)).astype(x.dtype)


def split_cols(a, sizes):
    return jnp.split(a, np.cumsum(sizes)[:-1].tolist(), axis=-1)


def rel_bucket(dist):
    dist = jnp.maximum(dist, 0)
    d = jnp.maximum(dist, 1).astype(jnp.float32)
    large = BUCKET_EXACT + (jnp.log(d / BUCKET_EXACT) / math.log(BUCKET_MAX_DIST / BUCKET_EXACT)
                            * (N_BUCKETS - BUCKET_EXACT)).astype(jnp.int32)
    large = jnp.minimum(large, N_BUCKETS - 1)
    return jnp.where(dist < BUCKET_EXACT, dist, large)


def masked_softmax(logits, mask):
    logits = jnp.where(mask, logits, NEG_INF)
    m = jnp.max(logits, axis=-1, keepdims=True)
    p = jnp.where(mask, jnp.exp(logits - m), 0.0)
    den = jnp.sum(p, axis=-1, keepdims=True)
    den_safe = jnp.maximum(den, 1e-30)
    return p / den_safe, m[..., 0] + jnp.log(den_safe[..., 0])


def sweep_query_blocks(fn, *q_arrays):
    b, s = q_arrays[0].shape[:2]
    nb = s // Q_BLOCK
    blocked = tuple(jnp.moveaxis(a.reshape((b, nb, Q_BLOCK) + a.shape[2:]), 1, 0) for a in q_arrays)
    out = lax.map(lambda args: fn(args[0], *args[1:]), (jnp.arange(nb, dtype=jnp.int32),) + blocked)
    return jnp.moveaxis(out, 0, 1).reshape((b, s) + out.shape[3:])


def rope(x, cos, sin):
    x1, x2 = jnp.split(x, 2, axis=-1)
    return jnp.concatenate([x1 * cos - x2 * sin, x2 * cos + x1 * sin], axis=-1)


def nsa_mixer(q, kv_a, gate_logits, cmp_pos, phi_k_w1, phi_k_w2, phi_v_w1, phi_v_w2, rel_table):
    b, s = q.shape[:2]
    k_cmp, v_cmp, k_slc, v_slc, k_win, v_win = jnp.split(kv_a, 6, axis=-1)
    scale = HEAD_DIM ** -0.5
    n_cmp = (s - A_CMP_LEN) // A_CMP_STRIDE + 1

    def compress(kv, w1, w2):
        ch = kv.reshape(b, s // A_CMP_STRIDE, A_CMP_STRIDE, HEAD_DIM)
        blocks = jnp.concatenate([ch[:, j:j + n_cmp] for j in range(A_CMP_LEN // A_CMP_STRIDE)], axis=2)
        blocks = (blocks + cmp_pos).reshape(b, n_cmp, A_CMP_LEN * HEAD_DIM)
        return jax.nn.gelu(blocks @ w1) @ w2

    kc = compress(k_cmp, phi_k_w1, phi_k_w2)
    vc = compress(v_cmp, phi_v_w1, phi_v_w2)
    cmp_start = jnp.arange(n_cmp) * A_CMP_STRIDE
    cmp_end = cmp_start + A_CMP_LEN - 1
    n_sel = s // A_SEL_BLOCK
    sel_start = jnp.arange(n_sel) * A_SEL_BLOCK
    overlap = ((cmp_start[:, None] <= sel_start[None, :] + A_SEL_BLOCK - 1)
               & (cmp_end[:, None] >= sel_start[None, :])).astype(jnp.float32)
    n_top = min(A_SEL_TOPK, n_sel)
    ks_blocks = k_slc.reshape(b, n_sel, A_SEL_BLOCK, HEAD_DIM)
    vs_blocks = v_slc.reshape(b, n_sel, A_SEL_BLOCK, HEAD_DIM)
    kw_pad = jnp.pad(k_win, ((0, 0), (A_WINDOW, 0), (0, 0)))
    vw_pad = jnp.pad(v_win, ((0, 0), (A_WINDOW, 0), (0, 0)))
    tab = rel_table[:, :A_HEADS]
    batch_ix = jnp.arange(b)[:, None, None]

    def block(bi, q_blk, g_blk):
        t = bi * Q_BLOCK + jnp.arange(Q_BLOCK)
        lc = jnp.einsum('bqhd,bcd->bhqc', q_blk, kc, preferred_element_type=jnp.float32) * scale
        p_c, _ = masked_softmax(lc, cmp_end[None, :] <= t[:, None])
        o_c = jnp.einsum('bhqc,bcd->bqhd', p_c.astype(vc.dtype), vc)
        imp = jnp.einsum('bhqc,cj->bqj', p_c, overlap)
        cur = (t // A_SEL_BLOCK)[:, None]
        j = jnp.arange(n_sel)[None, :]
        imp = jnp.where((j == 0) | (j == cur) | (j == cur - 1), FORCE_SCORE,
                        jnp.where(j > cur, -FORCE_SCORE, imp))
        _, sel = lax.top_k(imp, n_top)
        ks = ks_blocks[batch_ix, sel].reshape(b, Q_BLOCK, n_top * A_SEL_BLOCK, HEAD_DIM)
        vs = vs_blocks[batch_ix, sel].reshape(b, Q_BLOCK, n_top * A_SEL_BLOCK, HEAD_DIM)
        pos = (sel[..., None] * A_SEL_BLOCK + jnp.arange(A_SEL_BLOCK)).reshape(b, Q_BLOCK, n_top * A_SEL_BLOCK)
        dist = t[None, :, None] - pos
        ls = (jnp.einsum('bqhd,bqkd->bhqk', q_blk, ks, preferred_element_type=jnp.float32) * scale
              + jnp.moveaxis(tab[rel_bucket(dist)], -1, 1))
        p_s, _ = masked_softmax(ls, (dist >= 0)[:, None])
        o_s = jnp.einsum('bhqk,bqkd->bqhd', p_s.astype(vs.dtype), vs)
        kw = lax.dynamic_slice_in_dim(kw_pad, bi * Q_BLOCK, A_WINDOW + Q_BLOCK, axis=1)
        vw = lax.dynamic_slice_in_dim(vw_pad, bi * Q_BLOCK, A_WINDOW + Q_BLOCK, axis=1)
        kpos = bi * Q_BLOCK - A_WINDOW + jnp.arange(A_WINDOW + Q_BLOCK)
        dist_w = t[:, None] - kpos[None, :]
        mask_w = (kpos[None, :] >= 0) & (dist_w >= 0) & (dist_w < A_WINDOW)
        lw = (jnp.einsum('bqhd,bkd->bhqk', q_blk, kw, preferred_element_type=jnp.float32) * scale
              + jnp.moveaxis(tab[rel_bucket(dist_w)], -1, 0)[None])
        p_w, _ = masked_softmax(lw, mask_w)
        o_w = jnp.einsum('bhqk,bkd->bqhd', p_w.astype(vw.dtype), vw)
        g = jax.nn.sigmoid(g_blk.astype(jnp.float32)).astype(q_blk.dtype)
        return g[..., 0:1] * o_c + g[..., 1:2] * o_s + g[..., 2:3] * o_w

    return sweep_query_blocks(block, q, gate_logits)


def dilated_mixer(q, k, v, rel_table):
    scale = HEAD_DIM ** -0.5
    tab = rel_table[:, A_HEADS:]
    hsl = [slice(g * B_HEADS_PER_GROUP, (g + 1) * B_HEADS_PER_GROUP) for g in range(len(B_GROUPS))]
    k_groups = [k[:, :, hs] for hs in hsl]
    v_groups = [v[:, :, hs] for hs in hsl]

    def block(bi, q_blk):
        t = bi * Q_BLOCK + jnp.arange(Q_BLOCK)
        outs, lses = [], []
        for g, (window, dilation) in enumerate(B_GROUPS):
            offs = dilation * jnp.arange(window // dilation + 1)
            idx = t[:, None] - offs[None, :]
            safe = jnp.maximum(idx, 0)
            kg = k_groups[g][:, safe]
            vg = v_groups[g][:, safe]
            bias = jnp.moveaxis(tab[rel_bucket(offs), hsl[g]], -1, 0)
            logits = (jnp.einsum('bqhd,bqkhd->bhqk', q_blk[:, :, hsl[g]], kg, preferred_element_type=jnp.float32)
                      * scale + bias[None, :, None, :])
            p, lse = masked_softmax(logits, (idx >= 0)[None, None])
            outs.append(jnp.einsum('bhqk,bqkhd->bqhd', p.astype(vg.dtype), vg))
            lses.append(jnp.moveaxis(lse, -1, 1))
        wts = jax.nn.softmax(jnp.stack(lses), axis=0)
        return jnp.einsum('gbqh,gbqhd->bqhd', wts.astype(q_blk.dtype), jnp.stack(outs))

    return sweep_query_blocks(block, q)


def forgetting_mixer(q, k, v, f_logits, f_bias):
    s = q.shape[1]
    scale = HEAD_DIM ** -0.5
    log_f = jax.nn.log_sigmoid(f_logits.astype(jnp.float32) + f_bias.astype(jnp.float32))
    cum = jnp.cumsum(log_f, axis=1)
    cum_keys = jnp.moveaxis(cum, 1, 2)
    kpos = jnp.arange(s)

    def block(bi, q_blk, cum_blk):
        t = bi * Q_BLOCK + jnp.arange(Q_BLOCK)
        logits = (jnp.einsum('bqhd,bkhd->bhqk', q_blk, k, preferred_element_type=jnp.float32) * scale
                  + jnp.moveaxis(cum_blk, 1, 2)[..., None] - cum_keys[:, :, None, :])
        p, _ = masked_softmax(logits, kpos[None, :] <= t[:, None])
        return jnp.einsum('bhqk,bkhd->bqhd', p.astype(v.dtype), v)

    return sweep_query_blocks(block, q, cum)


def mla_mixer(q_lat, kv_lat, k_rope, q_norm, kv_norm, w_uq, w_ukv):
    b, s = q_lat.shape[:2]
    scale = (D_NOPE + D_ROPE) ** -0.5
    q = (rms_norm(q_lat, q_norm) @ w_uq).reshape(b, s, D_HEADS, D_NOPE + D_ROPE)
    kv = (rms_norm(kv_lat, kv_norm) @ w_ukv).reshape(b, s, D_HEADS, D_NOPE + D_VDIM)
    q_nope, q_rot = q[..., :D_NOPE], q[..., D_NOPE:]
    k_nope, v = kv[..., :D_NOPE], kv[..., D_NOPE:]
    inv_freq = ROPE_THETA ** (-jnp.arange(0, D_ROPE, 2, dtype=jnp.float32) / D_ROPE)
    ang = jnp.arange(s, dtype=jnp.float32)[:, None] * inv_freq[None, :]
    cos, sin = jnp.cos(ang).astype(q.dtype), jnp.sin(ang).astype(q.dtype)
    q_rot = rope(q_rot, cos[:, None], sin[:, None])
    k_rot = rope(k_rope, cos, sin)
    kpos = jnp.arange(s)

    def block(bi, qn_blk, qr_blk):
        t = bi * Q_BLOCK + jnp.arange(Q_BLOCK)
        logits = (jnp.einsum('bqhd,bkhd->bhqk', qn_blk, k_nope, preferred_element_type=jnp.float32)
                  + jnp.einsum('bqhr,bkr->bhqk', qr_blk, k_rot, preferred_element_type=jnp.float32)) * scale
        p, _ = masked_softmax(logits, kpos[None, :] <= t[:, None])
        return jnp.einsum('bhqk,bkhd->bqhd', p.astype(v.dtype), v)

    return sweep_query_blocks(block, q_nope, q_rot)


def setup_inputs(seed: int = 0) -> dict:
    key = jax.random.key(seed)
    ks = iter(list(jax.random.split(key, 32)))
    f32 = jnp.float32

    def nrm(shape, fan_in, scale=1.0):
        return scale * jax.random.normal(next(ks), shape, f32) * fan_in ** -0.5

    def gain(shape):
        return 1.0 + 0.05 * jax.random.normal(next(ks), shape, f32)

    return {
        'x': jax.random.normal(next(ks), (BATCH, SEQ, D_MODEL), f32),
        'rel_bias_table': 0.5 * jax.random.normal(next(ks), (N_BUCKETS, A_HEADS + B_HEADS), f32),
        'norm_attn_pre': gain((DEPTH, D_MODEL)),
        'norm_attn_post': gain((DEPTH, D_MODEL)),
        'norm_ffn_pre': gain((DEPTH, D_MODEL)),
        'norm_ffn_post': gain((DEPTH, D_MODEL)),
        'w_in': nrm((DEPTH, D_MODEL, D_IN), D_MODEL),
        'nsa_cmp_pos': 0.1 * jax.random.normal(next(ks), (DEPTH, A_CMP_LEN, HEAD_DIM), f32),
        'nsa_phi_k_w1': nrm((DEPTH, A_CMP_LEN * HEAD_DIM, A_PHI_HIDDEN), A_CMP_LEN * HEAD_DIM),
        'nsa_phi_k_w2': nrm((DEPTH, A_PHI_HIDDEN, HEAD_DIM), A_PHI_HIDDEN),
        'nsa_phi_v_w1': nrm((DEPTH, A_CMP_LEN * HEAD_DIM, A_PHI_HIDDEN), A_CMP_LEN * HEAD_DIM),
        'nsa_phi_v_w2': nrm((DEPTH, A_PHI_HIDDEN, HEAD_DIM), A_PHI_HIDDEN),
        'fox_forget_bias': 2.0 + 0.5 * jax.random.normal(next(ks), (DEPTH, C_HEADS), f32),
        'mla_q_norm': gain((DEPTH, D_Q_LORA)),
        'mla_kv_norm': gain((DEPTH, D_KV_LORA)),
        'mla_w_uq': nrm((DEPTH, D_Q_LORA, D_HEADS * (D_NOPE + D_ROPE)), D_Q_LORA),
        'mla_w_ukv': nrm((DEPTH, D_KV_LORA, D_HEADS * (D_NOPE + D_VDIM)), D_KV_LORA),
        'w_branch_a': nrm((DEPTH, A_OUT, D_MODEL), A_OUT),
        'w_branch_b': nrm((DEPTH, B_OUT, D_MODEL), B_OUT),
        'w_branch_c': nrm((DEPTH, C_OUT, D_MODEL), C_OUT),
        'w_branch_d': nrm((DEPTH, D_OUT, D_MODEL), D_OUT),
        'w_merge_gate': nrm((DEPTH, N_BRANCHES, D_MODEL, D_MODEL), D_MODEL),
        'w_o': nrm((DEPTH, D_MODEL, D_MODEL), D_MODEL),
        'ffn_w_up': nrm((DEPTH, D_MODEL, 2 * D_FF), D_MODEL),
        'ffn_conv_w': nrm((DEPTH, FFN_CONV_WIDTH, 2 * D_FF), FFN_CONV_WIDTH),
        'ffn_conv_b': 0.02 * jax.random.normal(next(ks), (DEPTH, 2 * D_FF), f32),
        'ffn_w_down': nrm((DEPTH, D_FF, D_MODEL), D_FF),
    }


def reference(x, rel_bias_table, norm_attn_pre, norm_attn_post, norm_ffn_pre, norm_ffn_post,
              w_in, nsa_cmp_pos, nsa_phi_k_w1, nsa_phi_k_w2, nsa_phi_v_w1, nsa_phi_v_w2,
              fox_forget_bias, mla_q_norm, mla_kv_norm, mla_w_uq, mla_w_ukv,
              w_branch_a, w_branch_b, w_branch_c, w_branch_d, w_merge_gate, w_o,
              ffn_w_up, ffn_conv_w, ffn_conv_b, ffn_w_down):
    batch, seq = x.shape[:2]
    for l in range(DEPTH):
        h = rms_norm(x, norm_attn_pre[l])
        (a_q, a_kv, a_g, b_qkv, c_qkv, c_f, d_qlat, d_kvlat, d_krope) = split_cols(h @ w_in[l], IN_SPLITS)
        o_a = nsa_mixer(a_q.reshape(batch, seq, A_HEADS, HEAD_DIM), a_kv,
                        a_g.reshape(batch, seq, A_HEADS, 3), nsa_cmp_pos[l],
                        nsa_phi_k_w1[l], nsa_phi_k_w2[l], nsa_phi_v_w1[l], nsa_phi_v_w2[l],
                        rel_bias_table)
        b_qkv = b_qkv.reshape(batch, seq, 3, B_HEADS, HEAD_DIM)
        o_b = dilated_mixer(b_qkv[:, :, 0], b_qkv[:, :, 1], b_qkv[:, :, 2], rel_bias_table)
        c_qkv = c_qkv.reshape(batch, seq, 3, C_HEADS, HEAD_DIM)
        o_c = forgetting_mixer(c_qkv[:, :, 0], c_qkv[:, :, 1], c_qkv[:, :, 2], c_f, fox_forget_bias[l])
        o_d = mla_mixer(d_qlat, d_kvlat, d_krope, mla_q_norm[l], mla_kv_norm[l], mla_w_uq[l], mla_w_ukv[l])
        branches = (
            (o_a.reshape(batch, seq, A_OUT), w_branch_a[l]),
            (o_b.reshape(batch, seq, B_OUT), w_branch_b[l]),
            (o_c.reshape(batch, seq, C_OUT), w_branch_c[l]),
            (o_d.reshape(batch, seq, D_OUT), w_branch_d[l]),
        )
        merged = None
        for i, (o_i, w_i) in enumerate(branches):
            term = jax.nn.sigmoid(h @ w_merge_gate[l, i]) * (o_i @ w_i)
            merged = term if i == 0 else merged + term
        x = x + rms_norm(merged @ w_o[l], norm_attn_post[l])
        h = rms_norm(x, norm_ffn_pre[l])
        u = h @ ffn_w_up[l]
        u_pad = jnp.pad(u, ((0, 0), (FFN_CONV_WIDTH - 1, 0), (0, 0)))
        conv = ffn_conv_b[l]
        for j in range(FFN_CONV_WIDTH):
            conv = conv + ffn_conv_w[l, j] * u_pad[:, j:j + seq]
        gate, val = jnp.split(conv, 2, axis=-1)
        y = (jax.nn.silu(gate) * val) @ ffn_w_down[l]
        x = x + rms_norm(y, norm_ffn_post[l])
    return x
```

```python
import functools
import math

import numpy as np
import jax
import jax.numpy as jnp
from jax import lax
from jax.experimental import pallas as pl
from jax.experimental.pallas import tpu as pltpu

F32 = jnp.float32
BF16 = jnp.bfloat16

LANES = 128
HEAD_DIM = 64
RMS_EPS = 1e-6
NEG_INF = -1e30
FORCE_SCORE = 1e9
VMEM_LIMIT_BYTES = 56 * 1024 * 1024

N_BUCKETS = 32
BUCKET_EXACT = 16
BUCKET_MAX_DIST = 2048

A_HEADS = 4
A_CMP_LEN = 32
A_CMP_STRIDE = 16
A_PHI_HIDDEN = 128
A_SEL_BLOCK = 64
A_SEL_TOPK = 16
A_WINDOW = 512
B_GROUPS = ((128, 1), (512, 4), (2048, 16))
B_HPG = 2
B_HEADS = B_HPG * len(B_GROUPS)
C_HEADS = 4
D_HEADS = 4
D_Q_LORA = 256
D_KV_LORA = 128
D_NOPE = 64
D_ROPE = 32
D_VDIM = 64
ROPE_THETA = 10000.0
N_BRANCHES = 4

G_AQ = 0
G_CMP = G_AQ + A_HEADS
G_SLC = G_CMP + 1
G_WIN = G_SLC + 1
G_BQ = G_WIN + 1
G_BKV = G_BQ + B_HEADS
G_CQ = G_BKV + B_HEADS
G_CKV = G_CQ + C_HEADS
G_DQL = G_CKV + C_HEADS
G_DKVL = G_DQL + 2
G_S1 = G_DKVL + 1
G_S2 = G_S1 + 1
N_SLOTS = G_S2 + 1
LANE_GATE = 0
LANE_FORGET = 3 * A_HEADS
LANE_ROPE = 64

TM_PROJ = 256
TQ_FLASH = 256
TQ_NSA = 128
TQ_DIL = 128
CONV_HALO = 16


def _cparams(n_grid):
    return pltpu.CompilerParams(dimension_semantics=("arbitrary",) * n_grid,
                                vmem_limit_bytes=VMEM_LIMIT_BYTES)


def _const_spec(shape):
    nd = len(shape)
    return pl.BlockSpec(shape, lambda *_: (0,) * nd, pipeline_mode=pl.Buffered(1))


def _rms(x, gain):
    return x * lax.rsqrt(jnp.mean(x * x, axis=-1, keepdims=True) + RMS_EPS) * gain


def _sigmoid(x):
    return 1.0 / (1.0 + jnp.exp(-x))


def _dot(a, b):
    return jnp.dot(a, b, preferred_element_type=F32)


def _dot_nt(a, b):
    return lax.dot_general(a, b, (((1,), (1,)), ((), ())), preferred_element_type=F32)


def _slot(h):
    return slice(h * LANES, (h + 1) * LANES)


def _inproj_kernel(x_ref, g_ref, w_ref, tab_ref, fb_ref, qn_ref, kvn_ref, wqa_ref, wqb_ref,
                   wka_ref, wv_ref,
                   aq_ref, cmp_ref, slc_ref, win_ref, bq_ref, bkv_ref, cq_ref, ckv_ref,
                   dq_ref, dk_ref, dv_ref, small_ref, carry_ref):
    i = pl.program_id(1)
    tm = x_ref.shape[1]
    h = _rms(x_ref[0], g_ref[...]).astype(BF16)

    def proj(g0, n):
        return _dot(h, w_ref[:, g0 * LANES:(g0 + n) * LANES])

    aq_ref[0] = proj(G_AQ, A_HEADS).astype(BF16)
    cmp_ref[0] = proj(G_CMP, 1)
    slc_ref[0] = proj(G_SLC, 1).astype(BF16)
    win_ref[0] = proj(G_WIN, 1).astype(BF16)
    bq_ref[0] = proj(G_BQ, B_HEADS).astype(BF16)
    bkv_ref[0] = proj(G_BKV, B_HEADS).astype(BF16)
    cq_ref[0] = proj(G_CQ, C_HEADS).astype(BF16)
    ckv_ref[0] = proj(G_CKV, C_HEADS).astype(BF16)

    s1 = proj(G_S1, 1)
    s2 = proj(G_S2, 1)

    @pl.when(i == 0)
    def _():
        carry_ref[...] = jnp.zeros_like(carry_ref)

    z = s1 + fb_ref[...]
    logf = jnp.minimum(z, 0.0) - jnp.log(1.0 + jnp.exp(-jnp.abs(z)))
    row = lax.broadcasted_iota(jnp.int32, (tm, tm), 0)
    col = lax.broadcasted_iota(jnp.int32, (tm, tm), 1)
    tri = (row >= col).astype(BF16)
    hi = logf.astype(BF16)
    r1 = logf - hi.astype(F32)
    mid = r1.astype(BF16)
    lo = (r1 - mid.astype(F32)).astype(BF16)
    cum = _dot(tri, hi) + _dot(tri, mid) + _dot(tri, lo) + carry_ref[0:1, :]
    carry_ref[0:1, :] = cum[tm - 1:tm, :]
    lane = lax.broadcasted_iota(jnp.int32, (tm, LANES), 1)
    is_forget = (lane >= LANE_FORGET) & (lane < LANE_FORGET + C_HEADS)
    small_ref[0] = jnp.where(is_forget, cum, s1)

    tab = tab_ref[...]
    c_q, s_q, c_k, s_k = (tab[:, _slot(n)] for n in range(4))
    qn = _rms(proj(G_DQL, 2), qn_ref[...]).astype(BF16)
    qa = _dot(qn, wqa_ref[...])
    qb = _dot(qn, wqb_ref[...])
    kvn = _rms(proj(G_DKVL, 1), kvn_ref[...]).astype(BF16)
    ka = _dot(kvn, wka_ref[...])
    k_rot = s1 * c_k + s2 * s_k
    for hd in range(D_HEADS):
        dq_ref[0, :, _slot(hd)] = (qa[:, _slot(hd)] * c_q + qb[:, _slot(hd)] * s_q).astype(BF16)
        dk_ref[0, :, _slot(hd)] = (ka[:, _slot(hd)] + k_rot).astype(BF16)
    dv_ref[0] = _dot(kvn, wv_ref[...]).astype(BF16)


def _inproj(x, gain, w_ext, tab, fbias, qn_g, kvn_g, wqa, wqb, wka, wv):
    b, s, d = x.shape
    tm = min(TM_PROJ, s)
    n_cols = w_ext.shape[1]

    def rows(width):
        return pl.BlockSpec((1, tm, width), lambda bi, i: (bi, i, 0))

    out_widths = [(A_HEADS, BF16), (1, F32), (1, BF16), (1, BF16), (B_HEADS, BF16), (B_HEADS, BF16),
                  (C_HEADS, BF16), (C_HEADS, BF16), (D_HEADS, BF16), (D_HEADS, BF16), (D_HEADS, BF16),
                  (1, F32)]
    return pl.pallas_call(
        _inproj_kernel,
        grid=(b, s // tm),
        in_specs=[rows(d),
                  _const_spec((1, d)),
                  _const_spec((d, n_cols)),
                  pl.BlockSpec((tm, 4 * LANES), lambda bi, i: (i, 0)),
                  _const_spec((1, LANES)),
                  _const_spec((1, D_Q_LORA)),
                  _const_spec((1, D_KV_LORA)),
                  _const_spec(wqa.shape), _const_spec(wqb.shape),
                  _const_spec(wka.shape), _const_spec(wv.shape)],
        out_specs=[rows(n * LANES) for n, _ in out_widths],
        out_shape=[jax.ShapeDtypeStruct((b, s, n * LANES), dt) for n, dt in out_widths],
        scratch_shapes=[pltpu.VMEM((8, LANES), F32)],
        compiler_params=_cparams(2),
    )(x, gain, w_ext, tab, fbias, qn_g, kvn_g, wqa, wqb, wka, wv)


def _gelu_tanh(x):
    return 0.5 * x * (1.0 + jnp.tanh(math.sqrt(2.0 / math.pi) * (x + 0.044715 * (x * x * x))))


def _compress_kernel(kc_ref, vc_ref, pos_ref, w1k_ref, w1v_ref, w2k_ref, w2v_ref, out_ref):
    nc = kc_ref.shape[1]
    half = A_CMP_STRIDE * HEAD_DIM
    pos_a = pos_ref[0:1, :]
    pos_b = pos_ref[1:2, :]

    def phi(chunks, w1_ref, w2_ref):
        u = _dot((chunks + pos_a).astype(BF16), w1_ref[0:half, :])
        v = _dot((chunks + pos_b).astype(BF16), w1_ref[half:2 * half, :])
        hid = u + pltpu.roll(v, nc - 1, 0)
        return _dot(_gelu_tanh(hid).astype(BF16), w2_ref[...])

    out_ref[0] = (phi(kc_ref[0], w1k_ref, w2k_ref) + phi(vc_ref[0], w1v_ref, w2v_ref)).astype(BF16)


def _compress(kc_in, vc_in, pos2, w1k, w1v, w2k_pad, w2v_pad):
    b, nc, width = kc_in.shape
    blk = pl.BlockSpec((1, nc, width), lambda bi: (bi, 0, 0))
    return pl.pallas_call(
        _compress_kernel,
        grid=(b,),
        in_specs=[blk, blk, _const_spec(pos2.shape), _const_spec(w1k.shape), _const_spec(w1v.shape),
                  _const_spec(w2k_pad.shape), _const_spec(w2v_pad.shape)],
        out_specs=pl.BlockSpec((1, nc, LANES), lambda bi: (bi, 0, 0)),
        out_shape=jax.ShapeDtypeStruct((b, nc, LANES), BF16),
        compiler_params=_cparams(1),
    )(kc_in, vc_in, pos2, w1k, w1v, w2k_pad, w2v_pad)


def _nsa_cmp_win_kernel(q_ref, kvc_ref, kvw_ref, small_ref, ovl_ref, bias_ref,
                        part_ref, sel_ref, *, n_top):
    i = pl.program_id(1)
    tq = q_ref.shape[1]
    nc = kvc_ref.shape[1]
    n_sel = ovl_ref.shape[0]
    t0 = pl.multiple_of(i * tq, tq)
    q4 = jnp.concatenate([q_ref[0, :, _slot(h)] for h in range(A_HEADS)], axis=0)
    rows = A_HEADS * tq

    kvc = kvc_ref[0]
    s = _dot_nt(q4, kvc)
    t_row = t0 + (lax.broadcasted_iota(jnp.int32, (rows, nc), 0) & (tq - 1))
    c_end = lax.broadcasted_iota(jnp.int32, (rows, nc), 1) * A_CMP_STRIDE + (A_CMP_LEN - 1)
    valid = c_end <= t_row
    s = jnp.where(valid, s, NEG_INF)
    m = jnp.max(s, axis=1, keepdims=True)
    p = jnp.where(valid, jnp.exp(s - m), 0.0)
    den = jnp.maximum(jnp.sum(p, axis=1, keepdims=True), 1e-30)
    p = p / den
    o_c = _dot(p.astype(BF16), kvc)

    p_sum = p[0:tq] + p[tq:2 * tq] + p[2 * tq:3 * tq] + p[3 * tq:4 * tq]
    p_hi = p_sum.astype(BF16)
    p_lo = (p_sum - p_hi.astype(F32)).astype(BF16)
    imp_t = _dot_nt(ovl_ref[...], p_hi) + _dot_nt(ovl_ref[...], p_lo)

    j_idx = lax.broadcasted_iota(jnp.int32, (n_sel, tq), 0)
    cur = lax.shift_right_logical(t0 + lax.broadcasted_iota(jnp.int32, (n_sel, tq), 1), 6)
    forced = (j_idx == 0) | (j_idx == cur) | (j_idx == cur - 1)
    imp_t = jnp.where(forced, FORCE_SCORE, jnp.where(j_idx > cur, -FORCE_SCORE, imp_t))
    rank = jnp.zeros((n_sel, tq), F32)
    for c in range(n_sel):
        row_c = imp_t[c:c + 1, :]
        beats = (row_c > imp_t) | ((row_c == imp_t) & (j_idx > c))
        rank = rank + jnp.where(beats, 1.0, 0.0)
    sel_t = jnp.where(rank < float(n_top), 1.0, 0.0).astype(BF16)
    eye = (lax.broadcasted_iota(jnp.int32, (tq, tq), 0)
           == lax.broadcasted_iota(jnp.int32, (tq, tq), 1)).astype(BF16)
    sel_ref[0] = _dot_nt(eye, sel_t).astype(BF16)

    span = A_WINDOW + tq
    kvw = kvw_ref[0, pl.ds(t0, span), :]
    sw = _dot_nt(q4, kvw) + bias_ref[...]
    kpos = t0 - A_WINDOW + lax.broadcasted_iota(jnp.int32, (rows, span), 1)
    sw = jnp.where(kpos >= 0, sw, NEG_INF)
    mw = jnp.max(sw, axis=1, keepdims=True)
    pw = jnp.exp(sw - mw)
    o_w = _dot(pw.astype(BF16), kvw) / jnp.sum(pw, axis=1, keepdims=True)

    g = _sigmoid(small_ref[0])
    for h in range(A_HEADS):
        g_c = g[:, LANE_GATE + 3 * h:LANE_GATE + 3 * h + 1]
        g_w = g[:, LANE_GATE + 3 * h + 2:LANE_GATE + 3 * h + 3]
        part_ref[0, :, _slot(h)] = g_c * o_c[h * tq:(h + 1) * tq] + g_w * o_w[h * tq:(h + 1) * tq]


def _nsa_cmp_win(aq, kvc, kvw_pad, small, ovl, bias_w):
    b, s, _ = aq.shape
    tq = min(TQ_NSA, s)
    nc = kvc.shape[1]
    n_sel = ovl.shape[0]
    n_top = min(A_SEL_TOPK, n_sel)
    return pl.pallas_call(
        functools.partial(_nsa_cmp_win_kernel, n_top=n_top),
        grid=(b, s // tq),
        in_specs=[pl.BlockSpec((1, tq, A_HEADS * LANES), lambda bi, i: (bi, i, 0)),
                  pl.BlockSpec((1, nc, LANES), lambda bi, i: (bi, 0, 0)),
                  pl.BlockSpec((1, s + A_WINDOW, LANES), lambda bi, i: (bi, 0, 0)),
                  pl.BlockSpec((1, tq, LANES), lambda bi, i: (bi, i, 0)),
                  _const_spec(ovl.shape),
                  _const_spec(bias_w.shape)],
        out_specs=[pl.BlockSpec((1, tq, A_HEADS * LANES), lambda bi, i: (bi, i, 0)),
                   pl.BlockSpec((1, tq, n_sel), lambda bi, i: (bi, i, 0))],
        out_shape=[jax.ShapeDtypeStruct((b, s, A_HEADS * LANES), F32),
                   jax.ShapeDtypeStruct((b, s, n_sel), BF16)],
        compiler_params=_cparams(2),
    )(aq, kvc, kvw_pad, small, ovl, bias_w)


def _flash_kernel(*refs, mode, nh):
    if mode == "sel":
        q_ref, k_ref, sel_ref, e_ref, bias_ref, part_ref, small_ref, o_ref, m_ref, l_ref, acc_ref = refs
        v_ref = k_ref
    elif mode == "fox":
        q_ref, k_ref, small_ref, cumt_ref, o_ref, m_ref, l_ref, acc_ref = refs
        v_ref = k_ref
    else:
        q_ref, k_ref, v_ref, o_ref, m_ref, l_ref, acc_ref = refs
    i = pl.program_id(1)
    tq = q_ref.shape[1]
    tk = tq
    shared_kv = mode == "sel"

    m_ref[...] = jnp.full(m_ref.shape, NEG_INF, F32)
    l_ref[...] = jnp.zeros(l_ref.shape, F32)
    acc_ref[...] = jnp.zeros(acc_ref.shape, F32)

    row = lax.broadcasted_iota(jnp.int32, (tq, tk), 0)
    col = lax.broadcasted_iota(jnp.int32, (tq, tk), 1)
    causal = row >= col
    if mode == "fox":
        small = small_ref[0]
        cum_q = [small[:, LANE_FORGET + h:LANE_FORGET + h + 1] for h in range(nh)]

    def step(j, diag):
        k0 = pl.multiple_of(j * tk, tk)
        keep = causal if diag else None
        if mode == "sel":
            sel_keys = _dot(sel_ref[0], e_ref[:, pl.ds(k0, tk)]) > 0.5
            keep = (sel_keys & causal) if diag else sel_keys
            dd = jnp.minimum(i - j, bias_ref.shape[0] - 1)
        for h in range(nh):
            kv_cols = _slot(0) if shared_kv else _slot(h)
            k_h = k_ref[0, pl.ds(k0, tk), kv_cols]
            v_h = v_ref[0, pl.ds(k0, tk), kv_cols]
            s = _dot_nt(q_ref[0, :, _slot(h)], k_h)
            if mode == "sel":
                s = s + bias_ref[dd, h]
            if mode == "fox":
                s = s + cum_q[h] - cumt_ref[0, h:h + 1, pl.ds(k0, tk)]
            if keep is not None:
                s = jnp.where(keep, s, NEG_INF)
            m_prev = m_ref[h]
            m_new = jnp.maximum(m_prev, jnp.max(s, axis=1, keepdims=True))
            p = jnp.exp(s - jnp.concatenate([m_new] * (tk // LANES), axis=1))
            alpha = jnp.exp(m_prev - m_new)
            l_ref[h] = alpha * l_ref[h] + jnp.sum(p, axis=1, keepdims=True)
            acc_ref[h] = alpha * acc_ref[h] + _dot(p.astype(BF16), v_h)
            m_ref[h] = m_new

    def full_tile(j, carry):
        step(j, False)
        return carry

    lax.fori_loop(0, i, full_tile, 0)
    step(i, True)

    if mode == "sel":
        g = _sigmoid(small_ref[0])
    for h in range(nh):
        o_h = acc_ref[h] / l_ref[h]
        if mode == "sel":
            g_s = g[:, LANE_GATE + 3 * h + 1:LANE_GATE + 3 * h + 2]
            o_h = part_ref[0, :, _slot(h)] + g_s * o_h
        o_ref[0, :, _slot(h)] = o_h.astype(BF16)


def _flash(mode, q, k, v=None, sel=None, expand=None, bias=None, part=None, small=None, cum_t=None):
    b, s, qw = q.shape
    nh = qw // LANES
    tq = min(TQ_FLASH, s)
    q_spec = pl.BlockSpec((1, tq, qw), lambda bi, i: (bi, i, 0))
    small_spec = pl.BlockSpec((1, tq, LANES), lambda bi, i: (bi, i, 0))

    def seq_spec(arr):
        return pl.BlockSpec((1,) + arr.shape[1:], lambda bi, i: (bi, 0, 0))

    if mode == "sel":
        args = (q, k, sel, expand, bias, part, small)
        in_specs = [q_spec, seq_spec(k),
                    pl.BlockSpec((1, tq, sel.shape[2]), lambda bi, i: (bi, i, 0)),
                    _const_spec(expand.shape), _const_spec(bias.shape), q_spec, small_spec]
    elif mode == "fox":
        args = (q, k, small, cum_t)
        in_specs = [q_spec, seq_spec(k), small_spec, seq_spec(cum_t)]
    else:
        args = (q, k, v)
        in_specs = [q_spec, seq_spec(k), seq_spec(v)]
    return pl.pallas_call(
        functools.partial(_flash_kernel, mode=mode, nh=nh),
        grid=(b, s // tq),
        in_specs=in_specs,
        out_specs=q_spec,
        out_shape=jax.ShapeDtypeStruct((b, s, qw), BF16),
        scratch_shapes=[pltpu.VMEM((nh, tq, LANES), F32)] * 3,
        compiler_params=_cparams(2),
    )(*args)


def _dilated_kernel(q_ref, kvp_ref, kvc_ref, bias_ref, o_ref, lse_ref):
    i = pl.program_id(2)
    tq = q_ref.shape[1]
    col = lax.broadcasted_iota(jnp.int32, (tq, 2 * tq), 1)
    in_seq = (col >= tq) | (i > 0)
    for h in range(B_HPG):
        kv = jnp.concatenate([kvp_ref[0, :, _slot(h)], kvc_ref[0, :, _slot(h)]], axis=0)
        s = _dot_nt(q_ref[0, :, _slot(h)], kv) + bias_ref[h]
        s = jnp.where(in_seq, s, NEG_INF)
        m = jnp.max(s, axis=1, keepdims=True)
        p = jnp.exp(s - m)
        den = jnp.sum(p, axis=1, keepdims=True)
        o_ref[0, :, _slot(h)] = (_dot(p.astype(BF16), kv) / den).astype(BF16)
        lse_ref[0, :, _slot(h)] = jnp.broadcast_to(m + jnp.log(den), (tq, LANES))


def _dilated(q_cls, kv_cls, bias, dil):
    b, n, _ = q_cls.shape
    tq = TQ_DIL
    width = B_HPG * LANES
    cur = pl.BlockSpec((1, tq, width), lambda bi, r, i: (bi, i, r))
    prev = pl.BlockSpec((1, tq, width), lambda bi, r, i: (bi, jnp.maximum(i - 1, 0), r))
    return pl.pallas_call(
        _dilated_kernel,
        grid=(b, dil, n // tq),
        in_specs=[cur, prev, cur, _const_spec(bias.shape)],
        out_specs=[cur, cur],
        out_shape=[jax.ShapeDtypeStruct(q_cls.shape, BF16), jax.ShapeDtypeStruct(q_cls.shape, F32)],
        compiler_params=_cparams(3),
    )(q_cls, kv_cls, kv_cls, bias)


def _merge_kernel(x_ref, oa_ref, ob0_ref, ob1_ref, ob2_ref, l0_ref, l1_ref, l2_ref, oc_ref, od_ref,
                  gpre_ref, gpost_ref, wg_ref, wba_ref, wbb_ref, wbc_ref, wbd_ref, wo_ref, out_ref):
    x = x_ref[0]
    h = _rms(x, gpre_ref[...]).astype(BF16)
    l0, l1, l2 = l0_ref[0], l1_ref[0], l2_ref[0]
    mx = jnp.maximum(jnp.maximum(l0, l1), l2)
    e0, e1, e2 = jnp.exp(l0 - mx), jnp.exp(l1 - mx), jnp.exp(l2 - mx)
    ob = (e0 * ob0_ref[0].astype(F32) + e1 * ob1_ref[0].astype(F32)
          + e2 * ob2_ref[0].astype(F32)) / (e0 + e1 + e2)
    branches = ((oa_ref[0], wba_ref), (ob.astype(BF16), wbb_ref), (oc_ref[0], wbc_ref), (od_ref[0], wbd_ref))
    merged = None
    for n, (o_n, wb_ref) in enumerate(branches):
        term = _sigmoid(_dot(h, wg_ref[n])) * _dot(o_n, wb_ref[...])
        merged = term if merged is None else merged + term
    z = _dot(merged.astype(BF16), wo_ref[...])
    out_ref[0] = x + _rms(z, gpost_ref[...])


def _merge(x, oa, obs, lses, oc, od, gpre, gpost, wg, wba, wbb, wbc, wbd, wo):
    b, s, d = x.shape
    tm = min(TM_PROJ, s)

    def rows(arr):
        return pl.BlockSpec((1, tm, arr.shape[2]), lambda bi, i: (bi, i, 0))

    acts = (x, oa, *obs, *lses, oc, od)
    consts = (gpre, gpost, wg, wba, wbb, wbc, wbd, wo)
    return pl.pallas_call(
        _merge_kernel,
        grid=(b, s // tm),
        in_specs=[rows(a) for a in acts] + [_const_spec(c.shape) for c in consts],
        out_specs=rows(x),
        out_shape=jax.ShapeDtypeStruct(x.shape, F32),
        compiler_params=_cparams(2),
    )(*acts, *consts)


def _ffn_kernel(x_ref, xp_ref, gpre_ref, gpost_ref, wup_ref, cw_ref, cb_ref, wdn_ref, out_ref, *, n_chunks):
    i = pl.program_id(1)
    tm = x_ref.shape[1]
    d_ff = wdn_ref.shape[0]
    chunk = d_ff // n_chunks
    x = x_ref[0]
    halo = jnp.where(i > 0, xp_ref[0], 0.0)
    he = _rms(jnp.concatenate([halo, x], axis=0), gpre_ref[...]).astype(BF16)

    def conv(c0):
        u = _dot(he, wup_ref[:, c0:c0 + chunk])
        w = cw_ref[:, c0:c0 + chunk]
        out = cb_ref[:, c0:c0 + chunk] + w[0:1] * pltpu.roll(u, 2, 0)[CONV_HALO:]
        out = out + w[1:2] * pltpu.roll(u, 1, 0)[CONV_HALO:]
        return out + w[2:3] * u[CONV_HALO:]

    y = jnp.zeros((tm, x.shape[1]), F32)
    for c in range(n_chunks):
        gate = conv(c * chunk)
        val = conv(d_ff + c * chunk)
        act = gate * _sigmoid(gate) * val
        y = y + _dot(act.astype(BF16), wdn_ref[c * chunk:(c + 1) * chunk, :])
    out_ref[0] = x + _rms(y, gpost_ref[...])


def _ffn(x, gpre, gpost, w_up, conv_w, conv_b, w_down):
    b, s, d = x.shape
    tm = min(TM_PROJ, s)
    d_ff = w_down.shape[0]
    n_chunks = 2 if d_ff % (2 * LANES) == 0 else 1
    halo_blocks = tm // CONV_HALO
    return pl.pallas_call(
        functools.partial(_ffn_kernel, n_chunks=n_chunks),
        grid=(b, s // tm),
        in_specs=[pl.BlockSpec((1, tm, d), lambda bi, i: (bi, i, 0)),
                  pl.BlockSpec((1, CONV_HALO, d), lambda bi, i: (bi, jnp.maximum(i * halo_blocks - 1, 0), 0)),
                  _const_spec(gpre.shape), _const_spec(gpost.shape), _const_spec(w_up.shape),
                  _const_spec(conv_w.shape), _const_spec(conv_b.shape), _const_spec(w_down.shape)],
        out_specs=pl.BlockSpec((1, tm, d), lambda bi, i: (bi, i, 0)),
        out_shape=jax.ShapeDtypeStruct(x.shape, F32),
        compiler_params=_cparams(2),
    )(x, x, gpre, gpost, w_up, conv_w, conv_b, w_down)


def _rel_bucket(dist):
    dist = jnp.maximum(dist, 0)
    d = jnp.maximum(dist, 1).astype(F32)
    large = BUCKET_EXACT + (jnp.log(d / BUCKET_EXACT) / math.log(BUCKET_MAX_DIST / BUCKET_EXACT)
                            * (N_BUCKETS - BUCKET_EXACT)).astype(jnp.int32)
    large = jnp.minimum(large, N_BUCKETS - 1)
    return jnp.where(dist < BUCKET_EXACT, dist, large)


def _in_proj_columns():
    src = np.full((N_SLOTS * LANES,), -1, np.int64)
    scl = np.ones((N_SLOTS * LANES,), np.float32)
    o_aq = 0
    o_akv = o_aq + A_HEADS * HEAD_DIM
    o_ag = o_akv + 6 * HEAD_DIM
    o_b = o_ag + 3 * A_HEADS
    o_c = o_b + 3 * B_HEADS * HEAD_DIM
    o_cf = o_c + 3 * C_HEADS * HEAD_DIM
    o_dq = o_cf + C_HEADS
    o_dkv = o_dq + D_Q_LORA
    o_dkr = o_dkv + D_KV_LORA
    e = np.arange(HEAD_DIM)
    q_scale = HEAD_DIM ** -0.5

    def put(slot, lane0, cols, scale=1.0):
        dst = slot * LANES + lane0 + np.arange(len(cols))
        src[dst] = cols
        scl[dst] = scale

    for h in range(A_HEADS):
        put(G_AQ + h, 0, o_aq + h * HEAD_DIM + e, q_scale)
    for n, slot in enumerate((G_CMP, G_SLC, G_WIN)):
        put(slot, 0, o_akv + (2 * n) * HEAD_DIM + e)
        put(slot, HEAD_DIM, o_akv + (2 * n + 1) * HEAD_DIM + e)
    for h in range(B_HEADS):
        put(G_BQ + h, 0, o_b + h * HEAD_DIM + e, q_scale)
        put(G_BKV + h, 0, o_b + (B_HEADS + h) * HEAD_DIM + e)
        put(G_BKV + h, HEAD_DIM, o_b + (2 * B_HEADS + h) * HEAD_DIM + e)
    for h in range(C_HEADS):
        put(G_CQ + h, 0, o_c + h * HEAD_DIM + e, q_scale)
        put(G_CKV + h, 0, o_c + (C_HEADS + h) * HEAD_DIM + e)
        put(G_CKV + h, HEAD_DIM, o_c + (2 * C_HEADS + h) * HEAD_DIM + e)
    put(G_DQL, 0, o_dq + np.arange(D_Q_LORA))
    put(G_DKVL, 0, o_dkv + np.arange(D_KV_LORA))
    put(G_S1, LANE_GATE, o_ag + np.arange(3 * A_HEADS))
    put(G_S1, LANE_FORGET, o_cf + np.arange(C_HEADS))
    put(G_S1, LANE_ROPE, o_dkr + np.arange(D_ROPE))
    half = D_ROPE // 2
    put(G_S2, LANE_ROPE, o_dkr + half + np.arange(half), -1.0)
    put(G_S2, LANE_ROPE + half, o_dkr + np.arange(half))
    return src, scl


def _gather_cols(w, src, scl):
    cols = jnp.take(w, jnp.asarray(np.maximum(src, 0)), axis=1)
    return jnp.where(jnp.asarray(src >= 0)[None, :], cols * jnp.asarray(scl)[None, :], 0.0)


def _mla_weight_columns():
    per_q = D_NOPE + D_ROPE
    half = D_ROPE // 2
    qa = np.full((D_HEADS * LANES,), -1, np.int64)
    qb = np.full((D_HEADS * LANES,), -1, np.int64)
    qb_s = np.ones((D_HEADS * LANES,), np.float32)
    ka = np.full((D_HEADS * LANES,), -1, np.int64)
    va = np.full((D_HEADS * LANES,), -1, np.int64)
    for h in range(D_HEADS):
        qa[h * LANES + np.arange(per_q)] = h * per_q + np.arange(per_q)
        rot = h * per_q + D_NOPE
        qb[h * LANES + D_NOPE + np.arange(half)] = rot + half + np.arange(half)
        qb_s[h * LANES + D_NOPE + np.arange(half)] = -1.0
        qb[h * LANES + D_NOPE + half + np.arange(half)] = rot + np.arange(half)
        ka[h * LANES + np.arange(D_NOPE)] = h * (D_NOPE + D_VDIM) + np.arange(D_NOPE)
        va[h * LANES + np.arange(D_VDIM)] = h * (D_NOPE + D_VDIM) + D_NOPE + np.arange(D_VDIM)
    ones = np.ones_like(qb_s)
    return (qa, ones), (qb, qb_s), (ka, ones), (va, ones)


def _pad_branch_rows(w, n_heads, lane0):
    d = w.shape[1]
    out = jnp.zeros((n_heads, LANES, d), w.dtype)
    out = out.at[:, lane0:lane0 + HEAD_DIM, :].set(w.reshape(n_heads, HEAD_DIM, d))
    return out.reshape(n_heads * LANES, d)


def _rope_tables(s):
    inv_freq = ROPE_THETA ** (-jnp.arange(0, D_ROPE, 2, dtype=F32) / D_ROPE)
    ang = jnp.arange(s, dtype=F32)[:, None] * inv_freq[None, :]
    cos2 = jnp.concatenate([jnp.cos(ang)] * 2, axis=1)
    sin2 = jnp.concatenate([jnp.sin(ang)] * 2, axis=1)
    scale = (D_NOPE + D_ROPE) ** -0.5
    z_lo = jnp.zeros((s, D_NOPE), F32)
    z_hi = jnp.zeros((s, LANES - D_NOPE - D_ROPE), F32)
    c_q = jnp.concatenate([jnp.full((s, D_NOPE), scale, F32), scale * cos2, z_hi], axis=1)
    s_q = jnp.concatenate([z_lo, scale * sin2, z_hi], axis=1)
    c_k = jnp.concatenate([z_lo, cos2, z_hi], axis=1)
    s_k = jnp.concatenate([z_lo, sin2, z_hi], axis=1)
    return jnp.concatenate([c_q, s_q, c_k, s_k], axis=1)


def _bias_tables(rel_table, s):
    by_dist = rel_table[_rel_bucket(jnp.arange(s))]
    tqa = min(TQ_NSA, s)
    q = jnp.arange(tqa)[:, None]
    c = jnp.arange(A_WINDOW + tqa)[None, :]
    dist = q + A_WINDOW - c
    ok = (dist >= 0) & (dist < A_WINDOW)
    tile = by_dist[jnp.clip(dist, 0, s - 1)][..., :A_HEADS]
    bias_w = jnp.where(ok[..., None], tile, NEG_INF)
    bias_w = jnp.moveaxis(bias_w, -1, 0).reshape(A_HEADS * tqa, A_WINDOW + tqa)
    tf = min(TQ_FLASH, s)
    n_off = min(s // tf, -(-(BUCKET_MAX_DIST + tf - 1) // tf) + 1)
    off = jnp.arange(n_off)[:, None, None] * tf
    dist = off + jnp.arange(tf)[None, :, None] - jnp.arange(tf)[None, None, :]
    bias_s = by_dist[jnp.clip(dist, 0, s - 1)][..., :A_HEADS]
    bias_s = jnp.moveaxis(bias_s, -1, 1)
    q = jnp.arange(TQ_DIL)[:, None]
    c = jnp.arange(2 * TQ_DIL)[None, :]
    m = q + TQ_DIL - c
    bias_d = []
    for g, (window, dil) in enumerate(B_GROUPS):
        ok = (m >= 0) & (m <= window // dil)
        heads = slice(A_HEADS + g * B_HPG, A_HEADS + (g + 1) * B_HPG)
        tile = by_dist[jnp.clip(m * dil, 0, s - 1)][..., heads]
        bias_d.append(jnp.moveaxis(jnp.where(ok[..., None], tile, NEG_INF), -1, 0))
    return bias_w, bias_s, bias_d


def _selection_constants(s):
    nc = s // A_CMP_STRIDE
    n_cmp = (s - A_CMP_LEN) // A_CMP_STRIDE + 1
    n_sel = s // A_SEL_BLOCK
    c = np.arange(nc)[:, None]
    j = np.arange(n_sel)[None, :]
    c_start = c * A_CMP_STRIDE
    overlap = ((c_start <= j * A_SEL_BLOCK + A_SEL_BLOCK - 1) & (c_start + A_CMP_LEN - 1 >= j * A_SEL_BLOCK)
               & (c < n_cmp))
    expand = (np.arange(s)[None, :] // A_SEL_BLOCK) == np.arange(n_sel)[:, None]
    return jnp.asarray(overlap.T, BF16), jnp.asarray(expand, BF16)


def kernel(x, rel_bias_table, norm_attn_pre, norm_attn_post, norm_ffn_pre, norm_ffn_post, w_in, nsa_cmp_pos, nsa_phi_k_w1, nsa_phi_k_w2, nsa_phi_v_w1, nsa_phi_v_w2, fox_forget_bias, mla_q_norm, mla_kv_norm, mla_w_uq, mla_w_ukv, w_branch_a, w_branch_b, w_branch_c, w_branch_d, w_merge_gate, w_o, ffn_w_up, ffn_conv_w, ffn_conv_b, ffn_w_down):
    b, s, d = x.shape
    depth = w_in.shape[0]
    assert s % (TQ_DIL * B_GROUPS[-1][1]) == 0, "every dilation class needs whole 128-row tiles"

    in_src, in_scl = _in_proj_columns()
    (qa_i, qa_s), (qb_i, qb_s), (ka_i, ka_s), (va_i, va_s) = _mla_weight_columns()
    rope_tab = _rope_tables(s)
    bias_w, bias_s, bias_d = _bias_tables(rel_bias_table, s)
    overlap, expand = _selection_constants(s)
    half = A_CMP_STRIDE * HEAD_DIM

    for l in range(depth):
        w_ext = _gather_cols(w_in[l], in_src, in_scl).astype(BF16)
        fbias = jnp.zeros((1, LANES), F32).at[0, LANE_FORGET:LANE_FORGET + C_HEADS].set(fox_forget_bias[l])
        (aq, cmp, slc, win, bq, bkv, cq, ckv, dq, dk, dv, small) = _inproj(
            x, norm_attn_pre[l][None], w_ext, rope_tab, fbias,
            mla_q_norm[l][None], mla_kv_norm[l][None],
            _gather_cols(mla_w_uq[l], qa_i, qa_s).astype(BF16),
            _gather_cols(mla_w_uq[l], qb_i, qb_s).astype(BF16),
            _gather_cols(mla_w_ukv[l], ka_i, ka_s).astype(BF16),
            _gather_cols(mla_w_ukv[l], va_i, va_s).astype(BF16))

        nc = s // A_CMP_STRIDE
        w2k_pad = jnp.pad(nsa_phi_k_w2[l], ((0, 0), (0, LANES - HEAD_DIM))).astype(BF16)
        w2v_pad = jnp.pad(nsa_phi_v_w2[l], ((0, 0), (LANES - HEAD_DIM, 0))).astype(BF16)
        kvc = _compress(cmp[..., :HEAD_DIM].reshape(b, nc, half), cmp[..., HEAD_DIM:].reshape(b, nc, half),
                        nsa_cmp_pos[l].reshape(2, half),
                        nsa_phi_k_w1[l].astype(BF16), nsa_phi_v_w1[l].astype(BF16), w2k_pad, w2v_pad)
        kvw_pad = jnp.pad(win, ((0, 0), (A_WINDOW, 0), (0, 0)))
        part, sel = _nsa_cmp_win(aq, kvc, kvw_pad, small, overlap, bias_w)
        o_a = _flash("sel", aq, slc, sel=sel, expand=expand, bias=bias_s, part=part, small=small)

        obs, lses = [], []
        for g, (_, dil) in enumerate(B_GROUPS):
            cols = slice(g * B_HPG * LANES, (g + 1) * B_HPG * LANES)
            cls_shape = (b, s // dil, dil * B_HPG * LANES)
            o_g, lse_g = _dilated(bq[..., cols].reshape(cls_shape), bkv[..., cols].reshape(cls_shape),
                                  bias_d[g], dil)
            obs.append(o_g.reshape(b, s, B_HPG * LANES))
            lses.append(lse_g.reshape(b, s, B_HPG * LANES))

        cum_t = jnp.swapaxes(small[..., LANE_FORGET:LANE_FORGET + C_HEADS], 1, 2)
        o_c = _flash("fox", cq, ckv, small=small, cum_t=cum_t)
        o_d = _flash("mla", dq, dk, v=dv)

        x = _merge(x, o_a, obs, lses, o_c, o_d,
                   norm_attn_pre[l][None], norm_attn_post[l][None],
                   w_merge_gate[l].astype(BF16),
                   _pad_branch_rows(w_branch_a[l], A_HEADS, HEAD_DIM).astype(BF16),
                   _pad_branch_rows(w_branch_b[l], B_HPG, HEAD_DIM).astype(BF16),
                   _pad_branch_rows(w_branch_c[l], C_HEADS, HEAD_DIM).astype(BF16),
                   _pad_branch_rows(w_branch_d[l], D_HEADS, 0).astype(BF16),
                   w_o[l].astype(BF16))
        x = _ffn(x, norm_ffn_pre[l][None], norm_ffn_post[l][None], ffn_w_up[l].astype(BF16),
                 ffn_conv_w[l], ffn_conv_b[l][None], ffn_w_down[l].astype(BF16))
    return x
```

```python
import functools
import math

import numpy as np
import jax
import jax.numpy as jnp
from jax import lax
from jax.experimental import pallas as pl
from jax.experimental.pallas import tpu as pltpu

F32 = jnp.float32
BF16 = jnp.bfloat16

LANES = 128
HEAD_DIM = 64
RMS_EPS = 1e-6
NEG_INF = -1e30
FORCE_SCORE = 1e9
VMEM_LIMIT_BYTES = 56 * 1024 * 1024

N_BUCKETS = 32
BUCKET_EXACT = 16
BUCKET_MAX_DIST = 2048

A_HEADS = 4
A_CMP_LEN = 32
A_CMP_STRIDE = 16
A_PHI_HIDDEN = 128
A_SEL_BLOCK = 64
A_SEL_TOPK = 16
A_WINDOW = 512
B_GROUPS = ((128, 1), (512, 4), (2048, 16))
B_HPG = 2
B_HEADS = B_HPG * len(B_GROUPS)
C_HEADS = 4
D_HEADS = 4
D_Q_LORA = 256
D_KV_LORA = 128
D_NOPE = 64
D_ROPE = 32
D_VDIM = 64
ROPE_THETA = 10000.0
N_BRANCHES = 4

G_AQ = 0
G_CMP = G_AQ + A_HEADS
G_SLC = G_CMP + 1
G_WIN = G_SLC + 1
G_BQ = G_WIN + 1
G_BKV = G_BQ + B_HEADS
G_CQ = G_BKV + B_HEADS
G_CKV = G_CQ + C_HEADS
G_DQL = G_CKV + C_HEADS
G_DKVL = G_DQL + 2
G_S1 = G_DKVL + 1
G_S2 = G_S1 + 1
N_SLOTS = G_S2 + 1
LANE_GATE = 0
LANE_FORGET = 3 * A_HEADS
LANE_ROPE = 64

TM_PROJ = 256
TQ_FLASH = 256
TQ_NSA = 128
TQ_DIL = 128
CONV_HALO = 16


def _cparams(n_grid):
    return pltpu.CompilerParams(dimension_semantics=("arbitrary",) * n_grid,
                                vmem_limit_bytes=VMEM_LIMIT_BYTES)


def _const_spec(shape):
    nd = len(shape)
    return pl.BlockSpec(shape, lambda *_: (0,) * nd, pipeline_mode=pl.Buffered(1))


def _rms(x, gain):
    return x * lax.rsqrt(jnp.mean(x * x, axis=-1, keepdims=True) + RMS_EPS) * gain


def _sigmoid(x):
    return 1.0 / (1.0 + jnp.exp(-x))


def _dot(a, b):
    return jnp.dot(a, b, preferred_element_type=F32)


def _dot_nt(a, b):
    return lax.dot_general(a, b, (((1,), (1,)), ((), ())), preferred_element_type=F32)


def _slot(h):
    return slice(h * LANES, (h + 1) * LANES)


def _inproj_kernel(x_ref, g_ref, w_ref, tab_ref, fb_ref, qn_ref, kvn_ref, wqa_ref, wqb_ref,
                   wka_ref, wv_ref,
                   aq_ref, cmp_ref, slc_ref, win_ref, bq_ref, bkv_ref, cq_ref, ckv_ref,
                   dq_ref, dk_ref, dv_ref, small_ref, carry_ref):
    i = pl.program_id(1)
    tm = x_ref.shape[1]
    h = _rms(x_ref[0], g_ref[...]).astype(BF16)

    def proj(g0, n):
        return _dot(h, w_ref[:, g0 * LANES:(g0 + n) * LANES])

    aq_ref[0] = proj(G_AQ, A_HEADS).astype(BF16)
    cmp_ref[0] = proj(G_CMP, 1)
    slc_ref[0] = proj(G_SLC, 1).astype(BF16)
    win_ref[0] = proj(G_WIN, 1).astype(BF16)
    bq_ref[0] = proj(G_BQ, B_HEADS).astype(BF16)
    bkv_ref[0] = proj(G_BKV, B_HEADS).astype(BF16)
    cq_ref[0] = proj(G_CQ, C_HEADS).astype(BF16)
    ckv_ref[0] = proj(G_CKV, C_HEADS).astype(BF16)

    s1 = proj(G_S1, 1)
    s2 = proj(G_S2, 1)

    @pl.when(i == 0)
    def _():
        carry_ref[...] = jnp.zeros_like(carry_ref)

    z = s1 + fb_ref[...]
    logf = jnp.minimum(z, 0.0) - jnp.log(1.0 + jnp.exp(-jnp.abs(z)))
    row = lax.broadcasted_iota(jnp.int32, (tm, tm), 0)
    col = lax.broadcasted_iota(jnp.int32, (tm, tm), 1)
    tri = (row >= col).astype(BF16)
    hi = logf.astype(BF16)
    r1 = logf - hi.astype(F32)
    mid = r1.astype(BF16)
    lo = (r1 - mid.astype(F32)).astype(BF16)
    cum = _dot(tri, hi) + _dot(tri, mid) + _dot(tri, lo) + carry_ref[0:1, :]
    carry_ref[0:1, :] = cum[tm - 1:tm, :]
    lane = lax.broadcasted_iota(jnp.int32, (tm, LANES), 1)
    is_forget = (lane >= LANE_FORGET) & (lane < LANE_FORGET + C_HEADS)
    small_ref[0] = jnp.where(is_forget, cum, s1)

    tab = tab_ref[...]
    c_q, s_q, c_k, s_k = (tab[:, _slot(n)] for n in range(4))
    qn = _rms(proj(G_DQL, 2), qn_ref[...]).astype(BF16)
    qa = _dot(qn, wqa_ref[...])
    qb = _dot(qn, wqb_ref[...])
    kvn = _rms(proj(G_DKVL, 1), kvn_ref[...]).astype(BF16)
    ka = _dot(kvn, wka_ref[...])
    k_rot = s1 * c_k + s2 * s_k
    for hd in range(D_HEADS):
        dq_ref[0, :, _slot(hd)] = (qa[:, _slot(hd)] * c_q + qb[:, _slot(hd)] * s_q).astype(BF16)
        dk_ref[0, :, _slot(hd)] = (ka[:, _slot(hd)] + k_rot).astype(BF16)
    dv_ref[0] = _dot(kvn, wv_ref[...]).astype(BF16)


def _inproj(x, gain, w_ext, tab, fbias, qn_g, kvn_g, wqa, wqb, wka, wv):
    b, s, d = x.shape
    tm = min(TM_PROJ, s)
    n_cols = w_ext.shape[1]

    def rows(width):
        return pl.BlockSpec((1, tm, width), lambda bi, i: (bi, i, 0))

    out_widths = [(A_HEADS, BF16), (1, F32), (1, BF16), (1, BF16), (B_HEADS, BF16), (B_HEADS, BF16),
                  (C_HEADS, BF16), (C_HEADS, BF16), (D_HEADS, BF16), (D_HEADS, BF16), (D_HEADS, BF16),
                  (1, F32)]
    return pl.pallas_call(
        _inproj_kernel,
        grid=(b, s // tm),
        in_specs=[rows(d),
                  _const_spec((1, d)),
                  _const_spec((d, n_cols)),
                  pl.BlockSpec((tm, 4 * LANES), lambda bi, i: (i, 0)),
                  _const_spec((1, LANES)),
                  _const_spec((1, D_Q_LORA)),
                  _const_spec((1, D_KV_LORA)),
                  _const_spec(wqa.shape), _const_spec(wqb.shape),
                  _const_spec(wka.shape), _const_spec(wv.shape)],
        out_specs=[rows(n * LANES) for n, _ in out_widths],
        out_shape=[jax.ShapeDtypeStruct((b, s, n * LANES), dt) for n, dt in out_widths],
        scratch_shapes=[pltpu.VMEM((8, LANES), F32)],
        name="inproj",
        compiler_params=_cparams(2),
    )(x, gain, w_ext, tab, fbias, qn_g, kvn_g, wqa, wqb, wka, wv)


def _gelu_tanh(x):
    return 0.5 * x * (1.0 + jnp.tanh(math.sqrt(2.0 / math.pi) * (x + 0.044715 * (x * x * x))))


def _compress_kernel(kc_ref, vc_ref, pos_ref, w1k_ref, w1v_ref, w2k_ref, w2v_ref, out_ref):
    nc = kc_ref.shape[1]
    half = A_CMP_STRIDE * HEAD_DIM
    pos_a = pos_ref[0:1, :]
    pos_b = pos_ref[1:2, :]

    def phi(chunks, w1_ref, w2_ref):
        u = _dot((chunks + pos_a).astype(BF16), w1_ref[0:half, :])
        v = _dot((chunks + pos_b).astype(BF16), w1_ref[half:2 * half, :])
        hid = u + pltpu.roll(v, nc - 1, 0)
        return _dot(_gelu_tanh(hid).astype(BF16), w2_ref[...])

    out_ref[0] = (phi(kc_ref[0], w1k_ref, w2k_ref) + phi(vc_ref[0], w1v_ref, w2v_ref)).astype(BF16)


def _compress(kc_in, vc_in, pos2, w1k, w1v, w2k_pad, w2v_pad):
    b, nc, width = kc_in.shape
    blk = pl.BlockSpec((1, nc, width), lambda bi: (bi, 0, 0))
    return pl.pallas_call(
        _compress_kernel,
        grid=(b,),
        in_specs=[blk, blk, _const_spec(pos2.shape), _const_spec(w1k.shape), _const_spec(w1v.shape),
                  _const_spec(w2k_pad.shape), _const_spec(w2v_pad.shape)],
        out_specs=pl.BlockSpec((1, nc, LANES), lambda bi: (bi, 0, 0)),
        out_shape=jax.ShapeDtypeStruct((b, nc, LANES), BF16),
        name="compress",
        compiler_params=_cparams(1),
    )(kc_in, vc_in, pos2, w1k, w1v, w2k_pad, w2v_pad)


def _nsa_cmp_win_kernel(q_ref, kvc_ref, kvw_ref, small_ref, ovl_ref, bias_ref,
                        part_ref, sel_ref, *, n_top):
    i = pl.program_id(1)
    tq = q_ref.shape[1]
    nc = kvc_ref.shape[1]
    n_sel = ovl_ref.shape[0]
    t0 = pl.multiple_of(i * tq, tq)
    q4 = jnp.concatenate([q_ref[0, :, _slot(h)] for h in range(A_HEADS)], axis=0)
    rows = A_HEADS * tq

    kvc = kvc_ref[0]
    s = _dot_nt(q4, kvc)
    t_row = t0 + (lax.broadcasted_iota(jnp.int32, (rows, nc), 0) & (tq - 1))
    c_end = lax.broadcasted_iota(jnp.int32, (rows, nc), 1) * A_CMP_STRIDE + (A_CMP_LEN - 1)
    valid = c_end <= t_row
    s = jnp.where(valid, s, NEG_INF)
    m = jnp.max(s, axis=1, keepdims=True)
    p = jnp.where(valid, jnp.exp(s - m), 0.0)
    den = jnp.maximum(jnp.sum(p, axis=1, keepdims=True), 1e-30)
    p = p / den
    o_c = _dot(p.astype(BF16), kvc)

    p_sum = p[0:tq] + p[tq:2 * tq] + p[2 * tq:3 * tq] + p[3 * tq:4 * tq]
    p_hi = p_sum.astype(BF16)
    p_lo = (p_sum - p_hi.astype(F32)).astype(BF16)
    imp_t = _dot_nt(ovl_ref[...], p_hi) + _dot_nt(ovl_ref[...], p_lo)

    j_idx = lax.broadcasted_iota(jnp.int32, (n_sel, tq), 0)
    cur = lax.shift_right_logical(t0 + lax.broadcasted_iota(jnp.int32, (n_sel, tq), 1), 6)
    forced = (j_idx == 0) | (j_idx == cur) | (j_idx == cur - 1)
    imp_t = jnp.where(forced, FORCE_SCORE, jnp.where(j_idx > cur, -FORCE_SCORE, imp_t))
    rank = jnp.zeros((n_sel, tq), F32)
    for c in range(n_sel):
        row_c = imp_t[c:c + 1, :]
        beats = (row_c > imp_t) | ((row_c == imp_t) & (j_idx > c))
        rank = rank + jnp.where(beats, 1.0, 0.0)
    sel_t = jnp.where(rank < float(n_top), 1.0, 0.0).astype(BF16)
    eye = (lax.broadcasted_iota(jnp.int32, (tq, tq), 0)
           == lax.broadcasted_iota(jnp.int32, (tq, tq), 1)).astype(BF16)
    sel_ref[0] = _dot_nt(eye, sel_t).astype(BF16)

    span = A_WINDOW + tq
    kvw = kvw_ref[0, pl.ds(t0, span), :]
    sw = _dot_nt(q4, kvw) + bias_ref[...]
    kpos = t0 - A_WINDOW + lax.broadcasted_iota(jnp.int32, (rows, span), 1)
    sw = jnp.where(kpos >= 0, sw, NEG_INF)
    mw = jnp.max(sw, axis=1, keepdims=True)
    pw = jnp.exp(sw - mw)
    o_w = _dot(pw.astype(BF16), kvw) / jnp.sum(pw, axis=1, keepdims=True)

    g = _sigmoid(small_ref[0])
    for h in range(A_HEADS):
        g_c = g[:, LANE_GATE + 3 * h:LANE_GATE + 3 * h + 1]
        g_w = g[:, LANE_GATE + 3 * h + 2:LANE_GATE + 3 * h + 3]
        part_ref[0, :, _slot(h)] = g_c * o_c[h * tq:(h + 1) * tq] + g_w * o_w[h * tq:(h + 1) * tq]


def _nsa_cmp_win(aq, kvc, kvw_pad, small, ovl, bias_w):
    b, s, _ = aq.shape
    tq = min(TQ_NSA, s)
    nc = kvc.shape[1]
    n_sel = ovl.shape[0]
    n_top = min(A_SEL_TOPK, n_sel)
    return pl.pallas_call(
        functools.partial(_nsa_cmp_win_kernel, n_top=n_top),
        grid=(b, s // tq),
        in_specs=[pl.BlockSpec((1, tq, A_HEADS * LANES), lambda bi, i: (bi, i, 0)),
                  pl.BlockSpec((1, nc, LANES), lambda bi, i: (bi, 0, 0)),
                  pl.BlockSpec((1, s + A_WINDOW, LANES), lambda bi, i: (bi, 0, 0)),
                  pl.BlockSpec((1, tq, LANES), lambda bi, i: (bi, i, 0)),
                  _const_spec(ovl.shape),
                  _const_spec(bias_w.shape)],
        out_specs=[pl.BlockSpec((1, tq, A_HEADS * LANES), lambda bi, i: (bi, i, 0)),
                   pl.BlockSpec((1, tq, n_sel), lambda bi, i: (bi, i, 0))],
        out_shape=[jax.ShapeDtypeStruct((b, s, A_HEADS * LANES), F32),
                   jax.ShapeDtypeStruct((b, s, n_sel), BF16)],
        name="nsa_cmp_win",
        compiler_params=_cparams(2),
    )(aq, kvc, kvw_pad, small, ovl, bias_w)


def _flash_kernel(*refs, mode, nh):
    if mode == "sel":
        q_ref, k_ref, sel_ref, e_ref, bias_ref, part_ref, small_ref, o_ref, m_ref, l_ref, acc_ref = refs
        v_ref = k_ref
    elif mode == "fox":
        q_ref, k_ref, small_ref, cumt_ref, o_ref, m_ref, l_ref, acc_ref = refs
        v_ref = k_ref
    else:
        q_ref, k_ref, v_ref, o_ref, m_ref, l_ref, acc_ref = refs
    i = pl.program_id(1)
    tq = q_ref.shape[1]
    tk = tq
    shared_kv = mode == "sel"

    m_ref[...] = jnp.full(m_ref.shape, NEG_INF, F32)
    l_ref[...] = jnp.zeros(l_ref.shape, F32)
    acc_ref[...] = jnp.zeros(acc_ref.shape, F32)

    row = lax.broadcasted_iota(jnp.int32, (tq, tk), 0)
    col = lax.broadcasted_iota(jnp.int32, (tq, tk), 1)
    causal = row >= col
    if mode == "fox":
        small = small_ref[0]
        cum_q = [small[:, LANE_FORGET + h:LANE_FORGET + h + 1] for h in range(nh)]

    def step(j, diag):
        k0 = pl.multiple_of(j * tk, tk)
        keep = causal if diag else None
        if mode == "sel":
            sel_keys = _dot(sel_ref[0], e_ref[:, pl.ds(k0, tk)]) > 0.5
            keep = (sel_keys & causal) if diag else sel_keys
            dd = jnp.minimum(i - j, bias_ref.shape[0] - 1)
        logits = []
        for h in range(nh):
            kv_cols = _slot(0) if shared_kv else _slot(h)
            s = _dot_nt(q_ref[0, :, _slot(h)], k_ref[0, pl.ds(k0, tk), kv_cols])
            if mode == "sel":
                s = s + bias_ref[dd, h]
            if mode == "fox":
                s = s + cum_q[h] - cumt_ref[0, h:h + 1, pl.ds(k0, tk)]
            if keep is not None:
                s = jnp.where(keep, s, NEG_INF)
            logits.append(s)
        probs = []
        for h in range(nh):
            m_prev = m_ref[h]
            m_new = jnp.maximum(m_prev, jnp.max(logits[h], axis=1, keepdims=True))
            p = jnp.exp(logits[h] - jnp.concatenate([m_new] * (tk // LANES), axis=1))
            alpha = jnp.exp(m_prev - m_new)
            l_ref[h] = alpha * l_ref[h] + jnp.sum(p, axis=1, keepdims=True)
            m_ref[h] = m_new
            probs.append((p.astype(BF16), alpha))
        for h in range(nh):
            kv_cols = _slot(0) if shared_kv else _slot(h)
            p, alpha = probs[h]
            acc_ref[h] = alpha * acc_ref[h] + _dot(p, v_ref[0, pl.ds(k0, tk), kv_cols])

    def full_tile(j, carry):
        step(j, False)
        return carry

    lax.fori_loop(0, i, full_tile, 0)
    step(i, True)

    if mode == "sel":
        g = _sigmoid(small_ref[0])
    for h in range(nh):
        o_h = acc_ref[h] / l_ref[h]
        if mode == "sel":
            g_s = g[:, LANE_GATE + 3 * h + 1:LANE_GATE + 3 * h + 2]
            o_h = part_ref[0, :, _slot(h)] + g_s * o_h
        o_ref[0, :, _slot(h)] = o_h.astype(BF16)


def _flash(mode, q, k, v=None, sel=None, expand=None, bias=None, part=None, small=None, cum_t=None):
    b, s, qw = q.shape
    nh = qw // LANES
    tq = min(TQ_FLASH, s)
    q_spec = pl.BlockSpec((1, tq, qw), lambda bi, i: (bi, i, 0))
    small_spec = pl.BlockSpec((1, tq, LANES), lambda bi, i: (bi, i, 0))

    def seq_spec(arr):
        return pl.BlockSpec((1,) + arr.shape[1:], lambda bi, i: (bi, 0, 0))

    if mode == "sel":
        args = (q, k, sel, expand, bias, part, small)
        in_specs = [q_spec, seq_spec(k),
                    pl.BlockSpec((1, tq, sel.shape[2]), lambda bi, i: (bi, i, 0)),
                    _const_spec(expand.shape), _const_spec(bias.shape), q_spec, small_spec]
    elif mode == "fox":
        args = (q, k, small, cum_t)
        in_specs = [q_spec, seq_spec(k), small_spec, seq_spec(cum_t)]
    else:
        args = (q, k, v)
        in_specs = [q_spec, seq_spec(k), seq_spec(v)]
    return pl.pallas_call(
        functools.partial(_flash_kernel, mode=mode, nh=nh),
        grid=(b, s // tq),
        in_specs=in_specs,
        out_specs=q_spec,
        out_shape=jax.ShapeDtypeStruct((b, s, qw), BF16),
        scratch_shapes=[pltpu.VMEM((nh, tq, LANES), F32)] * 3,
        name="flash_" + mode,
        compiler_params=_cparams(2),
    )(*args)


def _dilated_kernel(q_ref, kvp_ref, kvc_ref, bias_ref, o_ref, lse_ref):
    i = pl.program_id(2)
    tq = q_ref.shape[1]
    col = lax.broadcasted_iota(jnp.int32, (tq, 2 * tq), 1)
    in_seq = (col >= tq) | (i > 0)
    for h in range(B_HPG):
        kv = jnp.concatenate([kvp_ref[0, :, _slot(h)], kvc_ref[0, :, _slot(h)]], axis=0)
        s = _dot_nt(q_ref[0, :, _slot(h)], kv) + bias_ref[h]
        s = jnp.where(in_seq, s, NEG_INF)
        m = jnp.max(s, axis=1, keepdims=True)
        p = jnp.exp(s - m)
        den = jnp.sum(p, axis=1, keepdims=True)
        o_ref[0, :, _slot(h)] = (_dot(p.astype(BF16), kv) / den).astype(BF16)
        lse_ref[0, :, _slot(h)] = jnp.broadcast_to(m + jnp.log(den), (tq, LANES))


def _dilated(q_cls, kv_cls, bias, dil):
    b, n, _ = q_cls.shape
    tq = TQ_DIL
    width = B_HPG * LANES
    cur = pl.BlockSpec((1, tq, width), lambda bi, r, i: (bi, i, r))
    prev = pl.BlockSpec((1, tq, width), lambda bi, r, i: (bi, jnp.maximum(i - 1, 0), r))
    return pl.pallas_call(
        _dilated_kernel,
        grid=(b, dil, n // tq),
        in_specs=[cur, prev, cur, _const_spec(bias.shape)],
        out_specs=[cur, cur],
        out_shape=[jax.ShapeDtypeStruct(q_cls.shape, BF16), jax.ShapeDtypeStruct(q_cls.shape, F32)],
        name="dilated",
        compiler_params=_cparams(3),
    )(q_cls, kv_cls, kv_cls, bias)


def _merge_kernel(x_ref, oa_ref, ob0_ref, ob1_ref, ob2_ref, l0_ref, l1_ref, l2_ref, oc_ref, od_ref,
                  gpre_ref, gpost_ref, wg_ref, wba_ref, wbb_ref, wbc_ref, wbd_ref, wo_ref, out_ref):
    x = x_ref[0]
    h = _rms(x, gpre_ref[...]).astype(BF16)
    l0, l1, l2 = l0_ref[0], l1_ref[0], l2_ref[0]
    mx = jnp.maximum(jnp.maximum(l0, l1), l2)
    e0, e1, e2 = jnp.exp(l0 - mx), jnp.exp(l1 - mx), jnp.exp(l2 - mx)
    ob = (e0 * ob0_ref[0].astype(F32) + e1 * ob1_ref[0].astype(F32)
          + e2 * ob2_ref[0].astype(F32)) / (e0 + e1 + e2)
    branches = ((oa_ref[0], wba_ref), (ob.astype(BF16), wbb_ref), (oc_ref[0], wbc_ref), (od_ref[0], wbd_ref))
    merged = None
    for n, (o_n, wb_ref) in enumerate(branches):
        term = _sigmoid(_dot(h, wg_ref[n])) * _dot(o_n, wb_ref[...])
        merged = term if merged is None else merged + term
    z = _dot(merged.astype(BF16), wo_ref[...])
    out_ref[0] = x + _rms(z, gpost_ref[...])


def _merge(x, oa, obs, lses, oc, od, gpre, gpost, wg, wba, wbb, wbc, wbd, wo):
    b, s, d = x.shape
    tm = min(TM_PROJ, s)

    def rows(arr):
        return pl.BlockSpec((1, tm, arr.shape[2]), lambda bi, i: (bi, i, 0))

    acts = (x, oa, *obs, *lses, oc, od)
    consts = (gpre, gpost, wg, wba, wbb, wbc, wbd, wo)
    return pl.pallas_call(
        _merge_kernel,
        grid=(b, s // tm),
        in_specs=[rows(a) for a in acts] + [_const_spec(c.shape) for c in consts],
        out_specs=rows(x),
        out_shape=jax.ShapeDtypeStruct(x.shape, F32),
        name="merge",
        compiler_params=_cparams(2),
    )(*acts, *consts)


def _ffn_kernel(x_ref, xp_ref, gpre_ref, gpost_ref, wup_ref, cw_ref, cb_ref, wdn_ref, out_ref, *, n_chunks):
    i = pl.program_id(1)
    tm = x_ref.shape[1]
    d_ff = wdn_ref.shape[0]
    chunk = d_ff // n_chunks
    x = x_ref[0]
    halo = jnp.where(i > 0, xp_ref[0], 0.0)
    he = _rms(jnp.concatenate([halo, x], axis=0), gpre_ref[...]).astype(BF16)

    def conv(c0):
        u = _dot(he, wup_ref[:, c0:c0 + chunk])
        w = cw_ref[:, c0:c0 + chunk]
        out = cb_ref[:, c0:c0 + chunk] + w[0:1] * pltpu.roll(u, 2, 0)[CONV_HALO:]
        out = out + w[1:2] * pltpu.roll(u, 1, 0)[CONV_HALO:]
        return out + w[2:3] * u[CONV_HALO:]

    y = jnp.zeros((tm, x.shape[1]), F32)
    for c in range(n_chunks):
        gate = conv(c * chunk)
        val = conv(d_ff + c * chunk)
        act = gate * _sigmoid(gate) * val
        y = y + _dot(act.astype(BF16), wdn_ref[c * chunk:(c + 1) * chunk, :])
    out_ref[0] = x + _rms(y, gpost_ref[...])


def _ffn(x, gpre, gpost, w_up, conv_w, conv_b, w_down):
    b, s, d = x.shape
    tm = min(TM_PROJ, s)
    d_ff = w_down.shape[0]
    n_chunks = 2 if d_ff % (2 * LANES) == 0 else 1
    halo_blocks = tm // CONV_HALO
    return pl.pallas_call(
        functools.partial(_ffn_kernel, n_chunks=n_chunks),
        grid=(b, s // tm),
        in_specs=[pl.BlockSpec((1, tm, d), lambda bi, i: (bi, i, 0)),
                  pl.BlockSpec((1, CONV_HALO, d), lambda bi, i: (bi, jnp.maximum(i * halo_blocks - 1, 0), 0)),
                  _const_spec(gpre.shape), _const_spec(gpost.shape), _const_spec(w_up.shape),
                  _const_spec(conv_w.shape), _const_spec(conv_b.shape), _const_spec(w_down.shape)],
        out_specs=pl.BlockSpec((1, tm, d), lambda bi, i: (bi, i, 0)),
        out_shape=jax.ShapeDtypeStruct(x.shape, F32),
        name="ffn",
        compiler_params=_cparams(2),
    )(x, x, gpre, gpost, w_up, conv_w, conv_b, w_down)


def _rel_bucket(dist):
    dist = jnp.maximum(dist, 0)
    d = jnp.maximum(dist, 1).astype(F32)
    large = BUCKET_EXACT + (jnp.log(d / BUCKET_EXACT) / math.log(BUCKET_MAX_DIST / BUCKET_EXACT)
                            * (N_BUCKETS - BUCKET_EXACT)).astype(jnp.int32)
    large = jnp.minimum(large, N_BUCKETS - 1)
    return jnp.where(dist < BUCKET_EXACT, dist, large)


def _in_proj_columns():
    src = np.full((N_SLOTS * LANES,), -1, np.int64)
    scl = np.ones((N_SLOTS * LANES,), np.float32)
    o_aq = 0
    o_akv = o_aq + A_HEADS * HEAD_DIM
    o_ag = o_akv + 6 * HEAD_DIM
    o_b = o_ag + 3 * A_HEADS
    o_c = o_b + 3 * B_HEADS * HEAD_DIM
    o_cf = o_c + 3 * C_HEADS * HEAD_DIM
    o_dq = o_cf + C_HEADS
    o_dkv = o_dq + D_Q_LORA
    o_dkr = o_dkv + D_KV_LORA
    e = np.arange(HEAD_DIM)
    q_scale = HEAD_DIM ** -0.5

    def put(slot, lane0, cols, scale=1.0):
        dst = slot * LANES + lane0 + np.arange(len(cols))
        src[dst] = cols
        scl[dst] = scale

    for h in range(A_HEADS):
        put(G_AQ + h, 0, o_aq + h * HEAD_DIM + e, q_scale)
    for n, slot in enumerate((G_CMP, G_SLC, G_WIN)):
        put(slot, 0, o_akv + (2 * n) * HEAD_DIM + e)
        put(slot, HEAD_DIM, o_akv + (2 * n + 1) * HEAD_DIM + e)
    for h in range(B_HEADS):
        put(G_BQ + h, 0, o_b + h * HEAD_DIM + e, q_scale)
        put(G_BKV + h, 0, o_b + (B_HEADS + h) * HEAD_DIM + e)
        put(G_BKV + h, HEAD_DIM, o_b + (2 * B_HEADS + h) * HEAD_DIM + e)
    for h in range(C_HEADS):
        put(G_CQ + h, 0, o_c + h * HEAD_DIM + e, q_scale)
        put(G_CKV + h, 0, o_c + (C_HEADS + h) * HEAD_DIM + e)
        put(G_CKV + h, HEAD_DIM, o_c + (2 * C_HEADS + h) * HEAD_DIM + e)
    put(G_DQL, 0, o_dq + np.arange(D_Q_LORA))
    put(G_DKVL, 0, o_dkv + np.arange(D_KV_LORA))
    put(G_S1, LANE_GATE, o_ag + np.arange(3 * A_HEADS))
    put(G_S1, LANE_FORGET, o_cf + np.arange(C_HEADS))
    put(G_S1, LANE_ROPE, o_dkr + np.arange(D_ROPE))
    half = D_ROPE // 2
    put(G_S2, LANE_ROPE, o_dkr + half + np.arange(half), -1.0)
    put(G_S2, LANE_ROPE + half, o_dkr + np.arange(half))
    return src, scl


def _gather_cols(w, src, scl):
    pieces = []
    n = len(src)
    a = 0
    while a < n:
        e = a + 1
        if src[a] < 0:
            while e < n and src[e] < 0:
                e += 1
            pieces.append(jnp.zeros((w.shape[0], e - a), w.dtype))
        else:
            while e < n and src[e] == src[e - 1] + 1 and scl[e] == scl[a]:
                e += 1
            run = w[:, int(src[a]):int(src[a]) + (e - a)]
            pieces.append(run if scl[a] == 1.0 else run * float(scl[a]))
        a = e
    return jnp.concatenate(pieces, axis=1)


def _mla_weight_columns():
    per_q = D_NOPE + D_ROPE
    half = D_ROPE // 2
    qa = np.full((D_HEADS * LANES,), -1, np.int64)
    qb = np.full((D_HEADS * LANES,), -1, np.int64)
    qb_s = np.ones((D_HEADS * LANES,), np.float32)
    ka = np.full((D_HEADS * LANES,), -1, np.int64)
    va = np.full((D_HEADS * LANES,), -1, np.int64)
    for h in range(D_HEADS):
        qa[h * LANES + np.arange(per_q)] = h * per_q + np.arange(per_q)
        rot = h * per_q + D_NOPE
        qb[h * LANES + D_NOPE + np.arange(half)] = rot + half + np.arange(half)
        qb_s[h * LANES + D_NOPE + np.arange(half)] = -1.0
        qb[h * LANES + D_NOPE + half + np.arange(half)] = rot + np.arange(half)
        ka[h * LANES + np.arange(D_NOPE)] = h * (D_NOPE + D_VDIM) + np.arange(D_NOPE)
        va[h * LANES + np.arange(D_VDIM)] = h * (D_NOPE + D_VDIM) + D_NOPE + np.arange(D_VDIM)
    ones = np.ones_like(qb_s)
    return (qa, ones), (qb, qb_s), (ka, ones), (va, ones)


def _pad_branch_rows(w, n_heads, lane0):
    d = w.shape[1]
    out = jnp.zeros((n_heads, LANES, d), w.dtype)
    out = out.at[:, lane0:lane0 + HEAD_DIM, :].set(w.reshape(n_heads, HEAD_DIM, d))
    return out.reshape(n_heads * LANES, d)


def _rope_tables(s):
    inv_freq = ROPE_THETA ** (-jnp.arange(0, D_ROPE, 2, dtype=F32) / D_ROPE)
    ang = jnp.arange(s, dtype=F32)[:, None] * inv_freq[None, :]
    cos2 = jnp.concatenate([jnp.cos(ang)] * 2, axis=1)
    sin2 = jnp.concatenate([jnp.sin(ang)] * 2, axis=1)
    scale = (D_NOPE + D_ROPE) ** -0.5
    z_lo = jnp.zeros((s, D_NOPE), F32)
    z_hi = jnp.zeros((s, LANES - D_NOPE - D_ROPE), F32)
    c_q = jnp.concatenate([jnp.full((s, D_NOPE), scale, F32), scale * cos2, z_hi], axis=1)
    s_q = jnp.concatenate([z_lo, scale * sin2, z_hi], axis=1)
    c_k = jnp.concatenate([z_lo, cos2, z_hi], axis=1)
    s_k = jnp.concatenate([z_lo, sin2, z_hi], axis=1)
    return jnp.concatenate([c_q, s_q, c_k, s_k], axis=1)


def _toeplitz(vec, n_rows, n_cols, off, step=1):
    base = vec[:, ::step]
    last = base.shape[1] - 1
    lo, hi = off - (n_cols - 1), off + n_rows - 1
    core = base[:, max(lo, 0):min(hi, last) + 1]
    front = jnp.repeat(base[:, :1], max(0, -lo), axis=1)
    back = jnp.repeat(base[:, last:], max(0, hi - last), axis=1)
    g = jnp.concatenate([front, core, back], axis=1)
    lg = n_rows + n_cols - 1
    t = jnp.tile(g, (1, n_rows + 1))[:, :n_rows * (lg + 1)].reshape(-1, n_rows, lg + 1)
    return t[:, :, :n_cols][:, :, ::-1]


def _bias_tables(rel_table, s):
    bucket = _rel_bucket(jnp.arange(s))
    hit = bucket[None, :, None] == jnp.arange(N_BUCKETS)[None, None, :]
    by_dist = jnp.sum(jnp.where(hit, rel_table.T[:, None, :], 0.0), axis=-1)
    tqa = min(TQ_NSA, s)
    dist = jnp.arange(tqa)[:, None] + A_WINDOW - jnp.arange(A_WINDOW + tqa)[None, :]
    ok = (dist >= 0) & (dist < A_WINDOW)
    bias_w = jnp.where(ok[None], _toeplitz(by_dist[:A_HEADS], tqa, A_WINDOW + tqa, A_WINDOW), NEG_INF)
    bias_w = bias_w.reshape(A_HEADS * tqa, A_WINDOW + tqa)
    tf = min(TQ_FLASH, s)
    n_off = min(s // tf, -(-(BUCKET_MAX_DIST + tf - 1) // tf) + 1)
    bias_s = jnp.stack([_toeplitz(by_dist[:A_HEADS], tf, tf, n * tf) for n in range(n_off)])
    m = jnp.arange(TQ_DIL)[:, None] + TQ_DIL - jnp.arange(2 * TQ_DIL)[None, :]
    bias_d = []
    for g, (window, dil) in enumerate(B_GROUPS):
        ok = (m >= 0) & (m <= window // dil)
        heads = by_dist[A_HEADS + g * B_HPG:A_HEADS + (g + 1) * B_HPG]
        bias_d.append(jnp.where(ok[None], _toeplitz(heads, TQ_DIL, 2 * TQ_DIL, TQ_DIL, dil), NEG_INF))
    return bias_w, bias_s, bias_d


def _selection_constants(s):
    nc = s // A_CMP_STRIDE
    n_cmp = (s - A_CMP_LEN) // A_CMP_STRIDE + 1
    n_sel = s // A_SEL_BLOCK
    c = np.arange(nc)[:, None]
    j = np.arange(n_sel)[None, :]
    c_start = c * A_CMP_STRIDE
    overlap = ((c_start <= j * A_SEL_BLOCK + A_SEL_BLOCK - 1) & (c_start + A_CMP_LEN - 1 >= j * A_SEL_BLOCK)
               & (c < n_cmp))
    expand = (np.arange(s)[None, :] // A_SEL_BLOCK) == np.arange(n_sel)[:, None]
    return jnp.asarray(overlap.T, BF16), jnp.asarray(expand, BF16)


def kernel(x, rel_bias_table, norm_attn_pre, norm_attn_post, norm_ffn_pre, norm_ffn_post, w_in, nsa_cmp_pos, nsa_phi_k_w1, nsa_phi_k_w2, nsa_phi_v_w1, nsa_phi_v_w2, fox_forget_bias, mla_q_norm, mla_kv_norm, mla_w_uq, mla_w_ukv, w_branch_a, w_branch_b, w_branch_c, w_branch_d, w_merge_gate, w_o, ffn_w_up, ffn_conv_w, ffn_conv_b, ffn_w_down):
    b, s, d = x.shape
    depth = w_in.shape[0]
    assert s % (TQ_DIL * B_GROUPS[-1][1]) == 0, "every dilation class needs whole 128-row tiles"

    in_src, in_scl = _in_proj_columns()
    (qa_i, qa_s), (qb_i, qb_s), (ka_i, ka_s), (va_i, va_s) = _mla_weight_columns()
    rope_tab = _rope_tables(s)
    bias_w, bias_s, bias_d = _bias_tables(rel_bias_table, s)
    overlap, expand = _selection_constants(s)
    half = A_CMP_STRIDE * HEAD_DIM

    for l in range(depth):
        w_ext = _gather_cols(w_in[l], in_src, in_scl).astype(BF16)
        fbias = jnp.zeros((1, LANES), F32).at[0, LANE_FORGET:LANE_FORGET + C_HEADS].set(fox_forget_bias[l])
        (aq, cmp, slc, win, bq, bkv, cq, ckv, dq, dk, dv, small) = _inproj(
            x, norm_attn_pre[l][None], w_ext, rope_tab, fbias,
            mla_q_norm[l][None], mla_kv_norm[l][None],
            _gather_cols(mla_w_uq[l], qa_i, qa_s).astype(BF16),
            _gather_cols(mla_w_uq[l], qb_i, qb_s).astype(BF16),
            _gather_cols(mla_w_ukv[l], ka_i, ka_s).astype(BF16),
            _gather_cols(mla_w_ukv[l], va_i, va_s).astype(BF16))

        nc = s // A_CMP_STRIDE
        w2k_pad = jnp.pad(nsa_phi_k_w2[l], ((0, 0), (0, LANES - HEAD_DIM))).astype(BF16)
        w2v_pad = jnp.pad(nsa_phi_v_w2[l], ((0, 0), (LANES - HEAD_DIM, 0))).astype(BF16)
        kvc = _compress(cmp[..., :HEAD_DIM].reshape(b, nc, half), cmp[..., HEAD_DIM:].reshape(b, nc, half),
                        nsa_cmp_pos[l].reshape(2, half),
                        nsa_phi_k_w1[l].astype(BF16), nsa_phi_v_w1[l].astype(BF16), w2k_pad, w2v_pad)
        kvw_pad = jnp.pad(win, ((0, 0), (A_WINDOW, 0), (0, 0)))
        part, sel = _nsa_cmp_win(aq, kvc, kvw_pad, small, overlap, bias_w)
        o_a = _flash("sel", aq, slc, sel=sel, expand=expand, bias=bias_s, part=part, small=small)

        obs, lses = [], []
        for g, (_, dil) in enumerate(B_GROUPS):
            cols = slice(g * B_HPG * LANES, (g + 1) * B_HPG * LANES)
            cls_shape = (b, s // dil, dil * B_HPG * LANES)
            o_g, lse_g = _dilated(bq[..., cols].reshape(cls_shape), bkv[..., cols].reshape(cls_shape),
                                  bias_d[g], dil)
            obs.append(o_g.reshape(b, s, B_HPG * LANES))
            lses.append(lse_g.reshape(b, s, B_HPG * LANES))

        cum_t = jnp.swapaxes(small[..., LANE_FORGET:LANE_FORGET + C_HEADS], 1, 2)
        o_c = _flash("fox", cq, ckv, small=small, cum_t=cum_t)
        o_d = _flash("mla", dq, dk, v=dv)

        x = _merge(x, o_a, obs, lses, o_c, o_d,
                   norm_attn_pre[l][None], norm_attn_post[l][None],
                   w_merge_gate[l].astype(BF16),
                   _pad_branch_rows(w_branch_a[l], A_HEADS, HEAD_DIM).astype(BF16),
                   _pad_branch_rows(w_branch_b[l], B_HPG, HEAD_DIM).astype(BF16),
                   _pad_branch_rows(w_branch_c[l], C_HEADS, HEAD_DIM).astype(BF16),
                   _pad_branch_rows(w_branch_d[l], D_HEADS, 0).astype(BF16),
                   w_o[l].astype(BF16))
        x = _ffn(x, norm_ffn_pre[l][None], norm_ffn_post[l][None], ffn_w_up[l].astype(BF16),
                 ffn_conv_w[l], ffn_conv_b[l][None], ffn_w_down[l].astype(BF16))
    return x
```

```python
import functools
import math

import numpy as np
import jax
import jax.numpy as jnp
from jax import lax
from jax.experimental import pallas as pl
from jax.experimental.pallas import tpu as pltpu

F32 = jnp.float32
BF16 = jnp.bfloat16

LANES = 128
HEAD_DIM = 64
RMS_EPS = 1e-6
NEG_INF = -1e30
FORCE_SCORE = 1e9
VMEM_LIMIT_BYTES = 56 * 1024 * 1024

N_BUCKETS = 32
BUCKET_EXACT = 16
BUCKET_MAX_DIST = 2048

A_HEADS = 4
A_CMP_LEN = 32
A_CMP_STRIDE = 16
A_PHI_HIDDEN = 128
A_SEL_BLOCK = 64
A_SEL_TOPK = 16
A_WINDOW = 512
B_GROUPS = ((128, 1), (512, 4), (2048, 16))
B_HPG = 2
B_HEADS = B_HPG * len(B_GROUPS)
C_HEADS = 4
D_HEADS = 4
D_Q_LORA = 256
D_KV_LORA = 128
D_NOPE = 64
D_ROPE = 32
D_VDIM = 64
ROPE_THETA = 10000.0
N_BRANCHES = 4

G_AQ = 0
G_CMP = G_AQ + A_HEADS
G_SLC = G_CMP + 1
G_WIN = G_SLC + 1
G_BQ = G_WIN + 1
G_BKV = G_BQ + B_HEADS
G_CQ = G_BKV + B_HEADS
G_CKV = G_CQ + C_HEADS
G_DQL = G_CKV + C_HEADS
G_DKVL = G_DQL + 2
G_S1 = G_DKVL + 1
G_S2 = G_S1 + 1
N_SLOTS = G_S2 + 1
LANE_GATE = 0
LANE_FORGET = 3 * A_HEADS
LANE_ROPE = 64

TM_PROJ = 256
TQ_FLASH = 512
TK_FLASH = 256
TQ_NSA = 128
TQ_DIL = 128
TQ_DIL_STEP = 512
CONV_HALO = 16


def _cparams(n_grid):
    return pltpu.CompilerParams(dimension_semantics=("arbitrary",) * n_grid,
                                vmem_limit_bytes=VMEM_LIMIT_BYTES)


def _const_spec(shape):
    nd = len(shape)
    return pl.BlockSpec(shape, lambda *_: (0,) * nd, pipeline_mode=pl.Buffered(1))


def _rms(x, gain):
    return x * lax.rsqrt(jnp.mean(x * x, axis=-1, keepdims=True) + RMS_EPS) * gain


def _sigmoid(x):
    return 1.0 / (1.0 + jnp.exp(-x))


def _dot(a, b):
    return jnp.dot(a, b, preferred_element_type=F32)


def _dot_nt(a, b):
    return lax.dot_general(a, b, (((1,), (1,)), ((), ())), preferred_element_type=F32)


def _slot(h):
    return slice(h * LANES, (h + 1) * LANES)


def _inproj_kernel(x_ref, g_ref, w_ref, tab_ref, fb_ref, qn_ref, kvn_ref, wqa_ref, wqb_ref,
                   wka_ref, wv_ref,
                   aq_ref, cmp_ref, slc_ref, slcv_ref, win_ref, bq_ref, bkv_ref, cq_ref, ckv_ref, cv_ref,
                   dq_ref, dk_ref, dv_ref, small_ref, carry_ref):
    i = pl.program_id(1)
    tm = x_ref.shape[1]
    h = _rms(x_ref[0], g_ref[...]).astype(BF16)

    def proj(g0, n):
        return _dot(h, w_ref[:, g0 * LANES:(g0 + n) * LANES])

    def key_half(n_slots):
        lane = lax.broadcasted_iota(jnp.int32, (tm, n_slots * LANES), 1)
        return (lane & (LANES - 1)) < HEAD_DIM

    aq_ref[0] = proj(G_AQ, A_HEADS).astype(BF16)
    cmp_ref[0] = proj(G_CMP, 1)
    slc = proj(G_SLC, 1)
    slc_ref[0] = slc.astype(BF16)
    slcv_ref[0] = jnp.where(key_half(1), 1.0, slc).astype(BF16)
    win_ref[0] = proj(G_WIN, 1).astype(BF16)
    bq_ref[0] = proj(G_BQ, B_HEADS).astype(BF16)
    bkv_ref[0] = proj(G_BKV, B_HEADS).astype(BF16)
    cq_ref[0] = proj(G_CQ, C_HEADS).astype(BF16)
    ckv = proj(G_CKV, C_HEADS)
    ckv_ref[0] = ckv.astype(BF16)
    cv_ref[0] = jnp.where(key_half(C_HEADS), 1.0, ckv).astype(BF16)

    s1 = proj(G_S1, 1)
    s2 = proj(G_S2, 1)

    @pl.when(i == 0)
    def _():
        carry_ref[...] = jnp.zeros_like(carry_ref)

    z = s1 + fb_ref[...]
    logf = jnp.minimum(z, 0.0) - jnp.log(1.0 + jnp.exp(-jnp.abs(z)))
    row = lax.broadcasted_iota(jnp.int32, (tm, tm), 0)
    col = lax.broadcasted_iota(jnp.int32, (tm, tm), 1)
    tri = (row >= col).astype(BF16)
    hi = logf.astype(BF16)
    r1 = logf - hi.astype(F32)
    mid = r1.astype(BF16)
    lo = (r1 - mid.astype(F32)).astype(BF16)
    cum = _dot(tri, hi) + _dot(tri, mid) + _dot(tri, lo) + carry_ref[0:1, :]
    carry_ref[0:1, :] = cum[tm - 1:tm, :]
    lane = lax.broadcasted_iota(jnp.int32, (tm, LANES), 1)
    is_forget = (lane >= LANE_FORGET) & (lane < LANE_FORGET + C_HEADS)
    small_ref[0] = jnp.where(is_forget, cum, s1)

    tab = tab_ref[...]
    c_q, s_q, c_k, s_k = (tab[:, _slot(n)] for n in range(4))
    qn = _rms(proj(G_DQL, 2), qn_ref[...]).astype(BF16)
    qa = _dot(qn, wqa_ref[...])
    qb = _dot(qn, wqb_ref[...])
    kvn = _rms(proj(G_DKVL, 1), kvn_ref[...]).astype(BF16)
    ka = _dot(kvn, wka_ref[...])
    k_rot = s1 * c_k + s2 * s_k
    for hd in range(D_HEADS):
        dq_ref[0, :, _slot(hd)] = (qa[:, _slot(hd)] * c_q + qb[:, _slot(hd)] * s_q).astype(BF16)
        dk_ref[0, :, _slot(hd)] = (ka[:, _slot(hd)] + k_rot).astype(BF16)
    dv_ref[0] = jnp.where(key_half(D_HEADS), _dot(kvn, wv_ref[...]), 1.0).astype(BF16)


def _inproj(x, gain, w_ext, tab, fbias, qn_g, kvn_g, wqa, wqb, wka, wv):
    b, s, d = x.shape
    tm = min(TM_PROJ, s)
    n_cols = w_ext.shape[1]

    def rows(width):
        return pl.BlockSpec((1, tm, width), lambda bi, i: (bi, i, 0))

    out_widths = [(A_HEADS, BF16), (1, F32), (1, BF16), (1, BF16), (1, BF16), (B_HEADS, BF16), (B_HEADS, BF16),
                  (C_HEADS, BF16), (C_HEADS, BF16), (C_HEADS, BF16),
                  (D_HEADS, BF16), (D_HEADS, BF16), (D_HEADS, BF16), (1, F32)]
    return pl.pallas_call(
        _inproj_kernel,
        grid=(b, s // tm),
        in_specs=[rows(d),
                  _const_spec((1, d)),
                  _const_spec((d, n_cols)),
                  pl.BlockSpec((tm, 4 * LANES), lambda bi, i: (i, 0)),
                  _const_spec((1, LANES)),
                  _const_spec((1, D_Q_LORA)),
                  _const_spec((1, D_KV_LORA)),
                  _const_spec(wqa.shape), _const_spec(wqb.shape),
                  _const_spec(wka.shape), _const_spec(wv.shape)],
        out_specs=[rows(n * LANES) for n, _ in out_widths],
        out_shape=[jax.ShapeDtypeStruct((b, s, n * LANES), dt) for n, dt in out_widths],
        scratch_shapes=[pltpu.VMEM((8, LANES), F32)],
        name="inproj",
        compiler_params=_cparams(2),
    )(x, gain, w_ext, tab, fbias, qn_g, kvn_g, wqa, wqb, wka, wv)


def _gelu_tanh(x):
    return 0.5 * x * (1.0 + jnp.tanh(math.sqrt(2.0 / math.pi) * (x + 0.044715 * (x * x * x))))


def _compress_kernel(kc_ref, vc_ref, pos_ref, w1k_ref, w1v_ref, w2k_ref, w2v_ref, out_ref):
    nc = kc_ref.shape[1]
    half = A_CMP_STRIDE * HEAD_DIM
    pos_a = pos_ref[0:1, :]
    pos_b = pos_ref[1:2, :]

    def phi(chunks, w1_ref, w2_ref):
        u = _dot((chunks + pos_a).astype(BF16), w1_ref[0:half, :])
        v = _dot((chunks + pos_b).astype(BF16), w1_ref[half:2 * half, :])
        hid = u + pltpu.roll(v, nc - 1, 0)
        return _dot(_gelu_tanh(hid).astype(BF16), w2_ref[...])

    out_ref[0] = (phi(kc_ref[0], w1k_ref, w2k_ref) + phi(vc_ref[0], w1v_ref, w2v_ref)).astype(BF16)


def _compress(kc_in, vc_in, pos2, w1k, w1v, w2k_pad, w2v_pad):
    b, nc, width = kc_in.shape
    blk = pl.BlockSpec((1, nc, width), lambda bi: (bi, 0, 0))
    return pl.pallas_call(
        _compress_kernel,
        grid=(b,),
        in_specs=[blk, blk, _const_spec(pos2.shape), _const_spec(w1k.shape), _const_spec(w1v.shape),
                  _const_spec(w2k_pad.shape), _const_spec(w2v_pad.shape)],
        out_specs=pl.BlockSpec((1, nc, LANES), lambda bi: (bi, 0, 0)),
        out_shape=jax.ShapeDtypeStruct((b, nc, LANES), BF16),
        name="compress",
        compiler_params=_cparams(1),
    )(kc_in, vc_in, pos2, w1k, w1v, w2k_pad, w2v_pad)


def _nsa_cmp_win_kernel(q_ref, kvc_ref, kvw_ref, small_ref, ovl_ref, bias_ref,
                        part_ref, sel_ref, *, n_top):
    i = pl.program_id(1)
    tq = q_ref.shape[1]
    nc = kvc_ref.shape[1]
    n_sel = ovl_ref.shape[0]
    t0 = pl.multiple_of(i * tq, tq)
    q4 = jnp.concatenate([q_ref[0, :, _slot(h)] for h in range(A_HEADS)], axis=0)
    rows = A_HEADS * tq

    kvc = kvc_ref[0]
    s = _dot_nt(q4, kvc)
    t_row = t0 + (lax.broadcasted_iota(jnp.int32, (rows, nc), 0) & (tq - 1))
    c_end = lax.broadcasted_iota(jnp.int32, (rows, nc), 1) * A_CMP_STRIDE + (A_CMP_LEN - 1)
    valid = c_end <= t_row
    s = jnp.where(valid, s, NEG_INF)
    m = jnp.max(s, axis=1, keepdims=True)
    p = jnp.where(valid, jnp.exp(s - m), 0.0)
    den = jnp.maximum(jnp.sum(p, axis=1, keepdims=True), 1e-30)
    p = p / den
    o_c = _dot(p.astype(BF16), kvc)

    p_sum = p[0:tq] + p[tq:2 * tq] + p[2 * tq:3 * tq] + p[3 * tq:4 * tq]
    p_hi = p_sum.astype(BF16)
    p_lo = (p_sum - p_hi.astype(F32)).astype(BF16)
    imp_t = _dot_nt(ovl_ref[...], p_hi) + _dot_nt(ovl_ref[...], p_lo)

    j_idx = lax.broadcasted_iota(jnp.int32, (n_sel, tq), 0)
    cur = lax.shift_right_logical(t0 + lax.broadcasted_iota(jnp.int32, (n_sel, tq), 1), 6)
    forced = (j_idx == 0) | (j_idx == cur) | (j_idx == cur - 1)
    imp_t = jnp.where(forced, FORCE_SCORE, jnp.where(j_idx > cur, -FORCE_SCORE, imp_t))
    rank = jnp.zeros((n_sel, tq), F32)
    for c in range(n_sel):
        row_c = imp_t[c:c + 1, :]
        beats = (row_c > imp_t) | ((row_c == imp_t) & (j_idx > c))
        rank = rank + jnp.where(beats, 1.0, 0.0)
    sel_t = jnp.where(rank < float(n_top), 1.0, 0.0).astype(BF16)
    eye = (lax.broadcasted_iota(jnp.int32, (tq, tq), 0)
           == lax.broadcasted_iota(jnp.int32, (tq, tq), 1)).astype(BF16)
    sel_ref[0] = _dot_nt(eye, sel_t).astype(BF16)

    span = A_WINDOW + tq
    kvw = kvw_ref[0, pl.ds(t0, span), :]
    sw = _dot_nt(q4, kvw) + bias_ref[...]
    kpos = t0 - A_WINDOW + lax.broadcasted_iota(jnp.int32, (rows, span), 1)
    sw = jnp.where(kpos >= 0, sw, NEG_INF)
    mw = jnp.max(sw, axis=1, keepdims=True)
    pw = jnp.exp(sw - mw)
    o_w = _dot(pw.astype(BF16), kvw) / jnp.sum(pw, axis=1, keepdims=True)

    g = _sigmoid(small_ref[0])
    for h in range(A_HEADS):
        g_c = g[:, LANE_GATE + 3 * h:LANE_GATE + 3 * h + 1]
        g_w = g[:, LANE_GATE + 3 * h + 2:LANE_GATE + 3 * h + 3]
        part_ref[0, :, _slot(h)] = g_c * o_c[h * tq:(h + 1) * tq] + g_w * o_w[h * tq:(h + 1) * tq]


def _nsa_cmp_win(aq, kvc, kvw_pad, small, ovl, bias_w):
    b, s, _ = aq.shape
    tq = min(TQ_NSA, s)
    nc = kvc.shape[1]
    n_sel = ovl.shape[0]
    n_top = min(A_SEL_TOPK, n_sel)
    return pl.pallas_call(
        functools.partial(_nsa_cmp_win_kernel, n_top=n_top),
        grid=(b, s // tq),
        in_specs=[pl.BlockSpec((1, tq, A_HEADS * LANES), lambda bi, i: (bi, i, 0)),
                  pl.BlockSpec((1, nc, LANES), lambda bi, i: (bi, 0, 0)),
                  pl.BlockSpec((1, s + A_WINDOW, LANES), lambda bi, i: (bi, 0, 0)),
                  pl.BlockSpec((1, tq, LANES), lambda bi, i: (bi, i, 0)),
                  _const_spec(ovl.shape),
                  _const_spec(bias_w.shape)],
        out_specs=[pl.BlockSpec((1, tq, A_HEADS * LANES), lambda bi, i: (bi, i, 0)),
                   pl.BlockSpec((1, tq, n_sel), lambda bi, i: (bi, i, 0))],
        out_shape=[jax.ShapeDtypeStruct((b, s, A_HEADS * LANES), F32),
                   jax.ShapeDtypeStruct((b, s, n_sel), BF16)],
        name="nsa_cmp_win",
        compiler_params=_cparams(2),
    )(aq, kvc, kvw_pad, small, ovl, bias_w)


def _flash_kernel(*refs, mode, nh, ones_lane):
    if mode == "sel":
        q_ref, k_ref, v_ref, sel_ref, e_ref, bias_ref, part_ref, small_ref, o_ref, m_ref, acc_ref = refs
    elif mode == "fox":
        q_ref, k_ref, v_ref, cumt_ref, o_ref, m_ref, acc_ref = refs
    else:
        q_ref, k_ref, v_ref, o_ref, m_ref, acc_ref = refs
    i = pl.program_id(1)
    tq = q_ref.shape[1]
    tk = min(TK_FLASH, tq)
    sub = tq // tk
    shared_kv = mode == "sel"

    m_ref[...] = jnp.full(m_ref.shape, NEG_INF, F32)
    acc_ref[...] = jnp.zeros(acc_ref.shape, F32)

    lag = lax.broadcasted_iota(jnp.int32, (tq, tk), 0) - lax.broadcasted_iota(jnp.int32, (tq, tk), 1)

    def step(j, diag):
        k0 = pl.multiple_of(j * tk, tk)
        keep = (lag >= (j - i * sub) * tk) if diag else None
        if mode == "sel":
            sel_keys = _dot(sel_ref[0], e_ref[:, pl.ds(k0, tk)]) > 0.5
            keep = (sel_keys & keep) if diag else sel_keys
            n_off = bias_ref.shape[0]
            offs = [jnp.clip(i * sub - j + a, 0, n_off - 1) for a in range(sub)]
        logits = []
        for h in range(nh):
            kv_cols = _slot(0) if shared_kv else _slot(h)
            s = _dot_nt(q_ref[0, :, _slot(h)], k_ref[0, pl.ds(k0, tk), kv_cols])
            if mode == "sel":
                s = s + jnp.concatenate([bias_ref[o, h] for o in offs], axis=0)
            if mode == "fox":
                s = s - cumt_ref[0, h:h + 1, pl.ds(k0, tk)]
            if keep is not None:
                s = jnp.where(keep, s, NEG_INF)
            logits.append(s)
        probs = []
        for h in range(nh):
            m_prev = m_ref[h]
            m_new = jnp.maximum(m_prev, jnp.max(logits[h], axis=1, keepdims=True))
            p = jnp.exp(logits[h] - jnp.concatenate([m_new] * (tk // LANES), axis=1))
            alpha = jnp.exp(m_prev - m_new)
            m_ref[h] = m_new
            probs.append((p.astype(BF16), alpha))
        for h in range(nh):
            kv_cols = _slot(0) if shared_kv else _slot(h)
            p, alpha = probs[h]
            acc_ref[h] = alpha * acc_ref[h] + _dot(p, v_ref[0, pl.ds(k0, tk), kv_cols])

    def full_tile(j, carry):
        step(j, False)
        return carry

    lax.fori_loop(0, i * sub, full_tile, 0)
    for a in range(sub):
        step(i * sub + a, True)

    if mode == "sel":
        g = _sigmoid(small_ref[0])
    for h in range(nh):
        acc = acc_ref[h]
        o_h = acc / acc[:, ones_lane:ones_lane + 1]
        if mode == "sel":
            g_s = g[:, LANE_GATE + 3 * h + 1:LANE_GATE + 3 * h + 2]
            o_h = part_ref[0, :, _slot(h)] + g_s * o_h
        o_ref[0, :, _slot(h)] = o_h.astype(BF16)


def _flash(mode, q, k, v, sel=None, expand=None, bias=None, part=None, small=None, cum_t=None):
    b, s, qw = q.shape
    nh = qw // LANES
    tq = min(TQ_FLASH, s)
    q_spec = pl.BlockSpec((1, tq, qw), lambda bi, i: (bi, i, 0))
    small_spec = pl.BlockSpec((1, tq, LANES), lambda bi, i: (bi, i, 0))

    def seq_spec(arr):
        return pl.BlockSpec((1,) + arr.shape[1:], lambda bi, i: (bi, 0, 0))

    if mode == "sel":
        args = (q, k, v, sel, expand, bias, part, small)
        in_specs = [q_spec, seq_spec(k), seq_spec(v),
                    pl.BlockSpec((1, tq, sel.shape[2]), lambda bi, i: (bi, i, 0)),
                    _const_spec(expand.shape), _const_spec(bias.shape), q_spec, small_spec]
    elif mode == "fox":
        args = (q, k, v, cum_t)
        in_specs = [q_spec, seq_spec(k), seq_spec(v), seq_spec(cum_t)]
    else:
        args = (q, k, v)
        in_specs = [q_spec, seq_spec(k), seq_spec(v)]
    ones_lane = HEAD_DIM if mode == "mla" else 0
    return pl.pallas_call(
        functools.partial(_flash_kernel, mode=mode, nh=nh, ones_lane=ones_lane),
        grid=(b, s // tq),
        in_specs=in_specs,
        out_specs=q_spec,
        out_shape=jax.ShapeDtypeStruct((b, s, qw), BF16),
        scratch_shapes=[pltpu.VMEM((nh, tq, LANES), F32)] * 2,
        name="flash_" + mode,
        compiler_params=_cparams(2),
    )(*args)


def _dilated_kernel(q_ref, kvp_ref, kvc_ref, bias_ref, o_ref, lse_ref):
    i = pl.program_id(2)
    tb = kvp_ref.shape[1]
    n_sub = q_ref.shape[1] // tb
    col = lax.broadcasted_iota(jnp.int32, (tb, 2 * tb), 1)
    in_seq = (col >= tb) | (i > 0)
    for t in range(n_sub):
        rows = slice(t * tb, (t + 1) * tb)
        for h in range(B_HPG):
            prev = kvp_ref[0, :, _slot(h)] if t == 0 else kvc_ref[0, (t - 1) * tb:t * tb, _slot(h)]
            kv = jnp.concatenate([prev, kvc_ref[0, rows, _slot(h)]], axis=0)
            s = _dot_nt(q_ref[0, rows, _slot(h)], kv) + bias_ref[h]
            if t == 0:
                s = jnp.where(in_seq, s, NEG_INF)
            m = jnp.max(s, axis=1, keepdims=True)
            p = jnp.exp(s - m)
            den = jnp.sum(p, axis=1, keepdims=True)
            o_ref[0, rows, _slot(h)] = (_dot(p.astype(BF16), kv) / den).astype(BF16)
            lse_ref[0, rows, _slot(h)] = jnp.broadcast_to(m + jnp.log(den), (tb, LANES))


def _dilated(q_cls, kv_cls, bias, dil):
    b, n, _ = q_cls.shape
    tq = min(TQ_DIL_STEP, n)
    n_sub = tq // TQ_DIL
    width = B_HPG * LANES
    cur = pl.BlockSpec((1, tq, width), lambda bi, r, i: (bi, i, r))
    prev = pl.BlockSpec((1, TQ_DIL, width), lambda bi, r, i: (bi, jnp.maximum(i * n_sub - 1, 0), r))
    return pl.pallas_call(
        _dilated_kernel,
        grid=(b, dil, n // tq),
        in_specs=[cur, prev, cur, _const_spec(bias.shape)],
        out_specs=[cur, cur],
        out_shape=[jax.ShapeDtypeStruct(q_cls.shape, BF16), jax.ShapeDtypeStruct(q_cls.shape, F32)],
        name="dilated",
        compiler_params=_cparams(3),
    )(q_cls, kv_cls, kv_cls, bias)


def _merge_kernel(x_ref, oa_ref, ob0_ref, ob1_ref, ob2_ref, l0_ref, l1_ref, l2_ref, oc_ref, od_ref,
                  gpre_ref, gpost_ref, wg_ref, wba_ref, wbb_ref, wbc_ref, wbd_ref, wo_ref, out_ref):
    x = x_ref[0]
    h = _rms(x, gpre_ref[...]).astype(BF16)
    l0, l1, l2 = l0_ref[0], l1_ref[0], l2_ref[0]
    mx = jnp.maximum(jnp.maximum(l0, l1), l2)
    e0, e1, e2 = jnp.exp(l0 - mx), jnp.exp(l1 - mx), jnp.exp(l2 - mx)
    ob = (e0 * ob0_ref[0].astype(F32) + e1 * ob1_ref[0].astype(F32)
          + e2 * ob2_ref[0].astype(F32)) / (e0 + e1 + e2)
    branches = ((oa_ref[0], wba_ref), (ob.astype(BF16), wbb_ref), (oc_ref[0], wbc_ref), (od_ref[0], wbd_ref))
    merged = None
    for n, (o_n, wb_ref) in enumerate(branches):
        term = _sigmoid(_dot(h, wg_ref[n])) * _dot(o_n, wb_ref[...])
        merged = term if merged is None else merged + term
    z = _dot(merged.astype(BF16), wo_ref[...])
    out_ref[0] = x + _rms(z, gpost_ref[...])


def _merge(x, oa, obs, lses, oc, od, gpre, gpost, wg, wba, wbb, wbc, wbd, wo):
    b, s, d = x.shape
    tm = min(TM_PROJ, s)

    def rows(arr):
        return pl.BlockSpec((1, tm, arr.shape[2]), lambda bi, i: (bi, i, 0))

    acts = (x, oa, *obs, *lses, oc, od)
    consts = (gpre, gpost, wg, wba, wbb, wbc, wbd, wo)
    return pl.pallas_call(
        _merge_kernel,
        grid=(b, s // tm),
        in_specs=[rows(a) for a in acts] + [_const_spec(c.shape) for c in consts],
        out_specs=rows(x),
        out_shape=jax.ShapeDtypeStruct(x.shape, F32),
        name="merge",
        compiler_params=_cparams(2),
    )(*acts, *consts)


def _ffn_kernel(x_ref, xp_ref, gpre_ref, gpost_ref, wup_ref, cw_ref, cb_ref, wdn_ref, out_ref, *, n_chunks):
    i = pl.program_id(1)
    tm = x_ref.shape[1]
    d_ff = wdn_ref.shape[0]
    chunk = d_ff // n_chunks
    x = x_ref[0]
    halo = jnp.where(i > 0, xp_ref[0], 0.0)
    he = _rms(jnp.concatenate([halo, x], axis=0), gpre_ref[...]).astype(BF16)

    def conv(c0):
        u = _dot(he, wup_ref[:, c0:c0 + chunk])
        w = cw_ref[:, c0:c0 + chunk]
        out = cb_ref[:, c0:c0 + chunk] + w[0:1] * pltpu.roll(u, 2, 0)[CONV_HALO:]
        out = out + w[1:2] * pltpu.roll(u, 1, 0)[CONV_HALO:]
        return out + w[2:3] * u[CONV_HALO:]

    y = jnp.zeros((tm, x.shape[1]), F32)
    for c in range(n_chunks):
        gate = conv(c * chunk)
        val = conv(d_ff + c * chunk)
        act = gate * _sigmoid(gate) * val
        y = y + _dot(act.astype(BF16), wdn_ref[c * chunk:(c + 1) * chunk, :])
    out_ref[0] = x + _rms(y, gpost_ref[...])


def _ffn(x, gpre, gpost, w_up, conv_w, conv_b, w_down):
    b, s, d = x.shape
    tm = min(TM_PROJ, s)
    d_ff = w_down.shape[0]
    n_chunks = 2 if d_ff % (2 * LANES) == 0 else 1
    halo_blocks = tm // CONV_HALO
    return pl.pallas_call(
        functools.partial(_ffn_kernel, n_chunks=n_chunks),
        grid=(b, s // tm),
        in_specs=[pl.BlockSpec((1, tm, d), lambda bi, i: (bi, i, 0)),
                  pl.BlockSpec((1, CONV_HALO, d), lambda bi, i: (bi, jnp.maximum(i * halo_blocks - 1, 0), 0)),
                  _const_spec(gpre.shape), _const_spec(gpost.shape), _const_spec(w_up.shape),
                  _const_spec(conv_w.shape), _const_spec(conv_b.shape), _const_spec(w_down.shape)],
        out_specs=pl.BlockSpec((1, tm, d), lambda bi, i: (bi, i, 0)),
        out_shape=jax.ShapeDtypeStruct(x.shape, F32),
        name="ffn",
        compiler_params=_cparams(2),
    )(x, x, gpre, gpost, w_up, conv_w, conv_b, w_down)


def _rel_bucket(dist):
    dist = jnp.maximum(dist, 0)
    d = jnp.maximum(dist, 1).astype(F32)
    large = BUCKET_EXACT + (jnp.log(d / BUCKET_EXACT) / math.log(BUCKET_MAX_DIST / BUCKET_EXACT)
                            * (N_BUCKETS - BUCKET_EXACT)).astype(jnp.int32)
    large = jnp.minimum(large, N_BUCKETS - 1)
    return jnp.where(dist < BUCKET_EXACT, dist, large)


def _in_proj_columns():
    src = np.full((N_SLOTS * LANES,), -1, np.int64)
    scl = np.ones((N_SLOTS * LANES,), np.float32)
    o_aq = 0
    o_akv = o_aq + A_HEADS * HEAD_DIM
    o_ag = o_akv + 6 * HEAD_DIM
    o_b = o_ag + 3 * A_HEADS
    o_c = o_b + 3 * B_HEADS * HEAD_DIM
    o_cf = o_c + 3 * C_HEADS * HEAD_DIM
    o_dq = o_cf + C_HEADS
    o_dkv = o_dq + D_Q_LORA
    o_dkr = o_dkv + D_KV_LORA
    e = np.arange(HEAD_DIM)
    q_scale = HEAD_DIM ** -0.5

    def put(slot, lane0, cols, scale=1.0):
        dst = slot * LANES + lane0 + np.arange(len(cols))
        src[dst] = cols
        scl[dst] = scale

    for h in range(A_HEADS):
        put(G_AQ + h, 0, o_aq + h * HEAD_DIM + e, q_scale)
    for n, slot in enumerate((G_CMP, G_SLC, G_WIN)):
        put(slot, 0, o_akv + (2 * n) * HEAD_DIM + e)
        put(slot, HEAD_DIM, o_akv + (2 * n + 1) * HEAD_DIM + e)
    for h in range(B_HEADS):
        put(G_BQ + h, 0, o_b + h * HEAD_DIM + e, q_scale)
        put(G_BKV + h, 0, o_b + (B_HEADS + h) * HEAD_DIM + e)
        put(G_BKV + h, HEAD_DIM, o_b + (2 * B_HEADS + h) * HEAD_DIM + e)
    for h in range(C_HEADS):
        put(G_CQ + h, 0, o_c + h * HEAD_DIM + e, q_scale)
        put(G_CKV + h, 0, o_c + (C_HEADS + h) * HEAD_DIM + e)
        put(G_CKV + h, HEAD_DIM, o_c + (2 * C_HEADS + h) * HEAD_DIM + e)
    put(G_DQL, 0, o_dq + np.arange(D_Q_LORA))
    put(G_DKVL, 0, o_dkv + np.arange(D_KV_LORA))
    put(G_S1, LANE_GATE, o_ag + np.arange(3 * A_HEADS))
    put(G_S1, LANE_FORGET, o_cf + np.arange(C_HEADS))
    put(G_S1, LANE_ROPE, o_dkr + np.arange(D_ROPE))
    half = D_ROPE // 2
    put(G_S2, LANE_ROPE, o_dkr + half + np.arange(half), -1.0)
    put(G_S2, LANE_ROPE + half, o_dkr + np.arange(half))
    return src, scl


def _gather_cols(w, src, scl):
    pieces = []
    n = len(src)
    a = 0
    while a < n:
        e = a + 1
        if src[a] < 0:
            while e < n and src[e] < 0:
                e += 1
            pieces.append(jnp.zeros((w.shape[0], e - a), w.dtype))
        else:
            while e < n and src[e] == src[e - 1] + 1 and scl[e] == scl[a]:
                e += 1
            run = w[:, int(src[a]):int(src[a]) + (e - a)]
            pieces.append(run if scl[a] == 1.0 else run * float(scl[a]))
        a = e
    return jnp.concatenate(pieces, axis=1)


def _mla_weight_columns():
    per_q = D_NOPE + D_ROPE
    half = D_ROPE // 2
    qa = np.full((D_HEADS * LANES,), -1, np.int64)
    qb = np.full((D_HEADS * LANES,), -1, np.int64)
    qb_s = np.ones((D_HEADS * LANES,), np.float32)
    ka = np.full((D_HEADS * LANES,), -1, np.int64)
    va = np.full((D_HEADS * LANES,), -1, np.int64)
    for h in range(D_HEADS):
        qa[h * LANES + np.arange(per_q)] = h * per_q + np.arange(per_q)
        rot = h * per_q + D_NOPE
        qb[h * LANES + D_NOPE + np.arange(half)] = rot + half + np.arange(half)
        qb_s[h * LANES + D_NOPE + np.arange(half)] = -1.0
        qb[h * LANES + D_NOPE + half + np.arange(half)] = rot + np.arange(half)
        ka[h * LANES + np.arange(D_NOPE)] = h * (D_NOPE + D_VDIM) + np.arange(D_NOPE)
        va[h * LANES + np.arange(D_VDIM)] = h * (D_NOPE + D_VDIM) + D_NOPE + np.arange(D_VDIM)
    ones = np.ones_like(qb_s)
    return (qa, ones), (qb, qb_s), (ka, ones), (va, ones)


def _pad_branch_rows(w, n_heads, lane0):
    d = w.shape[1]
    out = jnp.zeros((n_heads, LANES, d), w.dtype)
    out = out.at[:, lane0:lane0 + HEAD_DIM, :].set(w.reshape(n_heads, HEAD_DIM, d))
    return out.reshape(n_heads * LANES, d)


def _rope_tables(s):
    inv_freq = ROPE_THETA ** (-jnp.arange(0, D_ROPE, 2, dtype=F32) / D_ROPE)
    ang = jnp.arange(s, dtype=F32)[:, None] * inv_freq[None, :]
    cos2 = jnp.concatenate([jnp.cos(ang)] * 2, axis=1)
    sin2 = jnp.concatenate([jnp.sin(ang)] * 2, axis=1)
    scale = (D_NOPE + D_ROPE) ** -0.5
    z_lo = jnp.zeros((s, D_NOPE), F32)
    z_hi = jnp.zeros((s, LANES - D_NOPE - D_ROPE), F32)
    c_q = jnp.concatenate([jnp.full((s, D_NOPE), scale, F32), scale * cos2, z_hi], axis=1)
    s_q = jnp.concatenate([z_lo, scale * sin2, z_hi], axis=1)
    c_k = jnp.concatenate([z_lo, cos2, z_hi], axis=1)
    s_k = jnp.concatenate([z_lo, sin2, z_hi], axis=1)
    return jnp.concatenate([c_q, s_q, c_k, s_k], axis=1)


def _toeplitz(vec, n_rows, n_cols, off, step=1):
    base = vec[:, ::step]
    last = base.shape[1] - 1
    lo, hi = off - (n_cols - 1), off + n_rows - 1
    core = base[:, max(lo, 0):min(hi, last) + 1]
    front = jnp.repeat(base[:, :1], max(0, -lo), axis=1)
    back = jnp.repeat(base[:, last:], max(0, hi - last), axis=1)
    g = jnp.concatenate([front, core, back], axis=1)
    lg = n_rows + n_cols - 1
    t = jnp.tile(g, (1, n_rows + 1))[:, :n_rows * (lg + 1)].reshape(-1, n_rows, lg + 1)
    return t[:, :, :n_cols][:, :, ::-1]


def _bias_tables(rel_table, s):
    bucket = _rel_bucket(jnp.arange(s))
    hit = bucket[None, :, None] == jnp.arange(N_BUCKETS)[None, None, :]
    by_dist = jnp.sum(jnp.where(hit, rel_table.T[:, None, :], 0.0), axis=-1)
    tqa = min(TQ_NSA, s)
    dist = jnp.arange(tqa)[:, None] + A_WINDOW - jnp.arange(A_WINDOW + tqa)[None, :]
    ok = (dist >= 0) & (dist < A_WINDOW)
    bias_w = jnp.where(ok[None], _toeplitz(by_dist[:A_HEADS], tqa, A_WINDOW + tqa, A_WINDOW), NEG_INF)
    bias_w = bias_w.reshape(A_HEADS * tqa, A_WINDOW + tqa)
    tf = min(TK_FLASH, s)
    n_off = min(s // tf, -(-(BUCKET_MAX_DIST + tf - 1) // tf) + 1)
    bias_s = jnp.stack([_toeplitz(by_dist[:A_HEADS], tf, tf, n * tf) for n in range(n_off)])
    m = jnp.arange(TQ_DIL)[:, None] + TQ_DIL - jnp.arange(2 * TQ_DIL)[None, :]
    bias_d = []
    for g, (window, dil) in enumerate(B_GROUPS):
        ok = (m >= 0) & (m <= window // dil)
        heads = by_dist[A_HEADS + g * B_HPG:A_HEADS + (g + 1) * B_HPG]
        bias_d.append(jnp.where(ok[None], _toeplitz(heads, TQ_DIL, 2 * TQ_DIL, TQ_DIL, dil), NEG_INF))
    return bias_w, bias_s, bias_d


def _selection_constants(s):
    nc = s // A_CMP_STRIDE
    n_cmp = (s - A_CMP_LEN) // A_CMP_STRIDE + 1
    n_sel = s // A_SEL_BLOCK
    c = np.arange(nc)[:, None]
    j = np.arange(n_sel)[None, :]
    c_start = c * A_CMP_STRIDE
    overlap = ((c_start <= j * A_SEL_BLOCK + A_SEL_BLOCK - 1) & (c_start + A_CMP_LEN - 1 >= j * A_SEL_BLOCK)
               & (c < n_cmp))
    expand = (np.arange(s)[None, :] // A_SEL_BLOCK) == np.arange(n_sel)[:, None]
    return jnp.asarray(overlap.T, BF16), jnp.asarray(expand, BF16)


def kernel(x, rel_bias_table, norm_attn_pre, norm_attn_post, norm_ffn_pre, norm_ffn_post, w_in, nsa_cmp_pos, nsa_phi_k_w1, nsa_phi_k_w2, nsa_phi_v_w1, nsa_phi_v_w2, fox_forget_bias, mla_q_norm, mla_kv_norm, mla_w_uq, mla_w_ukv, w_branch_a, w_branch_b, w_branch_c, w_branch_d, w_merge_gate, w_o, ffn_w_up, ffn_conv_w, ffn_conv_b, ffn_w_down):
    b, s, d = x.shape
    depth = w_in.shape[0]
    assert s % (TQ_DIL * B_GROUPS[-1][1]) == 0, "every dilation class needs whole 128-row tiles"

    in_src, in_scl = _in_proj_columns()
    (qa_i, qa_s), (qb_i, qb_s), (ka_i, ka_s), (va_i, va_s) = _mla_weight_columns()
    rope_tab = _rope_tables(s)
    bias_w, bias_s, bias_d = _bias_tables(rel_bias_table, s)
    overlap, expand = _selection_constants(s)
    half = A_CMP_STRIDE * HEAD_DIM

    for l in range(depth):
        w_ext = _gather_cols(w_in[l], in_src, in_scl).astype(BF16)
        fbias = jnp.zeros((1, LANES), F32).at[0, LANE_FORGET:LANE_FORGET + C_HEADS].set(fox_forget_bias[l])
        (aq, cmp, slc, slc_v, win, bq, bkv, cq, ckv, c_v, dq, dk, dv, small) = _inproj(
            x, norm_attn_pre[l][None], w_ext, rope_tab, fbias,
            mla_q_norm[l][None], mla_kv_norm[l][None],
            _gather_cols(mla_w_uq[l], qa_i, qa_s).astype(BF16),
            _gather_cols(mla_w_uq[l], qb_i, qb_s).astype(BF16),
            _gather_cols(mla_w_ukv[l], ka_i, ka_s).astype(BF16),
            _gather_cols(mla_w_ukv[l], va_i, va_s).astype(BF16))

        nc = s // A_CMP_STRIDE
        w2k_pad = jnp.pad(nsa_phi_k_w2[l], ((0, 0), (0, LANES - HEAD_DIM))).astype(BF16)
        w2v_pad = jnp.pad(nsa_phi_v_w2[l], ((0, 0), (LANES - HEAD_DIM, 0))).astype(BF16)
        kvc = _compress(cmp[..., :HEAD_DIM].reshape(b, nc, half), cmp[..., HEAD_DIM:].reshape(b, nc, half),
                        nsa_cmp_pos[l].reshape(2, half),
                        nsa_phi_k_w1[l].astype(BF16), nsa_phi_v_w1[l].astype(BF16), w2k_pad, w2v_pad)
        kvw_pad = jnp.pad(win, ((0, 0), (A_WINDOW, 0), (0, 0)))
        part, sel = _nsa_cmp_win(aq, kvc, kvw_pad, small, overlap, bias_w)
        o_a = _flash("sel", aq, slc, slc_v, sel=sel, expand=expand, bias=bias_s, part=part, small=small)

        obs, lses = [], []
        for g, (_, dil) in enumerate(B_GROUPS):
            cols = slice(g * B_HPG * LANES, (g + 1) * B_HPG * LANES)
            cls_shape = (b, s // dil, dil * B_HPG * LANES)
            o_g, lse_g = _dilated(bq[..., cols].reshape(cls_shape), bkv[..., cols].reshape(cls_shape),
                                  bias_d[g], dil)
            obs.append(o_g.reshape(b, s, B_HPG * LANES))
            lses.append(lse_g.reshape(b, s, B_HPG * LANES))

        cum_t = jnp.swapaxes(small[..., LANE_FORGET:LANE_FORGET + C_HEADS], 1, 2)
        o_c = _flash("fox", cq, ckv, c_v, cum_t=cum_t)
        o_d = _flash("mla", dq, dk, dv)

        x = _merge(x, o_a, obs, lses, o_c, o_d,
                   norm_attn_pre[l][None], norm_attn_post[l][None],
                   w_merge_gate[l].astype(BF16),
                   _pad_branch_rows(w_branch_a[l], A_HEADS, HEAD_DIM).astype(BF16),
                   _pad_branch_rows(w_branch_b[l], B_HPG, HEAD_DIM).astype(BF16),
                   _pad_branch_rows(w_branch_c[l], C_HEADS, HEAD_DIM).astype(BF16),
                   _pad_branch_rows(w_branch_d[l], D_HEADS, 0).astype(BF16),
                   w_o[l].astype(BF16))
        x = _ffn(x, norm_ffn_pre[l][None], norm_ffn_post[l][None], ffn_w_up[l].astype(BF16),
                 ffn_conv_w[l], ffn_conv_b[l][None], ffn_w_down[l].astype(BF16))
    return x
```

```python
import functools
import math

import numpy as np
import jax
import jax.numpy as jnp
from jax import lax
from jax.experimental import pallas as pl
from jax.experimental.pallas import tpu as pltpu

F32 = jnp.float32
BF16 = jnp.bfloat16

LANES = 128
HEAD_DIM = 64
RMS_EPS = 1e-6
NEG_INF = -1e30
FORCE_SCORE = 1e9
LOG2E = math.log2(math.e)
VMEM_LIMIT_BYTES = 56 * 1024 * 1024

N_BUCKETS = 32
BUCKET_EXACT = 16
BUCKET_MAX_DIST = 2048

A_HEADS = 4
A_CMP_LEN = 32
A_CMP_STRIDE = 16
A_PHI_HIDDEN = 128
A_SEL_BLOCK = 64
A_SEL_TOPK = 16
A_WINDOW = 512
B_GROUPS = ((128, 1), (512, 4), (2048, 16))
B_HPG = 2
B_HEADS = B_HPG * len(B_GROUPS)
C_HEADS = 4
D_HEADS = 4
D_Q_LORA = 256
D_KV_LORA = 128
D_NOPE = 64
D_ROPE = 32
D_VDIM = 64
ROPE_THETA = 10000.0
N_BRANCHES = 4

G_AQ = 0
G_CMP = G_AQ + A_HEADS
G_SLC = G_CMP + 1
G_WIN = G_SLC + 1
G_BQ = G_WIN + 1
G_BKV = G_BQ + B_HEADS
G_CQ = G_BKV + B_HEADS
G_CKV = G_CQ + C_HEADS
G_DQL = G_CKV + C_HEADS
G_DKVL = G_DQL + 2
G_S1 = G_DKVL + 1
G_S2 = G_S1 + 1
N_SLOTS = G_S2 + 1
LANE_GATE = 0
LANE_FORGET = 3 * A_HEADS
LANE_ROPE = 64

TM_PROJ = 512
TQ_NSA_STEP = 256
TQ_FLASH = 512
TK_FLASH = 512
BIAS_TILE = 256
TQ_NSA = 128
TQ_DIL = 128
TQ_DIL_STEP = 512
CONV_HALO = 16


def _cparams(n_grid):
    return pltpu.CompilerParams(dimension_semantics=("arbitrary",) * n_grid,
                                vmem_limit_bytes=VMEM_LIMIT_BYTES)


def _const_spec(shape):
    nd = len(shape)
    return pl.BlockSpec(shape, lambda *_: (0,) * nd, pipeline_mode=pl.Buffered(1))


def _rms(x, gain):
    return x * lax.rsqrt(jnp.mean(x * x, axis=-1, keepdims=True) + RMS_EPS) * gain


def _sigmoid(x):
    return 1.0 / (1.0 + jnp.exp(-x))


def _dot(a, b):
    return jnp.dot(a, b, preferred_element_type=F32)


def _dot_nt(a, b):
    return lax.dot_general(a, b, (((1,), (1,)), ((), ())), preferred_element_type=F32)


def _slot(h):
    return slice(h * LANES, (h + 1) * LANES)


def _inproj_kernel(x_ref, g_ref, w_ref, tab_ref, fb_ref, qn_ref, kvn_ref, wqa_ref, wqb_ref,
                   wka_ref, wv_ref,
                   aq_ref, cmp_ref, slc_ref, slcv_ref, win_ref, bq_ref, bkv_ref, cq_ref, ckv_ref, cv_ref,
                   dq_ref, dk_ref, dv_ref, small_ref, carry_ref):
    i = pl.program_id(1)
    tm = x_ref.shape[1]
    h = _rms(x_ref[0], g_ref[...]).astype(BF16)

    def proj(g0, n):
        return _dot(h, w_ref[:, g0 * LANES:(g0 + n) * LANES])

    def key_half(n_slots):
        lane = lax.broadcasted_iota(jnp.int32, (tm, n_slots * LANES), 1)
        return (lane & (LANES - 1)) < HEAD_DIM

    aq_ref[0] = proj(G_AQ, A_HEADS).astype(BF16)
    cmp_ref[0] = proj(G_CMP, 1)
    slc = proj(G_SLC, 1)
    slc_ref[0] = slc.astype(BF16)
    slcv_ref[0] = jnp.where(key_half(1), 1.0, slc).astype(BF16)
    win_ref[0] = proj(G_WIN, 1).astype(BF16)
    bq_ref[0] = proj(G_BQ, B_HEADS).astype(BF16)
    bkv_ref[0] = proj(G_BKV, B_HEADS).astype(BF16)
    cq_ref[0] = proj(G_CQ, C_HEADS).astype(BF16)
    ckv = proj(G_CKV, C_HEADS)
    ckv_ref[0] = ckv.astype(BF16)
    cv_ref[0] = jnp.where(key_half(C_HEADS), 1.0, ckv).astype(BF16)

    s1 = proj(G_S1, 1)
    s2 = proj(G_S2, 1)

    @pl.when(i == 0)
    def _():
        carry_ref[...] = jnp.zeros_like(carry_ref)

    z = s1 + fb_ref[...]
    logf = jnp.minimum(z, 0.0) - jnp.log(1.0 + jnp.exp(-jnp.abs(z)))
    row = lax.broadcasted_iota(jnp.int32, (tm, tm), 0)
    col = lax.broadcasted_iota(jnp.int32, (tm, tm), 1)
    tri = (row >= col).astype(BF16)
    hi = logf.astype(BF16)
    r1 = logf - hi.astype(F32)
    mid = r1.astype(BF16)
    lo = (r1 - mid.astype(F32)).astype(BF16)
    cum = _dot(tri, hi) + _dot(tri, mid) + _dot(tri, lo) + carry_ref[0:1, :]
    carry_ref[0:1, :] = cum[tm - 1:tm, :]
    lane = lax.broadcasted_iota(jnp.int32, (tm, LANES), 1)
    is_forget = (lane >= LANE_FORGET) & (lane < LANE_FORGET + C_HEADS)
    small_ref[0] = jnp.where(is_forget, cum, s1)

    tab = tab_ref[...]
    c_q, s_q, c_k, s_k = (tab[:, _slot(n)] for n in range(4))
    qn = _rms(proj(G_DQL, 2), qn_ref[...]).astype(BF16)
    qa = _dot(qn, wqa_ref[...])
    qb = _dot(qn, wqb_ref[...])
    kvn = _rms(proj(G_DKVL, 1), kvn_ref[...]).astype(BF16)
    ka = _dot(kvn, wka_ref[...])
    k_rot = s1 * c_k + s2 * s_k
    for hd in range(D_HEADS):
        dq_ref[0, :, _slot(hd)] = (qa[:, _slot(hd)] * c_q + qb[:, _slot(hd)] * s_q).astype(BF16)
        dk_ref[0, :, _slot(hd)] = (ka[:, _slot(hd)] + k_rot).astype(BF16)
    dv_ref[0] = jnp.where(key_half(D_HEADS), _dot(kvn, wv_ref[...]), 1.0).astype(BF16)


def _inproj(x, gain, w_ext, tab, fbias, qn_g, kvn_g, wqa, wqb, wka, wv):
    b, s, d = x.shape
    tm = min(TM_PROJ, s)
    n_cols = w_ext.shape[1]

    def rows(width):
        return pl.BlockSpec((1, tm, width), lambda bi, i: (bi, i, 0))

    out_widths = [(A_HEADS, BF16), (1, F32), (1, BF16), (1, BF16), (1, BF16), (B_HEADS, BF16), (B_HEADS, BF16),
                  (C_HEADS, BF16), (C_HEADS, BF16), (C_HEADS, BF16),
                  (D_HEADS, BF16), (D_HEADS, BF16), (D_HEADS, BF16), (1, F32)]
    return pl.pallas_call(
        _inproj_kernel,
        grid=(b, s // tm),
        in_specs=[rows(d),
                  _const_spec((1, d)),
                  _const_spec((d, n_cols)),
                  pl.BlockSpec((tm, 4 * LANES), lambda bi, i: (i, 0)),
                  _const_spec((1, LANES)),
                  _const_spec((1, D_Q_LORA)),
                  _const_spec((1, D_KV_LORA)),
                  _const_spec(wqa.shape), _const_spec(wqb.shape),
                  _const_spec(wka.shape), _const_spec(wv.shape)],
        out_specs=[rows(n * LANES) for n, _ in out_widths],
        out_shape=[jax.ShapeDtypeStruct((b, s, n * LANES), dt) for n, dt in out_widths],
        scratch_shapes=[pltpu.VMEM((8, LANES), F32)],
        name="inproj",
        compiler_params=_cparams(2),
    )(x, gain, w_ext, tab, fbias, qn_g, kvn_g, wqa, wqb, wka, wv)


def _gelu_tanh(x):
    return 0.5 * x * (1.0 + jnp.tanh(math.sqrt(2.0 / math.pi) * (x + 0.044715 * (x * x * x))))


def _compress_kernel(kc_ref, vc_ref, pos_ref, w1k_ref, w1v_ref, w2k_ref, w2v_ref, out_ref):
    nc = kc_ref.shape[1]
    half = A_CMP_STRIDE * HEAD_DIM
    pos_a = pos_ref[0:1, :]
    pos_b = pos_ref[1:2, :]

    def phi(chunks, w1_ref, w2_ref):
        u = _dot((chunks + pos_a).astype(BF16), w1_ref[0:half, :])
        v = _dot((chunks + pos_b).astype(BF16), w1_ref[half:2 * half, :])
        hid = u + pltpu.roll(v, nc - 1, 0)
        return _dot(_gelu_tanh(hid).astype(BF16), w2_ref[...])

    out_ref[0] = (phi(kc_ref[0], w1k_ref, w2k_ref) + phi(vc_ref[0], w1v_ref, w2v_ref)).astype(BF16)


def _compress(kc_in, vc_in, pos2, w1k, w1v, w2k_pad, w2v_pad):
    b, nc, width = kc_in.shape
    blk = pl.BlockSpec((1, nc, width), lambda bi: (bi, 0, 0))
    return pl.pallas_call(
        _compress_kernel,
        grid=(b,),
        in_specs=[blk, blk, _const_spec(pos2.shape), _const_spec(w1k.shape), _const_spec(w1v.shape),
                  _const_spec(w2k_pad.shape), _const_spec(w2v_pad.shape)],
        out_specs=pl.BlockSpec((1, nc, LANES), lambda bi: (bi, 0, 0)),
        out_shape=jax.ShapeDtypeStruct((b, nc, LANES), BF16),
        name="compress",
        compiler_params=_cparams(1),
    )(kc_in, vc_in, pos2, w1k, w1v, w2k_pad, w2v_pad)


def _nsa_cmp_win_kernel(q_ref, kvc_ref, kvw_ref, small_ref, ovl_ref, bias_ref,
                        part_ref, sel_ref, *, n_top):
    tq = min(TQ_NSA, q_ref.shape[1])
    for u in range(q_ref.shape[1] // tq):
        _nsa_cmp_win_tile(u * tq, tq, q_ref, kvc_ref, kvw_ref, small_ref, ovl_ref, bias_ref,
                          part_ref, sel_ref, n_top)


def _nsa_cmp_win_tile(r0, tq, q_ref, kvc_ref, kvw_ref, small_ref, ovl_ref, bias_ref, part_ref, sel_ref, n_top):
    nc = kvc_ref.shape[1]
    n_sel = ovl_ref.shape[0]
    t0 = pl.multiple_of(pl.program_id(1) * q_ref.shape[1] + r0, tq)
    tile = slice(r0, r0 + tq)
    q4 = jnp.concatenate([q_ref[0, tile, _slot(h)] for h in range(A_HEADS)], axis=0)
    rows = A_HEADS * tq

    kvc = kvc_ref[0]
    s = _dot_nt(q4, kvc)
    t_row = t0 + (lax.broadcasted_iota(jnp.int32, (rows, nc), 0) & (tq - 1))
    c_end = lax.broadcasted_iota(jnp.int32, (rows, nc), 1) * A_CMP_STRIDE + (A_CMP_LEN - 1)
    valid = c_end <= t_row
    s = jnp.where(valid, s, NEG_INF)
    m = jnp.max(s, axis=1, keepdims=True)
    p = jnp.where(valid, jnp.exp2(s - m), 0.0)
    den = jnp.maximum(jnp.sum(p, axis=1, keepdims=True), 1e-30)
    p = p / den
    o_c = _dot(p.astype(BF16), kvc)

    p_sum = p[0:tq] + p[tq:2 * tq] + p[2 * tq:3 * tq] + p[3 * tq:4 * tq]
    p_hi = p_sum.astype(BF16)
    p_lo = (p_sum - p_hi.astype(F32)).astype(BF16)
    imp_t = _dot_nt(ovl_ref[...], p_hi) + _dot_nt(ovl_ref[...], p_lo)

    j_idx = lax.broadcasted_iota(jnp.int32, (n_sel, tq), 0)
    cur = lax.shift_right_logical(t0 + lax.broadcasted_iota(jnp.int32, (n_sel, tq), 1), 6)
    forced = (j_idx == 0) | (j_idx == cur) | (j_idx == cur - 1)
    imp_t = jnp.where(forced, FORCE_SCORE, jnp.where(j_idx > cur, -FORCE_SCORE, imp_t))
    sub8 = 8
    groups = [imp_t[v * sub8:(v + 1) * sub8] for v in range(n_sel // sub8)]
    ranks = [jnp.zeros((sub8, tq), F32) for _ in groups]
    in_group = lax.broadcasted_iota(jnp.int32, (sub8, tq), 0)
    for c in range(n_sel):
        row_c = imp_t[c:c + 1, :]
        for v, grp in enumerate(groups):
            if v > c // sub8:
                beats = row_c >= grp
            elif v < c // sub8:
                beats = row_c > grp
            else:
                beats = (row_c > grp) | ((row_c == grp) & (in_group > c % sub8))
            ranks[v] = ranks[v] + jnp.where(beats, 1.0, 0.0)
    rank = jnp.concatenate(ranks, axis=0)
    sel_t = jnp.where(rank < float(n_top), 1.0, 0.0).astype(BF16)
    eye = (lax.broadcasted_iota(jnp.int32, (tq, tq), 0)
           == lax.broadcasted_iota(jnp.int32, (tq, tq), 1)).astype(BF16)
    sel_ref[0, tile] = _dot_nt(eye, sel_t).astype(BF16)

    span = A_WINDOW + tq
    kvw = kvw_ref[0, pl.ds(t0, span), :]
    sw = _dot_nt(q4, kvw) + bias_ref[...]
    kpos = t0 - A_WINDOW + lax.broadcasted_iota(jnp.int32, (rows, span), 1)
    sw = jnp.where(kpos >= 0, sw, NEG_INF)
    mw = jnp.max(sw, axis=1, keepdims=True)
    pw = jnp.exp2(sw - mw)
    o_w = _dot(pw.astype(BF16), kvw) / jnp.sum(pw, axis=1, keepdims=True)

    g = _sigmoid(small_ref[0, tile])
    for h in range(A_HEADS):
        g_c = g[:, LANE_GATE + 3 * h:LANE_GATE + 3 * h + 1]
        g_w = g[:, LANE_GATE + 3 * h + 2:LANE_GATE + 3 * h + 3]
        part_ref[0, tile, _slot(h)] = g_c * o_c[h * tq:(h + 1) * tq] + g_w * o_w[h * tq:(h + 1) * tq]


def _nsa_cmp_win(aq, kvc, kvw_pad, small, ovl, bias_w):
    b, s, _ = aq.shape
    tq = min(TQ_NSA_STEP, s)
    nc = kvc.shape[1]
    n_sel = ovl.shape[0]
    n_top = min(A_SEL_TOPK, n_sel)
    return pl.pallas_call(
        functools.partial(_nsa_cmp_win_kernel, n_top=n_top),
        grid=(b, s // tq),
        in_specs=[pl.BlockSpec((1, tq, A_HEADS * LANES), lambda bi, i: (bi, i, 0)),
                  pl.BlockSpec((1, nc, LANES), lambda bi, i: (bi, 0, 0)),
                  pl.BlockSpec((1, s + A_WINDOW, LANES), lambda bi, i: (bi, 0, 0)),
                  pl.BlockSpec((1, tq, LANES), lambda bi, i: (bi, i, 0)),
                  _const_spec(ovl.shape),
                  _const_spec(bias_w.shape)],
        out_specs=[pl.BlockSpec((1, tq, A_HEADS * LANES), lambda bi, i: (bi, i, 0)),
                   pl.BlockSpec((1, tq, n_sel), lambda bi, i: (bi, i, 0))],
        out_shape=[jax.ShapeDtypeStruct((b, s, A_HEADS * LANES), F32),
                   jax.ShapeDtypeStruct((b, s, n_sel), BF16)],
        name="nsa_cmp_win",
        compiler_params=_cparams(2),
    )(aq, kvc, kvw_pad, small, ovl, bias_w)


def _flash_kernel(*refs, mode, nh, ones_lane):
    if mode == "sel":
        q_ref, k_ref, v_ref, sel_ref, e_ref, bias_ref, part_ref, small_ref, o_ref, m_ref, acc_ref = refs
    elif mode == "fox":
        q_ref, k_ref, v_ref, cumt_ref, o_ref, m_ref, acc_ref = refs
    else:
        q_ref, k_ref, v_ref, o_ref, m_ref, acc_ref = refs
    i = pl.program_id(1)
    tq = q_ref.shape[1]
    tk = min(TK_FLASH, tq)
    sub = tq // tk
    shared_kv = mode == "sel"

    m_ref[...] = jnp.full(m_ref.shape, NEG_INF, F32)
    acc_ref[...] = jnp.zeros(acc_ref.shape, F32)

    lag = lax.broadcasted_iota(jnp.int32, (tq, tk), 0) - lax.broadcasted_iota(jnp.int32, (tq, tk), 1)

    def step(j, diag):
        k0 = pl.multiple_of(j * tk, tk)
        keep = (lag >= (j - i * sub) * tk) if diag else None
        if mode == "sel":
            sel_keys = _dot(sel_ref[0], e_ref[:, pl.ds(k0, tk)]) > 0.5
            keep = (sel_keys & keep) if diag else sel_keys
            n_off, bt = bias_ref.shape[0], bias_ref.shape[2]

            def bias_tile(h):
                strips = []
                for a in range(tq // bt):
                    offs = [jnp.clip(i * (tq // bt) + a - j * (tk // bt) - c, 0, n_off - 1)
                            for c in range(tk // bt)]
                    strips.append(jnp.concatenate([bias_ref[o, h] for o in offs], axis=1))
                return jnp.concatenate(strips, axis=0)
        logits = []
        for h in range(nh):
            kv_cols = _slot(0) if shared_kv else _slot(h)
            s = _dot_nt(q_ref[0, :, _slot(h)], k_ref[0, pl.ds(k0, tk), kv_cols])
            if mode == "sel":
                s = s + bias_tile(h)
            if mode == "fox":
                s = s - cumt_ref[0, h:h + 1, pl.ds(k0, tk)]
            if keep is not None:
                s = jnp.where(keep, s, NEG_INF)
            logits.append(s)
        probs = []
        for h in range(nh):
            m_prev = m_ref[h]
            m_new = jnp.maximum(m_prev, jnp.max(logits[h], axis=1, keepdims=True))
            p = jnp.exp2(logits[h] - jnp.concatenate([m_new] * (tk // LANES), axis=1))
            alpha = jnp.exp2(m_prev - m_new)
            m_ref[h] = m_new
            probs.append((p.astype(BF16), alpha))
        for h in range(nh):
            kv_cols = _slot(0) if shared_kv else _slot(h)
            p, alpha = probs[h]
            acc_ref[h] = alpha * acc_ref[h] + _dot(p, v_ref[0, pl.ds(k0, tk), kv_cols])

    def full_tile(j, carry):
        step(j, False)
        return carry

    lax.fori_loop(0, i * sub, full_tile, 0)
    for a in range(sub):
        step(i * sub + a, True)

    if mode == "sel":
        g = _sigmoid(small_ref[0])
    for h in range(nh):
        acc = acc_ref[h]
        o_h = acc / acc[:, ones_lane:ones_lane + 1]
        if mode == "sel":
            g_s = g[:, LANE_GATE + 3 * h + 1:LANE_GATE + 3 * h + 2]
            o_h = part_ref[0, :, _slot(h)] + g_s * o_h
        o_ref[0, :, _slot(h)] = o_h.astype(BF16)


def _flash(mode, q, k, v, sel=None, expand=None, bias=None, part=None, small=None, cum_t=None):
    b, s, qw = q.shape
    nh = qw // LANES
    tq = min(TQ_FLASH, s)
    q_spec = pl.BlockSpec((1, tq, qw), lambda bi, i: (bi, i, 0))
    small_spec = pl.BlockSpec((1, tq, LANES), lambda bi, i: (bi, i, 0))

    def seq_spec(arr):
        return pl.BlockSpec((1,) + arr.shape[1:], lambda bi, i: (bi, 0, 0))

    if mode == "sel":
        args = (q, k, v, sel, expand, bias, part, small)
        in_specs = [q_spec, seq_spec(k), seq_spec(v),
                    pl.BlockSpec((1, tq, sel.shape[2]), lambda bi, i: (bi, i, 0)),
                    _const_spec(expand.shape), _const_spec(bias.shape), q_spec, small_spec]
    elif mode == "fox":
        args = (q, k, v, cum_t)
        in_specs = [q_spec, seq_spec(k), seq_spec(v), seq_spec(cum_t)]
    else:
        args = (q, k, v)
        in_specs = [q_spec, seq_spec(k), seq_spec(v)]
    ones_lane = HEAD_DIM if mode == "mla" else 0
    return pl.pallas_call(
        functools.partial(_flash_kernel, mode=mode, nh=nh, ones_lane=ones_lane),
        grid=(b, s // tq),
        in_specs=in_specs,
        out_specs=q_spec,
        out_shape=jax.ShapeDtypeStruct((b, s, qw), BF16),
        scratch_shapes=[pltpu.VMEM((nh, tq, LANES), F32)] * 2,
        name="flash_" + mode,
        compiler_params=_cparams(2),
    )(*args)


def _dilated_kernel(q_ref, kvp_ref, kvc_ref, bias_ref, o_ref, lse_ref):
    i = pl.program_id(2)
    tb = kvp_ref.shape[1]
    n_sub = q_ref.shape[1] // tb
    col = lax.broadcasted_iota(jnp.int32, (tb, 2 * tb), 1)
    in_seq = (col >= tb) | (i > 0)
    for t in range(n_sub):
        rows = slice(t * tb, (t + 1) * tb)
        for h in range(B_HPG):
            prev = kvp_ref[0, :, _slot(h)] if t == 0 else kvc_ref[0, (t - 1) * tb:t * tb, _slot(h)]
            kv = jnp.concatenate([prev, kvc_ref[0, rows, _slot(h)]], axis=0)
            s = _dot_nt(q_ref[0, rows, _slot(h)], kv) + bias_ref[h]
            if t == 0:
                s = jnp.where(in_seq, s, NEG_INF)
            m = jnp.max(s, axis=1, keepdims=True)
            p = jnp.exp2(s - m)
            den = jnp.sum(p, axis=1, keepdims=True)
            o_ref[0, rows, _slot(h)] = (_dot(p.astype(BF16), kv) / den).astype(BF16)
            lse_ref[0, rows, _slot(h)] = jnp.broadcast_to(m + jnp.log2(den), (tb, LANES))


def _dilated(q_cls, kv_cls, bias, dil):
    b, n, _ = q_cls.shape
    tq = min(TQ_DIL_STEP, n)
    n_sub = tq // TQ_DIL
    width = B_HPG * LANES
    cur = pl.BlockSpec((1, tq, width), lambda bi, r, i: (bi, i, r))
    prev = pl.BlockSpec((1, TQ_DIL, width), lambda bi, r, i: (bi, jnp.maximum(i * n_sub - 1, 0), r))
    return pl.pallas_call(
        _dilated_kernel,
        grid=(b, dil, n // tq),
        in_specs=[cur, prev, cur, _const_spec(bias.shape)],
        out_specs=[cur, cur],
        out_shape=[jax.ShapeDtypeStruct(q_cls.shape, BF16), jax.ShapeDtypeStruct(q_cls.shape, F32)],
        name="dilated",
        compiler_params=_cparams(3),
    )(q_cls, kv_cls, kv_cls, bias)


def _merge_kernel(x_ref, oa_ref, ob0_ref, ob1_ref, ob2_ref, l0_ref, l1_ref, l2_ref, oc_ref, od_ref,
                  gpre_ref, gpost_ref, wg_ref, wba_ref, wbb_ref, wbc_ref, wbd_ref, wo_ref, out_ref):
    x = x_ref[0]
    h = _rms(x, gpre_ref[...]).astype(BF16)
    l0, l1, l2 = l0_ref[0], l1_ref[0], l2_ref[0]
    mx = jnp.maximum(jnp.maximum(l0, l1), l2)
    e0, e1, e2 = jnp.exp2(l0 - mx), jnp.exp2(l1 - mx), jnp.exp2(l2 - mx)
    ob = (e0 * ob0_ref[0].astype(F32) + e1 * ob1_ref[0].astype(F32)
          + e2 * ob2_ref[0].astype(F32)) / (e0 + e1 + e2)
    branches = ((oa_ref[0], wba_ref), (ob.astype(BF16), wbb_ref), (oc_ref[0], wbc_ref), (od_ref[0], wbd_ref))
    merged = None
    for n, (o_n, wb_ref) in enumerate(branches):
        term = _sigmoid(_dot(h, wg_ref[n])) * _dot(o_n, wb_ref[...])
        merged = term if merged is None else merged + term
    z = _dot(merged.astype(BF16), wo_ref[...])
    out_ref[0] = x + _rms(z, gpost_ref[...])


def _merge(x, oa, obs, lses, oc, od, gpre, gpost, wg, wba, wbb, wbc, wbd, wo):
    b, s, d = x.shape
    tm = min(TM_PROJ, s)

    def rows(arr):
        return pl.BlockSpec((1, tm, arr.shape[2]), lambda bi, i: (bi, i, 0))

    acts = (x, oa, *obs, *lses, oc, od)
    consts = (gpre, gpost, wg, wba, wbb, wbc, wbd, wo)
    return pl.pallas_call(
        _merge_kernel,
        grid=(b, s // tm),
        in_specs=[rows(a) for a in acts] + [_const_spec(c.shape) for c in consts],
        out_specs=rows(x),
        out_shape=jax.ShapeDtypeStruct(x.shape, F32),
        name="merge",
        compiler_params=_cparams(2),
    )(*acts, *consts)


def _ffn_kernel(x_ref, xp_ref, gpre_ref, gpost_ref, wup_ref, cw_ref, cb_ref, wdn_ref, out_ref, *, n_chunks):
    i = pl.program_id(1)
    tm = x_ref.shape[1]
    d_ff = wdn_ref.shape[0]
    chunk = d_ff // n_chunks
    x = x_ref[0]
    halo = jnp.where(i > 0, xp_ref[0], 0.0)
    he = _rms(jnp.concatenate([halo, x], axis=0), gpre_ref[...]).astype(BF16)

    def conv(c0):
        u = _dot(he, wup_ref[:, c0:c0 + chunk])
        w = cw_ref[:, c0:c0 + chunk]
        out = cb_ref[:, c0:c0 + chunk] + w[0:1] * pltpu.roll(u, 2, 0)[CONV_HALO:]
        out = out + w[1:2] * pltpu.roll(u, 1, 0)[CONV_HALO:]
        return out + w[2:3] * u[CONV_HALO:]

    y = jnp.zeros((tm, x.shape[1]), F32)
    for c in range(n_chunks):
        gate = conv(c * chunk)
        val = conv(d_ff + c * chunk)
        act = gate * _sigmoid(gate) * val
        y = y + _dot(act.astype(BF16), wdn_ref[c * chunk:(c + 1) * chunk, :])
    out_ref[0] = x + _rms(y, gpost_ref[...])


def _ffn(x, gpre, gpost, w_up, conv_w, conv_b, w_down):
    b, s, d = x.shape
    tm = min(TM_PROJ, s)
    d_ff = w_down.shape[0]
    n_chunks = 2 if d_ff % (2 * LANES) == 0 else 1
    halo_blocks = tm // CONV_HALO
    return pl.pallas_call(
        functools.partial(_ffn_kernel, n_chunks=n_chunks),
        grid=(b, s // tm),
        in_specs=[pl.BlockSpec((1, tm, d), lambda bi, i: (bi, i, 0)),
                  pl.BlockSpec((1, CONV_HALO, d), lambda bi, i: (bi, jnp.maximum(i * halo_blocks - 1, 0), 0)),
                  _const_spec(gpre.shape), _const_spec(gpost.shape), _const_spec(w_up.shape),
                  _const_spec(conv_w.shape), _const_spec(conv_b.shape), _const_spec(w_down.shape)],
        out_specs=pl.BlockSpec((1, tm, d), lambda bi, i: (bi, i, 0)),
        out_shape=jax.ShapeDtypeStruct(x.shape, F32),
        name="ffn",
        compiler_params=_cparams(2),
    )(x, x, gpre, gpost, w_up, conv_w, conv_b, w_down)


def _rel_bucket(dist):
    dist = jnp.maximum(dist, 0)
    d = jnp.maximum(dist, 1).astype(F32)
    large = BUCKET_EXACT + (jnp.log(d / BUCKET_EXACT) / math.log(BUCKET_MAX_DIST / BUCKET_EXACT)
                            * (N_BUCKETS - BUCKET_EXACT)).astype(jnp.int32)
    large = jnp.minimum(large, N_BUCKETS - 1)
    return jnp.where(dist < BUCKET_EXACT, dist, large)


def _in_proj_columns():
    src = np.full((N_SLOTS * LANES,), -1, np.int64)
    scl = np.ones((N_SLOTS * LANES,), np.float32)
    o_aq = 0
    o_akv = o_aq + A_HEADS * HEAD_DIM
    o_ag = o_akv + 6 * HEAD_DIM
    o_b = o_ag + 3 * A_HEADS
    o_c = o_b + 3 * B_HEADS * HEAD_DIM
    o_cf = o_c + 3 * C_HEADS * HEAD_DIM
    o_dq = o_cf + C_HEADS
    o_dkv = o_dq + D_Q_LORA
    o_dkr = o_dkv + D_KV_LORA
    e = np.arange(HEAD_DIM)
    q_scale = HEAD_DIM ** -0.5 * LOG2E

    def put(slot, lane0, cols, scale=1.0):
        dst = slot * LANES + lane0 + np.arange(len(cols))
        src[dst] = cols
        scl[dst] = scale

    for h in range(A_HEADS):
        put(G_AQ + h, 0, o_aq + h * HEAD_DIM + e, q_scale)
    for n, slot in enumerate((G_CMP, G_SLC, G_WIN)):
        put(slot, 0, o_akv + (2 * n) * HEAD_DIM + e)
        put(slot, HEAD_DIM, o_akv + (2 * n + 1) * HEAD_DIM + e)
    for h in range(B_HEADS):
        put(G_BQ + h, 0, o_b + h * HEAD_DIM + e, q_scale)
        put(G_BKV + h, 0, o_b + (B_HEADS + h) * HEAD_DIM + e)
        put(G_BKV + h, HEAD_DIM, o_b + (2 * B_HEADS + h) * HEAD_DIM + e)
    for h in range(C_HEADS):
        put(G_CQ + h, 0, o_c + h * HEAD_DIM + e, q_scale)
        put(G_CKV + h, 0, o_c + (C_HEADS + h) * HEAD_DIM + e)
        put(G_CKV + h, HEAD_DIM, o_c + (2 * C_HEADS + h) * HEAD_DIM + e)
    put(G_DQL, 0, o_dq + np.arange(D_Q_LORA))
    put(G_DKVL, 0, o_dkv + np.arange(D_KV_LORA))
    put(G_S1, LANE_GATE, o_ag + np.arange(3 * A_HEADS))
    put(G_S1, LANE_FORGET, o_cf + np.arange(C_HEADS))
    put(G_S1, LANE_ROPE, o_dkr + np.arange(D_ROPE))
    half = D_ROPE // 2
    put(G_S2, LANE_ROPE, o_dkr + half + np.arange(half), -1.0)
    put(G_S2, LANE_ROPE + half, o_dkr + np.arange(half))
    return src, scl


def _gather_cols(w, src, scl):
    pieces = []
    n = len(src)
    a = 0
    while a < n:
        e = a + 1
        if src[a] < 0:
            while e < n and src[e] < 0:
                e += 1
            pieces.append(jnp.zeros((w.shape[0], e - a), w.dtype))
        else:
            while e < n and src[e] == src[e - 1] + 1 and scl[e] == scl[a]:
                e += 1
            run = w[:, int(src[a]):int(src[a]) + (e - a)]
            pieces.append(run if scl[a] == 1.0 else run * float(scl[a]))
        a = e
    return jnp.concatenate(pieces, axis=1)


def _mla_weight_columns():
    per_q = D_NOPE + D_ROPE
    half = D_ROPE // 2
    qa = np.full((D_HEADS * LANES,), -1, np.int64)
    qb = np.full((D_HEADS * LANES,), -1, np.int64)
    qb_s = np.ones((D_HEADS * LANES,), np.float32)
    ka = np.full((D_HEADS * LANES,), -1, np.int64)
    va = np.full((D_HEADS * LANES,), -1, np.int64)
    for h in range(D_HEADS):
        qa[h * LANES + np.arange(per_q)] = h * per_q + np.arange(per_q)
        rot = h * per_q + D_NOPE
        qb[h * LANES + D_NOPE + np.arange(half)] = rot + half + np.arange(half)
        qb_s[h * LANES + D_NOPE + np.arange(half)] = -1.0
        qb[h * LANES + D_NOPE + half + np.arange(half)] = rot + np.arange(half)
        ka[h * LANES + np.arange(D_NOPE)] = h * (D_NOPE + D_VDIM) + np.arange(D_NOPE)
        va[h * LANES + np.arange(D_VDIM)] = h * (D_NOPE + D_VDIM) + D_NOPE + np.arange(D_VDIM)
    ones = np.ones_like(qb_s)
    return (qa, ones), (qb, qb_s), (ka, ones), (va, ones)


def _pad_branch_rows(w, n_heads, lane0):
    d = w.shape[1]
    out = jnp.zeros((n_heads, LANES, d), w.dtype)
    out = out.at[:, lane0:lane0 + HEAD_DIM, :].set(w.reshape(n_heads, HEAD_DIM, d))
    return out.reshape(n_heads * LANES, d)


def _rope_tables(s):
    inv_freq = ROPE_THETA ** (-jnp.arange(0, D_ROPE, 2, dtype=F32) / D_ROPE)
    ang = jnp.arange(s, dtype=F32)[:, None] * inv_freq[None, :]
    cos2 = jnp.concatenate([jnp.cos(ang)] * 2, axis=1)
    sin2 = jnp.concatenate([jnp.sin(ang)] * 2, axis=1)
    scale = (D_NOPE + D_ROPE) ** -0.5 * LOG2E
    z_lo = jnp.zeros((s, D_NOPE), F32)
    z_hi = jnp.zeros((s, LANES - D_NOPE - D_ROPE), F32)
    c_q = jnp.concatenate([jnp.full((s, D_NOPE), scale, F32), scale * cos2, z_hi], axis=1)
    s_q = jnp.concatenate([z_lo, scale * sin2, z_hi], axis=1)
    c_k = jnp.concatenate([z_lo, cos2, z_hi], axis=1)
    s_k = jnp.concatenate([z_lo, sin2, z_hi], axis=1)
    return jnp.concatenate([c_q, s_q, c_k, s_k], axis=1)


def _toeplitz(vec, n_rows, n_cols, off, step=1):
    base = vec[:, ::step]
    last = base.shape[1] - 1
    lo, hi = off - (n_cols - 1), off + n_rows - 1
    core = base[:, max(lo, 0):min(hi, last) + 1]
    front = jnp.repeat(base[:, :1], max(0, -lo), axis=1)
    back = jnp.repeat(base[:, last:], max(0, hi - last), axis=1)
    g = jnp.concatenate([front, core, back], axis=1)
    lg = n_rows + n_cols - 1
    t = jnp.tile(g, (1, n_rows + 1))[:, :n_rows * (lg + 1)].reshape(-1, n_rows, lg + 1)
    return t[:, :, :n_cols][:, :, ::-1]


def _bias_tables(rel_table, s):
    bucket = _rel_bucket(jnp.arange(s))
    hit = bucket[None, :, None] == jnp.arange(N_BUCKETS)[None, None, :]
    by_dist = jnp.sum(jnp.where(hit, LOG2E * rel_table.T[:, None, :], 0.0), axis=-1)
    tqa = min(TQ_NSA, s)
    dist = jnp.arange(tqa)[:, None] + A_WINDOW - jnp.arange(A_WINDOW + tqa)[None, :]
    ok = (dist >= 0) & (dist < A_WINDOW)
    bias_w = jnp.where(ok[None], _toeplitz(by_dist[:A_HEADS], tqa, A_WINDOW + tqa, A_WINDOW), NEG_INF)
    bias_w = bias_w.reshape(A_HEADS * tqa, A_WINDOW + tqa)
    tf = min(BIAS_TILE, s)
    n_off = min(s // tf, -(-(BUCKET_MAX_DIST + tf - 1) // tf) + 1)
    bias_s = jnp.stack([_toeplitz(by_dist[:A_HEADS], tf, tf, n * tf) for n in range(n_off)])
    m = jnp.arange(TQ_DIL)[:, None] + TQ_DIL - jnp.arange(2 * TQ_DIL)[None, :]
    bias_d = []
    for g, (window, dil) in enumerate(B_GROUPS):
        ok = (m >= 0) & (m <= window // dil)
        heads = by_dist[A_HEADS + g * B_HPG:A_HEADS + (g + 1) * B_HPG]
        bias_d.append(jnp.where(ok[None], _toeplitz(heads, TQ_DIL, 2 * TQ_DIL, TQ_DIL, dil), NEG_INF))
    return bias_w, bias_s, bias_d


def _selection_constants(s):
    nc = s // A_CMP_STRIDE
    n_cmp = (s - A_CMP_LEN) // A_CMP_STRIDE + 1
    n_sel = s // A_SEL_BLOCK
    c = np.arange(nc)[:, None]
    j = np.arange(n_sel)[None, :]
    c_start = c * A_CMP_STRIDE
    overlap = ((c_start <= j * A_SEL_BLOCK + A_SEL_BLOCK - 1) & (c_start + A_CMP_LEN - 1 >= j * A_SEL_BLOCK)
               & (c < n_cmp))
    expand = (np.arange(s)[None, :] // A_SEL_BLOCK) == np.arange(n_sel)[:, None]
    return jnp.asarray(overlap.T, BF16), jnp.asarray(expand, BF16)


def kernel(x, rel_bias_table, norm_attn_pre, norm_attn_post, norm_ffn_pre, norm_ffn_post, w_in, nsa_cmp_pos, nsa_phi_k_w1, nsa_phi_k_w2, nsa_phi_v_w1, nsa_phi_v_w2, fox_forget_bias, mla_q_norm, mla_kv_norm, mla_w_uq, mla_w_ukv, w_branch_a, w_branch_b, w_branch_c, w_branch_d, w_merge_gate, w_o, ffn_w_up, ffn_conv_w, ffn_conv_b, ffn_w_down):
    b, s, d = x.shape
    depth = w_in.shape[0]
    assert s % (TQ_DIL * B_GROUPS[-1][1]) == 0, "every dilation class needs whole 128-row tiles"

    in_src, in_scl = _in_proj_columns()
    (qa_i, qa_s), (qb_i, qb_s), (ka_i, ka_s), (va_i, va_s) = _mla_weight_columns()
    rope_tab = _rope_tables(s)
    bias_w, bias_s, bias_d = _bias_tables(rel_bias_table, s)
    overlap, expand = _selection_constants(s)
    half = A_CMP_STRIDE * HEAD_DIM

    for l in range(depth):
        w_ext = _gather_cols(w_in[l], in_src, in_scl).astype(BF16)
        fbias = jnp.zeros((1, LANES), F32).at[0, LANE_FORGET:LANE_FORGET + C_HEADS].set(fox_forget_bias[l])
        (aq, cmp, slc, slc_v, win, bq, bkv, cq, ckv, c_v, dq, dk, dv, small) = _inproj(
            x, norm_attn_pre[l][None], w_ext, rope_tab, fbias,
            mla_q_norm[l][None], mla_kv_norm[l][None],
            _gather_cols(mla_w_uq[l], qa_i, qa_s).astype(BF16),
            _gather_cols(mla_w_uq[l], qb_i, qb_s).astype(BF16),
            _gather_cols(mla_w_ukv[l], ka_i, ka_s).astype(BF16),
            _gather_cols(mla_w_ukv[l], va_i, va_s).astype(BF16))

        nc = s // A_CMP_STRIDE
        w2k_pad = jnp.pad(nsa_phi_k_w2[l], ((0, 0), (0, LANES - HEAD_DIM))).astype(BF16)
        w2v_pad = jnp.pad(nsa_phi_v_w2[l], ((0, 0), (LANES - HEAD_DIM, 0))).astype(BF16)
        kvc = _compress(cmp[..., :HEAD_DIM].reshape(b, nc, half), cmp[..., HEAD_DIM:].reshape(b, nc, half),
                        nsa_cmp_pos[l].reshape(2, half),
                        nsa_phi_k_w1[l].astype(BF16), nsa_phi_v_w1[l].astype(BF16), w2k_pad, w2v_pad)
        kvw_pad = jnp.pad(win, ((0, 0), (A_WINDOW, 0), (0, 0)))
        part, sel = _nsa_cmp_win(aq, kvc, kvw_pad, small, overlap, bias_w)
        o_a = _flash("sel", aq, slc, slc_v, sel=sel, expand=expand, bias=bias_s, part=part, small=small)

        obs, lses = [], []
        for g, (_, dil) in enumerate(B_GROUPS):
            cols = slice(g * B_HPG * LANES, (g + 1) * B_HPG * LANES)
            cls_shape = (b, s // dil, dil * B_HPG * LANES)
            o_g, lse_g = _dilated(bq[..., cols].reshape(cls_shape), bkv[..., cols].reshape(cls_shape),
                                  bias_d[g], dil)
            obs.append(o_g.reshape(b, s, B_HPG * LANES))
            lses.append(lse_g.reshape(b, s, B_HPG * LANES))

        cum_t = LOG2E * jnp.swapaxes(small[..., LANE_FORGET:LANE_FORGET + C_HEADS], 1, 2)
        o_c = _flash("fox", cq, ckv, c_v, cum_t=cum_t)
        o_d = _flash("mla", dq, dk, dv)

        x = _merge(x, o_a, obs, lses, o_c, o_d,
                   norm_attn_pre[l][None], norm_attn_post[l][None],
                   w_merge_gate[l].astype(BF16),
                   _pad_branch_rows(w_branch_a[l], A_HEADS, HEAD_DIM).astype(BF16),
                   _pad_branch_rows(w_branch_b[l], B_HPG, HEAD_DIM).astype(BF16),
                   _pad_branch_rows(w_branch_c[l], C_HEADS, HEAD_DIM).astype(BF16),
                   _pad_branch_rows(w_branch_d[l], D_HEADS, 0).astype(BF16),
                   w_o[l].astype(BF16))
        x = _ffn(x, norm_ffn_pre[l][None], norm_ffn_post[l][None], ffn_w_up[l].astype(BF16),
                 ffn_conv_w[l], ffn_conv_b[l][None], ffn_w_down[l].astype(BF16))
    return x
```

```python
import functools
import math

import numpy as np
import jax
import jax.numpy as jnp
from jax import lax
from jax.experimental import pallas as pl
from jax.experimental.pallas import tpu as pltpu

F32 = jnp.float32
BF16 = jnp.bfloat16

LANES = 128
MXU_WIDTH = 256
HEAD_DIM = 64
RMS_EPS = 1e-6
NEG_INF = -1e30
FORCE_SCORE = 1e9
LOG2E = math.log2(math.e)
VMEM_LIMIT_BYTES = 56 * 1024 * 1024

N_BUCKETS = 32
BUCKET_EXACT = 16
BUCKET_MAX_DIST = 2048

A_HEADS = 4
A_CMP_LEN = 32
A_CMP_STRIDE = 16
A_PHI_HIDDEN = 128
A_SEL_BLOCK = 64
A_SEL_TOPK = 16
A_WINDOW = 512
B_GROUPS = ((128, 1), (512, 4), (2048, 16))
B_HPG = 2
B_HEADS = B_HPG * len(B_GROUPS)
C_HEADS = 4
D_HEADS = 4
D_Q_LORA = 256
D_KV_LORA = 128
D_NOPE = 64
D_ROPE = 32
D_VDIM = 64
ROPE_THETA = 10000.0
N_BRANCHES = 4

G_AQ = 0
G_CMP = G_AQ + A_HEADS
G_SLC = G_CMP + 1
G_WIN = G_SLC + 1
G_BQ = G_WIN + 1
G_BKV = G_BQ + len(B_GROUPS)
G_CQ = G_BKV + B_HEADS
G_CKV = G_CQ + C_HEADS
G_DQL = G_CKV + C_HEADS
G_DKVL = G_DQL + 2
G_S1 = G_DKVL + 1
G_S2 = G_S1 + 1
N_SLOTS = G_S2 + 1
LANE_GATE = 0
LANE_FORGET = 3 * A_HEADS
LANE_ROPE = 64

TM_PROJ = 512
TQ_NSA_STEP = 256
TQ_FLASH = 512
TK_FLASH = 512
BIAS_TILE = 256
TQ_NSA = 128
TQ_DIL = 128
CONV_HALO = 16


def _cparams(n_grid):
    return pltpu.CompilerParams(dimension_semantics=("arbitrary",) * n_grid,
                                vmem_limit_bytes=VMEM_LIMIT_BYTES)


def _const_spec(shape):
    nd = len(shape)
    return pl.BlockSpec(shape, lambda *_: (0,) * nd, pipeline_mode=pl.Buffered(1))


def _rms(x, gain):
    return x * lax.rsqrt(jnp.mean(x * x, axis=-1, keepdims=True) + RMS_EPS) * gain


def _sigmoid(x):
    return 1.0 / (1.0 + jnp.exp(-x))


def _dot(a, b):
    return jnp.dot(a, b, preferred_element_type=F32)


def _dot_nt(a, b):
    return lax.dot_general(a, b, (((1,), (1,)), ((), ())), preferred_element_type=F32)


def _slot(h):
    return slice(h * LANES, (h + 1) * LANES)


def _inproj_kernel(x_ref, g_ref, w_ref, tab_ref, fb_ref, qn_ref, kvn_ref, wqa_ref, wqb_ref,
                   wka_ref, wv_ref,
                   aq_ref, cmp_ref, slc_ref, slcv_ref, win_ref, bq_ref, bkv_ref, cq_ref, ckv_ref, cv_ref,
                   dq_ref, dk_ref, dv_ref, small_ref, carry_ref):
    i = pl.program_id(1)
    tm = x_ref.shape[1]
    h = _rms(x_ref[0], g_ref[...]).astype(BF16)

    def proj(g0, n):
        return _dot(h, w_ref[:, g0 * LANES:(g0 + n) * LANES])

    def key_half(n_slots):
        lane = lax.broadcasted_iota(jnp.int32, (tm, n_slots * LANES), 1)
        return (lane & (LANES - 1)) < HEAD_DIM

    aq_ref[0] = proj(G_AQ, A_HEADS).astype(BF16)
    cmp_ref[0] = proj(G_CMP, 1)
    slc = proj(G_SLC, 1)
    slc_ref[0] = slc.astype(BF16)
    slcv_ref[0] = jnp.where(key_half(1), 1.0, slc).astype(BF16)
    win_ref[0] = proj(G_WIN, 1).astype(BF16)
    b_all = proj(G_BQ, len(B_GROUPS) + B_HEADS)
    for n in range(len(B_GROUPS)):
        bq_ref[0, n] = b_all[:, _slot(n)]
    for n in range(B_HEADS):
        bkv_ref[0, n] = b_all[:, _slot(len(B_GROUPS) + n)]
    cq_ref[0] = proj(G_CQ, C_HEADS).astype(BF16)
    ckv = proj(G_CKV, C_HEADS)
    ckv_ref[0] = ckv.astype(BF16)
    cv_ref[0] = jnp.where(key_half(C_HEADS), 1.0, ckv).astype(BF16)

    s1 = proj(G_S1, 1)
    s2 = proj(G_S2, 1)

    @pl.when(i == 0)
    def _():
        carry_ref[...] = jnp.zeros_like(carry_ref)

    z = s1 + fb_ref[...]
    logf = jnp.minimum(z, 0.0) - jnp.log(1.0 + jnp.exp(-jnp.abs(z)))
    row = lax.broadcasted_iota(jnp.int32, (tm, tm), 0)
    col = lax.broadcasted_iota(jnp.int32, (tm, tm), 1)
    tri = (row >= col).astype(BF16)
    hi = logf.astype(BF16)
    r1 = logf - hi.astype(F32)
    mid = r1.astype(BF16)
    lo = (r1 - mid.astype(F32)).astype(BF16)
    cum = _dot(tri, hi) + _dot(tri, mid) + _dot(tri, lo) + carry_ref[0:1, :]
    carry_ref[0:1, :] = cum[tm - 1:tm, :]
    lane = lax.broadcasted_iota(jnp.int32, (tm, LANES), 1)
    is_forget = (lane >= LANE_FORGET) & (lane < LANE_FORGET + C_HEADS)
    small_ref[0] = jnp.where(is_forget, cum, s1)

    tab = tab_ref[...]
    c_q, s_q, c_k, s_k = (tab[:, _slot(n)] for n in range(4))
    qn = _rms(proj(G_DQL, 2), qn_ref[...]).astype(BF16)
    qa = _dot(qn, wqa_ref[...])
    qb = _dot(qn, wqb_ref[...])
    kvn = _rms(proj(G_DKVL, 1), kvn_ref[...]).astype(BF16)
    ka = _dot(kvn, wka_ref[...])
    k_rot = s1 * c_k + s2 * s_k
    for hd in range(D_HEADS):
        dq_ref[0, :, _slot(hd)] = (qa[:, _slot(hd)] * c_q + qb[:, _slot(hd)] * s_q).astype(BF16)
        dk_ref[0, :, _slot(hd)] = (ka[:, _slot(hd)] + k_rot).astype(BF16)
    dv_ref[0] = jnp.where(key_half(D_HEADS), _dot(kvn, wv_ref[...]), 1.0).astype(BF16)


def _inproj(x, gain, w_ext, tab, fbias, qn_g, kvn_g, wqa, wqb, wka, wv):
    b, s, d = x.shape
    tm = min(TM_PROJ, s)
    n_cols = w_ext.shape[1]

    def rows(width):
        return pl.BlockSpec((1, tm, width), lambda bi, i: (bi, i, 0))

    def slots(n):
        return pl.BlockSpec((1, n, tm, LANES), lambda bi, i: (bi, 0, i, 0))

    out_widths = [(A_HEADS, BF16), (1, F32), (1, BF16), (1, BF16), (1, BF16), (-len(B_GROUPS), F32), (-B_HEADS, F32),
                  (C_HEADS, BF16), (C_HEADS, BF16), (C_HEADS, BF16),
                  (D_HEADS, BF16), (D_HEADS, BF16), (D_HEADS, BF16), (1, F32)]
    return pl.pallas_call(
        _inproj_kernel,
        grid=(b, s // tm),
        in_specs=[rows(d),
                  _const_spec((1, d)),
                  _const_spec((d, n_cols)),
                  pl.BlockSpec((tm, 4 * LANES), lambda bi, i: (i, 0)),
                  _const_spec((1, LANES)),
                  _const_spec((1, D_Q_LORA)),
                  _const_spec((1, D_KV_LORA)),
                  _const_spec(wqa.shape), _const_spec(wqb.shape),
                  _const_spec(wka.shape), _const_spec(wv.shape)],
        out_specs=[rows(n * LANES) if n > 0 else slots(-n) for n, _ in out_widths],
        out_shape=[jax.ShapeDtypeStruct((b, s, n * LANES) if n > 0 else (b, -n, s, LANES), dt)
                   for n, dt in out_widths],
        scratch_shapes=[pltpu.VMEM((8, LANES), F32)],
        name="inproj",
        compiler_params=_cparams(2),
    )(x, gain, w_ext, tab, fbias, qn_g, kvn_g, wqa, wqb, wka, wv)


def _gelu_tanh(x):
    return 0.5 * x * (1.0 + jnp.tanh(math.sqrt(2.0 / math.pi) * (x + 0.044715 * (x * x * x))))


def _compress_kernel(kc_ref, vc_ref, pos_ref, w1k_ref, w1v_ref, w2k_ref, w2v_ref, out_ref):
    nc = kc_ref.shape[1]
    half = A_CMP_STRIDE * HEAD_DIM
    pos_a = pos_ref[0:1, :]
    pos_b = pos_ref[1:2, :]

    def phi(chunks, w1_ref, w2_ref):
        u = _dot((chunks + pos_a).astype(BF16), w1_ref[0:half, :])
        v = _dot((chunks + pos_b).astype(BF16), w1_ref[half:2 * half, :])
        hid = u + pltpu.roll(v, nc - 1, 0)
        return _dot(_gelu_tanh(hid).astype(BF16), w2_ref[...])

    out_ref[0] = (phi(kc_ref[0], w1k_ref, w2k_ref) + phi(vc_ref[0], w1v_ref, w2v_ref)).astype(BF16)


def _compress(kc_in, vc_in, pos2, w1k, w1v, w2k_pad, w2v_pad):
    b, nc, width = kc_in.shape
    blk = pl.BlockSpec((1, nc, width), lambda bi: (bi, 0, 0))
    return pl.pallas_call(
        _compress_kernel,
        grid=(b,),
        in_specs=[blk, blk, _const_spec(pos2.shape), _const_spec(w1k.shape), _const_spec(w1v.shape),
                  _const_spec(w2k_pad.shape), _const_spec(w2v_pad.shape)],
        out_specs=pl.BlockSpec((1, nc, LANES), lambda bi: (bi, 0, 0)),
        out_shape=jax.ShapeDtypeStruct((b, nc, LANES), BF16),
        name="compress",
        compiler_params=_cparams(1),
    )(kc_in, vc_in, pos2, w1k, w1v, w2k_pad, w2v_pad)


def _nsa_cmp_win_kernel(q_ref, kvc_ref, kvw_ref, small_ref, ovl_ref, bias_ref,
                        part_ref, sel_ref, *, n_top):
    tq = min(TQ_NSA, q_ref.shape[1])
    for u in range(q_ref.shape[1] // tq):
        _nsa_cmp_win_tile(u * tq, tq, q_ref, kvc_ref, kvw_ref, small_ref, ovl_ref, bias_ref,
                          part_ref, sel_ref, n_top)


def _nsa_cmp_win_tile(r0, tq, q_ref, kvc_ref, kvw_ref, small_ref, ovl_ref, bias_ref, part_ref, sel_ref, n_top):
    nc = kvc_ref.shape[1]
    n_sel = ovl_ref.shape[0]
    t0 = pl.multiple_of(pl.program_id(1) * q_ref.shape[1] + r0, tq)
    tile = slice(r0, r0 + tq)
    q4 = jnp.concatenate([q_ref[0, tile, _slot(h)] for h in range(A_HEADS)], axis=0)
    rows = A_HEADS * tq

    kvc = kvc_ref[0]
    s = _dot_nt(q4, kvc)
    t_row = t0 + (lax.broadcasted_iota(jnp.int32, (rows, nc), 0) & (tq - 1))
    c_end = lax.broadcasted_iota(jnp.int32, (rows, nc), 1) * A_CMP_STRIDE + (A_CMP_LEN - 1)
    valid = c_end <= t_row
    s = jnp.where(valid, s, NEG_INF)
    m = jnp.max(s, axis=1, keepdims=True)
    p = jnp.where(valid, jnp.exp2(s - m), 0.0)
    den = jnp.maximum(jnp.sum(p, axis=1, keepdims=True), 1e-30)
    p = p / den
    o_c = _dot(p.astype(BF16), kvc)

    p_sum = p[0:tq] + p[tq:2 * tq] + p[2 * tq:3 * tq] + p[3 * tq:4 * tq]
    p_hi = p_sum.astype(BF16)
    p_lo = (p_sum - p_hi.astype(F32)).astype(BF16)
    imp_t = _dot_nt(ovl_ref[...], p_hi) + _dot_nt(ovl_ref[...], p_lo)

    j_idx = lax.broadcasted_iota(jnp.int32, (n_sel, tq), 0)
    cur = lax.shift_right_logical(t0 + lax.broadcasted_iota(jnp.int32, (n_sel, tq), 1), 6)
    forced = (j_idx == 0) | (j_idx == cur) | (j_idx == cur - 1)
    imp_t = jnp.where(forced, FORCE_SCORE, jnp.where(j_idx > cur, -FORCE_SCORE, imp_t))
    sub8 = 8
    groups = [imp_t[v * sub8:(v + 1) * sub8] for v in range(n_sel // sub8)]
    ranks = [jnp.zeros((sub8, tq), F32) for _ in groups]
    in_group = lax.broadcasted_iota(jnp.int32, (sub8, tq), 0)
    for c in range(n_sel):
        row_c = imp_t[c:c + 1, :]
        for v, grp in enumerate(groups):
            if v > c // sub8:
                beats = row_c >= grp
            elif v < c // sub8:
                beats = row_c > grp
            else:
                beats = (row_c > grp) | ((row_c == grp) & (in_group > c % sub8))
            ranks[v] = ranks[v] + jnp.where(beats, 1.0, 0.0)
    rank = jnp.concatenate(ranks, axis=0)
    sel_t = jnp.where(rank < float(n_top), 1.0, 0.0).astype(BF16)
    eye = (lax.broadcasted_iota(jnp.int32, (tq, tq), 0)
           == lax.broadcasted_iota(jnp.int32, (tq, tq), 1)).astype(BF16)
    sel_ref[0, tile] = _dot_nt(eye, sel_t).astype(BF16)

    span = A_WINDOW + tq
    kvw = kvw_ref[0, pl.ds(t0, span), :]
    sw = _dot_nt(q4, kvw) + bias_ref[...]
    kpos = t0 - A_WINDOW + lax.broadcasted_iota(jnp.int32, (rows, span), 1)
    sw = jnp.where(kpos >= 0, sw, NEG_INF)
    mw = jnp.max(sw, axis=1, keepdims=True)
    pw = jnp.exp2(sw - mw)
    o_w = _dot(pw.astype(BF16), kvw) / jnp.sum(pw, axis=1, keepdims=True)

    g = _sigmoid(small_ref[0, tile])
    for h in range(A_HEADS):
        g_c = g[:, LANE_GATE + 3 * h:LANE_GATE + 3 * h + 1]
        g_w = g[:, LANE_GATE + 3 * h + 2:LANE_GATE + 3 * h + 3]
        part_ref[0, tile, _slot(h)] = g_c * o_c[h * tq:(h + 1) * tq] + g_w * o_w[h * tq:(h + 1) * tq]


def _nsa_cmp_win(aq, kvc, kvw_pad, small, ovl, bias_w):
    b, s, _ = aq.shape
    tq = min(TQ_NSA_STEP, s)
    nc = kvc.shape[1]
    n_sel = ovl.shape[0]
    n_top = min(A_SEL_TOPK, n_sel)
    return pl.pallas_call(
        functools.partial(_nsa_cmp_win_kernel, n_top=n_top),
        grid=(b, s // tq),
        in_specs=[pl.BlockSpec((1, tq, A_HEADS * LANES), lambda bi, i: (bi, i, 0)),
                  pl.BlockSpec((1, nc, LANES), lambda bi, i: (bi, 0, 0)),
                  pl.BlockSpec((1, s + A_WINDOW, LANES), lambda bi, i: (bi, 0, 0)),
                  pl.BlockSpec((1, tq, LANES), lambda bi, i: (bi, i, 0)),
                  _const_spec(ovl.shape),
                  _const_spec(bias_w.shape)],
        out_specs=[pl.BlockSpec((1, tq, A_HEADS * LANES), lambda bi, i: (bi, i, 0)),
                   pl.BlockSpec((1, tq, n_sel), lambda bi, i: (bi, i, 0))],
        out_shape=[jax.ShapeDtypeStruct((b, s, A_HEADS * LANES), F32),
                   jax.ShapeDtypeStruct((b, s, n_sel), BF16)],
        name="nsa_cmp_win",
        compiler_params=_cparams(2),
    )(aq, kvc, kvw_pad, small, ovl, bias_w)


def _flash_kernel(*refs, mode, nh, ones_lane):
    if mode == "sel":
        q_ref, k_ref, v_ref, sel_ref, e_ref, bias_ref, part_ref, small_ref, o_ref, m_ref, acc_ref = refs
    elif mode == "fox":
        q_ref, k_ref, v_ref, cumt_ref, o_ref, m_ref, acc_ref = refs
    else:
        q_ref, k_ref, v_ref, o_ref, m_ref, acc_ref = refs
    i = pl.program_id(1)
    tq = q_ref.shape[1]
    tk = min(TK_FLASH, tq)
    sub = tq // tk
    shared_kv = mode == "sel"

    m_ref[...] = jnp.full(m_ref.shape, NEG_INF, F32)
    acc_ref[...] = jnp.zeros(acc_ref.shape, F32)

    lag = lax.broadcasted_iota(jnp.int32, (tq, tk), 0) - lax.broadcasted_iota(jnp.int32, (tq, tk), 1)

    def step(j, diag):
        k0 = pl.multiple_of(j * tk, tk)
        keep = (lag >= (j - i * sub) * tk) if diag else None
        if mode == "sel":
            sel_keys = _dot(sel_ref[0], e_ref[:, pl.ds(k0, tk)]) > 0.5
            keep = (sel_keys & keep) if diag else sel_keys
            n_off, bt = bias_ref.shape[0], bias_ref.shape[2]

            def bias_tile(h):
                strips = []
                for a in range(tq // bt):
                    offs = [jnp.clip(i * (tq // bt) + a - j * (tk // bt) - c, 0, n_off - 1)
                            for c in range(tk // bt)]
                    strips.append(jnp.concatenate([bias_ref[o, h] for o in offs], axis=1))
                return jnp.concatenate(strips, axis=0)
        logits = []
        for h in range(nh):
            kv_cols = _slot(0) if shared_kv else _slot(h)
            s = _dot_nt(q_ref[0, :, _slot(h)], k_ref[0, pl.ds(k0, tk), kv_cols])
            if mode == "sel":
                s = s + bias_tile(h)
            if mode == "fox":
                s = s - cumt_ref[0, h:h + 1, pl.ds(k0, tk)]
            if keep is not None:
                s = jnp.where(keep, s, NEG_INF)
            logits.append(s)
        probs = []
        for h in range(nh):
            m_prev = m_ref[h]
            m_new = jnp.maximum(m_prev, jnp.max(logits[h], axis=1, keepdims=True))
            p = jnp.exp2(logits[h] - jnp.concatenate([m_new] * (tk // LANES), axis=1))
            alpha = jnp.exp2(m_prev - m_new)
            m_ref[h] = m_new
            probs.append((p.astype(BF16), alpha))
        for h in range(nh):
            kv_cols = _slot(0) if shared_kv else _slot(h)
            p, alpha = probs[h]
            acc_ref[h] = alpha * acc_ref[h] + _dot(p, v_ref[0, pl.ds(k0, tk), kv_cols])

    def full_tile(j, carry):
        step(j, False)
        return carry

    lax.fori_loop(0, i * sub, full_tile, 0)
    for a in range(sub):
        step(i * sub + a, True)

    if mode == "sel":
        g = _sigmoid(small_ref[0])
    for h in range(nh):
        acc = acc_ref[h]
        o_h = acc / acc[:, ones_lane:ones_lane + 1]
        if mode == "sel":
            g_s = g[:, LANE_GATE + 3 * h + 1:LANE_GATE + 3 * h + 2]
            o_h = part_ref[0, :, _slot(h)] + g_s * o_h
        o_ref[0, :, _slot(h)] = o_h.astype(BF16)


def _flash(mode, q, k, v, sel=None, expand=None, bias=None, part=None, small=None, cum_t=None):
    b, s, qw = q.shape
    nh = qw // LANES
    tq = min(TQ_FLASH, s)
    q_spec = pl.BlockSpec((1, tq, qw), lambda bi, i: (bi, i, 0))
    small_spec = pl.BlockSpec((1, tq, LANES), lambda bi, i: (bi, i, 0))

    def seq_spec(arr):
        return pl.BlockSpec((1,) + arr.shape[1:], lambda bi, i: (bi, 0, 0))

    if mode == "sel":
        args = (q, k, v, sel, expand, bias, part, small)
        in_specs = [q_spec, seq_spec(k), seq_spec(v),
                    pl.BlockSpec((1, tq, sel.shape[2]), lambda bi, i: (bi, i, 0)),
                    _const_spec(expand.shape), _const_spec(bias.shape), q_spec, small_spec]
    elif mode == "fox":
        args = (q, k, v, cum_t)
        in_specs = [q_spec, seq_spec(k), seq_spec(v), seq_spec(cum_t)]
    else:
        args = (q, k, v)
        in_specs = [q_spec, seq_spec(k), seq_spec(v)]
    ones_lane = HEAD_DIM if mode == "mla" else 0
    return pl.pallas_call(
        functools.partial(_flash_kernel, mode=mode, nh=nh, ones_lane=ones_lane),
        grid=(b, s // tq),
        in_specs=in_specs,
        out_specs=q_spec,
        out_shape=jax.ShapeDtypeStruct((b, s, qw), BF16),
        scratch_shapes=[pltpu.VMEM((nh, tq, LANES), F32)] * 2,
        name="flash_" + mode,
        compiler_params=_cparams(2),
    )(*args)


def _dilated_kernel(q_ref, kvp_ref, kvc_ref, bias_ref, o_ref, o_sc, l_sc):
    i = pl.program_id(1)
    rows = q_ref.shape[2]
    tb = TQ_DIL
    low = lax.broadcasted_iota(jnp.int32, (tb, LANES), 1) < HEAD_DIM
    col = lax.broadcasted_iota(jnp.int32, (tb, 2 * tb), 1)
    in_seq = (col >= tb) | (i > 0)
    for g, (_, dil) in enumerate(B_GROUPS):
        span = dil * tb
        for r in range(dil):
            for c in range(rows // span):
                cur = pl.ds(r + c * span, tb, stride=dil)
                if c > 0:
                    prev_ref, prev = kvc_ref, pl.ds(r + (c - 1) * span, tb, stride=dil)
                else:
                    prev_ref, prev = kvp_ref, pl.ds(rows - span + r, tb, stride=dil)
                q2 = q_ref[0, g, cur, :]
                outs, lses = [], []
                for h in range(B_HPG):
                    q_h = jnp.where(low if h == 0 else ~low, q2, 0.0).astype(BF16)
                    head = B_HPG * g + h
                    kv = jnp.concatenate([prev_ref[0, head, prev, :], kvc_ref[0, head, cur, :]],
                                         axis=0).astype(BF16)
                    s = _dot_nt(q_h, kv) + bias_ref[g, h]
                    if c == 0:
                        s = jnp.where(in_seq, s, NEG_INF)
                    m = jnp.max(s, axis=1, keepdims=True)
                    p = jnp.exp2(s - m)
                    den = jnp.sum(p, axis=1, keepdims=True)
                    outs.append(_dot(p.astype(BF16), kv) / den)
                    lses.append(m + jnp.log2(den))
                o_sc[g, cur, :] = jnp.where(low, outs[1], outs[0])
                l_sc[g, cur, :] = jnp.where(low, lses[1], lses[0])
    step = 2 * tb
    for t in range(rows // step):
        rs = slice(t * step, (t + 1) * step)
        l0, l1, l2 = l_sc[0, rs, :], l_sc[1, rs, :], l_sc[2, rs, :]
        mx = jnp.maximum(jnp.maximum(l0, l1), l2)
        e0, e1, e2 = jnp.exp2(l0 - mx), jnp.exp2(l1 - mx), jnp.exp2(l2 - mx)
        mixed = (e0 * o_sc[0, rs, :] + e1 * o_sc[1, rs, :] + e2 * o_sc[2, rs, :]) / (e0 + e1 + e2)
        o_ref[0, rs, :] = mixed.astype(BF16)


def _dilated(bq, bkv, bias):
    b, _, s, _ = bq.shape
    rows = TQ_DIL * B_GROUPS[-1][1]
    n_groups = len(B_GROUPS)

    def tile(n_slots, prev):
        if prev:
            return pl.BlockSpec((1, n_slots, rows, LANES), lambda bi, i: (bi, 0, jnp.maximum(i - 1, 0), 0))
        return pl.BlockSpec((1, n_slots, rows, LANES), lambda bi, i: (bi, 0, i, 0))

    return pl.pallas_call(
        _dilated_kernel,
        grid=(b, s // rows),
        in_specs=[tile(bq.shape[1], False), tile(bkv.shape[1], True), tile(bkv.shape[1], False),
                  _const_spec(bias.shape)],
        out_specs=pl.BlockSpec((1, rows, LANES), lambda bi, i: (bi, i, 0)),
        out_shape=jax.ShapeDtypeStruct((b, s, LANES), BF16),
        scratch_shapes=[pltpu.VMEM((n_groups, rows, LANES), F32)] * 2,
        name="dilated",
        compiler_params=_cparams(2),
    )(bq, bkv, bkv, bias)


def _merge_kernel(x_ref, oa_ref, ob_ref, oc_ref, od_ref,
                  gpre_ref, gpost_ref, wg_ref, wba_ref, wbb_ref, wbc_ref, wbd_ref, wo_ref, out_ref):
    x = x_ref[0]
    h = _rms(x, gpre_ref[...]).astype(BF16)
    branches = ((oa_ref[0], wba_ref), (ob_ref[0], wbb_ref), (oc_ref[0], wbc_ref), (od_ref[0], wbd_ref))
    merged = None
    for n, (o_n, wb_ref) in enumerate(branches):
        term = _sigmoid(_dot(h, wg_ref[n])) * _dot(o_n, wb_ref[...])
        merged = term if merged is None else merged + term
    z = _dot(merged.astype(BF16), wo_ref[...])
    out_ref[0] = x + _rms(z, gpost_ref[...])


def _merge(x, oa, ob, oc, od, gpre, gpost, wg, wba, wbb, wbc, wbd, wo):
    b, s, d = x.shape
    tm = min(TM_PROJ, s)

    def rows(arr):
        return pl.BlockSpec((1, tm, arr.shape[2]), lambda bi, i: (bi, i, 0))

    acts = (x, oa, ob, oc, od)
    consts = (gpre, gpost, wg, wba, wbb, wbc, wbd, wo)
    return pl.pallas_call(
        _merge_kernel,
        grid=(b, s // tm),
        in_specs=[rows(a) for a in acts] + [_const_spec(c.shape) for c in consts],
        out_specs=rows(x),
        out_shape=jax.ShapeDtypeStruct(x.shape, F32),
        name="merge",
        compiler_params=_cparams(2),
    )(*acts, *consts)


def _ffn_kernel(x_ref, xp_ref, gpre_ref, gpost_ref, wup_ref, cw_ref, cb_ref, wdn_ref, out_ref, *, cuts):
    i = pl.program_id(1)
    tm = x_ref.shape[1]
    d_ff = wdn_ref.shape[0]
    x = x_ref[0]
    halo = jnp.where(i > 0, xp_ref[0], 0.0)
    he = _rms(jnp.concatenate([halo, x], axis=0), gpre_ref[...]).astype(BF16)

    def conv(c0, chunk):
        u = _dot(he, wup_ref[:, c0:c0 + chunk])
        w = cw_ref[:, c0:c0 + chunk]
        out = cb_ref[:, c0:c0 + chunk] + w[0:1] * pltpu.roll(u, 2, 0)[CONV_HALO:]
        out = out + w[1:2] * pltpu.roll(u, 1, 0)[CONV_HALO:]
        return out + w[2:3] * u[CONV_HALO:]

    y = jnp.zeros((tm, x.shape[1]), F32)
    for c0, c1 in zip(cuts[:-1], cuts[1:]):
        gate = conv(c0, c1 - c0)
        val = conv(d_ff + c0, c1 - c0)
        act = gate * _sigmoid(gate) * val
        y = y + _dot(act.astype(BF16), wdn_ref[c0:c1, :])
    out_ref[0] = x + _rms(y, gpost_ref[...])


def _ffn(x, gpre, gpost, w_up, conv_w, conv_b, w_down):
    b, s, d = x.shape
    tm = min(TM_PROJ, s)
    d_ff = w_down.shape[0]
    n_tiles = d_ff // MXU_WIDTH
    cuts = (0, (n_tiles // 2) * MXU_WIDTH, d_ff) if d_ff % MXU_WIDTH == 0 and n_tiles > 1 else (0, d_ff)
    halo_blocks = tm // CONV_HALO
    return pl.pallas_call(
        functools.partial(_ffn_kernel, cuts=cuts),
        grid=(b, s // tm),
        in_specs=[pl.BlockSpec((1, tm, d), lambda bi, i: (bi, i, 0)),
                  pl.BlockSpec((1, CONV_HALO, d), lambda bi, i: (bi, jnp.maximum(i * halo_blocks - 1, 0), 0)),
                  _const_spec(gpre.shape), _const_spec(gpost.shape), _const_spec(w_up.shape),
                  _const_spec(conv_w.shape), _const_spec(conv_b.shape), _const_spec(w_down.shape)],
        out_specs=pl.BlockSpec((1, tm, d), lambda bi, i: (bi, i, 0)),
        out_shape=jax.ShapeDtypeStruct(x.shape, F32),
        name="ffn",
        compiler_params=_cparams(2),
    )(x, x, gpre, gpost, w_up, conv_w, conv_b, w_down)


def _rel_bucket(dist):
    dist = jnp.maximum(dist, 0)
    d = jnp.maximum(dist, 1).astype(F32)
    large = BUCKET_EXACT + (jnp.log(d / BUCKET_EXACT) / math.log(BUCKET_MAX_DIST / BUCKET_EXACT)
                            * (N_BUCKETS - BUCKET_EXACT)).astype(jnp.int32)
    large = jnp.minimum(large, N_BUCKETS - 1)
    return jnp.where(dist < BUCKET_EXACT, dist, large)


def _in_proj_columns():
    src = np.full((N_SLOTS * LANES,), -1, np.int64)
    scl = np.ones((N_SLOTS * LANES,), np.float32)
    o_aq = 0
    o_akv = o_aq + A_HEADS * HEAD_DIM
    o_ag = o_akv + 6 * HEAD_DIM
    o_b = o_ag + 3 * A_HEADS
    o_c = o_b + 3 * B_HEADS * HEAD_DIM
    o_cf = o_c + 3 * C_HEADS * HEAD_DIM
    o_dq = o_cf + C_HEADS
    o_dkv = o_dq + D_Q_LORA
    o_dkr = o_dkv + D_KV_LORA
    e = np.arange(HEAD_DIM)
    q_scale = HEAD_DIM ** -0.5 * LOG2E

    def put(slot, lane0, cols, scale=1.0):
        dst = slot * LANES + lane0 + np.arange(len(cols))
        src[dst] = cols
        scl[dst] = scale

    for h in range(A_HEADS):
        put(G_AQ + h, 0, o_aq + h * HEAD_DIM + e, q_scale)
    for n, slot in enumerate((G_CMP, G_SLC, G_WIN)):
        put(slot, 0, o_akv + (2 * n) * HEAD_DIM + e)
        put(slot, HEAD_DIM, o_akv + (2 * n + 1) * HEAD_DIM + e)
    for h in range(B_HEADS):
        odd = h % B_HPG
        put(G_BQ + h // B_HPG, odd * HEAD_DIM, o_b + h * HEAD_DIM + e, q_scale)
        put(G_BKV + h, odd * HEAD_DIM, o_b + (B_HEADS + h) * HEAD_DIM + e)
        put(G_BKV + h, (1 - odd) * HEAD_DIM, o_b + (2 * B_HEADS + h) * HEAD_DIM + e)
    for h in range(C_HEADS):
        put(G_CQ + h, 0, o_c + h * HEAD_DIM + e, q_scale)
        put(G_CKV + h, 0, o_c + (C_HEADS + h) * HEAD_DIM + e)
        put(G_CKV + h, HEAD_DIM, o_c + (2 * C_HEADS + h) * HEAD_DIM + e)
    put(G_DQL, 0, o_dq + np.arange(D_Q_LORA))
    put(G_DKVL, 0, o_dkv + np.arange(D_KV_LORA))
    put(G_S1, LANE_GATE, o_ag + np.arange(3 * A_HEADS))
    put(G_S1, LANE_FORGET, o_cf + np.arange(C_HEADS))
    put(G_S1, LANE_ROPE, o_dkr + np.arange(D_ROPE))
    half = D_ROPE // 2
    put(G_S2, LANE_ROPE, o_dkr + half + np.arange(half), -1.0)
    put(G_S2, LANE_ROPE + half, o_dkr + np.arange(half))
    return src, scl


def _gather_cols(w, src, scl):
    pieces = []
    n = len(src)
    a = 0
    while a < n:
        e = a + 1
        if src[a] < 0:
            while e < n and src[e] < 0:
                e += 1
            pieces.append(jnp.zeros((w.shape[0], e - a), w.dtype))
        else:
            while e < n and src[e] == src[e - 1] + 1 and scl[e] == scl[a]:
                e += 1
            run = w[:, int(src[a]):int(src[a]) + (e - a)]
            pieces.append(run if scl[a] == 1.0 else run * float(scl[a]))
        a = e
    return jnp.concatenate(pieces, axis=1)


def _mla_weight_columns():
    per_q = D_NOPE + D_ROPE
    half = D_ROPE // 2
    qa = np.full((D_HEADS * LANES,), -1, np.int64)
    qb = np.full((D_HEADS * LANES,), -1, np.int64)
    qb_s = np.ones((D_HEADS * LANES,), np.float32)
    ka = np.full((D_HEADS * LANES,), -1, np.int64)
    va = np.full((D_HEADS * LANES,), -1, np.int64)
    for h in range(D_HEADS):
        qa[h * LANES + np.arange(per_q)] = h * per_q + np.arange(per_q)
        rot = h * per_q + D_NOPE
        qb[h * LANES + D_NOPE + np.arange(half)] = rot + half + np.arange(half)
        qb_s[h * LANES + D_NOPE + np.arange(half)] = -1.0
        qb[h * LANES + D_NOPE + half + np.arange(half)] = rot + np.arange(half)
        ka[h * LANES + np.arange(D_NOPE)] = h * (D_NOPE + D_VDIM) + np.arange(D_NOPE)
        va[h * LANES + np.arange(D_VDIM)] = h * (D_NOPE + D_VDIM) + D_NOPE + np.arange(D_VDIM)
    ones = np.ones_like(qb_s)
    return (qa, ones), (qb, qb_s), (ka, ones), (va, ones)


def _pad_branch_rows(w, n_heads, lane0):
    d = w.shape[1]
    out = jnp.zeros((n_heads, LANES, d), w.dtype)
    out = out.at[:, lane0:lane0 + HEAD_DIM, :].set(w.reshape(n_heads, HEAD_DIM, d))
    return out.reshape(n_heads * LANES, d)


def _rope_tables(s):
    inv_freq = ROPE_THETA ** (-jnp.arange(0, D_ROPE, 2, dtype=F32) / D_ROPE)
    ang = jnp.arange(s, dtype=F32)[:, None] * inv_freq[None, :]
    cos2 = jnp.concatenate([jnp.cos(ang)] * 2, axis=1)
    sin2 = jnp.concatenate([jnp.sin(ang)] * 2, axis=1)
    scale = (D_NOPE + D_ROPE) ** -0.5 * LOG2E
    z_lo = jnp.zeros((s, D_NOPE), F32)
    z_hi = jnp.zeros((s, LANES - D_NOPE - D_ROPE), F32)
    c_q = jnp.concatenate([jnp.full((s, D_NOPE), scale, F32), scale * cos2, z_hi], axis=1)
    s_q = jnp.concatenate([z_lo, scale * sin2, z_hi], axis=1)
    c_k = jnp.concatenate([z_lo, cos2, z_hi], axis=1)
    s_k = jnp.concatenate([z_lo, sin2, z_hi], axis=1)
    return jnp.concatenate([c_q, s_q, c_k, s_k], axis=1)


def _toeplitz(vec, n_rows, n_cols, off, step=1):
    base = vec[:, ::step]
    last = base.shape[1] - 1
    lo, hi = off - (n_cols - 1), off + n_rows - 1
    core = base[:, max(lo, 0):min(hi, last) + 1]
    front = jnp.repeat(base[:, :1], max(0, -lo), axis=1)
    back = jnp.repeat(base[:, last:], max(0, hi - last), axis=1)
    g = jnp.concatenate([front, core, back], axis=1)
    lg = n_rows + n_cols - 1
    t = jnp.tile(g, (1, n_rows + 1))[:, :n_rows * (lg + 1)].reshape(-1, n_rows, lg + 1)
    return t[:, :, :n_cols][:, :, ::-1]


def _bias_tables(rel_table, s):
    bucket = _rel_bucket(jnp.arange(s))
    hit = bucket[None, :, None] == jnp.arange(N_BUCKETS)[None, None, :]
    by_dist = jnp.sum(jnp.where(hit, LOG2E * rel_table.T[:, None, :], 0.0), axis=-1)
    tqa = min(TQ_NSA, s)
    dist = jnp.arange(tqa)[:, None] + A_WINDOW - jnp.arange(A_WINDOW + tqa)[None, :]
    ok = (dist >= 0) & (dist < A_WINDOW)
    bias_w = jnp.where(ok[None], _toeplitz(by_dist[:A_HEADS], tqa, A_WINDOW + tqa, A_WINDOW), NEG_INF)
    bias_w = bias_w.reshape(A_HEADS * tqa, A_WINDOW + tqa)
    tf = min(BIAS_TILE, s)
    n_off = min(s // tf, -(-(BUCKET_MAX_DIST + tf - 1) // tf) + 1)
    bias_s = jnp.stack([_toeplitz(by_dist[:A_HEADS], tf, tf, n * tf) for n in range(n_off)])
    m = jnp.arange(TQ_DIL)[:, None] + TQ_DIL - jnp.arange(2 * TQ_DIL)[None, :]
    bias_d = []
    for g, (window, dil) in enumerate(B_GROUPS):
        ok = (m >= 0) & (m <= window // dil)
        heads = by_dist[A_HEADS + g * B_HPG:A_HEADS + (g + 1) * B_HPG]
        bias_d.append(jnp.where(ok[None], _toeplitz(heads, TQ_DIL, 2 * TQ_DIL, TQ_DIL, dil), NEG_INF))
    return bias_w, bias_s, jnp.stack(bias_d)


def _selection_constants(s):
    nc = s // A_CMP_STRIDE
    n_cmp = (s - A_CMP_LEN) // A_CMP_STRIDE + 1
    n_sel = s // A_SEL_BLOCK
    c = np.arange(nc)[:, None]
    j = np.arange(n_sel)[None, :]
    c_start = c * A_CMP_STRIDE
    overlap = ((c_start <= j * A_SEL_BLOCK + A_SEL_BLOCK - 1) & (c_start + A_CMP_LEN - 1 >= j * A_SEL_BLOCK)
               & (c < n_cmp))
    expand = (np.arange(s)[None, :] // A_SEL_BLOCK) == np.arange(n_sel)[:, None]
    return jnp.asarray(overlap.T, BF16), jnp.asarray(expand, BF16)


def kernel(x, rel_bias_table, norm_attn_pre, norm_attn_post, norm_ffn_pre, norm_ffn_post, w_in, nsa_cmp_pos, nsa_phi_k_w1, nsa_phi_k_w2, nsa_phi_v_w1, nsa_phi_v_w2, fox_forget_bias, mla_q_norm, mla_kv_norm, mla_w_uq, mla_w_ukv, w_branch_a, w_branch_b, w_branch_c, w_branch_d, w_merge_gate, w_o, ffn_w_up, ffn_conv_w, ffn_conv_b, ffn_w_down):
    b, s, d = x.shape
    depth = w_in.shape[0]
    assert s % (TQ_DIL * B_GROUPS[-1][1]) == 0, "every dilation class needs whole 128-row tiles"

    in_src, in_scl = _in_proj_columns()
    (qa_i, qa_s), (qb_i, qb_s), (ka_i, ka_s), (va_i, va_s) = _mla_weight_columns()
    rope_tab = _rope_tables(s)
    bias_w, bias_s, bias_d = _bias_tables(rel_bias_table, s)
    overlap, expand = _selection_constants(s)
    half = A_CMP_STRIDE * HEAD_DIM

    for l in range(depth):
        w_ext = _gather_cols(w_in[l], in_src, in_scl).astype(BF16)
        fbias = jnp.zeros((1, LANES), F32).at[0, LANE_FORGET:LANE_FORGET + C_HEADS].set(fox_forget_bias[l])
        (aq, cmp, slc, slc_v, win, bq, bkv, cq, ckv, c_v, dq, dk, dv, small) = _inproj(
            x, norm_attn_pre[l][None], w_ext, rope_tab, fbias,
            mla_q_norm[l][None], mla_kv_norm[l][None],
            _gather_cols(mla_w_uq[l], qa_i, qa_s).astype(BF16),
            _gather_cols(mla_w_uq[l], qb_i, qb_s).astype(BF16),
            _gather_cols(mla_w_ukv[l], ka_i, ka_s).astype(BF16),
            _gather_cols(mla_w_ukv[l], va_i, va_s).astype(BF16))

        nc = s // A_CMP_STRIDE
        w2k_pad = jnp.pad(nsa_phi_k_w2[l], ((0, 0), (0, LANES - HEAD_DIM))).astype(BF16)
        w2v_pad = jnp.pad(nsa_phi_v_w2[l], ((0, 0), (LANES - HEAD_DIM, 0))).astype(BF16)
        kvc = _compress(cmp[..., :HEAD_DIM].reshape(b, nc, half), cmp[..., HEAD_DIM:].reshape(b, nc, half),
                        nsa_cmp_pos[l].reshape(2, half),
                        nsa_phi_k_w1[l].astype(BF16), nsa_phi_v_w1[l].astype(BF16), w2k_pad, w2v_pad)
        kvw_pad = jnp.pad(win, ((0, 0), (A_WINDOW, 0), (0, 0)))
        part, sel = _nsa_cmp_win(aq, kvc, kvw_pad, small, overlap, bias_w)
        o_a = _flash("sel", aq, slc, slc_v, sel=sel, expand=expand, bias=bias_s, part=part, small=small)

        o_b = _dilated(bq, bkv, bias_d)

        cum_t = LOG2E * jnp.swapaxes(small[..., LANE_FORGET:LANE_FORGET + C_HEADS], 1, 2)
        o_c = _flash("fox", cq, ckv, c_v, cum_t=cum_t)
        o_d = _flash("mla", dq, dk, dv)

        x = _merge(x, o_a, o_b, o_c, o_d,
                   norm_attn_pre[l][None], norm_attn_post[l][None],
                   w_merge_gate[l].astype(BF16),
                   _pad_branch_rows(w_branch_a[l], A_HEADS, HEAD_DIM).astype(BF16),
                   jnp.concatenate([w_branch_b[l][HEAD_DIM:], w_branch_b[l][:HEAD_DIM]]).astype(BF16),
                   _pad_branch_rows(w_branch_c[l], C_HEADS, HEAD_DIM).astype(BF16),
                   _pad_branch_rows(w_branch_d[l], D_HEADS, 0).astype(BF16),
                   w_o[l].astype(BF16))
        x = _ffn(x, norm_ffn_pre[l][None], norm_ffn_post[l][None], ffn_w_up[l].astype(BF16),
                 ffn_conv_w[l], ffn_conv_b[l][None], ffn_w_down[l].astype(BF16))
    return x
```

```python
import functools
import math

import numpy as np
import jax
import jax.numpy as jnp
from jax import lax
from jax.experimental import pallas as pl
from jax.experimental.pallas import tpu as pltpu

F32 = jnp.float32
BF16 = jnp.bfloat16

LANES = 128
MXU_WIDTH = 256
HEAD_DIM = 64
RMS_EPS = 1e-6
NEG_INF = -1e30
FORCE_SCORE = 1e9
LOG2E = math.log2(math.e)
VMEM_LIMIT_BYTES = 56 * 1024 * 1024

N_BUCKETS = 32
BUCKET_EXACT = 16
BUCKET_MAX_DIST = 2048

A_HEADS = 4
A_CMP_LEN = 32
A_CMP_STRIDE = 16
A_PHI_HIDDEN = 128
A_SEL_BLOCK = 64
A_SEL_TOPK = 16
A_WINDOW = 512
B_GROUPS = ((128, 1), (512, 4), (2048, 16))
B_HPG = 2
B_HEADS = B_HPG * len(B_GROUPS)
C_HEADS = 4
D_HEADS = 4
D_Q_LORA = 256
D_KV_LORA = 128
D_NOPE = 64
D_ROPE = 32
D_VDIM = 64
ROPE_THETA = 10000.0
N_BRANCHES = 4

G_AQ = 0
G_CMP = G_AQ + A_HEADS
G_SLC = G_CMP + 1
G_WIN = G_SLC + 1
G_BQ = G_WIN + 1
G_BKV = G_BQ + len(B_GROUPS)
G_CQ = G_BKV + B_HEADS
G_CKV = G_CQ + C_HEADS
G_DQL = G_CKV + C_HEADS
G_DKVL = G_DQL + 2
G_S1 = G_DKVL + 1
G_S2 = G_S1 + 1
N_SLOTS = G_S2 + 1
LANE_GATE = 0
LANE_FORGET = 3 * A_HEADS
LANE_ROPE = 64

TM_PROJ = 512
TQ_NSA_STEP = 256
TQ_FLASH = 512
VT_ROWS = 80
CHUNK_SLOTS = 8
LANE_EXTRA = HEAD_DIM
N_EXTRA = 3
BIAS_TILE = 256
TQ_NSA = 128
TQ_DIL = 128
CONV_HALO = 16


def _cparams(n_grid):
    return pltpu.CompilerParams(dimension_semantics=("arbitrary",) * n_grid,
                                vmem_limit_bytes=VMEM_LIMIT_BYTES)


def _const_spec(shape):
    nd = len(shape)
    return pl.BlockSpec(shape, lambda *_: (0,) * nd, pipeline_mode=pl.Buffered(1))


def _rms(x, gain):
    return x * lax.rsqrt(jnp.mean(x * x, axis=-1, keepdims=True) + RMS_EPS) * gain


def _sigmoid(x):
    return 1.0 / (1.0 + jnp.exp(-x))


def _dot(a, b):
    return jnp.dot(a, b, preferred_element_type=F32)


def _dot_nt(a, b):
    return lax.dot_general(a, b, (((1,), (1,)), ((), ())), preferred_element_type=F32)


def _slot(h):
    return slice(h * LANES, (h + 1) * LANES)


def _inproj_kernel(x_ref, g_ref, w_ref, wvt_ref, tab_ref, fb_ref, qn_ref, kvn_ref, wq_ref, wk_ref, wdvt_ref,
                   place_ref,
                   aq_ref, cmp_ref, slc_ref, slcvt_ref, win_ref, bq_ref, bkv_ref, cq_ref, ck_ref, cvt_ref,
                   dq_ref, dk_ref, dvt_ref, small_ref, carry_ref):
    i = pl.program_id(1)
    tm = x_ref.shape[1]
    h = _rms(x_ref[0], g_ref[...]).astype(BF16)

    n_slots = w_ref.shape[1] // LANES
    wide = [_dot(h, w_ref[:, c * LANES:min(c + CHUNK_SLOTS, n_slots) * LANES])
            for c in range(0, n_slots, CHUNK_SLOTS)]

    def proj(g0, n):
        pieces = []
        g = g0
        while g < g0 + n:
            c, off = divmod(g, CHUNK_SLOTS)
            take = min(g0 + n - g, CHUNK_SLOTS - off)
            pieces.append(wide[c][:, off * LANES:(off + take) * LANES])
            g += take
        return pieces[0] if len(pieces) == 1 else jnp.concatenate(pieces, axis=1)

    def store_transposed(out_ref, vt, n_heads):
        for n in range(n_heads):
            out_ref[0, n * VT_ROWS:n * VT_ROWS + HEAD_DIM, :] = vt[n * HEAD_DIM:(n + 1) * HEAD_DIM].astype(BF16)
            out_ref[0, n * VT_ROWS + HEAD_DIM:(n + 1) * VT_ROWS, :] = jnp.ones((VT_ROWS - HEAD_DIM, tm), BF16)

    aq_ref[0] = proj(G_AQ, A_HEADS).astype(BF16)
    cmp_ref[0] = proj(G_CMP, 1)
    slc_ref[0] = proj(G_SLC, 1).astype(BF16)
    win_ref[0] = proj(G_WIN, 1).astype(BF16)
    b_all = proj(G_BQ, len(B_GROUPS) + B_HEADS)
    for n in range(len(B_GROUPS)):
        bq_ref[0, n] = b_all[:, _slot(n)]
    for n in range(B_HEADS):
        bkv_ref[0, n] = b_all[:, _slot(len(B_GROUPS) + n)]

    vt = _dot_nt(wvt_ref[...], h)
    store_transposed(slcvt_ref, vt[0:HEAD_DIM], 1)
    store_transposed(cvt_ref, vt[HEAD_DIM:], C_HEADS)

    s1 = proj(G_S1, 1)
    s2 = proj(G_S2, 1)

    @pl.when(i == 0)
    def _():
        carry_ref[...] = jnp.zeros_like(carry_ref)

    def split3(v):
        hi = v.astype(BF16)
        r1 = v - hi.astype(F32)
        mid = r1.astype(BF16)
        return hi, mid, (r1 - mid.astype(F32)).astype(BF16)

    z = s1 + fb_ref[...]
    logf = jnp.minimum(z, 0.0) - jnp.log(1.0 + jnp.exp(-jnp.abs(z)))
    row = lax.broadcasted_iota(jnp.int32, (tm, tm), 0)
    col = lax.broadcasted_iota(jnp.int32, (tm, tm), 1)
    tri = (row >= col).astype(BF16)
    cum = sum(_dot(tri, part) for part in split3(logf)) + carry_ref[0:1, :]
    carry_ref[0:1, :] = cum[tm - 1:tm, :]
    lane = lax.broadcasted_iota(jnp.int32, (tm, LANES), 1)
    is_forget = (lane >= LANE_FORGET) & (lane < LANE_FORGET + C_HEADS)
    small_ref[0] = jnp.where(is_forget, cum, s1)

    extras = sum(_dot(part, place_ref[n]) for n, part in enumerate(split3(-LOG2E * cum)))
    lane_c = lax.broadcasted_iota(jnp.int32, (tm, C_HEADS * LANES), 1) & (LANES - 1)
    is_extra = (lane_c >= LANE_EXTRA) & (lane_c < LANE_EXTRA + N_EXTRA)
    cq_ref[0] = jnp.where(is_extra, 1.0, proj(G_CQ, C_HEADS)).astype(BF16)
    ck_ref[0] = jnp.where(is_extra, extras, proj(G_CKV, C_HEADS)).astype(BF16)

    tab = tab_ref[...]
    c_q, s_q, c_k, s_k = (tab[:, _slot(n)] for n in range(4))
    qn = _rms(proj(G_DQL, 2), qn_ref[...]).astype(BF16)
    qab = _dot(qn, wq_ref[...])
    kvn = _rms(proj(G_DKVL, 1), kvn_ref[...]).astype(BF16)
    ka = _dot(kvn, wk_ref[...])
    k_rot = s1 * c_k + s2 * s_k
    for hd in range(D_HEADS):
        q_rot = qab[:, _slot(hd)] * c_q + qab[:, _slot(D_HEADS + hd)] * s_q
        dq_ref[0, :, _slot(hd)] = q_rot.astype(BF16)
        dk_ref[0, :, _slot(hd)] = (ka[:, _slot(hd)] + k_rot).astype(BF16)
    store_transposed(dvt_ref, _dot_nt(wdvt_ref[...], kvn), D_HEADS)


def _inproj(x, gain, w_ext, wvt, tab, fbias, qn_g, kvn_g, wq, wk, wdvt, place):
    b, s, d = x.shape
    tm = min(TM_PROJ, s)

    def out(kind, n, dtype):
        if kind == "rows":
            return (pl.BlockSpec((1, tm, n * LANES), lambda bi, i: (bi, i, 0)),
                    jax.ShapeDtypeStruct((b, s, n * LANES), dtype))
        if kind == "slots":
            return (pl.BlockSpec((1, n, tm, LANES), lambda bi, i: (bi, 0, i, 0)),
                    jax.ShapeDtypeStruct((b, n, s, LANES), dtype))
        return (pl.BlockSpec((1, n * VT_ROWS, tm), lambda bi, i: (bi, 0, i)),
                jax.ShapeDtypeStruct((b, n * VT_ROWS, s), dtype))

    outs = [out("rows", A_HEADS, BF16), out("rows", 1, F32), out("rows", 1, BF16), out("vt", 1, BF16),
            out("rows", 1, BF16), out("slots", len(B_GROUPS), F32), out("slots", B_HEADS, F32),
            out("rows", C_HEADS, BF16), out("rows", C_HEADS, BF16), out("vt", C_HEADS, BF16),
            out("rows", D_HEADS, BF16), out("rows", D_HEADS, BF16), out("vt", D_HEADS, BF16),
            out("rows", 1, F32)]
    consts = (gain, w_ext, wvt)
    consts2 = (fbias, qn_g, kvn_g, wq, wk, wdvt, place)
    return pl.pallas_call(
        _inproj_kernel,
        grid=(b, s // tm),
        in_specs=[pl.BlockSpec((1, tm, d), lambda bi, i: (bi, i, 0))]
        + [_const_spec(c.shape) for c in consts]
        + [pl.BlockSpec((tm, 4 * LANES), lambda bi, i: (i, 0))]
        + [_const_spec(c.shape) for c in consts2],
        out_specs=[o[0] for o in outs],
        out_shape=[o[1] for o in outs],
        scratch_shapes=[pltpu.VMEM((8, LANES), F32)],
        name="inproj",
        compiler_params=_cparams(2),
    )(x, *consts, tab, *consts2)


def _gelu_tanh(x):
    return 0.5 * x * (1.0 + jnp.tanh(math.sqrt(2.0 / math.pi) * (x + 0.044715 * (x * x * x))))


def _compress_kernel(kc_ref, vc_ref, pos_ref, w1k_ref, w1v_ref, w2k_ref, w2v_ref, out_ref):
    nc = kc_ref.shape[1]
    half = A_CMP_STRIDE * HEAD_DIM
    pos_a = pos_ref[0:1, :]
    pos_b = pos_ref[1:2, :]

    def phi(chunks, w1_ref, w2_ref):
        u = _dot((chunks + pos_a).astype(BF16), w1_ref[0:half, :])
        v = _dot((chunks + pos_b).astype(BF16), w1_ref[half:2 * half, :])
        hid = u + pltpu.roll(v, nc - 1, 0)
        return _dot(_gelu_tanh(hid).astype(BF16), w2_ref[...])

    out_ref[0] = (phi(kc_ref[0], w1k_ref, w2k_ref) + phi(vc_ref[0], w1v_ref, w2v_ref)).astype(BF16)


def _compress(kc_in, vc_in, pos2, w1k, w1v, w2k_pad, w2v_pad):
    b, nc, width = kc_in.shape
    blk = pl.BlockSpec((1, nc, width), lambda bi: (bi, 0, 0))
    return pl.pallas_call(
        _compress_kernel,
        grid=(b,),
        in_specs=[blk, blk, _const_spec(pos2.shape), _const_spec(w1k.shape), _const_spec(w1v.shape),
                  _const_spec(w2k_pad.shape), _const_spec(w2v_pad.shape)],
        out_specs=pl.BlockSpec((1, nc, LANES), lambda bi: (bi, 0, 0)),
        out_shape=jax.ShapeDtypeStruct((b, nc, LANES), BF16),
        name="compress",
        compiler_params=_cparams(1),
    )(kc_in, vc_in, pos2, w1k, w1v, w2k_pad, w2v_pad)


def _nsa_cmp_win_kernel(q_ref, kvc_ref, kvw_ref, small_ref, ovl_ref, bias_ref,
                        part_ref, sel_ref, *, n_top):
    tq = min(TQ_NSA, q_ref.shape[1])
    for u in range(q_ref.shape[1] // tq):
        _nsa_cmp_win_tile(u * tq, tq, q_ref, kvc_ref, kvw_ref, small_ref, ovl_ref, bias_ref,
                          part_ref, sel_ref, n_top)


def _nsa_cmp_win_tile(r0, tq, q_ref, kvc_ref, kvw_ref, small_ref, ovl_ref, bias_ref, part_ref, sel_ref, n_top):
    nc = kvc_ref.shape[1]
    n_sel = ovl_ref.shape[0]
    t0 = pl.multiple_of(pl.program_id(1) * q_ref.shape[1] + r0, tq)
    tile = slice(r0, r0 + tq)
    q4 = jnp.concatenate([q_ref[0, tile, _slot(h)] for h in range(A_HEADS)], axis=0)
    rows = A_HEADS * tq

    kvc = kvc_ref[0]
    s = _dot_nt(q4, kvc)
    t_row = t0 + (lax.broadcasted_iota(jnp.int32, (rows, nc), 0) & (tq - 1))
    c_end = lax.broadcasted_iota(jnp.int32, (rows, nc), 1) * A_CMP_STRIDE + (A_CMP_LEN - 1)
    valid = c_end <= t_row
    s = jnp.where(valid, s, NEG_INF)
    m = jnp.max(s, axis=1, keepdims=True)
    p = jnp.where(valid, jnp.exp2(s - m), 0.0)
    den = jnp.maximum(jnp.sum(p, axis=1, keepdims=True), 1e-30)
    p = p / den
    o_c = _dot(p.astype(BF16), kvc)

    p_sum = p[0:tq] + p[tq:2 * tq] + p[2 * tq:3 * tq] + p[3 * tq:4 * tq]
    p_hi = p_sum.astype(BF16)
    p_lo = (p_sum - p_hi.astype(F32)).astype(BF16)
    imp_t = _dot_nt(ovl_ref[...], p_hi) + _dot_nt(ovl_ref[...], p_lo)

    j_idx = lax.broadcasted_iota(jnp.int32, (n_sel, tq), 0)
    cur = lax.shift_right_logical(t0 + lax.broadcasted_iota(jnp.int32, (n_sel, tq), 1), 6)
    forced = (j_idx == 0) | (j_idx == cur) | (j_idx == cur - 1)
    imp_t = jnp.where(forced, FORCE_SCORE, jnp.where(j_idx > cur, -FORCE_SCORE, imp_t))
    sub8 = 8
    groups = [imp_t[v * sub8:(v + 1) * sub8] for v in range(n_sel // sub8)]
    ranks = [jnp.zeros((sub8, tq), F32) for _ in groups]
    in_group = lax.broadcasted_iota(jnp.int32, (sub8, tq), 0)
    for c in range(n_sel):
        row_c = imp_t[c:c + 1, :]
        for v, grp in enumerate(groups):
            if v > c // sub8:
                beats = row_c >= grp
            elif v < c // sub8:
                beats = row_c > grp
            else:
                beats = (row_c > grp) | ((row_c == grp) & (in_group > c % sub8))
            ranks[v] = ranks[v] + jnp.where(beats, 1.0, 0.0)
    rank = jnp.concatenate(ranks, axis=0)
    sel_t = jnp.where(rank < float(n_top), 1.0, 0.0).astype(BF16)
    sel_ref[0, :, tile] = sel_t

    span = A_WINDOW + tq
    kvw = kvw_ref[0, pl.ds(t0, span), :]
    sw = _dot_nt(q4, kvw) + bias_ref[...]
    kpos = t0 - A_WINDOW + lax.broadcasted_iota(jnp.int32, (rows, span), 1)
    sw = jnp.where(kpos >= 0, sw, NEG_INF)
    mw = jnp.max(sw, axis=1, keepdims=True)
    pw = jnp.exp2(sw - mw)
    o_w = _dot(pw.astype(BF16), kvw) / jnp.sum(pw, axis=1, keepdims=True)

    g = _sigmoid(small_ref[0, tile])
    gated = []
    for h in range(A_HEADS):
        g_c = g[:, LANE_GATE + 3 * h:LANE_GATE + 3 * h + 1]
        g_w = g[:, LANE_GATE + 3 * h + 2:LANE_GATE + 3 * h + 3]
        gated.append(g_c * o_c[h * tq:(h + 1) * tq] + g_w * o_w[h * tq:(h + 1) * tq])
    low = lax.broadcasted_iota(jnp.int32, (tq, LANES), 1) < HEAD_DIM
    for pair in range(A_HEADS // 2):
        part_ref[0, tile, _slot(pair)] = jnp.where(low, pltpu.roll(gated[2 * pair], HEAD_DIM, 1),
                                                   gated[2 * pair + 1])


def _nsa_cmp_win(aq, kvc, kvw_pad, small, ovl, bias_w):
    b, s, _ = aq.shape
    tq = min(TQ_NSA_STEP, s)
    nc = kvc.shape[1]
    n_sel = ovl.shape[0]
    n_top = min(A_SEL_TOPK, n_sel)
    return pl.pallas_call(
        functools.partial(_nsa_cmp_win_kernel, n_top=n_top),
        grid=(b, s // tq),
        in_specs=[pl.BlockSpec((1, tq, A_HEADS * LANES), lambda bi, i: (bi, i, 0)),
                  pl.BlockSpec((1, nc, LANES), lambda bi, i: (bi, 0, 0)),
                  pl.BlockSpec((1, s + A_WINDOW, LANES), lambda bi, i: (bi, 0, 0)),
                  pl.BlockSpec((1, tq, LANES), lambda bi, i: (bi, i, 0)),
                  _const_spec(ovl.shape),
                  _const_spec(bias_w.shape)],
        out_specs=[pl.BlockSpec((1, tq, A_HEADS * HEAD_DIM), lambda bi, i: (bi, i, 0)),
                   pl.BlockSpec((1, n_sel, tq), lambda bi, i: (bi, 0, i))],
        out_shape=[jax.ShapeDtypeStruct((b, s, A_HEADS * HEAD_DIM), F32),
                   jax.ShapeDtypeStruct((b, n_sel, s), BF16)],
        name="nsa_cmp_win",
        compiler_params=_cparams(2),
    )(aq, kvc, kvw_pad, small, ovl, bias_w)


def _flash_kernel(*refs, mode, nh):
    if mode == "sel":
        q_ref, k_ref, vt_ref, selt_ref, et_ref, bias_ref, part_ref, small_ref, o_ref, m_ref, acc_ref, sa_ref, sb_ref = refs
    else:
        q_ref, k_ref, vt_ref, o_ref, m_ref, acc_ref, sa_ref, sb_ref = refs
    i = pl.program_id(1)
    tq = q_ref.shape[1]
    tk = tq // 2
    shared_kv = mode == "sel"

    m_ref[...] = jnp.full(m_ref.shape, NEG_INF, F32)
    acc_ref[...] = jnp.zeros(acc_ref.shape, F32)

    lead = lax.broadcasted_iota(jnp.int32, (tk, tq), 1) - lax.broadcasted_iota(jnp.int32, (tk, tq), 0)

    def logits_of(j, s_ref, diag=None):
        k0 = pl.multiple_of(j * tk, tk)
        keep = None if diag is None else (lead >= diag * tk)
        if mode == "sel":
            sel_keys = _dot(et_ref[pl.ds(k0, tk), :], selt_ref[0]) > 0.5
            keep = sel_keys if diag is None else (sel_keys & keep)
            n_off, bt = bias_ref.shape[0], bias_ref.shape[2]

            def bias_tile(h):
                strips = []
                for c in range(tk // bt):
                    offs = [jnp.clip(i * (tq // bt) + a - j * (tk // bt) - c, 0, n_off - 1)
                            for a in range(tq // bt)]
                    strips.append(jnp.concatenate([bias_ref[o, h] for o in offs], axis=1))
                return jnp.concatenate(strips, axis=0)
        for h in range(nh):
            s = _dot_nt(k_ref[0, pl.ds(k0, tk), _slot(0 if shared_kv else h)], q_ref[0, :, _slot(h)])
            if mode == "sel":
                s = s + bias_tile(h)
            if keep is not None:
                s = jnp.where(keep, s, NEG_INF)
            s_ref[h] = s

    def accumulate(j, s_ref):
        k0 = pl.multiple_of(j * tk, tk)
        probs = []
        for h in range(nh):
            s = s_ref[h]
            m_prev = m_ref[h]
            m_new = jnp.maximum(m_prev, jnp.max(s, axis=0, keepdims=True))
            p = jnp.exp2(s - m_new)
            alpha = jnp.exp2(m_prev - m_new)
            m_ref[h] = m_new
            probs.append((p.astype(BF16), alpha))
        for h in range(nh):
            v0 = 0 if shared_kv else h * VT_ROWS
            p, alpha = probs[h]
            acc_ref[h] = alpha * acc_ref[h] + _dot(vt_ref[0, v0:v0 + VT_ROWS, pl.ds(k0, tk)], p)

    @pl.when(i == 0)
    def _():
        logits_of(0, sa_ref, diag=0)

    @pl.when(i > 0)
    def _():
        logits_of(0, sa_ref)

    def tile_pair(p, carry):
        logits_of(2 * p + 1, sb_ref)
        accumulate(2 * p, sa_ref)
        logits_of(2 * p + 2, sa_ref)
        accumulate(2 * p + 1, sb_ref)
        return carry

    lax.fori_loop(0, i - 1, tile_pair, 0)

    @pl.when(i > 0)
    def _():
        logits_of(2 * i - 1, sb_ref)
        accumulate(2 * i - 2, sa_ref)
        logits_of(2 * i, sa_ref, diag=0)
        accumulate(2 * i - 1, sb_ref)

    logits_of(2 * i + 1, sb_ref, diag=1)
    accumulate(2 * i, sa_ref)
    accumulate(2 * i + 1, sb_ref)

    low = lax.broadcasted_iota(jnp.int32, (tq, LANES), 1) < HEAD_DIM
    if mode == "sel":
        g = _sigmoid(small_ref[0])
    for pair in range(nh // 2):
        halves = []
        for h in (2 * pair, 2 * pair + 1):
            acc = acc_ref[h]
            o_t = acc[0:HEAD_DIM] / acc[HEAD_DIM:HEAD_DIM + 1]
            halves.append(jnp.concatenate([o_t, jnp.zeros_like(o_t)], axis=0).T)
        o_pair = jnp.where(low, halves[0], pltpu.roll(halves[1], HEAD_DIM, 1))
        if mode == "sel":
            gate = [g[:, LANE_GATE + 3 * h + 1:LANE_GATE + 3 * h + 2] for h in (2 * pair, 2 * pair + 1)]
            o_pair = part_ref[0, :, _slot(pair)] + jnp.where(low, gate[0], gate[1]) * o_pair
        o_ref[0, :, _slot(pair)] = o_pair.astype(BF16)


def _flash(mode, q, k, vt, sel_t=None, expand_t=None, bias=None, part=None, small=None):
    b, s, qw = q.shape
    nh = qw // LANES
    tq = min(TQ_FLASH, s)
    ow = nh * HEAD_DIM

    def rows(width):
        return pl.BlockSpec((1, tq, width), lambda bi, i: (bi, i, 0))

    def seq_spec(arr):
        return pl.BlockSpec((1,) + arr.shape[1:], lambda bi, i: (bi, 0, 0))

    args = (q, k, vt)
    in_specs = [rows(qw), seq_spec(k), seq_spec(vt)]
    if mode == "sel":
        args += (sel_t, expand_t, bias, part, small)
        in_specs += [pl.BlockSpec((1, sel_t.shape[1], tq), lambda bi, i: (bi, 0, i)),
                     _const_spec(expand_t.shape), _const_spec(bias.shape), rows(ow), rows(LANES)]
    return pl.pallas_call(
        functools.partial(_flash_kernel, mode=mode, nh=nh),
        grid=(b, s // tq),
        in_specs=in_specs,
        out_specs=rows(ow),
        out_shape=jax.ShapeDtypeStruct((b, s, ow), BF16),
        scratch_shapes=[pltpu.VMEM((nh, 1, tq), F32), pltpu.VMEM((nh, VT_ROWS, tq), F32),
                        pltpu.VMEM((nh, tq // 2, tq), F32), pltpu.VMEM((nh, tq // 2, tq), F32)],
        name="flash_" + mode,
        compiler_params=_cparams(2),
    )(*args)


def _dilated_kernel(q_ref, kvp_ref, kvc_ref, bias_ref, o_ref, o_sc, l_sc):
    i = pl.program_id(1)
    rows = q_ref.shape[2]
    tb = TQ_DIL
    low = lax.broadcasted_iota(jnp.int32, (tb, LANES), 1) < HEAD_DIM
    col = lax.broadcasted_iota(jnp.int32, (tb, 2 * tb), 1)
    in_seq = (col >= tb) | (i > 0)
    for g, (_, dil) in enumerate(B_GROUPS):
        span = dil * tb
        for r in range(dil):
            for c in range(rows // span):
                cur = pl.ds(r + c * span, tb, stride=dil)
                if c > 0:
                    prev_ref, prev = kvc_ref, pl.ds(r + (c - 1) * span, tb, stride=dil)
                else:
                    prev_ref, prev = kvp_ref, pl.ds(rows - span + r, tb, stride=dil)
                q2 = q_ref[0, g, cur, :]
                outs, lses = [], []
                for h in range(B_HPG):
                    q_h = jnp.where(low if h == 0 else ~low, q2, 0.0).astype(BF16)
                    head = B_HPG * g + h
                    kv = jnp.concatenate([prev_ref[0, head, prev, :], kvc_ref[0, head, cur, :]],
                                         axis=0).astype(BF16)
                    s = _dot_nt(q_h, kv) + bias_ref[g, h]
                    if c == 0:
                        s = jnp.where(in_seq, s, NEG_INF)
                    m = jnp.max(s, axis=1, keepdims=True)
                    p = jnp.exp2(s - m)
                    den = jnp.sum(p, axis=1, keepdims=True)
                    outs.append(_dot(p.astype(BF16), kv) / den)
                    lses.append(m + jnp.log2(den))
                o_sc[g, cur, :] = jnp.where(low, outs[1], outs[0])
                l_sc[g, cur, :] = jnp.where(low, lses[1], lses[0])
    step = 2 * tb
    for t in range(rows // step):
        rs = slice(t * step, (t + 1) * step)
        l0, l1, l2 = l_sc[0, rs, :], l_sc[1, rs, :], l_sc[2, rs, :]
        mx = jnp.maximum(jnp.maximum(l0, l1), l2)
        e0, e1, e2 = jnp.exp2(l0 - mx), jnp.exp2(l1 - mx), jnp.exp2(l2 - mx)
        mixed = (e0 * o_sc[0, rs, :] + e1 * o_sc[1, rs, :] + e2 * o_sc[2, rs, :]) / (e0 + e1 + e2)
        o_ref[0, rs, :] = mixed.astype(BF16)


def _dilated(bq, bkv, bias):
    b, _, s, _ = bq.shape
    rows = TQ_DIL * B_GROUPS[-1][1]
    n_groups = len(B_GROUPS)

    def tile(n_slots, prev):
        if prev:
            return pl.BlockSpec((1, n_slots, rows, LANES), lambda bi, i: (bi, 0, jnp.maximum(i - 1, 0), 0))
        return pl.BlockSpec((1, n_slots, rows, LANES), lambda bi, i: (bi, 0, i, 0))

    return pl.pallas_call(
        _dilated_kernel,
        grid=(b, s // rows),
        in_specs=[tile(bq.shape[1], False), tile(bkv.shape[1], True), tile(bkv.shape[1], False),
                  _const_spec(bias.shape)],
        out_specs=pl.BlockSpec((1, rows, LANES), lambda bi, i: (bi, i, 0)),
        out_shape=jax.ShapeDtypeStruct((b, s, LANES), BF16),
        scratch_shapes=[pltpu.VMEM((n_groups, rows, LANES), F32)] * 2,
        name="dilated",
        compiler_params=_cparams(2),
    )(bq, bkv, bkv, bias)


def _merge_kernel(x_ref, oa_ref, ob_ref, oc_ref, od_ref,
                  gpre_ref, gpost_ref, wg_ref, wba_ref, wbb_ref, wbc_ref, wbd_ref, wo_ref, out_ref):
    x = x_ref[0]
    h = _rms(x, gpre_ref[...]).astype(BF16)
    branches = ((oa_ref[0], wba_ref), (ob_ref[0], wbb_ref), (oc_ref[0], wbc_ref), (od_ref[0], wbd_ref))
    merged = None
    for n, (o_n, wb_ref) in enumerate(branches):
        term = _sigmoid(_dot(h, wg_ref[n])) * _dot(o_n, wb_ref[...])
        merged = term if merged is None else merged + term
    z = _dot(merged.astype(BF16), wo_ref[...])
    out_ref[0] = x + _rms(z, gpost_ref[...])


def _merge(x, oa, ob, oc, od, gpre, gpost, wg, wba, wbb, wbc, wbd, wo):
    b, s, d = x.shape
    tm = min(TM_PROJ, s)

    def rows(arr):
        return pl.BlockSpec((1, tm, arr.shape[2]), lambda bi, i: (bi, i, 0))

    acts = (x, oa, ob, oc, od)
    consts = (gpre, gpost, wg, wba, wbb, wbc, wbd, wo)
    return pl.pallas_call(
        _merge_kernel,
        grid=(b, s // tm),
        in_specs=[rows(a) for a in acts] + [_const_spec(c.shape) for c in consts],
        out_specs=rows(x),
        out_shape=jax.ShapeDtypeStruct(x.shape, F32),
        name="merge",
        compiler_params=_cparams(2),
    )(*acts, *consts)


def _ffn_kernel(x_ref, xp_ref, gpre_ref, gpost_ref, wup_ref, cw_ref, cb_ref, wdn_ref, out_ref, *, cuts):
    i = pl.program_id(1)
    tm = x_ref.shape[1]
    d_ff = wdn_ref.shape[0]
    x = x_ref[0]
    halo = jnp.where(i > 0, xp_ref[0], 0.0)
    he = _rms(jnp.concatenate([halo, x], axis=0), gpre_ref[...]).astype(BF16)

    def conv(c0, chunk):
        u = _dot(he, wup_ref[:, c0:c0 + chunk])
        w = cw_ref[:, c0:c0 + chunk]
        out = cb_ref[:, c0:c0 + chunk] + w[0:1] * pltpu.roll(u, 2, 0)[CONV_HALO:]
        out = out + w[1:2] * pltpu.roll(u, 1, 0)[CONV_HALO:]
        return out + w[2:3] * u[CONV_HALO:]

    y = jnp.zeros((tm, x.shape[1]), F32)
    for c0, c1 in zip(cuts[:-1], cuts[1:]):
        gate = conv(c0, c1 - c0)
        val = conv(d_ff + c0, c1 - c0)
        act = gate * _sigmoid(gate) * val
        y = y + _dot(act.astype(BF16), wdn_ref[c0:c1, :])
    out_ref[0] = x + _rms(y, gpost_ref[...])


def _ffn(x, gpre, gpost, w_up, conv_w, conv_b, w_down):
    b, s, d = x.shape
    tm = min(TM_PROJ, s)
    d_ff = w_down.shape[0]
    n_tiles = d_ff // MXU_WIDTH
    cuts = (0, (n_tiles // 2) * MXU_WIDTH, d_ff) if d_ff % MXU_WIDTH == 0 and n_tiles > 1 else (0, d_ff)
    halo_blocks = tm // CONV_HALO
    return pl.pallas_call(
        functools.partial(_ffn_kernel, cuts=cuts),
        grid=(b, s // tm),
        in_specs=[pl.BlockSpec((1, tm, d), lambda bi, i: (bi, i, 0)),
                  pl.BlockSpec((1, CONV_HALO, d), lambda bi, i: (bi, jnp.maximum(i * halo_blocks - 1, 0), 0)),
                  _const_spec(gpre.shape), _const_spec(gpost.shape), _const_spec(w_up.shape),
                  _const_spec(conv_w.shape), _const_spec(conv_b.shape), _const_spec(w_down.shape)],
        out_specs=pl.BlockSpec((1, tm, d), lambda bi, i: (bi, i, 0)),
        out_shape=jax.ShapeDtypeStruct(x.shape, F32),
        name="ffn",
        compiler_params=_cparams(2),
    )(x, x, gpre, gpost, w_up, conv_w, conv_b, w_down)


def _rel_bucket(dist):
    dist = jnp.maximum(dist, 0)
    d = jnp.maximum(dist, 1).astype(F32)
    large = BUCKET_EXACT + (jnp.log(d / BUCKET_EXACT) / math.log(BUCKET_MAX_DIST / BUCKET_EXACT)
                            * (N_BUCKETS - BUCKET_EXACT)).astype(jnp.int32)
    large = jnp.minimum(large, N_BUCKETS - 1)
    return jnp.where(dist < BUCKET_EXACT, dist, large)


def _in_proj_columns():
    src = np.full((N_SLOTS * LANES,), -1, np.int64)
    scl = np.ones((N_SLOTS * LANES,), np.float32)
    o_aq = 0
    o_akv = o_aq + A_HEADS * HEAD_DIM
    o_ag = o_akv + 6 * HEAD_DIM
    o_b = o_ag + 3 * A_HEADS
    o_c = o_b + 3 * B_HEADS * HEAD_DIM
    o_cf = o_c + 3 * C_HEADS * HEAD_DIM
    o_dq = o_cf + C_HEADS
    o_dkv = o_dq + D_Q_LORA
    o_dkr = o_dkv + D_KV_LORA
    e = np.arange(HEAD_DIM)
    q_scale = HEAD_DIM ** -0.5 * LOG2E

    def put(slot, lane0, cols, scale=1.0):
        dst = slot * LANES + lane0 + np.arange(len(cols))
        src[dst] = cols
        scl[dst] = scale

    for h in range(A_HEADS):
        put(G_AQ + h, 0, o_aq + h * HEAD_DIM + e, q_scale)
    for n, slot in enumerate((G_CMP, G_SLC, G_WIN)):
        put(slot, 0, o_akv + (2 * n) * HEAD_DIM + e)
        put(slot, HEAD_DIM, o_akv + (2 * n + 1) * HEAD_DIM + e)
    for h in range(B_HEADS):
        odd = h % B_HPG
        put(G_BQ + h // B_HPG, odd * HEAD_DIM, o_b + h * HEAD_DIM + e, q_scale)
        put(G_BKV + h, odd * HEAD_DIM, o_b + (B_HEADS + h) * HEAD_DIM + e)
        put(G_BKV + h, (1 - odd) * HEAD_DIM, o_b + (2 * B_HEADS + h) * HEAD_DIM + e)
    for h in range(C_HEADS):
        put(G_CQ + h, 0, o_c + h * HEAD_DIM + e, q_scale)
        put(G_CKV + h, 0, o_c + (C_HEADS + h) * HEAD_DIM + e)
        put(G_CKV + h, HEAD_DIM, o_c + (2 * C_HEADS + h) * HEAD_DIM + e)
    put(G_DQL, 0, o_dq + np.arange(D_Q_LORA))
    put(G_DKVL, 0, o_dkv + np.arange(D_KV_LORA))
    put(G_S1, LANE_GATE, o_ag + np.arange(3 * A_HEADS))
    put(G_S1, LANE_FORGET, o_cf + np.arange(C_HEADS))
    put(G_S1, LANE_ROPE, o_dkr + np.arange(D_ROPE))
    half = D_ROPE // 2
    put(G_S2, LANE_ROPE, o_dkr + half + np.arange(half), -1.0)
    put(G_S2, LANE_ROPE + half, o_dkr + np.arange(half))
    return src, scl


def _gather_cols(w, src, scl):
    pieces = []
    n = len(src)
    a = 0
    while a < n:
        e = a + 1
        if src[a] < 0:
            while e < n and src[e] < 0:
                e += 1
            pieces.append(jnp.zeros((w.shape[0], e - a), w.dtype))
        else:
            while e < n and src[e] == src[e - 1] + 1 and scl[e] == scl[a]:
                e += 1
            run = w[:, int(src[a]):int(src[a]) + (e - a)]
            pieces.append(run if scl[a] == 1.0 else run * float(scl[a]))
        a = e
    return jnp.concatenate(pieces, axis=1)


def _mla_weight_columns():
    per_q = D_NOPE + D_ROPE
    half = D_ROPE // 2
    qa = np.full((D_HEADS * LANES,), -1, np.int64)
    qb = np.full((D_HEADS * LANES,), -1, np.int64)
    qb_s = np.ones((D_HEADS * LANES,), np.float32)
    ka = np.full((D_HEADS * LANES,), -1, np.int64)
    for h in range(D_HEADS):
        qa[h * LANES + np.arange(per_q)] = h * per_q + np.arange(per_q)
        rot = h * per_q + D_NOPE
        qb[h * LANES + D_NOPE + np.arange(half)] = rot + half + np.arange(half)
        qb_s[h * LANES + D_NOPE + np.arange(half)] = -1.0
        qb[h * LANES + D_NOPE + half + np.arange(half)] = rot + np.arange(half)
        ka[h * LANES + np.arange(D_NOPE)] = h * (D_NOPE + D_VDIM) + np.arange(D_NOPE)
    ones = np.ones_like(qb_s)
    return (qa, ones), (qb, qb_s), (ka, ones)


def _value_rows(w_in_l, w_ukv_l):
    o_akv = A_HEADS * HEAD_DIM
    o_c = o_akv + 6 * HEAD_DIM + 3 * A_HEADS + 3 * B_HEADS * HEAD_DIM
    starts = [o_akv + 3 * HEAD_DIM] + [o_c + (2 * C_HEADS + h) * HEAD_DIM for h in range(C_HEADS)]
    wvt = jnp.concatenate([w_in_l[:, a:a + HEAD_DIM] for a in starts], axis=1).T
    per_kv = D_NOPE + D_VDIM
    wdvt = jnp.concatenate([w_ukv_l[:, h * per_kv + D_NOPE:(h + 1) * per_kv] for h in range(D_HEADS)], axis=1).T
    return wvt.astype(BF16), wdvt.astype(BF16)


def _forget_placement():
    place = np.zeros((N_EXTRA, LANES, C_HEADS * LANES), np.float32)
    for n in range(N_EXTRA):
        for h in range(C_HEADS):
            place[n, LANE_FORGET + h, h * LANES + LANE_EXTRA + n] = 1.0
    return jnp.asarray(place, BF16)


def _rope_tables(s):
    inv_freq = ROPE_THETA ** (-jnp.arange(0, D_ROPE, 2, dtype=F32) / D_ROPE)
    ang = jnp.arange(s, dtype=F32)[:, None] * inv_freq[None, :]
    cos2 = jnp.concatenate([jnp.cos(ang)] * 2, axis=1)
    sin2 = jnp.concatenate([jnp.sin(ang)] * 2, axis=1)
    scale = (D_NOPE + D_ROPE) ** -0.5 * LOG2E
    z_lo = jnp.zeros((s, D_NOPE), F32)
    z_hi = jnp.zeros((s, LANES - D_NOPE - D_ROPE), F32)
    c_q = jnp.concatenate([jnp.full((s, D_NOPE), scale, F32), scale * cos2, z_hi], axis=1)
    s_q = jnp.concatenate([z_lo, scale * sin2, z_hi], axis=1)
    c_k = jnp.concatenate([z_lo, cos2, z_hi], axis=1)
    s_k = jnp.concatenate([z_lo, sin2, z_hi], axis=1)
    return jnp.concatenate([c_q, s_q, c_k, s_k], axis=1)


def _toeplitz(vec, n_rows, n_cols, off, step=1):
    base = vec[:, ::step]
    last = base.shape[1] - 1
    lo, hi = off - (n_cols - 1), off + n_rows - 1
    core = base[:, max(lo, 0):min(hi, last) + 1]
    front = jnp.repeat(base[:, :1], max(0, -lo), axis=1)
    back = jnp.repeat(base[:, last:], max(0, hi - last), axis=1)
    g = jnp.concatenate([front, core, back], axis=1)
    lg = n_rows + n_cols - 1
    t = jnp.tile(g, (1, n_rows + 1))[:, :n_rows * (lg + 1)].reshape(-1, n_rows, lg + 1)
    return t[:, :, :n_cols][:, :, ::-1]


def _bias_tables(rel_table, s):
    bucket = _rel_bucket(jnp.arange(s))
    hit = bucket[None, :, None] == jnp.arange(N_BUCKETS)[None, None, :]
    by_dist = jnp.sum(jnp.where(hit, LOG2E * rel_table.T[:, None, :], 0.0), axis=-1)
    tqa = min(TQ_NSA, s)
    dist = jnp.arange(tqa)[:, None] + A_WINDOW - jnp.arange(A_WINDOW + tqa)[None, :]
    ok = (dist >= 0) & (dist < A_WINDOW)
    bias_w = jnp.where(ok[None], _toeplitz(by_dist[:A_HEADS], tqa, A_WINDOW + tqa, A_WINDOW), NEG_INF)
    bias_w = bias_w.reshape(A_HEADS * tqa, A_WINDOW + tqa)
    tf = min(BIAS_TILE, s)
    n_off = min(s // tf, -(-(BUCKET_MAX_DIST + tf - 1) // tf) + 1)
    bias_s = jnp.stack([_toeplitz(by_dist[:A_HEADS], tf, tf, n * tf) for n in range(n_off)])
    m = jnp.arange(TQ_DIL)[:, None] + TQ_DIL - jnp.arange(2 * TQ_DIL)[None, :]
    bias_d = []
    for g, (window, dil) in enumerate(B_GROUPS):
        ok = (m >= 0) & (m <= window // dil)
        heads = by_dist[A_HEADS + g * B_HPG:A_HEADS + (g + 1) * B_HPG]
        bias_d.append(jnp.where(ok[None], _toeplitz(heads, TQ_DIL, 2 * TQ_DIL, TQ_DIL, dil), NEG_INF))
    return bias_w, bias_s, jnp.stack(bias_d)


def _selection_constants(s):
    nc = s // A_CMP_STRIDE
    n_cmp = (s - A_CMP_LEN) // A_CMP_STRIDE + 1
    n_sel = s // A_SEL_BLOCK
    c = np.arange(nc)[:, None]
    j = np.arange(n_sel)[None, :]
    c_start = c * A_CMP_STRIDE
    overlap = ((c_start <= j * A_SEL_BLOCK + A_SEL_BLOCK - 1) & (c_start + A_CMP_LEN - 1 >= j * A_SEL_BLOCK)
               & (c < n_cmp))
    expand_t = (np.arange(s)[:, None] // A_SEL_BLOCK) == np.arange(n_sel)[None, :]
    return jnp.asarray(overlap.T, BF16), jnp.asarray(expand_t, BF16)


def kernel(x, rel_bias_table, norm_attn_pre, norm_attn_post, norm_ffn_pre, norm_ffn_post, w_in, nsa_cmp_pos, nsa_phi_k_w1, nsa_phi_k_w2, nsa_phi_v_w1, nsa_phi_v_w2, fox_forget_bias, mla_q_norm, mla_kv_norm, mla_w_uq, mla_w_ukv, w_branch_a, w_branch_b, w_branch_c, w_branch_d, w_merge_gate, w_o, ffn_w_up, ffn_conv_w, ffn_conv_b, ffn_w_down):
    b, s, d = x.shape
    depth = w_in.shape[0]
    assert s % (TQ_DIL * B_GROUPS[-1][1]) == 0, "every dilation class needs whole 128-row tiles"

    in_src, in_scl = _in_proj_columns()
    (qa_i, qa_s), (qb_i, qb_s), (ka_i, ka_s) = _mla_weight_columns()
    rope_tab = _rope_tables(s)
    bias_w, bias_s, bias_d = _bias_tables(rel_bias_table, s)
    bias_s_t = jnp.swapaxes(bias_s, 2, 3)
    overlap, expand_t = _selection_constants(s)
    place = _forget_placement()
    half = A_CMP_STRIDE * HEAD_DIM

    for l in range(depth):
        w_ext = _gather_cols(w_in[l], in_src, in_scl).astype(BF16)
        fbias = jnp.zeros((1, LANES), F32).at[0, LANE_FORGET:LANE_FORGET + C_HEADS].set(fox_forget_bias[l])
        wvt, wdvt = _value_rows(w_in[l], mla_w_ukv[l])
        wq = jnp.concatenate([_gather_cols(mla_w_uq[l], qa_i, qa_s), _gather_cols(mla_w_uq[l], qb_i, qb_s)],
                             axis=1).astype(BF16)
        (aq, cmp, slc, slc_vt, win, bq, bkv, cq, ck, c_vt, dq, dk, d_vt, small) = _inproj(
            x, norm_attn_pre[l][None], w_ext, wvt, rope_tab, fbias,
            mla_q_norm[l][None], mla_kv_norm[l][None], wq,
            _gather_cols(mla_w_ukv[l], ka_i, ka_s).astype(BF16), wdvt, place)

        nc = s // A_CMP_STRIDE
        w2k_pad = jnp.pad(nsa_phi_k_w2[l], ((0, 0), (0, LANES - HEAD_DIM))).astype(BF16)
        w2v_pad = jnp.pad(nsa_phi_v_w2[l], ((0, 0), (LANES - HEAD_DIM, 0))).astype(BF16)
        kvc = _compress(cmp[..., :HEAD_DIM].reshape(b, nc, half), cmp[..., HEAD_DIM:].reshape(b, nc, half),
                        nsa_cmp_pos[l].reshape(2, half),
                        nsa_phi_k_w1[l].astype(BF16), nsa_phi_v_w1[l].astype(BF16), w2k_pad, w2v_pad)
        kvw_pad = jnp.pad(win, ((0, 0), (A_WINDOW, 0), (0, 0)))
        part, sel_t = _nsa_cmp_win(aq, kvc, kvw_pad, small, overlap, bias_w)
        o_a = _flash("sel", aq, slc, slc_vt, sel_t=sel_t, expand_t=expand_t, bias=bias_s_t, part=part, small=small)

        o_b = _dilated(bq, bkv, bias_d)

        o_c = _flash("fox", cq, ck, c_vt)
        o_d = _flash("mla", dq, dk, d_vt)

        x = _merge(x, o_a, o_b, o_c, o_d,
                   norm_attn_pre[l][None], norm_attn_post[l][None],
                   w_merge_gate[l].astype(BF16),
                   w_branch_a[l].astype(BF16),
                   jnp.concatenate([w_branch_b[l][HEAD_DIM:], w_branch_b[l][:HEAD_DIM]]).astype(BF16),
                   w_branch_c[l].astype(BF16),
                   w_branch_d[l].astype(BF16),
                   w_o[l].astype(BF16))
        x = _ffn(x, norm_ffn_pre[l][None], norm_ffn_post[l][None], ffn_w_up[l].astype(BF16),
                 ffn_conv_w[l], ffn_conv_b[l][None], ffn_w_down[l].astype(BF16))
    return x
```

```python
import functools
import math

import numpy as np
import jax
import jax.numpy as jnp
from jax import lax
from jax.experimental import pallas as pl
from jax.experimental.pallas import tpu as pltpu

F32 = jnp.float32
BF16 = jnp.bfloat16

LANES = 128
MXU_WIDTH = 256
HEAD_DIM = 64
RMS_EPS = 1e-6
NEG_INF = -1e30
FORCE_SCORE = 1e9
LOG2E = math.log2(math.e)
VMEM_LIMIT_BYTES = 56 * 1024 * 1024

N_BUCKETS = 32
BUCKET_EXACT = 16
BUCKET_MAX_DIST = 2048

A_HEADS = 4
A_CMP_LEN = 32
A_CMP_STRIDE = 16
A_PHI_HIDDEN = 128
A_SEL_BLOCK = 64
A_SEL_TOPK = 16
A_WINDOW = 512
B_GROUPS = ((128, 1), (512, 4), (2048, 16))
B_HPG = 2
B_HEADS = B_HPG * len(B_GROUPS)
C_HEADS = 4
D_HEADS = 4
D_Q_LORA = 256
D_KV_LORA = 128
D_NOPE = 64
D_ROPE = 32
D_VDIM = 64
ROPE_THETA = 10000.0
N_BRANCHES = 4

G_AQ = 0
G_CMP = G_AQ + A_HEADS
G_SLC = G_CMP + 1
G_WIN = G_SLC + 1
G_BQ = G_WIN + 1
G_BKV = G_BQ + len(B_GROUPS)
G_CQ = G_BKV + B_HEADS
G_CKV = G_CQ + C_HEADS
G_DQL = G_CKV + C_HEADS
G_DKVL = G_DQL + 2
G_S1 = G_DKVL + 1
G_S2 = G_S1 + 1
N_SLOTS = G_S2 + 1
LANE_GATE = 0
LANE_FORGET = 3 * A_HEADS
LANE_ROPE = 64

TM_PROJ = 512
TQ_NSA_STEP = 256
TQ_FLASH = 512
VT_ROWS = 80
CHUNK_SLOTS = 8
LANE_EXTRA = HEAD_DIM
N_EXTRA = 3
BIAS_TILE = 256
TQ_NSA = 128
TQ_DIL = 128
CONV_HALO = 16


def _cparams(n_grid):
    return pltpu.CompilerParams(dimension_semantics=("arbitrary",) * n_grid,
                                vmem_limit_bytes=VMEM_LIMIT_BYTES)


def _const_spec(shape):
    nd = len(shape)
    return pl.BlockSpec(shape, lambda *_: (0,) * nd, pipeline_mode=pl.Buffered(1))


def _rms(x, gain):
    return x * lax.rsqrt(jnp.mean(x * x, axis=-1, keepdims=True) + RMS_EPS) * gain


def _sigmoid(x):
    return 1.0 / (1.0 + jnp.exp(-x))


def _dot(a, b):
    return jnp.dot(a, b, preferred_element_type=F32)


def _dot_nt(a, b):
    return lax.dot_general(a, b, (((1,), (1,)), ((), ())), preferred_element_type=F32)


def _slot(h):
    return slice(h * LANES, (h + 1) * LANES)


def _inproj_kernel(x_ref, g_ref, w_ref, wvt_ref, tab_ref, fb_ref, qn_ref, kvn_ref, wq_ref, wk_ref, wdvt_ref,
                   place_ref,
                   aq_ref, cmp_ref, slc_ref, slcvt_ref, win_ref, bq_ref, bkv_ref, cq_ref, ck_ref, cvt_ref,
                   dq_ref, dk_ref, dvt_ref, small_ref, carry_ref):
    i = pl.program_id(1)
    tm = x_ref.shape[1]
    h = _rms(x_ref[0], g_ref[...]).astype(BF16)

    n_slots = w_ref.shape[1] // LANES
    wide = [_dot(h, w_ref[:, c * LANES:min(c + CHUNK_SLOTS, n_slots) * LANES])
            for c in range(0, n_slots, CHUNK_SLOTS)]

    def proj(g0, n):
        pieces = []
        g = g0
        while g < g0 + n:
            c, off = divmod(g, CHUNK_SLOTS)
            take = min(g0 + n - g, CHUNK_SLOTS - off)
            pieces.append(wide[c][:, off * LANES:(off + take) * LANES])
            g += take
        return pieces[0] if len(pieces) == 1 else jnp.concatenate(pieces, axis=1)

    def store_transposed(out_ref, vt, n_heads):
        for n in range(n_heads):
            out_ref[0, n * VT_ROWS:n * VT_ROWS + HEAD_DIM, :] = vt[n * HEAD_DIM:(n + 1) * HEAD_DIM].astype(BF16)
            out_ref[0, n * VT_ROWS + HEAD_DIM:(n + 1) * VT_ROWS, :] = jnp.ones((VT_ROWS - HEAD_DIM, tm), BF16)

    aq_ref[0] = proj(G_AQ, A_HEADS).astype(BF16)
    cmp_ref[0] = proj(G_CMP, 1)
    slc_ref[0] = proj(G_SLC, 1).astype(BF16)
    win_ref[0] = proj(G_WIN, 1).astype(BF16)
    b_all = proj(G_BQ, len(B_GROUPS) + B_HEADS)
    for n in range(len(B_GROUPS)):
        bq_ref[0, n] = b_all[:, _slot(n)]
    for n in range(B_HEADS):
        bkv_ref[0, n] = b_all[:, _slot(len(B_GROUPS) + n)]

    vt = _dot_nt(wvt_ref[...], h)
    store_transposed(slcvt_ref, vt[0:HEAD_DIM], 1)
    store_transposed(cvt_ref, vt[HEAD_DIM:], C_HEADS)

    s1 = proj(G_S1, 1)
    s2 = proj(G_S2, 1)

    @pl.when(i == 0)
    def _():
        carry_ref[...] = jnp.zeros_like(carry_ref)

    def split3(v):
        hi = v.astype(BF16)
        r1 = v - hi.astype(F32)
        mid = r1.astype(BF16)
        return hi, mid, (r1 - mid.astype(F32)).astype(BF16)

    z = s1 + fb_ref[...]
    logf = jnp.minimum(z, 0.0) - jnp.log(1.0 + jnp.exp(-jnp.abs(z)))
    row = lax.broadcasted_iota(jnp.int32, (tm, tm), 0)
    col = lax.broadcasted_iota(jnp.int32, (tm, tm), 1)
    tri = (row >= col).astype(BF16)
    cum = sum(_dot(tri, part) for part in split3(logf)) + carry_ref[0:1, :]
    carry_ref[0:1, :] = cum[tm - 1:tm, :]
    lane = lax.broadcasted_iota(jnp.int32, (tm, LANES), 1)
    is_forget = (lane >= LANE_FORGET) & (lane < LANE_FORGET + C_HEADS)
    small_ref[0] = jnp.where(is_forget, cum, s1)

    extras = sum(_dot(part, place_ref[n]) for n, part in enumerate(split3(-LOG2E * cum)))
    lane_c = lax.broadcasted_iota(jnp.int32, (tm, C_HEADS * LANES), 1) & (LANES - 1)
    is_extra = (lane_c >= LANE_EXTRA) & (lane_c < LANE_EXTRA + N_EXTRA)
    cq_ref[0] = jnp.where(is_extra, 1.0, proj(G_CQ, C_HEADS)).astype(BF16)
    ck_ref[0] = jnp.where(is_extra, extras, proj(G_CKV, C_HEADS)).astype(BF16)

    tab = tab_ref[...]
    c_q, s_q, c_k, s_k = (tab[:, _slot(n)] for n in range(4))
    qn = _rms(proj(G_DQL, 2), qn_ref[...]).astype(BF16)
    qab = _dot(qn, wq_ref[...])
    kvn = _rms(proj(G_DKVL, 1), kvn_ref[...]).astype(BF16)
    ka = _dot(kvn, wk_ref[...])
    k_rot = s1 * c_k + s2 * s_k
    for hd in range(D_HEADS):
        q_rot = qab[:, _slot(hd)] * c_q + qab[:, _slot(D_HEADS + hd)] * s_q
        dq_ref[0, :, _slot(hd)] = q_rot.astype(BF16)
        dk_ref[0, :, _slot(hd)] = (ka[:, _slot(hd)] + k_rot).astype(BF16)
    store_transposed(dvt_ref, _dot_nt(wdvt_ref[...], kvn), D_HEADS)


def _inproj(x, gain, w_ext, wvt, tab, fbias, qn_g, kvn_g, wq, wk, wdvt, place):
    b, s, d = x.shape
    tm = min(TM_PROJ, s)

    def out(kind, n, dtype):
        if kind == "rows":
            return (pl.BlockSpec((1, tm, n * LANES), lambda bi, i: (bi, i, 0)),
                    jax.ShapeDtypeStruct((b, s, n * LANES), dtype))
        if kind == "slots":
            return (pl.BlockSpec((1, n, tm, LANES), lambda bi, i: (bi, 0, i, 0)),
                    jax.ShapeDtypeStruct((b, n, s, LANES), dtype))
        return (pl.BlockSpec((1, n * VT_ROWS, tm), lambda bi, i: (bi, 0, i)),
                jax.ShapeDtypeStruct((b, n * VT_ROWS, s), dtype))

    outs = [out("rows", A_HEADS, BF16), out("rows", 1, F32), out("rows", 1, BF16), out("vt", 1, BF16),
            out("rows", 1, BF16), out("slots", len(B_GROUPS), F32), out("slots", B_HEADS, F32),
            out("rows", C_HEADS, BF16), out("rows", C_HEADS, BF16), out("vt", C_HEADS, BF16),
            out("rows", D_HEADS, BF16), out("rows", D_HEADS, BF16), out("vt", D_HEADS, BF16),
            out("rows", 1, F32)]
    consts = (gain, w_ext, wvt)
    consts2 = (fbias, qn_g, kvn_g, wq, wk, wdvt, place)
    return pl.pallas_call(
        _inproj_kernel,
        grid=(b, s // tm),
        in_specs=[pl.BlockSpec((1, tm, d), lambda bi, i: (bi, i, 0))]
        + [_const_spec(c.shape) for c in consts]
        + [pl.BlockSpec((tm, 4 * LANES), lambda bi, i: (i, 0))]
        + [_const_spec(c.shape) for c in consts2],
        out_specs=[o[0] for o in outs],
        out_shape=[o[1] for o in outs],
        scratch_shapes=[pltpu.VMEM((8, LANES), F32)],
        name="inproj",
        compiler_params=_cparams(2),
    )(x, *consts, tab, *consts2)


def _gelu_tanh(x):
    return 0.5 * x * (1.0 + jnp.tanh(math.sqrt(2.0 / math.pi) * (x + 0.044715 * (x * x * x))))


def _compress_kernel(kc_ref, vc_ref, pos_ref, w1k_ref, w1v_ref, w2k_ref, w2v_ref, out_ref):
    nc = kc_ref.shape[1]
    half = A_CMP_STRIDE * HEAD_DIM
    pos_a = pos_ref[0:1, :]
    pos_b = pos_ref[1:2, :]

    def phi(chunks, w1_ref, w2_ref):
        u = _dot((chunks + pos_a).astype(BF16), w1_ref[0:half, :])
        v = _dot((chunks + pos_b).astype(BF16), w1_ref[half:2 * half, :])
        hid = u + pltpu.roll(v, nc - 1, 0)
        return _dot(_gelu_tanh(hid).astype(BF16), w2_ref[...])

    out_ref[0] = (phi(kc_ref[0], w1k_ref, w2k_ref) + phi(vc_ref[0], w1v_ref, w2v_ref)).astype(BF16)


def _compress(kc_in, vc_in, pos2, w1k, w1v, w2k_pad, w2v_pad):
    b, nc, width = kc_in.shape
    blk = pl.BlockSpec((1, nc, width), lambda bi: (bi, 0, 0))
    return pl.pallas_call(
        _compress_kernel,
        grid=(b,),
        in_specs=[blk, blk, _const_spec(pos2.shape), _const_spec(w1k.shape), _const_spec(w1v.shape),
                  _const_spec(w2k_pad.shape), _const_spec(w2v_pad.shape)],
        out_specs=pl.BlockSpec((1, nc, LANES), lambda bi: (bi, 0, 0)),
        out_shape=jax.ShapeDtypeStruct((b, nc, LANES), BF16),
        name="compress",
        compiler_params=_cparams(1),
    )(kc_in, vc_in, pos2, w1k, w1v, w2k_pad, w2v_pad)


def _nsa_cmp_win_kernel(q_ref, kvc_ref, kvw_ref, small_ref, ovl_ref, bias_ref,
                        part_ref, sel_ref, *, n_top):
    tq = min(TQ_NSA, q_ref.shape[1])
    for u in range(q_ref.shape[1] // tq):
        _nsa_cmp_win_tile(u * tq, tq, q_ref, kvc_ref, kvw_ref, small_ref, ovl_ref, bias_ref,
                          part_ref, sel_ref, n_top)


def _nsa_cmp_win_tile(r0, tq, q_ref, kvc_ref, kvw_ref, small_ref, ovl_ref, bias_ref, part_ref, sel_ref, n_top):
    nc = kvc_ref.shape[1]
    n_sel = ovl_ref.shape[0]
    t0 = pl.multiple_of(pl.program_id(1) * q_ref.shape[1] + r0, tq)
    tile = slice(r0, r0 + tq)
    q4 = jnp.concatenate([q_ref[0, tile, _slot(h)] for h in range(A_HEADS)], axis=0)
    rows = A_HEADS * tq

    kvc = kvc_ref[0]
    s = _dot_nt(q4, kvc)
    t_row = t0 + (lax.broadcasted_iota(jnp.int32, (rows, nc), 0) & (tq - 1))
    c_end = lax.broadcasted_iota(jnp.int32, (rows, nc), 1) * A_CMP_STRIDE + (A_CMP_LEN - 1)
    valid = c_end <= t_row
    s = jnp.where(valid, s, NEG_INF)
    m = jnp.max(s, axis=1, keepdims=True)
    p = jnp.where(valid, jnp.exp2(s - m), 0.0)
    den = jnp.maximum(jnp.sum(p, axis=1, keepdims=True), 1e-30)
    p = p / den
    o_c = _dot(p.astype(BF16), kvc)

    p_sum = p[0:tq] + p[tq:2 * tq] + p[2 * tq:3 * tq] + p[3 * tq:4 * tq]
    p_hi = p_sum.astype(BF16)
    p_lo = (p_sum - p_hi.astype(F32)).astype(BF16)
    imp_t = _dot_nt(ovl_ref[...], p_hi) + _dot_nt(ovl_ref[...], p_lo)

    j_idx = lax.broadcasted_iota(jnp.int32, (n_sel, tq), 0)
    cur = lax.shift_right_logical(t0 + lax.broadcasted_iota(jnp.int32, (n_sel, tq), 1), 6)
    forced = (j_idx == 0) | (j_idx == cur) | (j_idx == cur - 1)
    imp_t = jnp.where(forced, FORCE_SCORE, jnp.where(j_idx > cur, -FORCE_SCORE, imp_t))
    sub8 = 8
    groups = [imp_t[v * sub8:(v + 1) * sub8] for v in range(n_sel // sub8)]
    ranks = [jnp.zeros((sub8, tq), F32) for _ in groups]
    in_group = lax.broadcasted_iota(jnp.int32, (sub8, tq), 0)
    for c in range(n_sel):
        row_c = imp_t[c:c + 1, :]
        for v, grp in enumerate(groups):
            if v > c // sub8:
                beats = row_c >= grp
            elif v < c // sub8:
                beats = row_c > grp
            else:
                beats = (row_c > grp) | ((row_c == grp) & (in_group > c % sub8))
            ranks[v] = ranks[v] + jnp.where(beats, 1.0, 0.0)
    rank = jnp.concatenate(ranks, axis=0)
    sel_t = jnp.where(rank < float(n_top), 1.0, 0.0).astype(BF16)
    sel_ref[0, :, tile] = sel_t

    span = A_WINDOW + tq
    kvw = kvw_ref[0, pl.ds(t0, span), :]
    sw = _dot_nt(q4, kvw) + bias_ref[...]
    kpos = t0 - A_WINDOW + lax.broadcasted_iota(jnp.int32, (rows, span), 1)
    sw = jnp.where(kpos >= 0, sw, NEG_INF)
    mw = jnp.max(sw, axis=1, keepdims=True)
    pw = jnp.exp2(sw - mw)
    o_w = _dot(pw.astype(BF16), kvw) / jnp.sum(pw, axis=1, keepdims=True)

    g = _sigmoid(small_ref[0, tile])
    gated = []
    for h in range(A_HEADS):
        g_c = g[:, LANE_GATE + 3 * h:LANE_GATE + 3 * h + 1]
        g_w = g[:, LANE_GATE + 3 * h + 2:LANE_GATE + 3 * h + 3]
        gated.append(g_c * o_c[h * tq:(h + 1) * tq] + g_w * o_w[h * tq:(h + 1) * tq])
    low = lax.broadcasted_iota(jnp.int32, (tq, LANES), 1) < HEAD_DIM
    for pair in range(A_HEADS // 2):
        part_ref[0, tile, _slot(pair)] = jnp.where(low, pltpu.roll(gated[2 * pair], HEAD_DIM, 1),
                                                   gated[2 * pair + 1])


def _nsa_cmp_win(aq, kvc, kvw_pad, small, ovl, bias_w):
    b, s, _ = aq.shape
    tq = min(TQ_NSA_STEP, s)
    nc = kvc.shape[1]
    n_sel = ovl.shape[0]
    n_top = min(A_SEL_TOPK, n_sel)
    return pl.pallas_call(
        functools.partial(_nsa_cmp_win_kernel, n_top=n_top),
        grid=(b, s // tq),
        in_specs=[pl.BlockSpec((1, tq, A_HEADS * LANES), lambda bi, i: (bi, i, 0)),
                  pl.BlockSpec((1, nc, LANES), lambda bi, i: (bi, 0, 0)),
                  pl.BlockSpec((1, s + A_WINDOW, LANES), lambda bi, i: (bi, 0, 0)),
                  pl.BlockSpec((1, tq, LANES), lambda bi, i: (bi, i, 0)),
                  _const_spec(ovl.shape),
                  _const_spec(bias_w.shape)],
        out_specs=[pl.BlockSpec((1, tq, A_HEADS * HEAD_DIM), lambda bi, i: (bi, i, 0)),
                   pl.BlockSpec((1, n_sel, tq), lambda bi, i: (bi, 0, i))],
        out_shape=[jax.ShapeDtypeStruct((b, s, A_HEADS * HEAD_DIM), F32),
                   jax.ShapeDtypeStruct((b, n_sel, s), BF16)],
        name="nsa_cmp_win",
        compiler_params=_cparams(2),
    )(aq, kvc, kvw_pad, small, ovl, bias_w)


def _flash_kernel(*refs, mode, nh):
    if mode == "sel":
        q_ref, k_ref, vt_ref, selt_ref, et_ref, bias_ref, part_ref, small_ref, o_ref, m_ref, acc_ref = refs
    else:
        q_ref, k_ref, vt_ref, o_ref, m_ref, acc_ref = refs
    i = pl.program_id(1)
    tq = q_ref.shape[1]
    tk = tq
    shared_kv = mode == "sel"

    m_ref[...] = jnp.full(m_ref.shape, NEG_INF, F32)
    acc_ref[...] = jnp.zeros(acc_ref.shape, F32)

    lead = lax.broadcasted_iota(jnp.int32, (tk, tq), 1) - lax.broadcasted_iota(jnp.int32, (tk, tq), 0)

    def logits_of(j, diag):
        k0 = pl.multiple_of(j * tk, tk)
        keep = (lead >= 0) if diag else None
        if mode == "sel":
            sel_keys = _dot(et_ref[pl.ds(k0, tk), :], selt_ref[0]) > 0.5
            keep = (sel_keys & keep) if diag else sel_keys
            n_off, bt = bias_ref.shape[0], bias_ref.shape[2]

            def bias_tile(h):
                strips = []
                for c in range(tk // bt):
                    offs = [jnp.clip(i * (tq // bt) + a - j * (tk // bt) - c, 0, n_off - 1)
                            for a in range(tq // bt)]
                    strips.append(jnp.concatenate([bias_ref[o, h] for o in offs], axis=1))
                return jnp.concatenate(strips, axis=0)

        def head_logits(h):
            s = _dot_nt(k_ref[0, pl.ds(k0, tk), _slot(0 if shared_kv else h)], q_ref[0, :, _slot(h)])
            if mode == "sel":
                s = s + bias_tile(h)
            if keep is not None:
                s = jnp.where(keep, s, NEG_INF)
            return s
        return head_logits

    def update(tiles):
        makers = [logits_of(j, diag) for j, diag in tiles]
        probs = []
        all_logits = [[make(h) for make in makers] for h in range(nh)]
        for h in range(nh):
            logits = all_logits[h]
            m_prev = m_ref[h]
            m_new = m_prev
            for s in logits:
                m_new = jnp.maximum(m_new, jnp.max(s, axis=0, keepdims=True))
            probs.append(([jnp.exp2(s - m_new).astype(BF16) for s in logits], jnp.exp2(m_prev - m_new)))
            m_ref[h] = m_new
        for h in range(nh):
            v0 = 0 if shared_kv else h * VT_ROWS
            ps, alpha = probs[h]
            acc = alpha * acc_ref[h]
            for (j, _), p in zip(tiles, ps):
                acc = acc + _dot(vt_ref[0, v0:v0 + VT_ROWS, pl.ds(pl.multiple_of(j * tk, tk), tk)], p)
            acc_ref[h] = acc

    def full_tile(j, carry):
        update([(j, False)])
        return carry

    lax.fori_loop(0, i, full_tile, 0)
    update([(i, True)])


    low = lax.broadcasted_iota(jnp.int32, (tq, LANES), 1) < HEAD_DIM
    if mode == "sel":
        g = _sigmoid(small_ref[0])
    for pair in range(nh // 2):
        halves = []
        for h in (2 * pair, 2 * pair + 1):
            acc = acc_ref[h]
            o_t = acc[0:HEAD_DIM] / acc[HEAD_DIM:HEAD_DIM + 1]
            halves.append(jnp.concatenate([o_t, jnp.zeros_like(o_t)], axis=0).T)
        o_pair = jnp.where(low, halves[0], pltpu.roll(halves[1], HEAD_DIM, 1))
        if mode == "sel":
            gate = [g[:, LANE_GATE + 3 * h + 1:LANE_GATE + 3 * h + 2] for h in (2 * pair, 2 * pair + 1)]
            o_pair = part_ref[0, :, _slot(pair)] + jnp.where(low, gate[0], gate[1]) * o_pair
        o_ref[0, :, _slot(pair)] = o_pair.astype(BF16)


def _flash(mode, q, k, vt, sel_t=None, expand_t=None, bias=None, part=None, small=None):
    b, s, qw = q.shape
    nh = qw // LANES
    tq = min(TQ_FLASH, s)
    ow = nh * HEAD_DIM

    def rows(width):
        return pl.BlockSpec((1, tq, width), lambda bi, i: (bi, i, 0))

    def seq_spec(arr):
        return pl.BlockSpec((1,) + arr.shape[1:], lambda bi, i: (bi, 0, 0))

    args = (q, k, vt)
    in_specs = [rows(qw), seq_spec(k), seq_spec(vt)]
    if mode == "sel":
        args += (sel_t, expand_t, bias, part, small)
        in_specs += [pl.BlockSpec((1, sel_t.shape[1], tq), lambda bi, i: (bi, 0, i)),
                     _const_spec(expand_t.shape), _const_spec(bias.shape), rows(ow), rows(LANES)]
    return pl.pallas_call(
        functools.partial(_flash_kernel, mode=mode, nh=nh),
        grid=(b, s // tq),
        in_specs=in_specs,
        out_specs=rows(ow),
        out_shape=jax.ShapeDtypeStruct((b, s, ow), BF16),
        scratch_shapes=[pltpu.VMEM((nh, 1, tq), F32), pltpu.VMEM((nh, VT_ROWS, tq), F32)],
        name="flash_" + mode,
        compiler_params=_cparams(2),
    )(*args)


def _dilated_kernel(q_ref, kvp_ref, kvc_ref, bias_ref, o_ref, o_sc, l_sc):
    i = pl.program_id(1)
    rows = q_ref.shape[2]
    tb = TQ_DIL
    low = lax.broadcasted_iota(jnp.int32, (tb, LANES), 1) < HEAD_DIM
    col = lax.broadcasted_iota(jnp.int32, (tb, 2 * tb), 1)
    in_seq = (col >= tb) | (i > 0)
    for g, (_, dil) in enumerate(B_GROUPS):
        span = dil * tb
        for r in range(dil):
            for c in range(rows // span):
                cur = pl.ds(r + c * span, tb, stride=dil)
                if c > 0:
                    prev_ref, prev = kvc_ref, pl.ds(r + (c - 1) * span, tb, stride=dil)
                else:
                    prev_ref, prev = kvp_ref, pl.ds(rows - span + r, tb, stride=dil)
                q2 = q_ref[0, g, cur, :]
                outs, lses = [], []
                for h in range(B_HPG):
                    q_h = jnp.where(low if h == 0 else ~low, q2, 0.0).astype(BF16)
                    head = B_HPG * g + h
                    kv = jnp.concatenate([prev_ref[0, head, prev, :], kvc_ref[0, head, cur, :]],
                                         axis=0).astype(BF16)
                    s = _dot_nt(q_h, kv) + bias_ref[g, h]
                    if c == 0:
                        s = jnp.where(in_seq, s, NEG_INF)
                    m = jnp.max(s, axis=1, keepdims=True)
                    p = jnp.exp2(s - m)
                    den = jnp.sum(p, axis=1, keepdims=True)
                    outs.append(_dot(p.astype(BF16), kv) / den)
                    lses.append(m + jnp.log2(den))
                o_sc[g, cur, :] = jnp.where(low, outs[1], outs[0])
                l_sc[g, cur, :] = jnp.where(low, lses[1], lses[0])
    step = 2 * tb
    for t in range(rows // step):
        rs = slice(t * step, (t + 1) * step)
        l0, l1, l2 = l_sc[0, rs, :], l_sc[1, rs, :], l_sc[2, rs, :]
        mx = jnp.maximum(jnp.maximum(l0, l1), l2)
        e0, e1, e2 = jnp.exp2(l0 - mx), jnp.exp2(l1 - mx), jnp.exp2(l2 - mx)
        mixed = (e0 * o_sc[0, rs, :] + e1 * o_sc[1, rs, :] + e2 * o_sc[2, rs, :]) / (e0 + e1 + e2)
        o_ref[0, rs, :] = mixed.astype(BF16)


def _dilated(bq, bkv, bias):
    b, _, s, _ = bq.shape
    rows = TQ_DIL * B_GROUPS[-1][1]
    n_groups = len(B_GROUPS)

    def tile(n_slots, prev):
        if prev:
            return pl.BlockSpec((1, n_slots, rows, LANES), lambda bi, i: (bi, 0, jnp.maximum(i - 1, 0), 0))
        return pl.BlockSpec((1, n_slots, rows, LANES), lambda bi, i: (bi, 0, i, 0))

    return pl.pallas_call(
        _dilated_kernel,
        grid=(b, s // rows),
        in_specs=[tile(bq.shape[1], False), tile(bkv.shape[1], True), tile(bkv.shape[1], False),
                  _const_spec(bias.shape)],
        out_specs=pl.BlockSpec((1, rows, LANES), lambda bi, i: (bi, i, 0)),
        out_shape=jax.ShapeDtypeStruct((b, s, LANES), BF16),
        scratch_shapes=[pltpu.VMEM((n_groups, rows, LANES), F32)] * 2,
        name="dilated",
        compiler_params=_cparams(2),
    )(bq, bkv, bkv, bias)


def _merge_kernel(x_ref, oa_ref, ob_ref, oc_ref, od_ref,
                  gpre_ref, gpost_ref, wg_ref, wba_ref, wbb_ref, wbc_ref, wbd_ref, wo_ref, out_ref):
    x = x_ref[0]
    h = _rms(x, gpre_ref[...]).astype(BF16)
    branches = ((oa_ref[0], wba_ref), (ob_ref[0], wbb_ref), (oc_ref[0], wbc_ref), (od_ref[0], wbd_ref))
    merged = None
    for n, (o_n, wb_ref) in enumerate(branches):
        term = _sigmoid(_dot(h, wg_ref[n])) * _dot(o_n, wb_ref[...])
        merged = term if merged is None else merged + term
    z = _dot(merged.astype(BF16), wo_ref[...])
    out_ref[0] = x + _rms(z, gpost_ref[...])


def _merge(x, oa, ob, oc, od, gpre, gpost, wg, wba, wbb, wbc, wbd, wo):
    b, s, d = x.shape
    tm = min(TM_PROJ, s)

    def rows(arr):
        return pl.BlockSpec((1, tm, arr.shape[2]), lambda bi, i: (bi, i, 0))

    acts = (x, oa, ob, oc, od)
    consts = (gpre, gpost, wg, wba, wbb, wbc, wbd, wo)
    return pl.pallas_call(
        _merge_kernel,
        grid=(b, s // tm),
        in_specs=[rows(a) for a in acts] + [_const_spec(c.shape) for c in consts],
        out_specs=rows(x),
        out_shape=jax.ShapeDtypeStruct(x.shape, F32),
        name="merge",
        compiler_params=_cparams(2),
    )(*acts, *consts)


def _ffn_kernel(x_ref, xp_ref, gpre_ref, gpost_ref, wup_ref, cw_ref, cb_ref, wdn_ref, out_ref, *, cuts):
    i = pl.program_id(1)
    tm = x_ref.shape[1]
    d_ff = wdn_ref.shape[0]
    x = x_ref[0]
    halo = jnp.where(i > 0, xp_ref[0], 0.0)
    he = _rms(jnp.concatenate([halo, x], axis=0), gpre_ref[...]).astype(BF16)

    def conv(c0, chunk):
        u = _dot(he, wup_ref[:, c0:c0 + chunk])
        w = cw_ref[:, c0:c0 + chunk]
        out = cb_ref[:, c0:c0 + chunk] + w[0:1] * pltpu.roll(u, 2, 0)[CONV_HALO:]
        out = out + w[1:2] * pltpu.roll(u, 1, 0)[CONV_HALO:]
        return out + w[2:3] * u[CONV_HALO:]

    y = jnp.zeros((tm, x.shape[1]), F32)
    for c0, c1 in zip(cuts[:-1], cuts[1:]):
        gate = conv(c0, c1 - c0)
        val = conv(d_ff + c0, c1 - c0)
        act = gate * _sigmoid(gate) * val
        y = y + _dot(act.astype(BF16), wdn_ref[c0:c1, :])
    out_ref[0] = x + _rms(y, gpost_ref[...])


def _ffn(x, gpre, gpost, w_up, conv_w, conv_b, w_down):
    b, s, d = x.shape
    tm = min(TM_PROJ, s)
    d_ff = w_down.shape[0]
    n_tiles = d_ff // MXU_WIDTH
    cuts = (0, (n_tiles // 2) * MXU_WIDTH, d_ff) if d_ff % MXU_WIDTH == 0 and n_tiles > 1 else (0, d_ff)
    halo_blocks = tm // CONV_HALO
    return pl.pallas_call(
        functools.partial(_ffn_kernel, cuts=cuts),
        grid=(b, s // tm),
        in_specs=[pl.BlockSpec((1, tm, d), lambda bi, i: (bi, i, 0)),
                  pl.BlockSpec((1, CONV_HALO, d), lambda bi, i: (bi, jnp.maximum(i * halo_blocks - 1, 0), 0)),
                  _const_spec(gpre.shape), _const_spec(gpost.shape), _const_spec(w_up.shape),
                  _const_spec(conv_w.shape), _const_spec(conv_b.shape), _const_spec(w_down.shape)],
        out_specs=pl.BlockSpec((1, tm, d), lambda bi, i: (bi, i, 0)),
        out_shape=jax.ShapeDtypeStruct(x.shape, F32),
        name="ffn",
        compiler_params=_cparams(2),
    )(x, x, gpre, gpost, w_up, conv_w, conv_b, w_down)


def _rel_bucket(dist):
    dist = jnp.maximum(dist, 0)
    d = jnp.maximum(dist, 1).astype(F32)
    large = BUCKET_EXACT + (jnp.log(d / BUCKET_EXACT) / math.log(BUCKET_MAX_DIST / BUCKET_EXACT)
                            * (N_BUCKETS - BUCKET_EXACT)).astype(jnp.int32)
    large = jnp.minimum(large, N_BUCKETS - 1)
    return jnp.where(dist < BUCKET_EXACT, dist, large)


def _in_proj_columns():
    src = np.full((N_SLOTS * LANES,), -1, np.int64)
    scl = np.ones((N_SLOTS * LANES,), np.float32)
    o_aq = 0
    o_akv = o_aq + A_HEADS * HEAD_DIM
    o_ag = o_akv + 6 * HEAD_DIM
    o_b = o_ag + 3 * A_HEADS
    o_c = o_b + 3 * B_HEADS * HEAD_DIM
    o_cf = o_c + 3 * C_HEADS * HEAD_DIM
    o_dq = o_cf + C_HEADS
    o_dkv = o_dq + D_Q_LORA
    o_dkr = o_dkv + D_KV_LORA
    e = np.arange(HEAD_DIM)
    q_scale = HEAD_DIM ** -0.5 * LOG2E

    def put(slot, lane0, cols, scale=1.0):
        dst = slot * LANES + lane0 + np.arange(len(cols))
        src[dst] = cols
        scl[dst] = scale

    for h in range(A_HEADS):
        put(G_AQ + h, 0, o_aq + h * HEAD_DIM + e, q_scale)
    for n, slot in enumerate((G_CMP, G_SLC, G_WIN)):
        put(slot, 0, o_akv + (2 * n) * HEAD_DIM + e)
        put(slot, HEAD_DIM, o_akv + (2 * n + 1) * HEAD_DIM + e)
    for h in range(B_HEADS):
        odd = h % B_HPG
        put(G_BQ + h // B_HPG, odd * HEAD_DIM, o_b + h * HEAD_DIM + e, q_scale)
        put(G_BKV + h, odd * HEAD_DIM, o_b + (B_HEADS + h) * HEAD_DIM + e)
        put(G_BKV + h, (1 - odd) * HEAD_DIM, o_b + (2 * B_HEADS + h) * HEAD_DIM + e)
    for h in range(C_HEADS):
        put(G_CQ + h, 0, o_c + h * HEAD_DIM + e, q_scale)
        put(G_CKV + h, 0, o_c + (C_HEADS + h) * HEAD_DIM + e)
        put(G_CKV + h, HEAD_DIM, o_c + (2 * C_HEADS + h) * HEAD_DIM + e)
    put(G_DQL, 0, o_dq + np.arange(D_Q_LORA))
    put(G_DKVL, 0, o_dkv + np.arange(D_KV_LORA))
    put(G_S1, LANE_GATE, o_ag + np.arange(3 * A_HEADS))
    put(G_S1, LANE_FORGET, o_cf + np.arange(C_HEADS))
    put(G_S1, LANE_ROPE, o_dkr + np.arange(D_ROPE))
    half = D_ROPE // 2
    put(G_S2, LANE_ROPE, o_dkr + half + np.arange(half), -1.0)
    put(G_S2, LANE_ROPE + half, o_dkr + np.arange(half))
    return src, scl


def _gather_cols(w, src, scl):
    pieces = []
    n = len(src)
    a = 0
    while a < n:
        e = a + 1
        if src[a] < 0:
            while e < n and src[e] < 0:
                e += 1
            pieces.append(jnp.zeros((w.shape[0], e - a), w.dtype))
        else:
            while e < n and src[e] == src[e - 1] + 1 and scl[e] == scl[a]:
                e += 1
            run = w[:, int(src[a]):int(src[a]) + (e - a)]
            pieces.append(run if scl[a] == 1.0 else run * float(scl[a]))
        a = e
    return jnp.concatenate(pieces, axis=1)


def _mla_weight_columns():
    per_q = D_NOPE + D_ROPE
    half = D_ROPE // 2
    qa = np.full((D_HEADS * LANES,), -1, np.int64)
    qb = np.full((D_HEADS * LANES,), -1, np.int64)
    qb_s = np.ones((D_HEADS * LANES,), np.float32)
    ka = np.full((D_HEADS * LANES,), -1, np.int64)
    for h in range(D_HEADS):
        qa[h * LANES + np.arange(per_q)] = h * per_q + np.arange(per_q)
        rot = h * per_q + D_NOPE
        qb[h * LANES + D_NOPE + np.arange(half)] = rot + half + np.arange(half)
        qb_s[h * LANES + D_NOPE + np.arange(half)] = -1.0
        qb[h * LANES + D_NOPE + half + np.arange(half)] = rot + np.arange(half)
        ka[h * LANES + np.arange(D_NOPE)] = h * (D_NOPE + D_VDIM) + np.arange(D_NOPE)
    ones = np.ones_like(qb_s)
    return (qa, ones), (qb, qb_s), (ka, ones)


def _value_rows(w_in_l, w_ukv_l):
    o_akv = A_HEADS * HEAD_DIM
    o_c = o_akv + 6 * HEAD_DIM + 3 * A_HEADS + 3 * B_HEADS * HEAD_DIM
    starts = [o_akv + 3 * HEAD_DIM] + [o_c + (2 * C_HEADS + h) * HEAD_DIM for h in range(C_HEADS)]
    wvt = jnp.concatenate([w_in_l[:, a:a + HEAD_DIM] for a in starts], axis=1).T
    per_kv = D_NOPE + D_VDIM
    wdvt = jnp.concatenate([w_ukv_l[:, h * per_kv + D_NOPE:(h + 1) * per_kv] for h in range(D_HEADS)], axis=1).T
    return wvt.astype(BF16), wdvt.astype(BF16)


def _forget_placement():
    place = np.zeros((N_EXTRA, LANES, C_HEADS * LANES), np.float32)
    for n in range(N_EXTRA):
        for h in range(C_HEADS):
            place[n, LANE_FORGET + h, h * LANES + LANE_EXTRA + n] = 1.0
    return jnp.asarray(place, BF16)


def _rope_tables(s):
    inv_freq = ROPE_THETA ** (-jnp.arange(0, D_ROPE, 2, dtype=F32) / D_ROPE)
    ang = jnp.arange(s, dtype=F32)[:, None] * inv_freq[None, :]
    cos2 = jnp.concatenate([jnp.cos(ang)] * 2, axis=1)
    sin2 = jnp.concatenate([jnp.sin(ang)] * 2, axis=1)
    scale = (D_NOPE + D_ROPE) ** -0.5 * LOG2E
    z_lo = jnp.zeros((s, D_NOPE), F32)
    z_hi = jnp.zeros((s, LANES - D_NOPE - D_ROPE), F32)
    c_q = jnp.concatenate([jnp.full((s, D_NOPE), scale, F32), scale * cos2, z_hi], axis=1)
    s_q = jnp.concatenate([z_lo, scale * sin2, z_hi], axis=1)
    c_k = jnp.concatenate([z_lo, cos2, z_hi], axis=1)
    s_k = jnp.concatenate([z_lo, sin2, z_hi], axis=1)
    return jnp.concatenate([c_q, s_q, c_k, s_k], axis=1)


def _toeplitz(vec, n_rows, n_cols, off, step=1):
    base = vec[:, ::step]
    last = base.shape[1] - 1
    lo, hi = off - (n_cols - 1), off + n_rows - 1
    core = base[:, max(lo, 0):min(hi, last) + 1]
    front = jnp.repeat(base[:, :1], max(0, -lo), axis=1)
    back = jnp.repeat(base[:, last:], max(0, hi - last), axis=1)
    g = jnp.concatenate([front, core, back], axis=1)
    lg = n_rows + n_cols - 1
    t = jnp.tile(g, (1, n_rows + 1))[:, :n_rows * (lg + 1)].reshape(-1, n_rows, lg + 1)
    return t[:, :, :n_cols][:, :, ::-1]


def _bias_tables(rel_table, s):
    bucket = _rel_bucket(jnp.arange(s))
    hit = bucket[None, :, None] == jnp.arange(N_BUCKETS)[None, None, :]
    by_dist = jnp.sum(jnp.where(hit, LOG2E * rel_table.T[:, None, :], 0.0), axis=-1)
    tqa = min(TQ_NSA, s)
    dist = jnp.arange(tqa)[:, None] + A_WINDOW - jnp.arange(A_WINDOW + tqa)[None, :]
    ok = (dist >= 0) & (dist < A_WINDOW)
    bias_w = jnp.where(ok[None], _toeplitz(by_dist[:A_HEADS], tqa, A_WINDOW + tqa, A_WINDOW), NEG_INF)
    bias_w = bias_w.reshape(A_HEADS * tqa, A_WINDOW + tqa)
    tf = min(BIAS_TILE, s)
    n_off = min(s // tf, -(-(BUCKET_MAX_DIST + tf - 1) // tf) + 1)
    bias_s = jnp.stack([_toeplitz(by_dist[:A_HEADS], tf, tf, n * tf) for n in range(n_off)])
    m = jnp.arange(TQ_DIL)[:, None] + TQ_DIL - jnp.arange(2 * TQ_DIL)[None, :]
    bias_d = []
    for g, (window, dil) in enumerate(B_GROUPS):
        ok = (m >= 0) & (m <= window // dil)
        heads = by_dist[A_HEADS + g * B_HPG:A_HEADS + (g + 1) * B_HPG]
        bias_d.append(jnp.where(ok[None], _toeplitz(heads, TQ_DIL, 2 * TQ_DIL, TQ_DIL, dil), NEG_INF))
    return bias_w, bias_s, jnp.stack(bias_d)


def _selection_constants(s):
    nc = s // A_CMP_STRIDE
    n_cmp = (s - A_CMP_LEN) // A_CMP_STRIDE + 1
    n_sel = s // A_SEL_BLOCK
    c = np.arange(nc)[:, None]
    j = np.arange(n_sel)[None, :]
    c_start = c * A_CMP_STRIDE
    overlap = ((c_start <= j * A_SEL_BLOCK + A_SEL_BLOCK - 1) & (c_start + A_CMP_LEN - 1 >= j * A_SEL_BLOCK)
               & (c < n_cmp))
    expand_t = (np.arange(s)[:, None] // A_SEL_BLOCK) == np.arange(n_sel)[None, :]
    return jnp.asarray(overlap.T, BF16), jnp.asarray(expand_t, BF16)


def kernel(x, rel_bias_table, norm_attn_pre, norm_attn_post, norm_ffn_pre, norm_ffn_post, w_in, nsa_cmp_pos, nsa_phi_k_w1, nsa_phi_k_w2, nsa_phi_v_w1, nsa_phi_v_w2, fox_forget_bias, mla_q_norm, mla_kv_norm, mla_w_uq, mla_w_ukv, w_branch_a, w_branch_b, w_branch_c, w_branch_d, w_merge_gate, w_o, ffn_w_up, ffn_conv_w, ffn_conv_b, ffn_w_down):
    b, s, d = x.shape
    depth = w_in.shape[0]
    assert s % (TQ_DIL * B_GROUPS[-1][1]) == 0, "every dilation class needs whole 128-row tiles"

    in_src, in_scl = _in_proj_columns()
    (qa_i, qa_s), (qb_i, qb_s), (ka_i, ka_s) = _mla_weight_columns()
    rope_tab = _rope_tables(s)
    bias_w, bias_s, bias_d = _bias_tables(rel_bias_table, s)
    bias_s_t = jnp.swapaxes(bias_s, 2, 3)
    overlap, expand_t = _selection_constants(s)
    place = _forget_placement()
    half = A_CMP_STRIDE * HEAD_DIM

    for l in range(depth):
        w_ext = _gather_cols(w_in[l], in_src, in_scl).astype(BF16)
        fbias = jnp.zeros((1, LANES), F32).at[0, LANE_FORGET:LANE_FORGET + C_HEADS].set(fox_forget_bias[l])
        wvt, wdvt = _value_rows(w_in[l], mla_w_ukv[l])
        wq = jnp.concatenate([_gather_cols(mla_w_uq[l], qa_i, qa_s), _gather_cols(mla_w_uq[l], qb_i, qb_s)],
                             axis=1).astype(BF16)
        (aq, cmp, slc, slc_vt, win, bq, bkv, cq, ck, c_vt, dq, dk, d_vt, small) = _inproj(
            x, norm_attn_pre[l][None], w_ext, wvt, rope_tab, fbias,
            mla_q_norm[l][None], mla_kv_norm[l][None], wq,
            _gather_cols(mla_w_ukv[l], ka_i, ka_s).astype(BF16), wdvt, place)

        nc = s // A_CMP_STRIDE
        w2k_pad = jnp.pad(nsa_phi_k_w2[l], ((0, 0), (0, LANES - HEAD_DIM))).astype(BF16)
        w2v_pad = jnp.pad(nsa_phi_v_w2[l], ((0, 0), (LANES - HEAD_DIM, 0))).astype(BF16)
        kvc = _compress(cmp[..., :HEAD_DIM].reshape(b, nc, half), cmp[..., HEAD_DIM:].reshape(b, nc, half),
                        nsa_cmp_pos[l].reshape(2, half),
                        nsa_phi_k_w1[l].astype(BF16), nsa_phi_v_w1[l].astype(BF16), w2k_pad, w2v_pad)
        kvw_pad = jnp.pad(win, ((0, 0), (A_WINDOW, 0), (0, 0)))
        part, sel_t = _nsa_cmp_win(aq, kvc, kvw_pad, small, overlap, bias_w)
        o_a = _flash("sel", aq, slc, slc_vt, sel_t=sel_t, expand_t=expand_t, bias=bias_s_t, part=part, small=small)

        o_b = _dilated(bq, bkv, bias_d)

        o_c = _flash("fox", cq, ck, c_vt)
        o_d = _flash("mla", dq, dk, d_vt)

        x = _merge(x, o_a, o_b, o_c, o_d,
                   norm_attn_pre[l][None], norm_attn_post[l][None],
                   w_merge_gate[l].astype(BF16),
                   w_branch_a[l].astype(BF16),
                   jnp.concatenate([w_branch_b[l][HEAD_DIM:], w_branch_b[l][:HEAD_DIM]]).astype(BF16),
                   w_branch_c[l].astype(BF16),
                   w_branch_d[l].astype(BF16),
                   w_o[l].astype(BF16))
        x = _ffn(x, norm_ffn_pre[l][None], norm_ffn_post[l][None], ffn_w_up[l].astype(BF16),
                 ffn_conv_w[l], ffn_conv_b[l][None], ffn_w_down[l].astype(BF16))
    return x
```

```python
import functools
import math

import numpy as np
import jax
import jax.numpy as jnp
from jax import lax
from jax.experimental import pallas as pl
from jax.experimental.pallas import tpu as pltpu

F32 = jnp.float32
BF16 = jnp.bfloat16

LANES = 128
MXU_WIDTH = 256
HEAD_DIM = 64
RMS_EPS = 1e-6
NEG_INF = -1e30
FORCE_SCORE = 1e9
LOG2E = math.log2(math.e)
VMEM_LIMIT_BYTES = 56 * 1024 * 1024

N_BUCKETS = 32
BUCKET_EXACT = 16
BUCKET_MAX_DIST = 2048

A_HEADS = 4
A_CMP_LEN = 32
A_CMP_STRIDE = 16
A_PHI_HIDDEN = 128
A_SEL_BLOCK = 64
A_SEL_TOPK = 16
A_WINDOW = 512
B_GROUPS = ((128, 1), (512, 4), (2048, 16))
B_HPG = 2
B_HEADS = B_HPG * len(B_GROUPS)
C_HEADS = 4
D_HEADS = 4
D_Q_LORA = 256
D_KV_LORA = 128
D_NOPE = 64
D_ROPE = 32
D_VDIM = 64
ROPE_THETA = 10000.0
N_BRANCHES = 4

G_AQ = 0
G_CMP = G_AQ + A_HEADS
G_SLC = G_CMP + 1
G_WIN = G_SLC + 1
G_BQ = G_WIN + 1
G_BKV = G_BQ + len(B_GROUPS)
G_CQ = G_BKV + B_HEADS
G_CKV = G_CQ + C_HEADS
G_DQL = G_CKV + C_HEADS
G_DKVL = G_DQL + 2
G_S1 = G_DKVL + 1
G_S2 = G_S1 + 1
N_SLOTS = G_S2 + 1
LANE_GATE = 0
LANE_FORGET = 3 * A_HEADS
LANE_ROPE = 64

TM_PROJ = 512
TQ_NSA_STEP = 256
TQ_FLASH = 512
CHUNK_SLOTS = 8
LANE_EXTRA = HEAD_DIM
N_EXTRA = 3
BIAS_TILE = 256
TQ_NSA = 128
TQ_DIL = 128
CONV_HALO = 16


def _cparams(n_grid):
    return pltpu.CompilerParams(dimension_semantics=("arbitrary",) * n_grid,
                                vmem_limit_bytes=VMEM_LIMIT_BYTES)


def _const_spec(shape):
    nd = len(shape)
    return pl.BlockSpec(shape, lambda *_: (0,) * nd, pipeline_mode=pl.Buffered(1))


def _rms(x, gain):
    return x * lax.rsqrt(jnp.mean(x * x, axis=-1, keepdims=True) + RMS_EPS) * gain


def _sigmoid(x):
    return 1.0 / (1.0 + jnp.exp(-x))


def _dot(a, b):
    return jnp.dot(a, b, preferred_element_type=F32)


def _dot_nt(a, b):
    return lax.dot_general(a, b, (((1,), (1,)), ((), ())), preferred_element_type=F32)


def _slot(h):
    return slice(h * LANES, (h + 1) * LANES)


def _inproj_kernel(x_ref, g_ref, w_ref, tab_ref, fb_ref, qn_ref, kvn_ref, wq_ref, wkv_ref, place_ref,
                   aq_ref, cmp_ref, slc_ref, slcv_ref, win_ref, bq_ref, bkv_ref, cq_ref, ck_ref, cv_ref,
                   dq_ref, dk_ref, dv_ref, small_ref, carry_ref):
    i = pl.program_id(1)
    tm = x_ref.shape[1]
    h = _rms(x_ref[0], g_ref[...]).astype(BF16)

    n_slots = w_ref.shape[1] // LANES
    wide = [_dot(h, w_ref[:, c * LANES:min(c + CHUNK_SLOTS, n_slots) * LANES])
            for c in range(0, n_slots, CHUNK_SLOTS)]

    def proj(g0, n):
        pieces = []
        g = g0
        while g < g0 + n:
            c, off = divmod(g, CHUNK_SLOTS)
            take = min(g0 + n - g, CHUNK_SLOTS - off)
            pieces.append(wide[c][:, off * LANES:(off + take) * LANES])
            g += take
        return pieces[0] if len(pieces) == 1 else jnp.concatenate(pieces, axis=1)

    def key_half(n):
        return (lax.broadcasted_iota(jnp.int32, (tm, n * LANES), 1) & (LANES - 1)) < HEAD_DIM

    aq_ref[0] = proj(G_AQ, A_HEADS).astype(BF16)
    cmp_ref[0] = proj(G_CMP, 1)
    slc = proj(G_SLC, 1)
    slc_ref[0] = slc.astype(BF16)
    slcv_ref[0] = jnp.where(key_half(1), 1.0, slc).astype(BF16)
    win_ref[0] = proj(G_WIN, 1).astype(BF16)
    b_all = proj(G_BQ, len(B_GROUPS) + B_HEADS)
    for n in range(len(B_GROUPS)):
        bq_ref[0, n] = b_all[:, _slot(n)]
    for n in range(B_HEADS):
        bkv_ref[0, n] = b_all[:, _slot(len(B_GROUPS) + n)]

    s1 = proj(G_S1, 1)
    s2 = proj(G_S2, 1)

    @pl.when(i == 0)
    def _():
        carry_ref[...] = jnp.zeros_like(carry_ref)

    def split3(v):
        hi = v.astype(BF16)
        r1 = v - hi.astype(F32)
        mid = r1.astype(BF16)
        return hi, mid, (r1 - mid.astype(F32)).astype(BF16)

    z = s1 + fb_ref[...]
    logf = jnp.minimum(z, 0.0) - jnp.log(1.0 + jnp.exp(-jnp.abs(z)))
    row = lax.broadcasted_iota(jnp.int32, (tm, tm), 0)
    col = lax.broadcasted_iota(jnp.int32, (tm, tm), 1)
    tri = (row >= col).astype(BF16)
    cum = sum(_dot(tri, part) for part in split3(logf)) + carry_ref[0:1, :]
    carry_ref[0:1, :] = cum[tm - 1:tm, :]
    lane = lax.broadcasted_iota(jnp.int32, (tm, LANES), 1)
    is_forget = (lane >= LANE_FORGET) & (lane < LANE_FORGET + C_HEADS)
    small_ref[0] = jnp.where(is_forget, cum, s1)

    extras = sum(_dot(part, place_ref[n]) for n, part in enumerate(split3(-LOG2E * cum)))
    lane_c = lax.broadcasted_iota(jnp.int32, (tm, C_HEADS * LANES), 1) & (LANES - 1)
    is_extra = (lane_c >= LANE_EXTRA) & (lane_c < LANE_EXTRA + N_EXTRA)
    cq_ref[0] = jnp.where(is_extra, 1.0, proj(G_CQ, C_HEADS)).astype(BF16)
    ckv = proj(G_CKV, C_HEADS)
    ck_ref[0] = jnp.where(is_extra, extras, ckv).astype(BF16)
    cv_ref[0] = jnp.where(key_half(C_HEADS), 1.0, ckv).astype(BF16)

    tab = tab_ref[...]
    c_q, s_q, c_k, s_k = (tab[:, _slot(n)] for n in range(4))
    qn = _rms(proj(G_DQL, 2), qn_ref[...]).astype(BF16)
    qab = _dot(qn, wq_ref[...])
    kvn = _rms(proj(G_DKVL, 1), kvn_ref[...]).astype(BF16)
    kv_up = _dot(kvn, wkv_ref[...])
    k_rot = s1 * c_k + s2 * s_k
    low1 = key_half(1)
    for hd in range(D_HEADS):
        q_rot = qab[:, _slot(hd)] * c_q + qab[:, _slot(D_HEADS + hd)] * s_q
        dq_ref[0, :, _slot(hd)] = q_rot.astype(BF16)
        dk_ref[0, :, _slot(hd)] = jnp.where(low1, kv_up[:, _slot(hd)], k_rot).astype(BF16)
    dv_ref[0] = jnp.where(key_half(D_HEADS), 1.0, kv_up).astype(BF16)


def _inproj(x, gain, w_ext, tab, fbias, qn_g, kvn_g, wq, wkv, place):
    b, s, d = x.shape
    tm = min(TM_PROJ, s)

    def out(kind, n, dtype):
        if kind == "rows":
            return (pl.BlockSpec((1, tm, n * LANES), lambda bi, i: (bi, i, 0)),
                    jax.ShapeDtypeStruct((b, s, n * LANES), dtype))
        return (pl.BlockSpec((1, n, tm, LANES), lambda bi, i: (bi, 0, i, 0)),
                jax.ShapeDtypeStruct((b, n, s, LANES), dtype))

    outs = [out("rows", A_HEADS, BF16), out("rows", 1, F32), out("rows", 1, BF16), out("rows", 1, BF16),
            out("rows", 1, BF16), out("slots", len(B_GROUPS), F32), out("slots", B_HEADS, F32),
            out("rows", C_HEADS, BF16), out("rows", C_HEADS, BF16), out("rows", C_HEADS, BF16),
            out("rows", D_HEADS, BF16), out("rows", D_HEADS, BF16), out("rows", D_HEADS, BF16),
            out("rows", 1, F32)]
    consts = (gain, w_ext)
    consts2 = (fbias, qn_g, kvn_g, wq, wkv, place)
    return pl.pallas_call(
        _inproj_kernel,
        grid=(b, s // tm),
        in_specs=[pl.BlockSpec((1, tm, d), lambda bi, i: (bi, i, 0))]
        + [_const_spec(c.shape) for c in consts]
        + [pl.BlockSpec((tm, 4 * LANES), lambda bi, i: (i, 0))]
        + [_const_spec(c.shape) for c in consts2],
        out_specs=[o[0] for o in outs],
        out_shape=[o[1] for o in outs],
        scratch_shapes=[pltpu.VMEM((8, LANES), F32)],
        name="inproj",
        compiler_params=_cparams(2),
    )(x, *consts, tab, *consts2)


def _gelu_tanh(x):
    return 0.5 * x * (1.0 + jnp.tanh(math.sqrt(2.0 / math.pi) * (x + 0.044715 * (x * x * x))))


def _compress_kernel(kc_ref, vc_ref, pos_ref, w1k_ref, w1v_ref, w2k_ref, w2v_ref, out_ref):
    nc = kc_ref.shape[1]
    half = A_CMP_STRIDE * HEAD_DIM
    pos_a = pos_ref[0:1, :]
    pos_b = pos_ref[1:2, :]

    def phi(chunks, w1_ref, w2_ref):
        u = _dot((chunks + pos_a).astype(BF16), w1_ref[0:half, :])
        v = _dot((chunks + pos_b).astype(BF16), w1_ref[half:2 * half, :])
        hid = u + pltpu.roll(v, nc - 1, 0)
        return _dot(_gelu_tanh(hid).astype(BF16), w2_ref[...])

    out_ref[0] = (phi(kc_ref[0], w1k_ref, w2k_ref) + phi(vc_ref[0], w1v_ref, w2v_ref)).astype(BF16)


def _compress(kc_in, vc_in, pos2, w1k, w1v, w2k_pad, w2v_pad):
    b, nc, width = kc_in.shape
    blk = pl.BlockSpec((1, nc, width), lambda bi: (bi, 0, 0))
    return pl.pallas_call(
        _compress_kernel,
        grid=(b,),
        in_specs=[blk, blk, _const_spec(pos2.shape), _const_spec(w1k.shape), _const_spec(w1v.shape),
                  _const_spec(w2k_pad.shape), _const_spec(w2v_pad.shape)],
        out_specs=pl.BlockSpec((1, nc, LANES), lambda bi: (bi, 0, 0)),
        out_shape=jax.ShapeDtypeStruct((b, nc, LANES), BF16),
        name="compress",
        compiler_params=_cparams(1),
    )(kc_in, vc_in, pos2, w1k, w1v, w2k_pad, w2v_pad)


def _nsa_cmp_win_kernel(q_ref, kvc_ref, kvw_ref, small_ref, ovl_ref, bias_ref,
                        part_ref, sel_ref, *, n_top):
    tq = min(TQ_NSA, q_ref.shape[1])
    for u in range(q_ref.shape[1] // tq):
        _nsa_cmp_win_tile(u * tq, tq, q_ref, kvc_ref, kvw_ref, small_ref, ovl_ref, bias_ref,
                          part_ref, sel_ref, n_top)


def _nsa_cmp_win_tile(r0, tq, q_ref, kvc_ref, kvw_ref, small_ref, ovl_ref, bias_ref, part_ref, sel_ref, n_top):
    nc = kvc_ref.shape[1]
    n_sel = ovl_ref.shape[0]
    t0 = pl.multiple_of(pl.program_id(1) * q_ref.shape[1] + r0, tq)
    tile = slice(r0, r0 + tq)
    q4 = jnp.concatenate([q_ref[0, tile, _slot(h)] for h in range(A_HEADS)], axis=0)
    rows = A_HEADS * tq

    kvc = kvc_ref[0]
    s = _dot_nt(q4, kvc)
    t_row = t0 + (lax.broadcasted_iota(jnp.int32, (rows, nc), 0) & (tq - 1))
    c_end = lax.broadcasted_iota(jnp.int32, (rows, nc), 1) * A_CMP_STRIDE + (A_CMP_LEN - 1)
    valid = c_end <= t_row
    s = jnp.where(valid, s, NEG_INF)
    m = jnp.max(s, axis=1, keepdims=True)
    p = jnp.where(valid, jnp.exp2(s - m), 0.0)
    den = jnp.maximum(jnp.sum(p, axis=1, keepdims=True), 1e-30)
    p = p / den
    o_c = _dot(p.astype(BF16), kvc)

    p_sum = p[0:tq] + p[tq:2 * tq] + p[2 * tq:3 * tq] + p[3 * tq:4 * tq]
    p_hi = p_sum.astype(BF16)
    p_lo = (p_sum - p_hi.astype(F32)).astype(BF16)
    imp_t = _dot_nt(ovl_ref[...], p_hi) + _dot_nt(ovl_ref[...], p_lo)

    j_idx = lax.broadcasted_iota(jnp.int32, (n_sel, tq), 0)
    cur = lax.shift_right_logical(t0 + lax.broadcasted_iota(jnp.int32, (n_sel, tq), 1), 6)
    forced = (j_idx == 0) | (j_idx == cur) | (j_idx == cur - 1)
    imp_t = jnp.where(forced, FORCE_SCORE, jnp.where(j_idx > cur, -FORCE_SCORE, imp_t))
    sub8 = 8
    groups = [imp_t[v * sub8:(v + 1) * sub8] for v in range(n_sel // sub8)]
    ranks = [jnp.zeros((sub8, tq), F32) for _ in groups]
    in_group = lax.broadcasted_iota(jnp.int32, (sub8, tq), 0)
    for c in range(n_sel):
        row_c = imp_t[c:c + 1, :]
        for v, grp in enumerate(groups):
            if v > c // sub8:
                beats = row_c >= grp
            elif v < c // sub8:
                beats = row_c > grp
            else:
                beats = (row_c > grp) | ((row_c == grp) & (in_group > c % sub8))
            ranks[v] = ranks[v] + jnp.where(beats, 1.0, 0.0)
    rank = jnp.concatenate(ranks, axis=0)
    sel_t = jnp.where(rank < float(n_top), 1.0, 0.0).astype(BF16)
    sel_ref[0, :, tile] = sel_t

    span = A_WINDOW + tq
    kvw = kvw_ref[0, pl.ds(t0, span), :]
    sw = _dot_nt(q4, kvw) + bias_ref[...]
    kpos = t0 - A_WINDOW + lax.broadcasted_iota(jnp.int32, (rows, span), 1)
    sw = jnp.where(kpos >= 0, sw, NEG_INF)
    mw = jnp.max(sw, axis=1, keepdims=True)
    pw = jnp.exp2(sw - mw)
    o_w = _dot(pw.astype(BF16), kvw) / jnp.sum(pw, axis=1, keepdims=True)

    g = _sigmoid(small_ref[0, tile])
    gated = []
    for h in range(A_HEADS):
        g_c = g[:, LANE_GATE + 3 * h:LANE_GATE + 3 * h + 1]
        g_w = g[:, LANE_GATE + 3 * h + 2:LANE_GATE + 3 * h + 3]
        gated.append(g_c * o_c[h * tq:(h + 1) * tq] + g_w * o_w[h * tq:(h + 1) * tq])
    low = lax.broadcasted_iota(jnp.int32, (tq, LANES), 1) < HEAD_DIM
    for pair in range(A_HEADS // 2):
        part_ref[0, tile, _slot(pair)] = jnp.where(low, pltpu.roll(gated[2 * pair], HEAD_DIM, 1),
                                                   gated[2 * pair + 1])


def _nsa_cmp_win(aq, kvc, kvw_pad, small, ovl, bias_w):
    b, s, _ = aq.shape
    tq = min(TQ_NSA_STEP, s)
    nc = kvc.shape[1]
    n_sel = ovl.shape[0]
    n_top = min(A_SEL_TOPK, n_sel)
    return pl.pallas_call(
        functools.partial(_nsa_cmp_win_kernel, n_top=n_top),
        grid=(b, s // tq),
        in_specs=[pl.BlockSpec((1, tq, A_HEADS * LANES), lambda bi, i: (bi, i, 0)),
                  pl.BlockSpec((1, nc, LANES), lambda bi, i: (bi, 0, 0)),
                  pl.BlockSpec((1, s + A_WINDOW, LANES), lambda bi, i: (bi, 0, 0)),
                  pl.BlockSpec((1, tq, LANES), lambda bi, i: (bi, i, 0)),
                  _const_spec(ovl.shape),
                  _const_spec(bias_w.shape)],
        out_specs=[pl.BlockSpec((1, tq, A_HEADS * HEAD_DIM), lambda bi, i: (bi, i, 0)),
                   pl.BlockSpec((1, n_sel, tq), lambda bi, i: (bi, 0, i))],
        out_shape=[jax.ShapeDtypeStruct((b, s, A_HEADS * HEAD_DIM), F32),
                   jax.ShapeDtypeStruct((b, n_sel, s), BF16)],
        name="nsa_cmp_win",
        compiler_params=_cparams(2),
    )(aq, kvc, kvw_pad, small, ovl, bias_w)


def _flash_kernel(*refs, mode, nh):
    if mode == "sel":
        q_ref, k_ref, v_ref, selt_ref, e_ref, bias_ref, part_ref, small_ref, o_ref, m_ref, acc_ref = refs
    else:
        q_ref, k_ref, v_ref, o_ref, m_ref, acc_ref = refs
    i = pl.program_id(1)
    tq = q_ref.shape[1]
    tk = tq
    shared_kv = mode == "sel"

    m_ref[...] = jnp.full(m_ref.shape, NEG_INF, F32)
    acc_ref[...] = jnp.zeros(acc_ref.shape, F32)

    lag = lax.broadcasted_iota(jnp.int32, (tq, tk), 0) - lax.broadcasted_iota(jnp.int32, (tq, tk), 1)

    def step(j, diag):
        k0 = pl.multiple_of(j * tk, tk)
        keep = (lag >= 0) if diag else None
        if mode == "sel":
            sel_keys = lax.dot_general(selt_ref[0], e_ref[:, pl.ds(k0, tk)], (((0,), (0,)), ((), ())),
                                       preferred_element_type=F32) > 0.5
            keep = (sel_keys & keep) if diag else sel_keys
            n_off, bt = bias_ref.shape[0], bias_ref.shape[2]

            def bias_tile(h):
                strips = []
                for a in range(tq // bt):
                    offs = [jnp.clip(i * (tq // bt) + a - j * (tk // bt) - c, 0, n_off - 1)
                            for c in range(tk // bt)]
                    strips.append(jnp.concatenate([bias_ref[o, h] for o in offs], axis=1))
                return jnp.concatenate(strips, axis=0)
        logits = []
        for h in range(nh):
            s = _dot_nt(q_ref[0, :, _slot(h)], k_ref[0, pl.ds(k0, tk), _slot(0 if shared_kv else h)])
            if mode == "sel":
                s = s + bias_tile(h)
            if keep is not None:
                s = jnp.where(keep, s, NEG_INF)
            logits.append(s)
        probs = []
        for h in range(nh):
            m_prev = m_ref[h]
            m_new = jnp.maximum(m_prev, jnp.max(logits[h], axis=1, keepdims=True))
            p = jnp.exp2(logits[h] - jnp.concatenate([m_new] * (tk // LANES), axis=1))
            alpha = jnp.exp2(m_prev - m_new)
            m_ref[h] = m_new
            probs.append((p.astype(BF16), alpha))
        for h in range(nh):
            p, alpha = probs[h]
            acc_ref[h] = alpha * acc_ref[h] + _dot(p, v_ref[0, pl.ds(k0, tk), _slot(0 if shared_kv else h)])

    def full_tile(j, carry):
        step(j, False)
        return carry

    lax.fori_loop(0, i, full_tile, 0)
    step(i, True)

    low = lax.broadcasted_iota(jnp.int32, (tq, LANES), 1) < HEAD_DIM
    if mode == "sel":
        g = _sigmoid(small_ref[0])
    for pair in range(nh // 2):
        outs = []
        for h in (2 * pair, 2 * pair + 1):
            acc = acc_ref[h]
            outs.append(acc / acc[:, 0:1])
        o_pair = jnp.where(low, pltpu.roll(outs[0], HEAD_DIM, 1), outs[1])
        if mode == "sel":
            gate = [g[:, LANE_GATE + 3 * h + 1:LANE_GATE + 3 * h + 2] for h in (2 * pair, 2 * pair + 1)]
            o_pair = part_ref[0, :, _slot(pair)] + jnp.where(low, gate[0], gate[1]) * o_pair
        o_ref[0, :, _slot(pair)] = o_pair.astype(BF16)


def _flash(mode, q, k, v, sel_t=None, expand=None, bias=None, part=None, small=None):
    b, s, qw = q.shape
    nh = qw // LANES
    tq = min(TQ_FLASH, s)
    ow = nh * HEAD_DIM

    def rows(width):
        return pl.BlockSpec((1, tq, width), lambda bi, i: (bi, i, 0))

    def seq_spec(arr):
        return pl.BlockSpec((1,) + arr.shape[1:], lambda bi, i: (bi, 0, 0))

    args = (q, k, v)
    in_specs = [rows(qw), seq_spec(k), seq_spec(v)]
    if mode == "sel":
        args += (sel_t, expand, bias, part, small)
        in_specs += [pl.BlockSpec((1, sel_t.shape[1], tq), lambda bi, i: (bi, 0, i)),
                     _const_spec(expand.shape), _const_spec(bias.shape), rows(ow), rows(LANES)]
    return pl.pallas_call(
        functools.partial(_flash_kernel, mode=mode, nh=nh),
        grid=(b, s // tq),
        in_specs=in_specs,
        out_specs=rows(ow),
        out_shape=jax.ShapeDtypeStruct((b, s, ow), BF16),
        scratch_shapes=[pltpu.VMEM((nh, tq, LANES), F32)] * 2,
        name="flash_" + mode,
        compiler_params=_cparams(2),
    )(*args)


def _dilated_kernel(q_ref, kvp_ref, kvc_ref, bias_ref, o_ref, o_sc, l_sc):
    i = pl.program_id(1)
    rows = q_ref.shape[2]
    tb = TQ_DIL
    low = lax.broadcasted_iota(jnp.int32, (tb, LANES), 1) < HEAD_DIM
    col = lax.broadcasted_iota(jnp.int32, (tb, 2 * tb), 1)
    in_seq = (col >= tb) | (i > 0)
    for g, (_, dil) in enumerate(B_GROUPS):
        span = dil * tb
        for r in range(dil):
            for c in range(rows // span):
                cur = pl.ds(r + c * span, tb, stride=dil)
                if c > 0:
                    prev_ref, prev = kvc_ref, pl.ds(r + (c - 1) * span, tb, stride=dil)
                else:
                    prev_ref, prev = kvp_ref, pl.ds(rows - span + r, tb, stride=dil)
                q2 = q_ref[0, g, cur, :]
                outs, lses = [], []
                for h in range(B_HPG):
                    q_h = jnp.where(low if h == 0 else ~low, q2, 0.0).astype(BF16)
                    head = B_HPG * g + h
                    kv = jnp.concatenate([prev_ref[0, head, prev, :], kvc_ref[0, head, cur, :]],
                                         axis=0).astype(BF16)
                    s = _dot_nt(q_h, kv) + bias_ref[g, h]
                    if c == 0:
                        s = jnp.where(in_seq, s, NEG_INF)
                    m = jnp.max(s, axis=1, keepdims=True)
                    p = jnp.exp2(s - m)
                    den = jnp.sum(p, axis=1, keepdims=True)
                    outs.append(_dot(p.astype(BF16), kv) / den)
                    lses.append(m + jnp.log2(den))
                o_sc[g, cur, :] = jnp.where(low, outs[1], outs[0])
                l_sc[g, cur, :] = jnp.where(low, lses[1], lses[0])
    step = 2 * tb
    for t in range(rows // step):
        rs = slice(t * step, (t + 1) * step)
        l0, l1, l2 = l_sc[0, rs, :], l_sc[1, rs, :], l_sc[2, rs, :]
        mx = jnp.maximum(jnp.maximum(l0, l1), l2)
        e0, e1, e2 = jnp.exp2(l0 - mx), jnp.exp2(l1 - mx), jnp.exp2(l2 - mx)
        mixed = (e0 * o_sc[0, rs, :] + e1 * o_sc[1, rs, :] + e2 * o_sc[2, rs, :]) / (e0 + e1 + e2)
        o_ref[0, rs, :] = mixed.astype(BF16)


def _dilated(bq, bkv, bias):
    b, _, s, _ = bq.shape
    rows = TQ_DIL * B_GROUPS[-1][1]
    n_groups = len(B_GROUPS)

    def tile(n_slots, prev):
        if prev:
            return pl.BlockSpec((1, n_slots, rows, LANES), lambda bi, i: (bi, 0, jnp.maximum(i - 1, 0), 0))
        return pl.BlockSpec((1, n_slots, rows, LANES), lambda bi, i: (bi, 0, i, 0))

    return pl.pallas_call(
        _dilated_kernel,
        grid=(b, s // rows),
        in_specs=[tile(bq.shape[1], False), tile(bkv.shape[1], True), tile(bkv.shape[1], False),
                  _const_spec(bias.shape)],
        out_specs=pl.BlockSpec((1, rows, LANES), lambda bi, i: (bi, i, 0)),
        out_shape=jax.ShapeDtypeStruct((b, s, LANES), BF16),
        scratch_shapes=[pltpu.VMEM((n_groups, rows, LANES), F32)] * 2,
        name="dilated",
        compiler_params=_cparams(2),
    )(bq, bkv, bkv, bias)


def _merge_kernel(x_ref, oa_ref, ob_ref, oc_ref, od_ref,
                  gpre_ref, gpost_ref, wg_ref, wba_ref, wbb_ref, wbc_ref, wbd_ref, wo_ref, out_ref):
    x = x_ref[0]
    h = _rms(x, gpre_ref[...]).astype(BF16)
    branches = ((oa_ref[0], wba_ref), (ob_ref[0], wbb_ref), (oc_ref[0], wbc_ref), (od_ref[0], wbd_ref))
    merged = None
    for n, (o_n, wb_ref) in enumerate(branches):
        term = _sigmoid(_dot(h, wg_ref[n])) * _dot(o_n, wb_ref[...])
        merged = term if merged is None else merged + term
    z = _dot(merged.astype(BF16), wo_ref[...])
    out_ref[0] = x + _rms(z, gpost_ref[...])


def _merge(x, oa, ob, oc, od, gpre, gpost, wg, wba, wbb, wbc, wbd, wo):
    b, s, d = x.shape
    tm = min(TM_PROJ, s)

    def rows(arr):
        return pl.BlockSpec((1, tm, arr.shape[2]), lambda bi, i: (bi, i, 0))

    acts = (x, oa, ob, oc, od)
    consts = (gpre, gpost, wg, wba, wbb, wbc, wbd, wo)
    return pl.pallas_call(
        _merge_kernel,
        grid=(b, s // tm),
        in_specs=[rows(a) for a in acts] + [_const_spec(c.shape) for c in consts],
        out_specs=rows(x),
        out_shape=jax.ShapeDtypeStruct(x.shape, F32),
        name="merge",
        compiler_params=_cparams(2),
    )(*acts, *consts)


def _ffn_kernel(x_ref, xp_ref, gpre_ref, gpost_ref, wup_ref, cw_ref, cb_ref, wdn_ref, out_ref, *, cuts):
    i = pl.program_id(1)
    tm = x_ref.shape[1]
    d_ff = wdn_ref.shape[0]
    x = x_ref[0]
    halo = jnp.where(i > 0, xp_ref[0], 0.0)
    he = _rms(jnp.concatenate([halo, x], axis=0), gpre_ref[...]).astype(BF16)

    def conv(c0, chunk):
        u = _dot(he, wup_ref[:, c0:c0 + chunk])
        w = cw_ref[:, c0:c0 + chunk]
        out = cb_ref[:, c0:c0 + chunk] + w[0:1] * pltpu.roll(u, 2, 0)[CONV_HALO:]
        out = out + w[1:2] * pltpu.roll(u, 1, 0)[CONV_HALO:]
        return out + w[2:3] * u[CONV_HALO:]

    y = jnp.zeros((tm, x.shape[1]), F32)
    for c0, c1 in zip(cuts[:-1], cuts[1:]):
        gate = conv(c0, c1 - c0)
        val = conv(d_ff + c0, c1 - c0)
        act = gate * _sigmoid(gate) * val
        y = y + _dot(act.astype(BF16), wdn_ref[c0:c1, :])
    out_ref[0] = x + _rms(y, gpost_ref[...])


def _ffn(x, gpre, gpost, w_up, conv_w, conv_b, w_down):
    b, s, d = x.shape
    tm = min(TM_PROJ, s)
    d_ff = w_down.shape[0]
    n_tiles = d_ff // MXU_WIDTH
    cuts = (0, (n_tiles // 2) * MXU_WIDTH, d_ff) if d_ff % MXU_WIDTH == 0 and n_tiles > 1 else (0, d_ff)
    halo_blocks = tm // CONV_HALO
    return pl.pallas_call(
        functools.partial(_ffn_kernel, cuts=cuts),
        grid=(b, s // tm),
        in_specs=[pl.BlockSpec((1, tm, d), lambda bi, i: (bi, i, 0)),
                  pl.BlockSpec((1, CONV_HALO, d), lambda bi, i: (bi, jnp.maximum(i * halo_blocks - 1, 0), 0)),
                  _const_spec(gpre.shape), _const_spec(gpost.shape), _const_spec(w_up.shape),
                  _const_spec(conv_w.shape), _const_spec(conv_b.shape), _const_spec(w_down.shape)],
        out_specs=pl.BlockSpec((1, tm, d), lambda bi, i: (bi, i, 0)),
        out_shape=jax.ShapeDtypeStruct(x.shape, F32),
        name="ffn",
        compiler_params=_cparams(2),
    )(x, x, gpre, gpost, w_up, conv_w, conv_b, w_down)


def _rel_bucket(dist):
    dist = jnp.maximum(dist, 0)
    d = jnp.maximum(dist, 1).astype(F32)
    large = BUCKET_EXACT + (jnp.log(d / BUCKET_EXACT) / math.log(BUCKET_MAX_DIST / BUCKET_EXACT)
                            * (N_BUCKETS - BUCKET_EXACT)).astype(jnp.int32)
    large = jnp.minimum(large, N_BUCKETS - 1)
    return jnp.where(dist < BUCKET_EXACT, dist, large)


def _in_proj_columns():
    src = np.full((N_SLOTS * LANES,), -1, np.int64)
    scl = np.ones((N_SLOTS * LANES,), np.float32)
    o_aq = 0
    o_akv = o_aq + A_HEADS * HEAD_DIM
    o_ag = o_akv + 6 * HEAD_DIM
    o_b = o_ag + 3 * A_HEADS
    o_c = o_b + 3 * B_HEADS * HEAD_DIM
    o_cf = o_c + 3 * C_HEADS * HEAD_DIM
    o_dq = o_cf + C_HEADS
    o_dkv = o_dq + D_Q_LORA
    o_dkr = o_dkv + D_KV_LORA
    e = np.arange(HEAD_DIM)
    q_scale = HEAD_DIM ** -0.5 * LOG2E

    def put(slot, lane0, cols, scale=1.0):
        dst = slot * LANES + lane0 + np.arange(len(cols))
        src[dst] = cols
        scl[dst] = scale

    for h in range(A_HEADS):
        put(G_AQ + h, 0, o_aq + h * HEAD_DIM + e, q_scale)
    for n, slot in enumerate((G_CMP, G_SLC, G_WIN)):
        put(slot, 0, o_akv + (2 * n) * HEAD_DIM + e)
        put(slot, HEAD_DIM, o_akv + (2 * n + 1) * HEAD_DIM + e)
    for h in range(B_HEADS):
        odd = h % B_HPG
        put(G_BQ + h // B_HPG, odd * HEAD_DIM, o_b + h * HEAD_DIM + e, q_scale)
        put(G_BKV + h, odd * HEAD_DIM, o_b + (B_HEADS + h) * HEAD_DIM + e)
        put(G_BKV + h, (1 - odd) * HEAD_DIM, o_b + (2 * B_HEADS + h) * HEAD_DIM + e)
    for h in range(C_HEADS):
        put(G_CQ + h, 0, o_c + h * HEAD_DIM + e, q_scale)
        put(G_CKV + h, 0, o_c + (C_HEADS + h) * HEAD_DIM + e)
        put(G_CKV + h, HEAD_DIM, o_c + (2 * C_HEADS + h) * HEAD_DIM + e)
    put(G_DQL, 0, o_dq + np.arange(D_Q_LORA))
    put(G_DKVL, 0, o_dkv + np.arange(D_KV_LORA))
    put(G_S1, LANE_GATE, o_ag + np.arange(3 * A_HEADS))
    put(G_S1, LANE_FORGET, o_cf + np.arange(C_HEADS))
    put(G_S1, LANE_ROPE, o_dkr + np.arange(D_ROPE))
    half = D_ROPE // 2
    put(G_S2, LANE_ROPE, o_dkr + half + np.arange(half), -1.0)
    put(G_S2, LANE_ROPE + half, o_dkr + np.arange(half))
    return src, scl


def _gather_cols(w, src, scl):
    pieces = []
    n = len(src)
    a = 0
    while a < n:
        e = a + 1
        if src[a] < 0:
            while e < n and src[e] < 0:
                e += 1
            pieces.append(jnp.zeros((w.shape[0], e - a), w.dtype))
        else:
            while e < n and src[e] == src[e - 1] + 1 and scl[e] == scl[a]:
                e += 1
            run = w[:, int(src[a]):int(src[a]) + (e - a)]
            pieces.append(run if scl[a] == 1.0 else run * float(scl[a]))
        a = e
    return jnp.concatenate(pieces, axis=1)


def _mla_weight_columns():
    per_q = D_NOPE + D_ROPE
    half = D_ROPE // 2
    qa = np.full((D_HEADS * LANES,), -1, np.int64)
    qb = np.full((D_HEADS * LANES,), -1, np.int64)
    qb_s = np.ones((D_HEADS * LANES,), np.float32)
    for h in range(D_HEADS):
        qa[h * LANES + np.arange(per_q)] = h * per_q + np.arange(per_q)
        rot = h * per_q + D_NOPE
        qb[h * LANES + D_NOPE + np.arange(half)] = rot + half + np.arange(half)
        qb_s[h * LANES + D_NOPE + np.arange(half)] = -1.0
        qb[h * LANES + D_NOPE + half + np.arange(half)] = rot + np.arange(half)
    ones = np.ones_like(qb_s)
    return (qa, ones), (qb, qb_s)


def _forget_placement():
    place = np.zeros((N_EXTRA, LANES, C_HEADS * LANES), np.float32)
    for n in range(N_EXTRA):
        for h in range(C_HEADS):
            place[n, LANE_FORGET + h, h * LANES + LANE_EXTRA + n] = 1.0
    return jnp.asarray(place, BF16)


def _rope_tables(s):
    inv_freq = ROPE_THETA ** (-jnp.arange(0, D_ROPE, 2, dtype=F32) / D_ROPE)
    ang = jnp.arange(s, dtype=F32)[:, None] * inv_freq[None, :]
    cos2 = jnp.concatenate([jnp.cos(ang)] * 2, axis=1)
    sin2 = jnp.concatenate([jnp.sin(ang)] * 2, axis=1)
    scale = (D_NOPE + D_ROPE) ** -0.5 * LOG2E
    z_lo = jnp.zeros((s, D_NOPE), F32)
    z_hi = jnp.zeros((s, LANES - D_NOPE - D_ROPE), F32)
    c_q = jnp.concatenate([jnp.full((s, D_NOPE), scale, F32), scale * cos2, z_hi], axis=1)
    s_q = jnp.concatenate([z_lo, scale * sin2, z_hi], axis=1)
    c_k = jnp.concatenate([z_lo, cos2, z_hi], axis=1)
    s_k = jnp.concatenate([z_lo, sin2, z_hi], axis=1)
    return jnp.concatenate([c_q, s_q, c_k, s_k], axis=1)


def _toeplitz(vec, n_rows, n_cols, off, step=1):
    base = vec[:, ::step]
    last = base.shape[1] - 1
    lo, hi = off - (n_cols - 1), off + n_rows - 1
    core = base[:, max(lo, 0):min(hi, last) + 1]
    front = jnp.repeat(base[:, :1], max(0, -lo), axis=1)
    back = jnp.repeat(base[:, last:], max(0, hi - last), axis=1)
    g = jnp.concatenate([front, core, back], axis=1)
    lg = n_rows + n_cols - 1
    t = jnp.tile(g, (1, n_rows + 1))[:, :n_rows * (lg + 1)].reshape(-1, n_rows, lg + 1)
    return t[:, :, :n_cols][:, :, ::-1]


def _bias_tables(rel_table, s):
    bucket = _rel_bucket(jnp.arange(s))
    hit = bucket[None, :, None] == jnp.arange(N_BUCKETS)[None, None, :]
    by_dist = jnp.sum(jnp.where(hit, LOG2E * rel_table.T[:, None, :], 0.0), axis=-1)
    tqa = min(TQ_NSA, s)
    dist = jnp.arange(tqa)[:, None] + A_WINDOW - jnp.arange(A_WINDOW + tqa)[None, :]
    ok = (dist >= 0) & (dist < A_WINDOW)
    bias_w = jnp.where(ok[None], _toeplitz(by_dist[:A_HEADS], tqa, A_WINDOW + tqa, A_WINDOW), NEG_INF)
    bias_w = bias_w.reshape(A_HEADS * tqa, A_WINDOW + tqa)
    tf = min(BIAS_TILE, s)
    n_off = min(s // tf, -(-(BUCKET_MAX_DIST + tf - 1) // tf) + 1)
    bias_s = jnp.stack([_toeplitz(by_dist[:A_HEADS], tf, tf, n * tf) for n in range(n_off)])
    m = jnp.arange(TQ_DIL)[:, None] + TQ_DIL - jnp.arange(2 * TQ_DIL)[None, :]
    bias_d = []
    for g, (window, dil) in enumerate(B_GROUPS):
        ok = (m >= 0) & (m <= window // dil)
        heads = by_dist[A_HEADS + g * B_HPG:A_HEADS + (g + 1) * B_HPG]
        bias_d.append(jnp.where(ok[None], _toeplitz(heads, TQ_DIL, 2 * TQ_DIL, TQ_DIL, dil), NEG_INF))
    return bias_w, bias_s, jnp.stack(bias_d)


def _selection_constants(s):
    nc = s // A_CMP_STRIDE
    n_cmp = (s - A_CMP_LEN) // A_CMP_STRIDE + 1
    n_sel = s // A_SEL_BLOCK
    c = np.arange(nc)[:, None]
    j = np.arange(n_sel)[None, :]
    c_start = c * A_CMP_STRIDE
    overlap = ((c_start <= j * A_SEL_BLOCK + A_SEL_BLOCK - 1) & (c_start + A_CMP_LEN - 1 >= j * A_SEL_BLOCK)
               & (c < n_cmp))
    expand = (np.arange(s)[None, :] // A_SEL_BLOCK) == np.arange(n_sel)[:, None]
    return jnp.asarray(overlap.T, BF16), jnp.asarray(expand, BF16)


def kernel(x, rel_bias_table, norm_attn_pre, norm_attn_post, norm_ffn_pre, norm_ffn_post, w_in, nsa_cmp_pos, nsa_phi_k_w1, nsa_phi_k_w2, nsa_phi_v_w1, nsa_phi_v_w2, fox_forget_bias, mla_q_norm, mla_kv_norm, mla_w_uq, mla_w_ukv, w_branch_a, w_branch_b, w_branch_c, w_branch_d, w_merge_gate, w_o, ffn_w_up, ffn_conv_w, ffn_conv_b, ffn_w_down):
    b, s, d = x.shape
    depth = w_in.shape[0]
    assert s % (TQ_DIL * B_GROUPS[-1][1]) == 0, "every dilation class needs whole 128-row tiles"

    in_src, in_scl = _in_proj_columns()
    assert D_NOPE + D_VDIM == LANES, "an MLA head's [k_nope | v] up-projection fills one slot"
    (qa_i, qa_s), (qb_i, qb_s) = _mla_weight_columns()
    rope_tab = _rope_tables(s)
    bias_w, bias_s, bias_d = _bias_tables(rel_bias_table, s)
    overlap, expand = _selection_constants(s)
    place = _forget_placement()
    half = A_CMP_STRIDE * HEAD_DIM

    for l in range(depth):
        w_ext = _gather_cols(w_in[l], in_src, in_scl).astype(BF16)
        fbias = jnp.zeros((1, LANES), F32).at[0, LANE_FORGET:LANE_FORGET + C_HEADS].set(fox_forget_bias[l])
        wq = jnp.concatenate([_gather_cols(mla_w_uq[l], qa_i, qa_s), _gather_cols(mla_w_uq[l], qb_i, qb_s)],
                             axis=1).astype(BF16)
        (aq, cmp, slc, slc_v, win, bq, bkv, cq, ck, c_v, dq, dk, d_v, small) = _inproj(
            x, norm_attn_pre[l][None], w_ext, rope_tab, fbias,
            mla_q_norm[l][None], mla_kv_norm[l][None], wq, mla_w_ukv[l].astype(BF16), place)

        nc = s // A_CMP_STRIDE
        w2k_pad = jnp.pad(nsa_phi_k_w2[l], ((0, 0), (0, LANES - HEAD_DIM))).astype(BF16)
        w2v_pad = jnp.pad(nsa_phi_v_w2[l], ((0, 0), (LANES - HEAD_DIM, 0))).astype(BF16)
        kvc = _compress(cmp[..., :HEAD_DIM].reshape(b, nc, half), cmp[..., HEAD_DIM:].reshape(b, nc, half),
                        nsa_cmp_pos[l].reshape(2, half),
                        nsa_phi_k_w1[l].astype(BF16), nsa_phi_v_w1[l].astype(BF16), w2k_pad, w2v_pad)
        kvw_pad = jnp.pad(win, ((0, 0), (A_WINDOW, 0), (0, 0)))
        part, sel_t = _nsa_cmp_win(aq, kvc, kvw_pad, small, overlap, bias_w)
        o_a = _flash("sel", aq, slc, slc_v, sel_t=sel_t, expand=expand, bias=bias_s, part=part, small=small)

        o_b = _dilated(bq, bkv, bias_d)

        o_c = _flash("fox", cq, ck, c_v)
        o_d = _flash("mla", dq, dk, d_v)

        x = _merge(x, o_a, o_b, o_c, o_d,
                   norm_attn_pre[l][None], norm_attn_post[l][None],
                   w_merge_gate[l].astype(BF16),
                   w_branch_a[l].astype(BF16),
                   jnp.concatenate([w_branch_b[l][HEAD_DIM:], w_branch_b[l][:HEAD_DIM]]).astype(BF16),
                   w_branch_c[l].astype(BF16),
                   w_branch_d[l].astype(BF16),
                   w_o[l].astype(BF16))
        x = _ffn(x, norm_ffn_pre[l][None], norm_ffn_post[l][None], ffn_w_up[l].astype(BF16),
                 ffn_conv_w[l], ffn_conv_b[l][None], ffn_w_down[l].astype(BF16))
    return x
```

```python
import functools
import math

import numpy as np
import jax
import jax.numpy as jnp
from jax import lax
from jax.experimental import pallas as pl
from jax.experimental.pallas import tpu as pltpu

F32 = jnp.float32
BF16 = jnp.bfloat16

LANES = 128
MXU_WIDTH = 256
HEAD_DIM = 64
RMS_EPS = 1e-6
NEG_INF = -1e30
FORCE_SCORE = 1e9
LOG2E = math.log2(math.e)
VMEM_LIMIT_BYTES = 56 * 1024 * 1024

N_BUCKETS = 32
BUCKET_EXACT = 16
BUCKET_MAX_DIST = 2048

A_HEADS = 4
A_CMP_LEN = 32
A_CMP_STRIDE = 16
A_PHI_HIDDEN = 128
A_SEL_BLOCK = 64
A_SEL_TOPK = 16
A_WINDOW = 512
B_GROUPS = ((128, 1), (512, 4), (2048, 16))
B_HPG = 2
B_HEADS = B_HPG * len(B_GROUPS)
C_HEADS = 4
D_HEADS = 4
D_Q_LORA = 256
D_KV_LORA = 128
D_NOPE = 64
D_ROPE = 32
D_VDIM = 64
ROPE_THETA = 10000.0
N_BRANCHES = 4

G_AQ = 0
G_CMP = G_AQ + A_HEADS // 2
G_SLC = G_CMP + 1
G_WIN = G_SLC + 1
G_BQ = G_WIN + 1
G_BKV = G_BQ + len(B_GROUPS)
G_CQ = G_BKV + B_HEADS
G_CKV = G_CQ + C_HEADS // 2
G_DQL = G_CKV + C_HEADS
G_DKVL = G_DQL + 2
G_S1 = G_DKVL + 1
G_S2 = G_S1 + 1
N_SLOTS = G_S2 + 1
LANE_GATE = 0
LANE_FORGET = 3 * A_HEADS
LANE_ROPE = 64

TM_PROJ = 512
TQ_NSA_STEP = 256
TQ_FLASH = 512
CHUNK_SLOTS = 8
LANE_EXTRA = HEAD_DIM
N_EXTRA = 3
BIAS_TILE = 256
TQ_NSA = 128
TQ_DIL = 128
CONV_HALO = 16


def _cparams(n_grid):
    return pltpu.CompilerParams(dimension_semantics=("arbitrary",) * n_grid,
                                vmem_limit_bytes=VMEM_LIMIT_BYTES)


def _const_spec(shape):
    nd = len(shape)
    return pl.BlockSpec(shape, lambda *_: (0,) * nd, pipeline_mode=pl.Buffered(1))


def _rms(x, gain):
    return x * lax.rsqrt(jnp.mean(x * x, axis=-1, keepdims=True) + RMS_EPS) * gain


def _sigmoid(x):
    return 1.0 / (1.0 + jnp.exp(-x))


def _dot(a, b):
    return jnp.dot(a, b, preferred_element_type=F32)


def _dot_nt(a, b):
    return lax.dot_general(a, b, (((1,), (1,)), ((), ())), preferred_element_type=F32)


def _slot(h):
    return slice(h * LANES, (h + 1) * LANES)


def _inproj_kernel(x_ref, g_ref, w_ref, tab_ref, fb_ref, qn_ref, kvn_ref, wq_ref, wkv_ref, place_ref,
                   aq_ref, cmp_ref, slc_ref, slcv_ref, win_ref, bq_ref, bkv_ref, cq_ref, ck_ref, cv_ref,
                   dq_ref, dk_ref, dv_ref, small_ref, carry_ref):
    i = pl.program_id(1)
    tm = x_ref.shape[1]
    h = _rms(x_ref[0], g_ref[...]).astype(BF16)

    n_slots = w_ref.shape[1] // LANES
    wide = [_dot(h, w_ref[:, c * LANES:min(c + CHUNK_SLOTS, n_slots) * LANES])
            for c in range(0, n_slots, CHUNK_SLOTS)]

    def proj(g0, n):
        pieces = []
        g = g0
        while g < g0 + n:
            c, off = divmod(g, CHUNK_SLOTS)
            take = min(g0 + n - g, CHUNK_SLOTS - off)
            pieces.append(wide[c][:, off * LANES:(off + take) * LANES])
            g += take
        return pieces[0] if len(pieces) == 1 else jnp.concatenate(pieces, axis=1)

    def key_half(n):
        return (lax.broadcasted_iota(jnp.int32, (tm, n * LANES), 1) & (LANES - 1)) < HEAD_DIM

    def padded_heads(pairs):
        slots = []
        for n in range(pairs.shape[1] // LANES):
            pair = pairs[:, _slot(n)]
            slots += [jnp.where(key_half(1), pair, 0.0),
                      jnp.where(key_half(1), pltpu.roll(pair, HEAD_DIM, 1), 0.0)]
        return jnp.concatenate(slots, axis=1)

    aq_ref[0] = padded_heads(proj(G_AQ, A_HEADS // 2)).astype(BF16)
    cmp_ref[0] = proj(G_CMP, 1)
    slc = proj(G_SLC, 1)
    slc_ref[0] = slc.astype(BF16)
    slcv_ref[0] = jnp.where(key_half(1), 1.0, slc).astype(BF16)
    win_ref[0] = proj(G_WIN, 1).astype(BF16)
    b_all = proj(G_BQ, len(B_GROUPS) + B_HEADS)
    for n in range(len(B_GROUPS)):
        bq_ref[0, n] = b_all[:, _slot(n)]
    for n in range(B_HEADS):
        bkv_ref[0, n] = b_all[:, _slot(len(B_GROUPS) + n)]

    s1 = proj(G_S1, 1)
    s2 = proj(G_S2, 1)

    @pl.when(i == 0)
    def _():
        carry_ref[...] = jnp.zeros_like(carry_ref)

    def split3(v):
        hi = v.astype(BF16)
        r1 = v - hi.astype(F32)
        mid = r1.astype(BF16)
        return hi, mid, (r1 - mid.astype(F32)).astype(BF16)

    z = s1 + fb_ref[...]
    logf = jnp.minimum(z, 0.0) - jnp.log(1.0 + jnp.exp(-jnp.abs(z)))
    row = lax.broadcasted_iota(jnp.int32, (tm, tm), 0)
    col = lax.broadcasted_iota(jnp.int32, (tm, tm), 1)
    tri = (row >= col).astype(BF16)
    cum = sum(_dot(tri, part) for part in split3(logf)) + carry_ref[0:1, :]
    carry_ref[0:1, :] = cum[tm - 1:tm, :]
    lane = lax.broadcasted_iota(jnp.int32, (tm, LANES), 1)
    is_forget = (lane >= LANE_FORGET) & (lane < LANE_FORGET + C_HEADS)
    small_ref[0] = jnp.where(is_forget, cum, s1)

    extras = sum(_dot(part, place_ref[n]) for n, part in enumerate(split3(-LOG2E * cum)))
    lane_c = lax.broadcasted_iota(jnp.int32, (tm, C_HEADS * LANES), 1) & (LANES - 1)
    is_extra = (lane_c >= LANE_EXTRA) & (lane_c < LANE_EXTRA + N_EXTRA)
    cq_ref[0] = jnp.where(is_extra, 1.0, padded_heads(proj(G_CQ, C_HEADS // 2))).astype(BF16)
    ckv = proj(G_CKV, C_HEADS)
    ck_ref[0] = jnp.where(is_extra, extras, ckv).astype(BF16)
    cv_ref[0] = jnp.where(key_half(C_HEADS), 1.0, ckv).astype(BF16)

    tab = tab_ref[...]
    c_q, s_q, c_k, s_k = (tab[:, _slot(n)] for n in range(4))
    qn = _rms(proj(G_DQL, 2), qn_ref[...]).astype(BF16)
    qab = _dot(qn, wq_ref[...])
    kvn = _rms(proj(G_DKVL, 1), kvn_ref[...]).astype(BF16)
    kv_up = _dot(kvn, wkv_ref[...])
    k_rot = s1 * c_k + s2 * s_k
    low1 = key_half(1)
    for hd in range(D_HEADS):
        q_rot = qab[:, _slot(hd)] * c_q + qab[:, _slot(D_HEADS + hd)] * s_q
        dq_ref[0, :, _slot(hd)] = q_rot.astype(BF16)
        dk_ref[0, :, _slot(hd)] = jnp.where(low1, kv_up[:, _slot(hd)], k_rot).astype(BF16)
    dv_ref[0] = jnp.where(key_half(D_HEADS), 1.0, kv_up).astype(BF16)


def _inproj(x, gain, w_ext, tab, fbias, qn_g, kvn_g, wq, wkv, place):
    b, s, d = x.shape
    tm = min(TM_PROJ, s)

    def out(kind, n, dtype):
        if kind == "rows":
            return (pl.BlockSpec((1, tm, n * LANES), lambda bi, i: (bi, i, 0)),
                    jax.ShapeDtypeStruct((b, s, n * LANES), dtype))
        return (pl.BlockSpec((1, n, tm, LANES), lambda bi, i: (bi, 0, i, 0)),
                jax.ShapeDtypeStruct((b, n, s, LANES), dtype))

    outs = [out("rows", A_HEADS, BF16), out("rows", 1, F32), out("rows", 1, BF16), out("rows", 1, BF16),
            out("rows", 1, BF16), out("slots", len(B_GROUPS), F32), out("slots", B_HEADS, F32),
            out("rows", C_HEADS, BF16), out("rows", C_HEADS, BF16), out("rows", C_HEADS, BF16),
            out("rows", D_HEADS, BF16), out("rows", D_HEADS, BF16), out("rows", D_HEADS, BF16),
            out("rows", 1, F32)]
    consts = (gain, w_ext)
    consts2 = (fbias, qn_g, kvn_g, wq, wkv, place)
    return pl.pallas_call(
        _inproj_kernel,
        grid=(b, s // tm),
        in_specs=[pl.BlockSpec((1, tm, d), lambda bi, i: (bi, i, 0))]
        + [_const_spec(c.shape) for c in consts]
        + [pl.BlockSpec((tm, 4 * LANES), lambda bi, i: (i, 0))]
        + [_const_spec(c.shape) for c in consts2],
        out_specs=[o[0] for o in outs],
        out_shape=[o[1] for o in outs],
        scratch_shapes=[pltpu.VMEM((8, LANES), F32)],
        name="inproj",
        compiler_params=_cparams(2),
    )(x, *consts, tab, *consts2)


def _gelu_tanh(x):
    return 0.5 * x * (1.0 + jnp.tanh(math.sqrt(2.0 / math.pi) * (x + 0.044715 * (x * x * x))))


def _compress_kernel(kc_ref, vc_ref, pos_ref, w1k_ref, w1v_ref, w2k_ref, w2v_ref, out_ref):
    nc = kc_ref.shape[1]
    half = A_CMP_STRIDE * HEAD_DIM
    pos_a = pos_ref[0:1, :]
    pos_b = pos_ref[1:2, :]

    def phi(chunks, w1_ref, w2_ref):
        u = _dot((chunks + pos_a).astype(BF16), w1_ref[0:half, :])
        v = _dot((chunks + pos_b).astype(BF16), w1_ref[half:2 * half, :])
        hid = u + pltpu.roll(v, nc - 1, 0)
        return _dot(_gelu_tanh(hid).astype(BF16), w2_ref[...])

    out_ref[0] = (phi(kc_ref[0], w1k_ref, w2k_ref) + phi(vc_ref[0], w1v_ref, w2v_ref)).astype(BF16)


def _compress(kc_in, vc_in, pos2, w1k, w1v, w2k_pad, w2v_pad):
    b, nc, width = kc_in.shape
    blk = pl.BlockSpec((1, nc, width), lambda bi: (bi, 0, 0))
    return pl.pallas_call(
        _compress_kernel,
        grid=(b,),
        in_specs=[blk, blk, _const_spec(pos2.shape), _const_spec(w1k.shape), _const_spec(w1v.shape),
                  _const_spec(w2k_pad.shape), _const_spec(w2v_pad.shape)],
        out_specs=pl.BlockSpec((1, nc, LANES), lambda bi: (bi, 0, 0)),
        out_shape=jax.ShapeDtypeStruct((b, nc, LANES), BF16),
        name="compress",
        compiler_params=_cparams(1),
    )(kc_in, vc_in, pos2, w1k, w1v, w2k_pad, w2v_pad)


def _nsa_cmp_win_kernel(q_ref, kvc_ref, kvw_ref, small_ref, ovl_ref, bias_ref,
                        part_ref, sel_ref, *, n_top):
    tq = min(TQ_NSA, q_ref.shape[1])
    for u in range(q_ref.shape[1] // tq):
        _nsa_cmp_win_tile(u * tq, tq, q_ref, kvc_ref, kvw_ref, small_ref, ovl_ref, bias_ref,
                          part_ref, sel_ref, n_top)


def _nsa_cmp_win_tile(r0, tq, q_ref, kvc_ref, kvw_ref, small_ref, ovl_ref, bias_ref, part_ref, sel_ref, n_top):
    nc = kvc_ref.shape[1]
    n_sel = ovl_ref.shape[0]
    t0 = pl.multiple_of(pl.program_id(1) * q_ref.shape[1] + r0, tq)
    tile = slice(r0, r0 + tq)
    q4 = jnp.concatenate([q_ref[0, tile, _slot(h)] for h in range(A_HEADS)], axis=0)
    rows = A_HEADS * tq

    kvc = kvc_ref[0]
    s = _dot_nt(q4, kvc)
    t_row = t0 + (lax.broadcasted_iota(jnp.int32, (rows, nc), 0) & (tq - 1))
    c_end = lax.broadcasted_iota(jnp.int32, (rows, nc), 1) * A_CMP_STRIDE + (A_CMP_LEN - 1)
    valid = c_end <= t_row
    s = jnp.where(valid, s, NEG_INF)
    m = jnp.max(s, axis=1, keepdims=True)
    p = jnp.where(valid, jnp.exp2(s - m), 0.0)
    den = jnp.maximum(jnp.sum(p, axis=1, keepdims=True), 1e-30)
    p = p / den
    o_c = _dot(p.astype(BF16), kvc)

    p_sum = p[0:tq] + p[tq:2 * tq] + p[2 * tq:3 * tq] + p[3 * tq:4 * tq]
    p_hi = p_sum.astype(BF16)
    p_lo = (p_sum - p_hi.astype(F32)).astype(BF16)
    imp_t = _dot_nt(ovl_ref[...], p_hi) + _dot_nt(ovl_ref[...], p_lo)

    j_idx = lax.broadcasted_iota(jnp.int32, (n_sel, tq), 0)
    cur = lax.shift_right_logical(t0 + lax.broadcasted_iota(jnp.int32, (n_sel, tq), 1), 6)
    forced = (j_idx == 0) | (j_idx == cur) | (j_idx == cur - 1)
    imp_t = jnp.where(forced, FORCE_SCORE, jnp.where(j_idx > cur, -FORCE_SCORE, imp_t))
    sub8 = 8
    groups = [imp_t[v * sub8:(v + 1) * sub8] for v in range(n_sel // sub8)]
    ranks = [jnp.zeros((sub8, tq), F32) for _ in groups]
    in_group = lax.broadcasted_iota(jnp.int32, (sub8, tq), 0)
    for c in range(n_sel):
        row_c = imp_t[c:c + 1, :]
        for v, grp in enumerate(groups):
            if v > c // sub8:
                beats = row_c >= grp
            elif v < c // sub8:
                beats = row_c > grp
            else:
                beats = (row_c > grp) | ((row_c == grp) & (in_group > c % sub8))
            ranks[v] = ranks[v] + jnp.where(beats, 1.0, 0.0)
    rank = jnp.concatenate(ranks, axis=0)
    sel_t = jnp.where(rank < float(n_top), 1.0, 0.0).astype(BF16)
    sel_ref[0, :, tile] = sel_t

    span = A_WINDOW + tq
    kvw = kvw_ref[0, pl.ds(t0, span), :]
    kpos = t0 - A_WINDOW + lax.broadcasted_iota(jnp.int32, (1, span), 1)
    sw = _dot_nt(q4, kvw) + bias_ref[...] + jnp.where(kpos >= 0, 0.0, NEG_INF)
    mw = jnp.max(sw, axis=1, keepdims=True)
    pw = jnp.exp2(sw - mw)
    o_w = _dot(pw.astype(BF16), kvw) / jnp.sum(pw, axis=1, keepdims=True)

    g = _sigmoid(small_ref[0, tile])
    gated = []
    for h in range(A_HEADS):
        g_c = g[:, LANE_GATE + 3 * h:LANE_GATE + 3 * h + 1]
        g_w = g[:, LANE_GATE + 3 * h + 2:LANE_GATE + 3 * h + 3]
        gated.append(g_c * o_c[h * tq:(h + 1) * tq] + g_w * o_w[h * tq:(h + 1) * tq])
    low = lax.broadcasted_iota(jnp.int32, (tq, LANES), 1) < HEAD_DIM
    for pair in range(A_HEADS // 2):
        part_ref[0, tile, _slot(pair)] = jnp.where(low, pltpu.roll(gated[2 * pair], HEAD_DIM, 1),
                                                   gated[2 * pair + 1])


def _nsa_cmp_win(aq, kvc, kvw_pad, small, ovl, bias_w):
    b, s, _ = aq.shape
    tq = min(TQ_NSA_STEP, s)
    nc = kvc.shape[1]
    n_sel = ovl.shape[0]
    n_top = min(A_SEL_TOPK, n_sel)
    return pl.pallas_call(
        functools.partial(_nsa_cmp_win_kernel, n_top=n_top),
        grid=(b, s // tq),
        in_specs=[pl.BlockSpec((1, tq, A_HEADS * LANES), lambda bi, i: (bi, i, 0)),
                  pl.BlockSpec((1, nc, LANES), lambda bi, i: (bi, 0, 0)),
                  pl.BlockSpec((1, s + A_WINDOW, LANES), lambda bi, i: (bi, 0, 0)),
                  pl.BlockSpec((1, tq, LANES), lambda bi, i: (bi, i, 0)),
                  _const_spec(ovl.shape),
                  _const_spec(bias_w.shape)],
        out_specs=[pl.BlockSpec((1, tq, A_HEADS * HEAD_DIM), lambda bi, i: (bi, i, 0)),
                   pl.BlockSpec((1, n_sel, tq), lambda bi, i: (bi, 0, i))],
        out_shape=[jax.ShapeDtypeStruct((b, s, A_HEADS * HEAD_DIM), F32),
                   jax.ShapeDtypeStruct((b, n_sel, s), BF16)],
        name="nsa_cmp_win",
        compiler_params=_cparams(2),
    )(aq, kvc, kvw_pad, small, ovl, bias_w)


def _flash_kernel(*refs, mode, nh):
    if mode == "sel":
        q_ref, k_ref, v_ref, selt_ref, e_ref, bias_ref, part_ref, small_ref, o_ref, m_ref, acc_ref = refs
    else:
        q_ref, k_ref, v_ref, o_ref, m_ref, acc_ref = refs
    i = pl.program_id(1)
    tq = q_ref.shape[1]
    tk = tq
    shared_kv = mode == "sel"

    m_ref[...] = jnp.full(m_ref.shape, NEG_INF, F32)
    acc_ref[...] = jnp.zeros(acc_ref.shape, F32)

    lag = lax.broadcasted_iota(jnp.int32, (tq, tk), 0) - lax.broadcasted_iota(jnp.int32, (tq, tk), 1)

    def step(j, diag):
        k0 = pl.multiple_of(j * tk, tk)
        keep = (lag >= 0) if diag else None
        if mode == "sel":
            sel_keys = lax.dot_general(selt_ref[0], e_ref[:, pl.ds(k0, tk)], (((0,), (0,)), ((), ())),
                                       preferred_element_type=F32) > 0.5
            keep = (sel_keys & keep) if diag else sel_keys
            n_off, bt = bias_ref.shape[0], bias_ref.shape[2]

            def bias_tile(h):
                strips = []
                for a in range(tq // bt):
                    offs = [jnp.clip(i * (tq // bt) + a - j * (tk // bt) - c, 0, n_off - 1)
                            for c in range(tk // bt)]
                    strips.append(jnp.concatenate([bias_ref[o, h] for o in offs], axis=1))
                return jnp.concatenate(strips, axis=0)
        logits = []
        for h in range(nh):
            s = _dot_nt(q_ref[0, :, _slot(h)], k_ref[0, pl.ds(k0, tk), _slot(0 if shared_kv else h)])
            if mode == "sel":
                s = s + bias_tile(h)
            if keep is not None:
                s = jnp.where(keep, s, NEG_INF)
            logits.append(s)
        probs = []
        for h in range(nh):
            m_prev = m_ref[h]
            m_new = jnp.maximum(m_prev, jnp.max(logits[h], axis=1, keepdims=True))
            p = jnp.exp2(logits[h] - jnp.concatenate([m_new] * (tk // LANES), axis=1))
            alpha = jnp.exp2(m_prev - m_new)
            m_ref[h] = m_new
            probs.append((p.astype(BF16), alpha))
        for h in range(nh):
            p, alpha = probs[h]
            acc_ref[h] = alpha * acc_ref[h] + _dot(p, v_ref[0, pl.ds(k0, tk), _slot(0 if shared_kv else h)])

    def full_tile(j, carry):
        step(j, False)
        return carry

    lax.fori_loop(0, i, full_tile, 0)
    step(i, True)

    low = lax.broadcasted_iota(jnp.int32, (tq, LANES), 1) < HEAD_DIM
    if mode == "sel":
        g = _sigmoid(small_ref[0])
    for pair in range(nh // 2):
        outs = []
        for h in (2 * pair, 2 * pair + 1):
            acc = acc_ref[h]
            outs.append(acc / acc[:, 0:1])
        o_pair = jnp.where(low, pltpu.roll(outs[0], HEAD_DIM, 1), outs[1])
        if mode == "sel":
            gate = [g[:, LANE_GATE + 3 * h + 1:LANE_GATE + 3 * h + 2] for h in (2 * pair, 2 * pair + 1)]
            o_pair = part_ref[0, :, _slot(pair)] + jnp.where(low, gate[0], gate[1]) * o_pair
        o_ref[0, :, _slot(pair)] = o_pair.astype(BF16)


def _flash(mode, q, k, v, sel_t=None, expand=None, bias=None, part=None, small=None):
    b, s, qw = q.shape
    nh = qw // LANES
    tq = min(TQ_FLASH, s)
    ow = nh * HEAD_DIM

    def rows(width):
        return pl.BlockSpec((1, tq, width), lambda bi, i: (bi, i, 0))

    def seq_spec(arr):
        return pl.BlockSpec((1,) + arr.shape[1:], lambda bi, i: (bi, 0, 0))

    args = (q, k, v)
    in_specs = [rows(qw), seq_spec(k), seq_spec(v)]
    if mode == "sel":
        args += (sel_t, expand, bias, part, small)
        in_specs += [pl.BlockSpec((1, sel_t.shape[1], tq), lambda bi, i: (bi, 0, i)),
                     _const_spec(expand.shape), _const_spec(bias.shape), rows(ow), rows(LANES)]
    return pl.pallas_call(
        functools.partial(_flash_kernel, mode=mode, nh=nh),
        grid=(b, s // tq),
        in_specs=in_specs,
        out_specs=rows(ow),
        out_shape=jax.ShapeDtypeStruct((b, s, ow), BF16),
        scratch_shapes=[pltpu.VMEM((nh, tq, LANES), F32)] * 2,
        name="flash_" + mode,
        compiler_params=_cparams(2),
    )(*args)


def _dilated_kernel(q_ref, kvp_ref, kvc_ref, bias_ref, o_ref, o_sc, l_sc):
    i = pl.program_id(1)
    rows = q_ref.shape[2]
    tb = TQ_DIL
    low = lax.broadcasted_iota(jnp.int32, (tb, LANES), 1) < HEAD_DIM
    col = lax.broadcasted_iota(jnp.int32, (tb, 2 * tb), 1)
    in_seq = (col >= tb) | (i > 0)
    for g, (_, dil) in enumerate(B_GROUPS):
        span = dil * tb
        for r in range(dil):
            for c in range(rows // span):
                cur = pl.ds(r + c * span, tb, stride=dil)
                if c > 0:
                    prev_ref, prev = kvc_ref, pl.ds(r + (c - 1) * span, tb, stride=dil)
                else:
                    prev_ref, prev = kvp_ref, pl.ds(rows - span + r, tb, stride=dil)
                q2 = q_ref[0, g, cur, :]
                outs, lses = [], []
                for h in range(B_HPG):
                    q_h = jnp.where(low if h == 0 else ~low, q2, 0.0).astype(BF16)
                    head = B_HPG * g + h
                    kv = jnp.concatenate([prev_ref[0, head, prev, :], kvc_ref[0, head, cur, :]],
                                         axis=0).astype(BF16)
                    s = _dot_nt(q_h, kv) + bias_ref[g, h]
                    if c == 0:
                        s = jnp.where(in_seq, s, NEG_INF)
                    m = jnp.max(s, axis=1, keepdims=True)
                    p = jnp.exp2(s - m)
                    den = jnp.sum(p, axis=1, keepdims=True)
                    outs.append(_dot(p.astype(BF16), kv) / den)
                    lses.append(m + jnp.log2(den))
                o_sc[g, cur, :] = jnp.where(low, outs[1], outs[0])
                l_sc[g, cur, :] = jnp.where(low, lses[1], lses[0])
    step = 2 * tb
    for t in range(rows // step):
        rs = slice(t * step, (t + 1) * step)
        l0, l1, l2 = l_sc[0, rs, :], l_sc[1, rs, :], l_sc[2, rs, :]
        mx = jnp.maximum(jnp.maximum(l0, l1), l2)
        e0, e1, e2 = jnp.exp2(l0 - mx), jnp.exp2(l1 - mx), jnp.exp2(l2 - mx)
        mixed = (e0 * o_sc[0, rs, :] + e1 * o_sc[1, rs, :] + e2 * o_sc[2, rs, :]) / (e0 + e1 + e2)
        o_ref[0, rs, :] = mixed.astype(BF16)


def _dilated(bq, bkv, bias):
    b, _, s, _ = bq.shape
    rows = TQ_DIL * B_GROUPS[-1][1]
    n_groups = len(B_GROUPS)

    def tile(n_slots, prev):
        if prev:
            return pl.BlockSpec((1, n_slots, rows, LANES), lambda bi, i: (bi, 0, jnp.maximum(i - 1, 0), 0))
        return pl.BlockSpec((1, n_slots, rows, LANES), lambda bi, i: (bi, 0, i, 0))

    return pl.pallas_call(
        _dilated_kernel,
        grid=(b, s // rows),
        in_specs=[tile(bq.shape[1], False), tile(bkv.shape[1], True), tile(bkv.shape[1], False),
                  _const_spec(bias.shape)],
        out_specs=pl.BlockSpec((1, rows, LANES), lambda bi, i: (bi, i, 0)),
        out_shape=jax.ShapeDtypeStruct((b, s, LANES), BF16),
        scratch_shapes=[pltpu.VMEM((n_groups, rows, LANES), F32)] * 2,
        name="dilated",
        compiler_params=_cparams(2),
    )(bq, bkv, bkv, bias)


def _merge_kernel(x_ref, oa_ref, ob_ref, oc_ref, od_ref,
                  gpre_ref, gpost_ref, wg_ref, wba_ref, wbb_ref, wbc_ref, wbd_ref, wo_ref, out_ref):
    x = x_ref[0]
    h = _rms(x, gpre_ref[...]).astype(BF16)
    branches = ((oa_ref[0], wba_ref), (ob_ref[0], wbb_ref), (oc_ref[0], wbc_ref), (od_ref[0], wbd_ref))
    merged = None
    for n, (o_n, wb_ref) in enumerate(branches):
        term = _sigmoid(_dot(h, wg_ref[n])) * _dot(o_n, wb_ref[...])
        merged = term if merged is None else merged + term
    z = _dot(merged.astype(BF16), wo_ref[...])
    out_ref[0] = x + _rms(z, gpost_ref[...])


def _merge(x, oa, ob, oc, od, gpre, gpost, wg, wba, wbb, wbc, wbd, wo):
    b, s, d = x.shape
    tm = min(TM_PROJ, s)

    def rows(arr):
        return pl.BlockSpec((1, tm, arr.shape[2]), lambda bi, i: (bi, i, 0))

    acts = (x, oa, ob, oc, od)
    consts = (gpre, gpost, wg, wba, wbb, wbc, wbd, wo)
    return pl.pallas_call(
        _merge_kernel,
        grid=(b, s // tm),
        in_specs=[rows(a) for a in acts] + [_const_spec(c.shape) for c in consts],
        out_specs=rows(x),
        out_shape=jax.ShapeDtypeStruct(x.shape, F32),
        name="merge",
        compiler_params=_cparams(2),
    )(*acts, *consts)


def _ffn_kernel(x_ref, xp_ref, gpre_ref, gpost_ref, wup_ref, cw_ref, cb_ref, wdn_ref, out_ref, *, cuts):
    i = pl.program_id(1)
    tm = x_ref.shape[1]
    d_ff = wdn_ref.shape[0]
    x = x_ref[0]
    halo = jnp.where(i > 0, xp_ref[0], 0.0)
    he = _rms(jnp.concatenate([halo, x], axis=0), gpre_ref[...]).astype(BF16)

    def conv(c0, chunk):
        u = _dot(he, wup_ref[:, c0:c0 + chunk])
        w = cw_ref[:, c0:c0 + chunk]
        out = cb_ref[:, c0:c0 + chunk] + w[0:1] * pltpu.roll(u, 2, 0)[CONV_HALO:]
        out = out + w[1:2] * pltpu.roll(u, 1, 0)[CONV_HALO:]
        return out + w[2:3] * u[CONV_HALO:]

    y = jnp.zeros((tm, x.shape[1]), F32)
    for c0, c1 in zip(cuts[:-1], cuts[1:]):
        gate = conv(c0, c1 - c0)
        val = conv(d_ff + c0, c1 - c0)
        act = gate * _sigmoid(gate) * val
        y = y + _dot(act.astype(BF16), wdn_ref[c0:c1, :])
    out_ref[0] = x + _rms(y, gpost_ref[...])


def _ffn(x, gpre, gpost, w_up, conv_w, conv_b, w_down):
    b, s, d = x.shape
    tm = min(TM_PROJ, s)
    d_ff = w_down.shape[0]
    n_tiles = d_ff // MXU_WIDTH
    cuts = (0, (n_tiles // 2) * MXU_WIDTH, d_ff) if d_ff % MXU_WIDTH == 0 and n_tiles > 1 else (0, d_ff)
    halo_blocks = tm // CONV_HALO
    return pl.pallas_call(
        functools.partial(_ffn_kernel, cuts=cuts),
        grid=(b, s // tm),
        in_specs=[pl.BlockSpec((1, tm, d), lambda bi, i: (bi, i, 0)),
                  pl.BlockSpec((1, CONV_HALO, d), lambda bi, i: (bi, jnp.maximum(i * halo_blocks - 1, 0), 0)),
                  _const_spec(gpre.shape), _const_spec(gpost.shape), _const_spec(w_up.shape),
                  _const_spec(conv_w.shape), _const_spec(conv_b.shape), _const_spec(w_down.shape)],
        out_specs=pl.BlockSpec((1, tm, d), lambda bi, i: (bi, i, 0)),
        out_shape=jax.ShapeDtypeStruct(x.shape, F32),
        name="ffn",
        compiler_params=_cparams(2),
    )(x, x, gpre, gpost, w_up, conv_w, conv_b, w_down)


def _rel_bucket(dist):
    dist = jnp.maximum(dist, 0)
    d = jnp.maximum(dist, 1).astype(F32)
    large = BUCKET_EXACT + (jnp.log(d / BUCKET_EXACT) / math.log(BUCKET_MAX_DIST / BUCKET_EXACT)
                            * (N_BUCKETS - BUCKET_EXACT)).astype(jnp.int32)
    large = jnp.minimum(large, N_BUCKETS - 1)
    return jnp.where(dist < BUCKET_EXACT, dist, large)


def _in_proj_columns():
    src = np.full((N_SLOTS * LANES,), -1, np.int64)
    scl = np.ones((N_SLOTS * LANES,), np.float32)
    o_aq = 0
    o_akv = o_aq + A_HEADS * HEAD_DIM
    o_ag = o_akv + 6 * HEAD_DIM
    o_b = o_ag + 3 * A_HEADS
    o_c = o_b + 3 * B_HEADS * HEAD_DIM
    o_cf = o_c + 3 * C_HEADS * HEAD_DIM
    o_dq = o_cf + C_HEADS
    o_dkv = o_dq + D_Q_LORA
    o_dkr = o_dkv + D_KV_LORA
    e = np.arange(HEAD_DIM)
    q_scale = HEAD_DIM ** -0.5 * LOG2E

    def put(slot, lane0, cols, scale=1.0):
        dst = slot * LANES + lane0 + np.arange(len(cols))
        src[dst] = cols
        scl[dst] = scale

    for h in range(A_HEADS):
        put(G_AQ + h // 2, (h % 2) * HEAD_DIM, o_aq + h * HEAD_DIM + e, q_scale)
    for n, slot in enumerate((G_CMP, G_SLC, G_WIN)):
        put(slot, 0, o_akv + (2 * n) * HEAD_DIM + e)
        put(slot, HEAD_DIM, o_akv + (2 * n + 1) * HEAD_DIM + e)
    for h in range(B_HEADS):
        odd = h % B_HPG
        put(G_BQ + h // B_HPG, odd * HEAD_DIM, o_b + h * HEAD_DIM + e, q_scale)
        put(G_BKV + h, odd * HEAD_DIM, o_b + (B_HEADS + h) * HEAD_DIM + e)
        put(G_BKV + h, (1 - odd) * HEAD_DIM, o_b + (2 * B_HEADS + h) * HEAD_DIM + e)
    for h in range(C_HEADS):
        put(G_CQ + h // 2, (h % 2) * HEAD_DIM, o_c + h * HEAD_DIM + e, q_scale)
        put(G_CKV + h, 0, o_c + (C_HEADS + h) * HEAD_DIM + e)
        put(G_CKV + h, HEAD_DIM, o_c + (2 * C_HEADS + h) * HEAD_DIM + e)
    put(G_DQL, 0, o_dq + np.arange(D_Q_LORA))
    put(G_DKVL, 0, o_dkv + np.arange(D_KV_LORA))
    put(G_S1, LANE_GATE, o_ag + np.arange(3 * A_HEADS))
    put(G_S1, LANE_FORGET, o_cf + np.arange(C_HEADS))
    put(G_S1, LANE_ROPE, o_dkr + np.arange(D_ROPE))
    half = D_ROPE // 2
    put(G_S2, LANE_ROPE, o_dkr + half + np.arange(half), -1.0)
    put(G_S2, LANE_ROPE + half, o_dkr + np.arange(half))
    return src, scl


def _gather_cols(w, src, scl):
    pieces = []
    n = len(src)
    a = 0
    while a < n:
        e = a + 1
        if src[a] < 0:
            while e < n and src[e] < 0:
                e += 1
            pieces.append(jnp.zeros((w.shape[0], e - a), w.dtype))
        else:
            while e < n and src[e] == src[e - 1] + 1 and scl[e] == scl[a]:
                e += 1
            run = w[:, int(src[a]):int(src[a]) + (e - a)]
            pieces.append(run if scl[a] == 1.0 else run * float(scl[a]))
        a = e
    return jnp.concatenate(pieces, axis=1)


def _mla_weight_columns():
    per_q = D_NOPE + D_ROPE
    half = D_ROPE // 2
    qa = np.full((D_HEADS * LANES,), -1, np.int64)
    qb = np.full((D_HEADS * LANES,), -1, np.int64)
    qb_s = np.ones((D_HEADS * LANES,), np.float32)
    for h in range(D_HEADS):
        qa[h * LANES + np.arange(per_q)] = h * per_q + np.arange(per_q)
        rot = h * per_q + D_NOPE
        qb[h * LANES + D_NOPE + np.arange(half)] = rot + half + np.arange(half)
        qb_s[h * LANES + D_NOPE + np.arange(half)] = -1.0
        qb[h * LANES + D_NOPE + half + np.arange(half)] = rot + np.arange(half)
    ones = np.ones_like(qb_s)
    return (qa, ones), (qb, qb_s)


def _forget_placement():
    place = np.zeros((N_EXTRA, LANES, C_HEADS * LANES), np.float32)
    for n in range(N_EXTRA):
        for h in range(C_HEADS):
            place[n, LANE_FORGET + h, h * LANES + LANE_EXTRA + n] = 1.0
    return jnp.asarray(place, BF16)


def _rope_tables(s):
    inv_freq = ROPE_THETA ** (-jnp.arange(0, D_ROPE, 2, dtype=F32) / D_ROPE)
    ang = jnp.arange(s, dtype=F32)[:, None] * inv_freq[None, :]
    cos2 = jnp.concatenate([jnp.cos(ang)] * 2, axis=1)
    sin2 = jnp.concatenate([jnp.sin(ang)] * 2, axis=1)
    scale = (D_NOPE + D_ROPE) ** -0.5 * LOG2E
    z_lo = jnp.zeros((s, D_NOPE), F32)
    z_hi = jnp.zeros((s, LANES - D_NOPE - D_ROPE), F32)
    c_q = jnp.concatenate([jnp.full((s, D_NOPE), scale, F32), scale * cos2, z_hi], axis=1)
    s_q = jnp.concatenate([z_lo, scale * sin2, z_hi], axis=1)
    c_k = jnp.concatenate([z_lo, cos2, z_hi], axis=1)
    s_k = jnp.concatenate([z_lo, sin2, z_hi], axis=1)
    return jnp.concatenate([c_q, s_q, c_k, s_k], axis=1)


def _toeplitz(vec, n_rows, n_cols, off, step=1):
    base = vec[:, ::step]
    last = base.shape[1] - 1
    lo, hi = off - (n_cols - 1), off + n_rows - 1
    core = base[:, max(lo, 0):min(hi, last) + 1]
    front = jnp.repeat(base[:, :1], max(0, -lo), axis=1)
    back = jnp.repeat(base[:, last:], max(0, hi - last), axis=1)
    g = jnp.concatenate([front, core, back], axis=1)
    lg = n_rows + n_cols - 1
    t = jnp.tile(g, (1, n_rows + 1))[:, :n_rows * (lg + 1)].reshape(-1, n_rows, lg + 1)
    return t[:, :, :n_cols][:, :, ::-1]


def _bias_tables(rel_table, s):
    bucket = _rel_bucket(jnp.arange(s))
    hit = bucket[None, :, None] == jnp.arange(N_BUCKETS)[None, None, :]
    by_dist = jnp.sum(jnp.where(hit, LOG2E * rel_table.T[:, None, :], 0.0), axis=-1)
    tqa = min(TQ_NSA, s)
    dist = jnp.arange(tqa)[:, None] + A_WINDOW - jnp.arange(A_WINDOW + tqa)[None, :]
    ok = (dist >= 0) & (dist < A_WINDOW)
    bias_w = jnp.where(ok[None], _toeplitz(by_dist[:A_HEADS], tqa, A_WINDOW + tqa, A_WINDOW), NEG_INF)
    bias_w = bias_w.reshape(A_HEADS * tqa, A_WINDOW + tqa)
    tf = min(BIAS_TILE, s)
    n_off = min(s // tf, -(-(BUCKET_MAX_DIST + tf - 1) // tf) + 1)
    bias_s = jnp.stack([_toeplitz(by_dist[:A_HEADS], tf, tf, n * tf) for n in range(n_off)])
    m = jnp.arange(TQ_DIL)[:, None] + TQ_DIL - jnp.arange(2 * TQ_DIL)[None, :]
    bias_d = []
    for g, (window, dil) in enumerate(B_GROUPS):
        ok = (m >= 0) & (m <= window // dil)
        heads = by_dist[A_HEADS + g * B_HPG:A_HEADS + (g + 1) * B_HPG]
        bias_d.append(jnp.where(ok[None], _toeplitz(heads, TQ_DIL, 2 * TQ_DIL, TQ_DIL, dil), NEG_INF))
    return bias_w, bias_s, jnp.stack(bias_d)


def _selection_constants(s):
    nc = s // A_CMP_STRIDE
    n_cmp = (s - A_CMP_LEN) // A_CMP_STRIDE + 1
    n_sel = s // A_SEL_BLOCK
    c = np.arange(nc)[:, None]
    j = np.arange(n_sel)[None, :]
    c_start = c * A_CMP_STRIDE
    overlap = ((c_start <= j * A_SEL_BLOCK + A_SEL_BLOCK - 1) & (c_start + A_CMP_LEN - 1 >= j * A_SEL_BLOCK)
               & (c < n_cmp))
    expand = (np.arange(s)[None, :] // A_SEL_BLOCK) == np.arange(n_sel)[:, None]
    return jnp.asarray(overlap.T, BF16), jnp.asarray(expand, BF16)


def kernel(x, rel_bias_table, norm_attn_pre, norm_attn_post, norm_ffn_pre, norm_ffn_post, w_in, nsa_cmp_pos, nsa_phi_k_w1, nsa_phi_k_w2, nsa_phi_v_w1, nsa_phi_v_w2, fox_forget_bias, mla_q_norm, mla_kv_norm, mla_w_uq, mla_w_ukv, w_branch_a, w_branch_b, w_branch_c, w_branch_d, w_merge_gate, w_o, ffn_w_up, ffn_conv_w, ffn_conv_b, ffn_w_down):
    b, s, d = x.shape
    depth = w_in.shape[0]
    assert s % (TQ_DIL * B_GROUPS[-1][1]) == 0, "every dilation class needs whole 128-row tiles"

    in_src, in_scl = _in_proj_columns()
    assert D_NOPE + D_VDIM == LANES, "an MLA head's [k_nope | v] up-projection fills one slot"
    (qa_i, qa_s), (qb_i, qb_s) = _mla_weight_columns()
    rope_tab = _rope_tables(s)
    bias_w, bias_s, bias_d = _bias_tables(rel_bias_table, s)
    overlap, expand = _selection_constants(s)
    place = _forget_placement()
    half = A_CMP_STRIDE * HEAD_DIM

    for l in range(depth):
        w_ext = _gather_cols(w_in[l], in_src, in_scl).astype(BF16)
        fbias = jnp.zeros((1, LANES), F32).at[0, LANE_FORGET:LANE_FORGET + C_HEADS].set(fox_forget_bias[l])
        wq = jnp.concatenate([_gather_cols(mla_w_uq[l], qa_i, qa_s), _gather_cols(mla_w_uq[l], qb_i, qb_s)],
                             axis=1).astype(BF16)
        (aq, cmp, slc, slc_v, win, bq, bkv, cq, ck, c_v, dq, dk, d_v, small) = _inproj(
            x, norm_attn_pre[l][None], w_ext, rope_tab, fbias,
            mla_q_norm[l][None], mla_kv_norm[l][None], wq, mla_w_ukv[l].astype(BF16), place)

        nc = s // A_CMP_STRIDE
        w2k_pad = jnp.pad(nsa_phi_k_w2[l], ((0, 0), (0, LANES - HEAD_DIM))).astype(BF16)
        w2v_pad = jnp.pad(nsa_phi_v_w2[l], ((0, 0), (LANES - HEAD_DIM, 0))).astype(BF16)
        kvc = _compress(cmp[..., :HEAD_DIM].reshape(b, nc, half), cmp[..., HEAD_DIM:].reshape(b, nc, half),
                        nsa_cmp_pos[l].reshape(2, half),
                        nsa_phi_k_w1[l].astype(BF16), nsa_phi_v_w1[l].astype(BF16), w2k_pad, w2v_pad)
        kvw_pad = jnp.pad(win, ((0, 0), (A_WINDOW, 0), (0, 0)))
        part, sel_t = _nsa_cmp_win(aq, kvc, kvw_pad, small, overlap, bias_w)
        o_a = _flash("sel", aq, slc, slc_v, sel_t=sel_t, expand=expand, bias=bias_s, part=part, small=small)

        o_b = _dilated(bq, bkv, bias_d)

        o_c = _flash("fox", cq, ck, c_v)
        o_d = _flash("mla", dq, dk, d_v)

        x = _merge(x, o_a, o_b, o_c, o_d,
                   norm_attn_pre[l][None], norm_attn_post[l][None],
                   w_merge_gate[l].astype(BF16),
                   w_branch_a[l].astype(BF16),
                   jnp.concatenate([w_branch_b[l][HEAD_DIM:], w_branch_b[l][:HEAD_DIM]]).astype(BF16),
                   w_branch_c[l].astype(BF16),
                   w_branch_d[l].astype(BF16),
                   w_o[l].astype(BF16))
        x = _ffn(x, norm_ffn_pre[l][None], norm_ffn_post[l][None], ffn_w_up[l].astype(BF16),
                 ffn_conv_w[l], ffn_conv_b[l][None], ffn_w_down[l].astype(BF16))
    return x
```

```python
import functools
import math

import numpy as np
import jax
import jax.numpy as jnp
from jax import lax
from jax.experimental import pallas as pl
from jax.experimental.pallas import tpu as pltpu

F32 = jnp.float32
BF16 = jnp.bfloat16

LANES = 128
MXU_WIDTH = 256
HEAD_DIM = 64
RMS_EPS = 1e-6
NEG_INF = -1e30
FORCE_SCORE = 1e9
LOG2E = math.log2(math.e)
VMEM_LIMIT_BYTES = 56 * 1024 * 1024

N_BUCKETS = 32
BUCKET_EXACT = 16
BUCKET_MAX_DIST = 2048

A_HEADS = 4
A_CMP_LEN = 32
A_CMP_STRIDE = 16
A_PHI_HIDDEN = 128
A_SEL_BLOCK = 64
A_SEL_TOPK = 16
A_WINDOW = 512
B_GROUPS = ((128, 1), (512, 4), (2048, 16))
B_HPG = 2
B_HEADS = B_HPG * len(B_GROUPS)
C_HEADS = 4
D_HEADS = 4
D_Q_LORA = 256
D_KV_LORA = 128
D_NOPE = 64
D_ROPE = 32
D_VDIM = 64
ROPE_THETA = 10000.0
N_BRANCHES = 4

G_AQ = 0
G_CMP = G_AQ + A_HEADS // 2
G_SLC = G_CMP + 1
G_WIN = G_SLC + 1
G_BQ = G_WIN + 1
G_BKV = G_BQ + len(B_GROUPS)
G_CQ = G_BKV + B_HEADS
G_CKV = G_CQ + C_HEADS // 2
G_DQL = G_CKV + C_HEADS
G_DKVL = G_DQL + 2
G_S1 = G_DKVL + 1
G_S2 = G_S1 + 1
N_SLOTS = G_S2 + 1
LANE_GATE = 0
LANE_FORGET = 96
N_TERMS = 3
LANE_ROPE = 64

TM_PROJ = 512
FFN_SUBTILES = 1
TQ_NSA_STEP = 256
TQ_FLASH = 512
CHUNK_SLOTS = 8
BIAS_TILE = 256
TQ_NSA = 128
TQ_DIL = 128
CONV_HALO = 8


def _cparams(n_grid):
    return pltpu.CompilerParams(dimension_semantics=("arbitrary",) * n_grid,
                                vmem_limit_bytes=VMEM_LIMIT_BYTES)


def _const_spec(shape):
    nd = len(shape)
    return pl.BlockSpec(shape, lambda *_: (0,) * nd, pipeline_mode=pl.Buffered(1))


def _rms(x, gain):
    return x * lax.rsqrt(jnp.mean(x * x, axis=-1, keepdims=True) + RMS_EPS) * gain


def _sigmoid(x):
    return 1.0 / (1.0 + jnp.exp(-x))


def _dot(a, b):
    return jnp.dot(a, b, preferred_element_type=F32)


def _dot_nt(a, b):
    return lax.dot_general(a, b, (((1,), (1,)), ((), ())), preferred_element_type=F32)


def _slot(h):
    return slice(h * LANES, (h + 1) * LANES)


def _inproj_kernel(x_ref, g_ref, w_ref, tab_ref, fb_ref, qn_ref, kvn_ref, wq_ref, wkv_ref,
                   aq_ref, cmp_ref, slc_ref, slcv_ref, win_ref, bq_ref, bkv_ref, cq_ref, ck_ref, cv_ref,
                   dq_ref, dk_ref, dv_ref, small_ref, carry_ref):
    i = pl.program_id(1)
    tm = x_ref.shape[1]
    h = _rms(x_ref[0], g_ref[...]).astype(BF16)

    n_slots = w_ref.shape[1] // LANES
    wide = [_dot(h, w_ref[:, c * LANES:min(c + CHUNK_SLOTS, n_slots) * LANES])
            for c in range(0, n_slots, CHUNK_SLOTS)]

    def proj(g0, n):
        pieces = []
        g = g0
        while g < g0 + n:
            c, off = divmod(g, CHUNK_SLOTS)
            take = min(g0 + n - g, CHUNK_SLOTS - off)
            pieces.append(wide[c][:, off * LANES:(off + take) * LANES])
            g += take
        return pieces[0] if len(pieces) == 1 else jnp.concatenate(pieces, axis=1)

    def key_half(n):
        return (lax.broadcasted_iota(jnp.int32, (tm, n * LANES), 1) & (LANES - 1)) < HEAD_DIM

    def padded_heads(pairs):
        slots = []
        for n in range(pairs.shape[1] // LANES):
            pair = pairs[:, _slot(n)]
            slots += [jnp.where(key_half(1), pair, 0.0),
                      jnp.where(key_half(1), pltpu.roll(pair, HEAD_DIM, 1), 0.0)]
        return jnp.concatenate(slots, axis=1)

    aq_ref[0] = padded_heads(proj(G_AQ, A_HEADS // 2)).astype(BF16)
    cmp_ref[0] = proj(G_CMP, 1)
    slc = proj(G_SLC, 1)
    slc_ref[0] = slc.astype(BF16)
    slcv_ref[0] = jnp.where(key_half(1), 1.0, slc).astype(BF16)
    win_ref[0] = proj(G_WIN, 1).astype(BF16)
    b_all = proj(G_BQ, len(B_GROUPS) + B_HEADS)
    for n in range(len(B_GROUPS)):
        bq_ref[0, n] = b_all[:, _slot(n)]
    for n in range(B_HEADS):
        bkv_ref[0, n] = b_all[:, _slot(len(B_GROUPS) + n)]

    s1 = proj(G_S1, 1)
    s2 = proj(G_S2, 1)

    @pl.when(i == 0)
    def _():
        carry_ref[...] = jnp.zeros_like(carry_ref)

    def split3(v):
        hi = v.astype(BF16)
        r1 = v - hi.astype(F32)
        mid = r1.astype(BF16)
        return hi, mid, (r1 - mid.astype(F32)).astype(BF16)

    z = s1 + fb_ref[...]
    logf = jnp.minimum(z, 0.0) - jnp.log(1.0 + jnp.exp(-jnp.abs(z)))
    blk = min(LANES, tm)
    row = lax.broadcasted_iota(jnp.int32, (blk, blk), 0)
    col = lax.broadcasted_iota(jnp.int32, (blk, blk), 1)
    tri = (row >= col).astype(BF16)
    parts = split3(logf)
    running = carry_ref[0:1, :]
    blocks = []
    for b0 in range(0, tm, blk):
        blocks.append(sum(_dot(tri, part[b0:b0 + blk]) for part in parts) + running)
        running = blocks[-1][blk - 1:blk, :]
    cum = jnp.concatenate(blocks, axis=0)
    carry_ref[0:1, :] = running
    small_ref[0] = s1

    hi, mid, lo = split3(-LOG2E * cum)
    r = lax.broadcasted_iota(jnp.int32, (1, LANES), 1) - LANE_FORGET
    in_terms = (r >= 0) & (r < N_TERMS * C_HEADS)

    def is_term(n):
        hit = r == n
        for hd in range(1, C_HEADS):
            hit = hit | (r == N_TERMS * hd + n)
        return hit

    terms = jnp.where(is_term(0), hi.astype(F32), jnp.where(is_term(1), mid.astype(F32), lo.astype(F32)))
    lane4 = lax.broadcasted_iota(jnp.int32, (1, C_HEADS * LANES), 1)
    own = (lane4 & (LANES - 1)) - LANE_FORGET - N_TERMS * lax.shift_right_logical(lane4, 7)
    own_terms = (own >= 0) & (own < N_TERMS)
    cq_ref[0] = jnp.where(own_terms, 1.0, padded_heads(proj(G_CQ, C_HEADS // 2))).astype(BF16)
    ckv = proj(G_CKV, C_HEADS)
    ck_ref[0] = jnp.where(jnp.concatenate([in_terms] * C_HEADS, axis=1),
                          jnp.concatenate([terms] * C_HEADS, axis=1), ckv).astype(BF16)
    cv_ref[0] = jnp.where(key_half(C_HEADS), 1.0, ckv).astype(BF16)

    tab = tab_ref[...]
    c_q, s_q, c_k, s_k = (tab[:, _slot(n)] for n in range(4))
    qn = _rms(proj(G_DQL, 2), qn_ref[...]).astype(BF16)
    qab = _dot(qn, wq_ref[...])
    kvn = _rms(proj(G_DKVL, 1), kvn_ref[...]).astype(BF16)
    kv_up = _dot(kvn, wkv_ref[...])
    k_rot = s1 * c_k + s2 * s_k
    low1 = key_half(1)
    for hd in range(D_HEADS):
        q_rot = qab[:, _slot(hd)] * c_q + qab[:, _slot(D_HEADS + hd)] * s_q
        dq_ref[0, :, _slot(hd)] = q_rot.astype(BF16)
        dk_ref[0, :, _slot(hd)] = jnp.where(low1, kv_up[:, _slot(hd)], k_rot).astype(BF16)
    dv_ref[0] = jnp.where(key_half(D_HEADS), 1.0, kv_up).astype(BF16)


def _inproj(x, gain, w_ext, tab, fbias, qn_g, kvn_g, wq, wkv):
    b, s, d = x.shape
    tm = min(TM_PROJ, s)

    def out(kind, n, dtype):
        if kind == "rows":
            return (pl.BlockSpec((1, tm, n * LANES), lambda bi, i: (bi, i, 0)),
                    jax.ShapeDtypeStruct((b, s, n * LANES), dtype))
        return (pl.BlockSpec((1, n, tm, LANES), lambda bi, i: (bi, 0, i, 0)),
                jax.ShapeDtypeStruct((b, n, s, LANES), dtype))

    outs = [out("rows", A_HEADS, BF16), out("rows", 1, F32), out("rows", 1, BF16), out("rows", 1, BF16),
            out("rows", 1, BF16), out("slots", len(B_GROUPS), F32), out("slots", B_HEADS, F32),
            out("rows", C_HEADS, BF16), out("rows", C_HEADS, BF16), out("rows", C_HEADS, BF16),
            out("rows", D_HEADS, BF16), out("rows", D_HEADS, BF16), out("rows", D_HEADS, BF16),
            out("rows", 1, F32)]
    consts = (gain, w_ext)
    consts2 = (fbias, qn_g, kvn_g, wq, wkv)
    return pl.pallas_call(
        _inproj_kernel,
        grid=(b, s // tm),
        in_specs=[pl.BlockSpec((1, tm, d), lambda bi, i: (bi, i, 0))]
        + [_const_spec(c.shape) for c in consts]
        + [pl.BlockSpec((tm, 4 * LANES), lambda bi, i: (i, 0))]
        + [_const_spec(c.shape) for c in consts2],
        out_specs=[o[0] for o in outs],
        out_shape=[o[1] for o in outs],
        scratch_shapes=[pltpu.VMEM((8, LANES), F32)],
        name="inproj",
        compiler_params=_cparams(2),
    )(x, *consts, tab, *consts2)


def _gelu_tanh(x):
    return 0.5 * x * (1.0 + jnp.tanh(math.sqrt(2.0 / math.pi) * (x + 0.044715 * (x * x * x))))


def _compress_kernel(kc_ref, vc_ref, pos_ref, w1k_ref, w1v_ref, w2k_ref, w2v_ref, out_ref):
    nc = kc_ref.shape[1]
    half = A_CMP_STRIDE * HEAD_DIM
    pos_a = pos_ref[0:1, :]
    pos_b = pos_ref[1:2, :]

    def phi(chunks, w1_ref, w2_ref):
        u = _dot((chunks + pos_a).astype(BF16), w1_ref[0:half, :])
        v = _dot((chunks + pos_b).astype(BF16), w1_ref[half:2 * half, :])
        hid = u + pltpu.roll(v, nc - 1, 0)
        return _dot(_gelu_tanh(hid).astype(BF16), w2_ref[...])

    out_ref[0] = (phi(kc_ref[0], w1k_ref, w2k_ref) + phi(vc_ref[0], w1v_ref, w2v_ref)).astype(BF16)


def _compress(kc_in, vc_in, pos2, w1k, w1v, w2k_pad, w2v_pad):
    b, nc, width = kc_in.shape
    blk = pl.BlockSpec((1, nc, width), lambda bi: (bi, 0, 0))
    return pl.pallas_call(
        _compress_kernel,
        grid=(b,),
        in_specs=[blk, blk, _const_spec(pos2.shape), _const_spec(w1k.shape), _const_spec(w1v.shape),
                  _const_spec(w2k_pad.shape), _const_spec(w2v_pad.shape)],
        out_specs=pl.BlockSpec((1, nc, LANES), lambda bi: (bi, 0, 0)),
        out_shape=jax.ShapeDtypeStruct((b, nc, LANES), BF16),
        name="compress",
        compiler_params=_cparams(1),
    )(kc_in, vc_in, pos2, w1k, w1v, w2k_pad, w2v_pad)


def _nsa_cmp_win_kernel(q_ref, kvc_ref, kvw_ref, small_ref, ovl_ref, bias_ref,
                        part_ref, sel_ref, *, n_top):
    tq = min(TQ_NSA, q_ref.shape[1])
    for u in range(q_ref.shape[1] // tq):
        _nsa_cmp_win_tile(u * tq, tq, q_ref, kvc_ref, kvw_ref, small_ref, ovl_ref, bias_ref,
                          part_ref, sel_ref, n_top)


def _nsa_cmp_win_tile(r0, tq, q_ref, kvc_ref, kvw_ref, small_ref, ovl_ref, bias_ref, part_ref, sel_ref, n_top):
    nc = kvc_ref.shape[1]
    n_sel = ovl_ref.shape[0]
    t0 = pl.multiple_of(pl.program_id(1) * q_ref.shape[1] + r0, tq)
    tile = slice(r0, r0 + tq)
    q4 = jnp.concatenate([q_ref[0, tile, _slot(h)] for h in range(A_HEADS)], axis=0)
    rows = A_HEADS * tq

    kvc = kvc_ref[0]
    s = _dot_nt(q4, kvc)
    t_row = t0 + (lax.broadcasted_iota(jnp.int32, (rows, nc), 0) & (tq - 1))
    c_end = lax.broadcasted_iota(jnp.int32, (rows, nc), 1) * A_CMP_STRIDE + (A_CMP_LEN - 1)
    valid = c_end <= t_row
    s = jnp.where(valid, s, NEG_INF)
    m = jnp.max(s, axis=1, keepdims=True)
    p = jnp.where(valid, jnp.exp2(s - m), 0.0)
    den = jnp.maximum(jnp.sum(p, axis=1, keepdims=True), 1e-30)
    p = p / den
    o_c = _dot(p.astype(BF16), kvc)

    p_sum = p[0:tq] + p[tq:2 * tq] + p[2 * tq:3 * tq] + p[3 * tq:4 * tq]
    p_hi = p_sum.astype(BF16)
    p_lo = (p_sum - p_hi.astype(F32)).astype(BF16)
    imp_t = _dot_nt(ovl_ref[...], p_hi) + _dot_nt(ovl_ref[...], p_lo)

    j_idx = lax.broadcasted_iota(jnp.int32, (n_sel, tq), 0)
    cur = lax.shift_right_logical(t0 + lax.broadcasted_iota(jnp.int32, (n_sel, tq), 1), 6)
    forced = (j_idx == 0) | (j_idx == cur) | (j_idx == cur - 1)
    imp_t = jnp.where(forced, FORCE_SCORE, jnp.where(j_idx > cur, -FORCE_SCORE, imp_t))
    sub8 = 8
    groups = [imp_t[v * sub8:(v + 1) * sub8] for v in range(n_sel // sub8)]
    ranks = [jnp.zeros((sub8, tq), F32) for _ in groups]
    in_group = lax.broadcasted_iota(jnp.int32, (sub8, tq), 0)
    for c in range(n_sel):
        row_c = imp_t[c:c + 1, :]
        for v, grp in enumerate(groups):
            if v > c // sub8:
                beats = row_c >= grp
            elif v < c // sub8:
                beats = row_c > grp
            else:
                beats = (row_c > grp) | ((row_c == grp) & (in_group > c % sub8))
            ranks[v] = ranks[v] + jnp.where(beats, 1.0, 0.0)
    rank = jnp.concatenate(ranks, axis=0)
    sel_t = jnp.where(rank < float(n_top), 1.0, 0.0).astype(BF16)
    sel_ref[0, :, tile] = sel_t

    span = A_WINDOW + tq
    kvw = kvw_ref[0, pl.ds(t0, span), :]
    kpos = t0 - A_WINDOW + lax.broadcasted_iota(jnp.int32, (1, span), 1)
    sw = _dot_nt(q4, kvw) + bias_ref[...] + jnp.where(kpos >= 0, 0.0, NEG_INF)
    mw = jnp.max(sw, axis=1, keepdims=True)
    pw = jnp.exp2(sw - mw)
    o_w = _dot(pw.astype(BF16), kvw) / jnp.sum(pw, axis=1, keepdims=True)

    g = _sigmoid(small_ref[0, tile])
    gated = []
    for h in range(A_HEADS):
        g_c = g[:, LANE_GATE + 3 * h:LANE_GATE + 3 * h + 1]
        g_w = g[:, LANE_GATE + 3 * h + 2:LANE_GATE + 3 * h + 3]
        gated.append(g_c * o_c[h * tq:(h + 1) * tq] + g_w * o_w[h * tq:(h + 1) * tq])
    low = lax.broadcasted_iota(jnp.int32, (tq, LANES), 1) < HEAD_DIM
    for pair in range(A_HEADS // 2):
        part_ref[0, tile, _slot(pair)] = jnp.where(low, pltpu.roll(gated[2 * pair], HEAD_DIM, 1),
                                                   gated[2 * pair + 1])


def _nsa_cmp_win(aq, kvc, kvw_pad, small, ovl, bias_w):
    b, s, _ = aq.shape
    tq = min(TQ_NSA_STEP, s)
    nc = kvc.shape[1]
    n_sel = ovl.shape[0]
    n_top = min(A_SEL_TOPK, n_sel)
    return pl.pallas_call(
        functools.partial(_nsa_cmp_win_kernel, n_top=n_top),
        grid=(b, s // tq),
        in_specs=[pl.BlockSpec((1, tq, A_HEADS * LANES), lambda bi, i: (bi, i, 0)),
                  pl.BlockSpec((1, nc, LANES), lambda bi, i: (bi, 0, 0)),
                  pl.BlockSpec((1, s + A_WINDOW, LANES), lambda bi, i: (bi, 0, 0)),
                  pl.BlockSpec((1, tq, LANES), lambda bi, i: (bi, i, 0)),
                  _const_spec(ovl.shape),
                  _const_spec(bias_w.shape)],
        out_specs=[pl.BlockSpec((1, tq, A_HEADS * HEAD_DIM), lambda bi, i: (bi, i, 0)),
                   pl.BlockSpec((1, n_sel, tq), lambda bi, i: (bi, 0, i))],
        out_shape=[jax.ShapeDtypeStruct((b, s, A_HEADS * HEAD_DIM), F32),
                   jax.ShapeDtypeStruct((b, n_sel, s), BF16)],
        name="nsa_cmp_win",
        compiler_params=_cparams(2),
    )(aq, kvc, kvw_pad, small, ovl, bias_w)


def _flash_kernel(*refs, mode, nh):
    if mode == "sel":
        q_ref, k_ref, v_ref, selt_ref, e_ref, bias_ref, part_ref, small_ref, o_ref, m_ref, acc_ref = refs
    else:
        q_ref, k_ref, v_ref, o_ref, m_ref, acc_ref = refs
    i = pl.program_id(1)
    tq = q_ref.shape[1]
    tk = tq
    shared_kv = mode == "sel"

    m_ref[...] = jnp.full(m_ref.shape, NEG_INF, F32)
    acc_ref[...] = jnp.zeros(acc_ref.shape, F32)

    lag = lax.broadcasted_iota(jnp.int32, (tq, tk), 0) - lax.broadcasted_iota(jnp.int32, (tq, tk), 1)

    def step(j, diag):
        k0 = pl.multiple_of(j * tk, tk)
        keep = (lag >= 0) if diag else None
        if mode == "sel":
            sel_keys = lax.dot_general(selt_ref[0], e_ref[:, pl.ds(k0, tk)], (((0,), (0,)), ((), ())),
                                       preferred_element_type=F32) > 0.5
            keep = (sel_keys & keep) if diag else sel_keys
            n_off, bt = bias_ref.shape[0], bias_ref.shape[2]

            def bias_tile(h):
                strips = []
                for a in range(tq // bt):
                    offs = [jnp.clip(i * (tq // bt) + a - j * (tk // bt) - c, 0, n_off - 1)
                            for c in range(tk // bt)]
                    strips.append(jnp.concatenate([bias_ref[o, h] for o in offs], axis=1))
                return jnp.concatenate(strips, axis=0)
        logits = []
        for h in range(nh):
            s = _dot_nt(q_ref[0, :, _slot(h)], k_ref[0, pl.ds(k0, tk), _slot(0 if shared_kv else h)])
            if mode == "sel":
                s = s + bias_tile(h)
            if keep is not None:
                s = jnp.where(keep, s, NEG_INF)
            logits.append(s)
        probs = []
        for h in range(nh):
            m_prev = m_ref[h]
            m_new = jnp.maximum(m_prev, jnp.max(logits[h], axis=1, keepdims=True))
            p = jnp.exp2(logits[h] - jnp.concatenate([m_new] * (tk // LANES), axis=1))
            alpha = jnp.exp2(m_prev - m_new)
            m_ref[h] = m_new
            probs.append((p.astype(BF16), alpha))
        for h in range(nh):
            p, alpha = probs[h]
            acc_ref[h] = alpha * acc_ref[h] + _dot(p, v_ref[0, pl.ds(k0, tk), _slot(0 if shared_kv else h)])

    def full_tile(j, carry):
        step(j, False)
        return carry

    lax.fori_loop(0, i, full_tile, 0)
    step(i, True)

    low = lax.broadcasted_iota(jnp.int32, (tq, LANES), 1) < HEAD_DIM
    if mode == "sel":
        g = _sigmoid(small_ref[0])
    for pair in range(nh // 2):
        outs = []
        for h in (2 * pair, 2 * pair + 1):
            acc = acc_ref[h]
            outs.append(acc / acc[:, 0:1])
        o_pair = jnp.where(low, pltpu.roll(outs[0], HEAD_DIM, 1), outs[1])
        if mode == "sel":
            gate = [g[:, LANE_GATE + 3 * h + 1:LANE_GATE + 3 * h + 2] for h in (2 * pair, 2 * pair + 1)]
            o_pair = part_ref[0, :, _slot(pair)] + jnp.where(low, gate[0], gate[1]) * o_pair
        o_ref[0, :, _slot(pair)] = o_pair.astype(BF16)


def _flash(mode, q, k, v, sel_t=None, expand=None, bias=None, part=None, small=None):
    b, s, qw = q.shape
    nh = qw // LANES
    tq = min(TQ_FLASH, s)
    ow = nh * HEAD_DIM

    def rows(width):
        return pl.BlockSpec((1, tq, width), lambda bi, i: (bi, i, 0))

    def seq_spec(arr):
        return pl.BlockSpec((1,) + arr.shape[1:], lambda bi, i: (bi, 0, 0))

    args = (q, k, v)
    in_specs = [rows(qw), seq_spec(k), seq_spec(v)]
    if mode == "sel":
        args += (sel_t, expand, bias, part, small)
        in_specs += [pl.BlockSpec((1, sel_t.shape[1], tq), lambda bi, i: (bi, 0, i)),
                     _const_spec(expand.shape), _const_spec(bias.shape), rows(ow), rows(LANES)]
    return pl.pallas_call(
        functools.partial(_flash_kernel, mode=mode, nh=nh),
        grid=(b, s // tq),
        in_specs=in_specs,
        out_specs=rows(ow),
        out_shape=jax.ShapeDtypeStruct((b, s, ow), BF16),
        scratch_shapes=[pltpu.VMEM((nh, tq, LANES), F32)] * 2,
        name="flash_" + mode,
        compiler_params=_cparams(2),
    )(*args)


def _dilated_kernel(q_ref, kvp_ref, kvc_ref, bias_ref, o_ref, o_sc, l_sc):
    i = pl.program_id(1)
    rows = q_ref.shape[2]
    tb = TQ_DIL
    low = lax.broadcasted_iota(jnp.int32, (tb, LANES), 1) < HEAD_DIM
    col = lax.broadcasted_iota(jnp.int32, (tb, 2 * tb), 1)
    in_seq = (col >= tb) | (i > 0)
    for g, (_, dil) in enumerate(B_GROUPS):
        span = dil * tb
        for r in range(dil):
            for c in range(rows // span):
                cur = pl.ds(r + c * span, tb, stride=dil)
                if c > 0:
                    prev_ref, prev = kvc_ref, pl.ds(r + (c - 1) * span, tb, stride=dil)
                else:
                    prev_ref, prev = kvp_ref, pl.ds(rows - span + r, tb, stride=dil)
                q2 = q_ref[0, g, cur, :]
                outs, lses = [], []
                for h in range(B_HPG):
                    q_h = jnp.where(low if h == 0 else ~low, q2, 0.0).astype(BF16)
                    head = B_HPG * g + h
                    kv = jnp.concatenate([prev_ref[0, head, prev, :], kvc_ref[0, head, cur, :]],
                                         axis=0).astype(BF16)
                    s = _dot_nt(q_h, kv) + bias_ref[g, h]
                    if c == 0:
                        s = jnp.where(in_seq, s, NEG_INF)
                    m = jnp.max(s, axis=1, keepdims=True)
                    p = jnp.exp2(s - m)
                    den = jnp.sum(p, axis=1, keepdims=True)
                    outs.append(_dot(p.astype(BF16), kv) / den)
                    lses.append(m + jnp.log2(den))
                o_sc[g, cur, :] = jnp.where(low, outs[1], outs[0])
                l_sc[g, cur, :] = jnp.where(low, lses[1], lses[0])
    step = 2 * tb
    for t in range(rows // step):
        rs = slice(t * step, (t + 1) * step)
        l0, l1, l2 = l_sc[0, rs, :], l_sc[1, rs, :], l_sc[2, rs, :]
        mx = jnp.maximum(jnp.maximum(l0, l1), l2)
        e0, e1, e2 = jnp.exp2(l0 - mx), jnp.exp2(l1 - mx), jnp.exp2(l2 - mx)
        mixed = (e0 * o_sc[0, rs, :] + e1 * o_sc[1, rs, :] + e2 * o_sc[2, rs, :]) / (e0 + e1 + e2)
        o_ref[0, rs, :] = mixed.astype(BF16)


def _dilated(bq, bkv, bias):
    b, _, s, _ = bq.shape
    rows = TQ_DIL * B_GROUPS[-1][1]
    n_groups = len(B_GROUPS)

    def tile(n_slots, prev):
        if prev:
            return pl.BlockSpec((1, n_slots, rows, LANES), lambda bi, i: (bi, 0, jnp.maximum(i - 1, 0), 0))
        return pl.BlockSpec((1, n_slots, rows, LANES), lambda bi, i: (bi, 0, i, 0))

    return pl.pallas_call(
        _dilated_kernel,
        grid=(b, s // rows),
        in_specs=[tile(bq.shape[1], False), tile(bkv.shape[1], True), tile(bkv.shape[1], False),
                  _const_spec(bias.shape)],
        out_specs=pl.BlockSpec((1, rows, LANES), lambda bi, i: (bi, i, 0)),
        out_shape=jax.ShapeDtypeStruct((b, s, LANES), BF16),
        scratch_shapes=[pltpu.VMEM((n_groups, rows, LANES), F32)] * 2,
        name="dilated",
        compiler_params=_cparams(2),
    )(bq, bkv, bkv, bias)


def _merge_kernel(x_ref, oa_ref, ob_ref, oc_ref, od_ref,
                  gpre_ref, gpost_ref, wg_ref, wba_ref, wbb_ref, wbc_ref, wbd_ref, wo_ref, out_ref):
    x = x_ref[0]
    h = _rms(x, gpre_ref[...]).astype(BF16)
    branches = ((oa_ref[0], wba_ref), (ob_ref[0], wbb_ref), (oc_ref[0], wbc_ref), (od_ref[0], wbd_ref))
    merged = None
    for n, (o_n, wb_ref) in enumerate(branches):
        term = _sigmoid(_dot(h, wg_ref[n])) * _dot(o_n, wb_ref[...])
        merged = term if merged is None else merged + term
    z = _dot(merged.astype(BF16), wo_ref[...])
    out_ref[0] = x + _rms(z, gpost_ref[...])


def _merge(x, oa, ob, oc, od, gpre, gpost, wg, wba, wbb, wbc, wbd, wo):
    b, s, d = x.shape
    tm = min(TM_PROJ, s)

    def rows(arr):
        return pl.BlockSpec((1, tm, arr.shape[2]), lambda bi, i: (bi, i, 0))

    acts = (x, oa, ob, oc, od)
    consts = (gpre, gpost, wg, wba, wbb, wbc, wbd, wo)
    return pl.pallas_call(
        _merge_kernel,
        grid=(b, s // tm),
        in_specs=[rows(a) for a in acts] + [_const_spec(c.shape) for c in consts],
        out_specs=rows(x),
        out_shape=jax.ShapeDtypeStruct(x.shape, F32),
        name="merge",
        compiler_params=_cparams(2),
    )(*acts, *consts)


def _ffn_kernel(x_ref, gpre_ref, gpost_ref, wup_ref, cw_ref, cb_ref, wdn_ref, out_ref, tail_ref, *, cuts):
    i = pl.program_id(1)
    d_ff = wdn_ref.shape[0]
    tm = min(TM_PROJ, x_ref.shape[1])

    @pl.when(i == 0)
    def _():
        tail_ref[...] = jnp.zeros_like(tail_ref)

    for r0 in range(0, x_ref.shape[1], tm):
        x = x_ref[0, r0:r0 + tm, :]
        h = _rms(x, gpre_ref[...]).astype(BF16)

        def conv(c0, chunk):
            u = _dot(h, wup_ref[:, c0:c0 + chunk])
            ue = jnp.concatenate([tail_ref[:, c0:c0 + chunk], u], axis=0)
            tail_ref[:, c0:c0 + chunk] = u[tm - CONV_HALO:]
            w = cw_ref[:, c0:c0 + chunk]
            out = cb_ref[:, c0:c0 + chunk] + w[0:1] * pltpu.roll(ue, 2, 0)[CONV_HALO:]
            out = out + w[1:2] * pltpu.roll(ue, 1, 0)[CONV_HALO:]
            return out + w[2:3] * u

        y = jnp.zeros((tm, x.shape[1]), F32)
        for c0, c1 in zip(cuts[:-1], cuts[1:]):
            gate = conv(c0, c1 - c0)
            val = conv(d_ff + c0, c1 - c0)
            act = gate * _sigmoid(gate) * val
            y = y + _dot(act.astype(BF16), wdn_ref[c0:c1, :])
        out_ref[0, r0:r0 + tm, :] = x + _rms(y, gpost_ref[...])


def _ffn(x, gpre, gpost, w_up, conv_w, conv_b, w_down):
    b, s, d = x.shape
    tm = min(FFN_SUBTILES * TM_PROJ, s)
    d_ff = w_down.shape[0]
    n_tiles = d_ff // MXU_WIDTH
    cuts = (0, (n_tiles // 2) * MXU_WIDTH, d_ff) if d_ff % MXU_WIDTH == 0 and n_tiles > 1 else (0, d_ff)
    return pl.pallas_call(
        functools.partial(_ffn_kernel, cuts=cuts),
        grid=(b, s // tm),
        in_specs=[pl.BlockSpec((1, tm, d), lambda bi, i: (bi, i, 0)),
                  _const_spec(gpre.shape), _const_spec(gpost.shape), _const_spec(w_up.shape),
                  _const_spec(conv_w.shape), _const_spec(conv_b.shape), _const_spec(w_down.shape)],
        out_specs=pl.BlockSpec((1, tm, d), lambda bi, i: (bi, i, 0)),
        out_shape=jax.ShapeDtypeStruct(x.shape, F32),
        scratch_shapes=[pltpu.VMEM((CONV_HALO, w_up.shape[1]), F32)],
        name="ffn",
        compiler_params=_cparams(2),
    )(x, gpre, gpost, w_up, conv_w, conv_b, w_down)


def _rel_bucket(dist):
    dist = jnp.maximum(dist, 0)
    d = jnp.maximum(dist, 1).astype(F32)
    large = BUCKET_EXACT + (jnp.log(d / BUCKET_EXACT) / math.log(BUCKET_MAX_DIST / BUCKET_EXACT)
                            * (N_BUCKETS - BUCKET_EXACT)).astype(jnp.int32)
    large = jnp.minimum(large, N_BUCKETS - 1)
    return jnp.where(dist < BUCKET_EXACT, dist, large)


def _in_proj_columns():
    src = np.full((N_SLOTS * LANES,), -1, np.int64)
    scl = np.ones((N_SLOTS * LANES,), np.float32)
    o_aq = 0
    o_akv = o_aq + A_HEADS * HEAD_DIM
    o_ag = o_akv + 6 * HEAD_DIM
    o_b = o_ag + 3 * A_HEADS
    o_c = o_b + 3 * B_HEADS * HEAD_DIM
    o_cf = o_c + 3 * C_HEADS * HEAD_DIM
    o_dq = o_cf + C_HEADS
    o_dkv = o_dq + D_Q_LORA
    o_dkr = o_dkv + D_KV_LORA
    e = np.arange(HEAD_DIM)
    q_scale = HEAD_DIM ** -0.5 * LOG2E

    def put(slot, lane0, cols, scale=1.0):
        dst = slot * LANES + lane0 + np.arange(len(cols))
        src[dst] = cols
        scl[dst] = scale

    for h in range(A_HEADS):
        put(G_AQ + h // 2, (h % 2) * HEAD_DIM, o_aq + h * HEAD_DIM + e, q_scale)
    for n, slot in enumerate((G_CMP, G_SLC, G_WIN)):
        put(slot, 0, o_akv + (2 * n) * HEAD_DIM + e)
        put(slot, HEAD_DIM, o_akv + (2 * n + 1) * HEAD_DIM + e)
    for h in range(B_HEADS):
        odd = h % B_HPG
        put(G_BQ + h // B_HPG, odd * HEAD_DIM, o_b + h * HEAD_DIM + e, q_scale)
        put(G_BKV + h, odd * HEAD_DIM, o_b + (B_HEADS + h) * HEAD_DIM + e)
        put(G_BKV + h, (1 - odd) * HEAD_DIM, o_b + (2 * B_HEADS + h) * HEAD_DIM + e)
    for h in range(C_HEADS):
        put(G_CQ + h // 2, (h % 2) * HEAD_DIM, o_c + h * HEAD_DIM + e, q_scale)
        put(G_CKV + h, 0, o_c + (C_HEADS + h) * HEAD_DIM + e)
        put(G_CKV + h, HEAD_DIM, o_c + (2 * C_HEADS + h) * HEAD_DIM + e)
    put(G_DQL, 0, o_dq + np.arange(D_Q_LORA))
    put(G_DKVL, 0, o_dkv + np.arange(D_KV_LORA))
    put(G_S1, LANE_GATE, o_ag + np.arange(3 * A_HEADS))
    put(G_S1, LANE_FORGET, o_cf + np.repeat(np.arange(C_HEADS), N_TERMS))
    put(G_S1, LANE_ROPE, o_dkr + np.arange(D_ROPE))
    half = D_ROPE // 2
    put(G_S2, LANE_ROPE, o_dkr + half + np.arange(half), -1.0)
    put(G_S2, LANE_ROPE + half, o_dkr + np.arange(half))
    return src, scl


def _gather_cols(w, src, scl):
    pieces = []
    n = len(src)
    a = 0
    while a < n:
        e = a + 1
        if src[a] < 0:
            while e < n and src[e] < 0:
                e += 1
            pieces.append(jnp.zeros((w.shape[0], e - a), w.dtype))
        else:
            while e < n and src[e] == src[e - 1] + 1 and scl[e] == scl[a]:
                e += 1
            run = w[:, int(src[a]):int(src[a]) + (e - a)]
            pieces.append(run if scl[a] == 1.0 else run * float(scl[a]))
        a = e
    return jnp.concatenate(pieces, axis=1)


def _mla_weight_columns():
    per_q = D_NOPE + D_ROPE
    half = D_ROPE // 2
    qa = np.full((D_HEADS * LANES,), -1, np.int64)
    qb = np.full((D_HEADS * LANES,), -1, np.int64)
    qb_s = np.ones((D_HEADS * LANES,), np.float32)
    for h in range(D_HEADS):
        qa[h * LANES + np.arange(per_q)] = h * per_q + np.arange(per_q)
        rot = h * per_q + D_NOPE
        qb[h * LANES + D_NOPE + np.arange(half)] = rot + half + np.arange(half)
        qb_s[h * LANES + D_NOPE + np.arange(half)] = -1.0
        qb[h * LANES + D_NOPE + half + np.arange(half)] = rot + np.arange(half)
    ones = np.ones_like(qb_s)
    return (qa, ones), (qb, qb_s)


def _rope_tables(s):
    inv_freq = ROPE_THETA ** (-jnp.arange(0, D_ROPE, 2, dtype=F32) / D_ROPE)
    ang = jnp.arange(s, dtype=F32)[:, None] * inv_freq[None, :]
    cos2 = jnp.concatenate([jnp.cos(ang)] * 2, axis=1)
    sin2 = jnp.concatenate([jnp.sin(ang)] * 2, axis=1)
    scale = (D_NOPE + D_ROPE) ** -0.5 * LOG2E
    z_lo = jnp.zeros((s, D_NOPE), F32)
    z_hi = jnp.zeros((s, LANES - D_NOPE - D_ROPE), F32)
    c_q = jnp.concatenate([jnp.full((s, D_NOPE), scale, F32), scale * cos2, z_hi], axis=1)
    s_q = jnp.concatenate([z_lo, scale * sin2, z_hi], axis=1)
    c_k = jnp.concatenate([z_lo, cos2, z_hi], axis=1)
    s_k = jnp.concatenate([z_lo, sin2, z_hi], axis=1)
    return jnp.concatenate([c_q, s_q, c_k, s_k], axis=1)


def _toeplitz(vec, n_rows, n_cols, off, step=1):
    base = vec[:, ::step]
    last = base.shape[1] - 1
    lo, hi = off - (n_cols - 1), off + n_rows - 1
    core = base[:, max(lo, 0):min(hi, last) + 1]
    front = jnp.repeat(base[:, :1], max(0, -lo), axis=1)
    back = jnp.repeat(base[:, last:], max(0, hi - last), axis=1)
    g = jnp.concatenate([front, core, back], axis=1)
    lg = n_rows + n_cols - 1
    t = jnp.tile(g, (1, n_rows + 1))[:, :n_rows * (lg + 1)].reshape(-1, n_rows, lg + 1)
    return t[:, :, :n_cols][:, :, ::-1]


def _bias_tables(rel_table, s):
    bucket = _rel_bucket(jnp.arange(s))
    hit = bucket[None, :, None] == jnp.arange(N_BUCKETS)[None, None, :]
    by_dist = jnp.sum(jnp.where(hit, LOG2E * rel_table.T[:, None, :], 0.0), axis=-1)
    tqa = min(TQ_NSA, s)
    dist = jnp.arange(tqa)[:, None] + A_WINDOW - jnp.arange(A_WINDOW + tqa)[None, :]
    ok = (dist >= 0) & (dist < A_WINDOW)
    bias_w = jnp.where(ok[None], _toeplitz(by_dist[:A_HEADS], tqa, A_WINDOW + tqa, A_WINDOW), NEG_INF)
    bias_w = bias_w.reshape(A_HEADS * tqa, A_WINDOW + tqa)
    tf = min(BIAS_TILE, s)
    n_off = min(s // tf, -(-(BUCKET_MAX_DIST + tf - 1) // tf) + 1)
    bias_s = jnp.stack([_toeplitz(by_dist[:A_HEADS], tf, tf, n * tf) for n in range(n_off)])
    m = jnp.arange(TQ_DIL)[:, None] + TQ_DIL - jnp.arange(2 * TQ_DIL)[None, :]
    bias_d = []
    for g, (window, dil) in enumerate(B_GROUPS):
        ok = (m >= 0) & (m <= window // dil)
        heads = by_dist[A_HEADS + g * B_HPG:A_HEADS + (g + 1) * B_HPG]
        bias_d.append(jnp.where(ok[None], _toeplitz(heads, TQ_DIL, 2 * TQ_DIL, TQ_DIL, dil), NEG_INF))
    return bias_w, bias_s, jnp.stack(bias_d)


def _selection_constants(s):
    nc = s // A_CMP_STRIDE
    n_cmp = (s - A_CMP_LEN) // A_CMP_STRIDE + 1
    n_sel = s // A_SEL_BLOCK
    c = np.arange(nc)[:, None]
    j = np.arange(n_sel)[None, :]
    c_start = c * A_CMP_STRIDE
    overlap = ((c_start <= j * A_SEL_BLOCK + A_SEL_BLOCK - 1) & (c_start + A_CMP_LEN - 1 >= j * A_SEL_BLOCK)
               & (c < n_cmp))
    expand = (np.arange(s)[None, :] // A_SEL_BLOCK) == np.arange(n_sel)[:, None]
    return jnp.asarray(overlap.T, BF16), jnp.asarray(expand, BF16)


def kernel(x, rel_bias_table, norm_attn_pre, norm_attn_post, norm_ffn_pre, norm_ffn_post, w_in, nsa_cmp_pos, nsa_phi_k_w1, nsa_phi_k_w2, nsa_phi_v_w1, nsa_phi_v_w2, fox_forget_bias, mla_q_norm, mla_kv_norm, mla_w_uq, mla_w_ukv, w_branch_a, w_branch_b, w_branch_c, w_branch_d, w_merge_gate, w_o, ffn_w_up, ffn_conv_w, ffn_conv_b, ffn_w_down):
    b, s, d = x.shape
    depth = w_in.shape[0]
    assert s % (TQ_DIL * B_GROUPS[-1][1]) == 0, "every dilation class needs whole 128-row tiles"

    in_src, in_scl = _in_proj_columns()
    assert D_NOPE + D_VDIM == LANES, "an MLA head's [k_nope | v] up-projection fills one slot"
    (qa_i, qa_s), (qb_i, qb_s) = _mla_weight_columns()
    rope_tab = _rope_tables(s)
    bias_w, bias_s, bias_d = _bias_tables(rel_bias_table, s)
    overlap, expand = _selection_constants(s)
    half = A_CMP_STRIDE * HEAD_DIM

    for l in range(depth):
        w_ext = _gather_cols(w_in[l], in_src, in_scl).astype(BF16)
        fbias = jnp.zeros((1, LANES), F32).at[0, LANE_FORGET:LANE_FORGET + N_TERMS * C_HEADS].set(
            jnp.repeat(fox_forget_bias[l], N_TERMS))
        wq = jnp.concatenate([_gather_cols(mla_w_uq[l], qa_i, qa_s), _gather_cols(mla_w_uq[l], qb_i, qb_s)],
                             axis=1).astype(BF16)
        (aq, cmp, slc, slc_v, win, bq, bkv, cq, ck, c_v, dq, dk, d_v, small) = _inproj(
            x, norm_attn_pre[l][None], w_ext, rope_tab, fbias,
            mla_q_norm[l][None], mla_kv_norm[l][None], wq, mla_w_ukv[l].astype(BF16))

        nc = s // A_CMP_STRIDE
        w2k_pad = jnp.pad(nsa_phi_k_w2[l], ((0, 0), (0, LANES - HEAD_DIM))).astype(BF16)
        w2v_pad = jnp.pad(nsa_phi_v_w2[l], ((0, 0), (LANES - HEAD_DIM, 0))).astype(BF16)
        kvc = _compress(cmp[..., :HEAD_DIM].reshape(b, nc, half), cmp[..., HEAD_DIM:].reshape(b, nc, half),
                        nsa_cmp_pos[l].reshape(2, half),
                        nsa_phi_k_w1[l].astype(BF16), nsa_phi_v_w1[l].astype(BF16), w2k_pad, w2v_pad)
        kvw_pad = jnp.pad(win, ((0, 0), (A_WINDOW, 0), (0, 0)))
        part, sel_t = _nsa_cmp_win(aq, kvc, kvw_pad, small, overlap, bias_w)
        o_a = _flash("sel", aq, slc, slc_v, sel_t=sel_t, expand=expand, bias=bias_s, part=part, small=small)

        o_b = _dilated(bq, bkv, bias_d)

        o_c = _flash("fox", cq, ck, c_v)
        o_d = _flash("mla", dq, dk, d_v)

        x = _merge(x, o_a, o_b, o_c, o_d,
                   norm_attn_pre[l][None], norm_attn_post[l][None],
                   w_merge_gate[l].astype(BF16),
                   w_branch_a[l].astype(BF16),
                   jnp.concatenate([w_branch_b[l][HEAD_DIM:], w_branch_b[l][:HEAD_DIM]]).astype(BF16),
                   w_branch_c[l].astype(BF16),
                   w_branch_d[l].astype(BF16),
                   w_o[l].astype(BF16))
        x = _ffn(x, norm_ffn_pre[l][None], norm_ffn_post[l][None], ffn_w_up[l].astype(BF16),
                 ffn_conv_w[l], ffn_conv_b[l][None], ffn_w_down[l].astype(BF16))
    return x
```

```python
import functools
import math

import numpy as np
import jax
import jax.numpy as jnp
from jax import lax
from jax.experimental import pallas as pl
from jax.experimental.pallas import tpu as pltpu

F32 = jnp.float32
BF16 = jnp.bfloat16

LANES = 128
MXU_WIDTH = 256
HEAD_DIM = 64
RMS_EPS = 1e-6
NEG_INF = -1e30
FORCE_SCORE = 1e9
LOG2E = math.log2(math.e)
VMEM_LIMIT_BYTES = 56 * 1024 * 1024

N_BUCKETS = 32
BUCKET_EXACT = 16
BUCKET_MAX_DIST = 2048

A_HEADS = 4
A_CMP_LEN = 32
A_CMP_STRIDE = 16
A_PHI_HIDDEN = 128
A_SEL_BLOCK = 64
A_SEL_TOPK = 16
A_WINDOW = 512
B_GROUPS = ((128, 1), (512, 4), (2048, 16))
B_HPG = 2
B_HEADS = B_HPG * len(B_GROUPS)
C_HEADS = 4
D_HEADS = 4
D_Q_LORA = 256
D_KV_LORA = 128
D_NOPE = 64
D_ROPE = 32
D_VDIM = 64
ROPE_THETA = 10000.0
N_BRANCHES = 4

G_AQ = 0
G_CMP = G_AQ + A_HEADS // 2
G_SLC = G_CMP + 1
G_WIN = G_SLC + 1
G_BQ = G_WIN + 1
G_BKV = G_BQ + len(B_GROUPS)
G_CQ = G_BKV + B_HEADS
G_CKV = G_CQ + C_HEADS // 2
G_DQL = G_CKV + C_HEADS
G_DKVL = G_DQL + 2
G_S1 = G_DKVL + 1
G_S2 = G_S1 + 1
N_SLOTS = G_S2 + 1
LANE_GATE = 0
LANE_FORGET = 96
N_TERMS = 3
LANE_ROPE = 64

TM_PROJ = 512
TQ_NSA_STEP = 512
TQ_FLASH = 512
CHUNK_SLOTS = 8
BIAS_TILE = 256
TQ_NSA = 128
TQ_DIL = 128
CONV_HALO = 8


def _cparams(n_grid):
    return pltpu.CompilerParams(dimension_semantics=("arbitrary",) * n_grid,
                                vmem_limit_bytes=VMEM_LIMIT_BYTES)


def _const_spec(shape):
    nd = len(shape)
    return pl.BlockSpec(shape, lambda *_: (0,) * nd, pipeline_mode=pl.Buffered(1))


def _rms(x, gain):
    return x * lax.rsqrt(jnp.mean(x * x, axis=-1, keepdims=True) + RMS_EPS) * gain


def _sigmoid(x):
    return 1.0 / (1.0 + jnp.exp(-x))


def _dot(a, b):
    return jnp.dot(a, b, preferred_element_type=F32)


def _dot_nt(a, b):
    return lax.dot_general(a, b, (((1,), (1,)), ((), ())), preferred_element_type=F32)


def _slot(h):
    return slice(h * LANES, (h + 1) * LANES)


def _inproj_kernel(x_ref, g_ref, w_ref, tab_ref, fb_ref, qn_ref, kvn_ref, wq_ref, wkv_ref,
                   aq_ref, cmp_ref, slc_ref, slcv_ref, win_ref, bq_ref, bkv_ref, cq_ref, ck_ref, cv_ref,
                   dq_ref, dk_ref, dv_ref, small_ref, carry_ref):
    i = pl.program_id(1)
    tm = x_ref.shape[1]
    h = _rms(x_ref[0], g_ref[...]).astype(BF16)

    n_slots = w_ref.shape[1] // LANES
    wide = [_dot(h, w_ref[:, c * LANES:min(c + CHUNK_SLOTS, n_slots) * LANES])
            for c in range(0, n_slots, CHUNK_SLOTS)]

    def proj(g0, n):
        pieces = []
        g = g0
        while g < g0 + n:
            c, off = divmod(g, CHUNK_SLOTS)
            take = min(g0 + n - g, CHUNK_SLOTS - off)
            pieces.append(wide[c][:, off * LANES:(off + take) * LANES])
            g += take
        return pieces[0] if len(pieces) == 1 else jnp.concatenate(pieces, axis=1)

    def key_half(n):
        return (lax.broadcasted_iota(jnp.int32, (tm, n * LANES), 1) & (LANES - 1)) < HEAD_DIM

    def padded_heads(pairs):
        slots = []
        for n in range(pairs.shape[1] // LANES):
            pair = pairs[:, _slot(n)]
            slots += [jnp.where(key_half(1), pair, 0.0),
                      jnp.where(key_half(1), pltpu.roll(pair, HEAD_DIM, 1), 0.0)]
        return jnp.concatenate(slots, axis=1)

    aq_ref[0] = padded_heads(proj(G_AQ, A_HEADS // 2)).astype(BF16)
    cmp_ref[0] = proj(G_CMP, 1)
    slc = proj(G_SLC, 1)
    slc_ref[0] = slc.astype(BF16)
    slcv_ref[0] = jnp.where(key_half(1), 1.0, slc).astype(BF16)
    win_ref[0] = proj(G_WIN, 1).astype(BF16)
    b_all = proj(G_BQ, len(B_GROUPS) + B_HEADS)
    for n in range(len(B_GROUPS)):
        bq_ref[0, n] = b_all[:, _slot(n)]
    for n in range(B_HEADS):
        bkv_ref[0, n] = b_all[:, _slot(len(B_GROUPS) + n)]

    s1 = proj(G_S1, 1)
    s2 = proj(G_S2, 1)

    @pl.when(i == 0)
    def _():
        carry_ref[...] = jnp.zeros_like(carry_ref)

    def split3(v):
        hi = v.astype(BF16)
        r1 = v - hi.astype(F32)
        mid = r1.astype(BF16)
        return hi, mid, (r1 - mid.astype(F32)).astype(BF16)

    z = s1 + fb_ref[...]
    logf = jnp.minimum(z, 0.0) - jnp.log(1.0 + jnp.exp(-jnp.abs(z)))
    blk = min(LANES, tm)
    row = lax.broadcasted_iota(jnp.int32, (blk, blk), 0)
    col = lax.broadcasted_iota(jnp.int32, (blk, blk), 1)
    tri = (row >= col).astype(BF16)
    parts = split3(logf)
    running = carry_ref[0:1, :]
    blocks = []
    for b0 in range(0, tm, blk):
        blocks.append(sum(_dot(tri, part[b0:b0 + blk]) for part in parts) + running)
        running = blocks[-1][blk - 1:blk, :]
    cum = jnp.concatenate(blocks, axis=0)
    carry_ref[0:1, :] = running
    small_ref[0] = s1

    hi, mid, lo = split3(-LOG2E * cum)
    r = lax.broadcasted_iota(jnp.int32, (1, LANES), 1) - LANE_FORGET
    in_terms = (r >= 0) & (r < N_TERMS * C_HEADS)

    def is_term(n):
        hit = r == n
        for hd in range(1, C_HEADS):
            hit = hit | (r == N_TERMS * hd + n)
        return hit

    terms = jnp.where(is_term(0), hi.astype(F32), jnp.where(is_term(1), mid.astype(F32), lo.astype(F32)))
    lane4 = lax.broadcasted_iota(jnp.int32, (1, C_HEADS * LANES), 1)
    own = (lane4 & (LANES - 1)) - LANE_FORGET - N_TERMS * lax.shift_right_logical(lane4, 7)
    own_terms = (own >= 0) & (own < N_TERMS)
    cq_ref[0] = jnp.where(own_terms, 1.0, padded_heads(proj(G_CQ, C_HEADS // 2))).astype(BF16)
    ckv = proj(G_CKV, C_HEADS)
    ck_ref[0] = jnp.where(jnp.concatenate([in_terms] * C_HEADS, axis=1),
                          jnp.concatenate([terms] * C_HEADS, axis=1), ckv).astype(BF16)
    cv_ref[0] = jnp.where(key_half(C_HEADS), 1.0, ckv).astype(BF16)

    tab = tab_ref[...]
    c_q, s_q, c_k, s_k = (tab[:, _slot(n)] for n in range(4))
    qn = _rms(proj(G_DQL, 2), qn_ref[...]).astype(BF16)
    qab = _dot(qn, wq_ref[...])
    kvn = _rms(proj(G_DKVL, 1), kvn_ref[...]).astype(BF16)
    kv_up = _dot(kvn, wkv_ref[...])
    k_rot = s1 * c_k + s2 * s_k
    low1 = key_half(1)
    for hd in range(D_HEADS):
        q_rot = qab[:, _slot(hd)] * c_q + qab[:, _slot(D_HEADS + hd)] * s_q
        dq_ref[0, :, _slot(hd)] = q_rot.astype(BF16)
        dk_ref[0, :, _slot(hd)] = jnp.where(low1, kv_up[:, _slot(hd)], k_rot).astype(BF16)
    dv_ref[0] = jnp.where(key_half(D_HEADS), 1.0, kv_up).astype(BF16)


def _inproj(x, gain, w_ext, tab, fbias, qn_g, kvn_g, wq, wkv):
    b, s, d = x.shape
    tm = min(TM_PROJ, s)

    def out(kind, n, dtype):
        if kind == "rows":
            return (pl.BlockSpec((1, tm, n * LANES), lambda bi, i: (bi, i, 0)),
                    jax.ShapeDtypeStruct((b, s, n * LANES), dtype))
        return (pl.BlockSpec((1, n, tm, LANES), lambda bi, i: (bi, 0, i, 0)),
                jax.ShapeDtypeStruct((b, n, s, LANES), dtype))

    outs = [out("rows", A_HEADS, BF16), out("rows", 1, F32), out("rows", 1, BF16), out("rows", 1, BF16),
            out("rows", 1, BF16), out("slots", len(B_GROUPS), F32), out("slots", B_HEADS, F32),
            out("rows", C_HEADS, BF16), out("rows", C_HEADS, BF16), out("rows", C_HEADS, BF16),
            out("rows", D_HEADS, BF16), out("rows", D_HEADS, BF16), out("rows", D_HEADS, BF16),
            out("rows", 1, F32)]
    consts = (gain, w_ext)
    consts2 = (fbias, qn_g, kvn_g, wq, wkv)
    return pl.pallas_call(
        _inproj_kernel,
        grid=(b, s // tm),
        in_specs=[pl.BlockSpec((1, tm, d), lambda bi, i: (bi, i, 0))]
        + [_const_spec(c.shape) for c in consts]
        + [pl.BlockSpec((tm, 4 * LANES), lambda bi, i: (i, 0))]
        + [_const_spec(c.shape) for c in consts2],
        out_specs=[o[0] for o in outs],
        out_shape=[o[1] for o in outs],
        scratch_shapes=[pltpu.VMEM((8, LANES), F32)],
        name="inproj",
        compiler_params=_cparams(2),
    )(x, *consts, tab, *consts2)


def _gelu_tanh(x):
    return 0.5 * x * (1.0 + jnp.tanh(math.sqrt(2.0 / math.pi) * (x + 0.044715 * (x * x * x))))


def _compress_kernel(kc_ref, vc_ref, pos_ref, w1k_ref, w1v_ref, w2k_ref, w2v_ref, out_ref):
    nc = kc_ref.shape[1]
    half = A_CMP_STRIDE * HEAD_DIM
    pos_a = pos_ref[0:1, :]
    pos_b = pos_ref[1:2, :]

    def phi(chunks, w1_ref, w2_ref):
        u = _dot((chunks + pos_a).astype(BF16), w1_ref[0:half, :])
        v = _dot((chunks + pos_b).astype(BF16), w1_ref[half:2 * half, :])
        hid = u + pltpu.roll(v, nc - 1, 0)
        return _dot(_gelu_tanh(hid).astype(BF16), w2_ref[...])

    out_ref[0] = (phi(kc_ref[0], w1k_ref, w2k_ref) + phi(vc_ref[0], w1v_ref, w2v_ref)).astype(BF16)


def _compress(kc_in, vc_in, pos2, w1k, w1v, w2k_pad, w2v_pad):
    b, nc, width = kc_in.shape
    blk = pl.BlockSpec((1, nc, width), lambda bi: (bi, 0, 0))
    return pl.pallas_call(
        _compress_kernel,
        grid=(b,),
        in_specs=[blk, blk, _const_spec(pos2.shape), _const_spec(w1k.shape), _const_spec(w1v.shape),
                  _const_spec(w2k_pad.shape), _const_spec(w2v_pad.shape)],
        out_specs=pl.BlockSpec((1, nc, LANES), lambda bi: (bi, 0, 0)),
        out_shape=jax.ShapeDtypeStruct((b, nc, LANES), BF16),
        name="compress",
        compiler_params=_cparams(1),
    )(kc_in, vc_in, pos2, w1k, w1v, w2k_pad, w2v_pad)


def _nsa_cmp_win_kernel(q_ref, kvc_ref, kvw_ref, small_ref, ovl_ref, bias_ref,
                        part_ref, sel_ref, *, n_top):
    tq = min(TQ_NSA, q_ref.shape[1])
    for u in range(q_ref.shape[1] // tq):
        _nsa_cmp_win_tile(u * tq, tq, q_ref, kvc_ref, kvw_ref, small_ref, ovl_ref, bias_ref,
                          part_ref, sel_ref, n_top)


def _nsa_cmp_win_tile(r0, tq, q_ref, kvc_ref, kvw_ref, small_ref, ovl_ref, bias_ref, part_ref, sel_ref, n_top):
    nc = kvc_ref.shape[1]
    n_sel = ovl_ref.shape[0]
    t0 = pl.multiple_of(pl.program_id(1) * q_ref.shape[1] + r0, tq)
    tile = slice(r0, r0 + tq)
    q4 = jnp.concatenate([q_ref[0, tile, _slot(h)] for h in range(A_HEADS)], axis=0)
    rows = A_HEADS * tq

    kvc = kvc_ref[0]
    s = _dot_nt(q4, kvc)
    t_row = t0 + lax.broadcasted_iota(jnp.int32, (tq, nc), 0)
    c_end = lax.broadcasted_iota(jnp.int32, (tq, nc), 1) * A_CMP_STRIDE + (A_CMP_LEN - 1)
    valid = jnp.concatenate([c_end <= t_row] * A_HEADS, axis=0)
    s = jnp.where(valid, s, NEG_INF)
    m = jnp.max(s, axis=1, keepdims=True)
    p = jnp.where(valid, jnp.exp2(s - m), 0.0)
    den = jnp.maximum(jnp.sum(p, axis=1, keepdims=True), 1e-30)
    p = p / den
    o_c = _dot(p.astype(BF16), kvc)

    p_sum = p[0:tq] + p[tq:2 * tq] + p[2 * tq:3 * tq] + p[3 * tq:4 * tq]
    p_hi = p_sum.astype(BF16)
    p_lo = (p_sum - p_hi.astype(F32)).astype(BF16)
    imp_t = _dot_nt(ovl_ref[...], p_hi) + _dot_nt(ovl_ref[...], p_lo)

    j_idx = lax.broadcasted_iota(jnp.int32, (n_sel, tq), 0)
    cur = lax.shift_right_logical(t0 + lax.broadcasted_iota(jnp.int32, (n_sel, tq), 1), 6)
    forced = (j_idx == 0) | (j_idx == cur) | (j_idx == cur - 1)
    imp_t = jnp.where(forced, FORCE_SCORE, jnp.where(j_idx > cur, -FORCE_SCORE, imp_t))
    sub8 = 8
    groups = [imp_t[v * sub8:(v + 1) * sub8] for v in range(n_sel // sub8)]
    ranks = [jnp.zeros((sub8, tq), F32) for _ in groups]
    in_group = lax.broadcasted_iota(jnp.int32, (sub8, tq), 0)
    for c in range(n_sel):
        row_c = imp_t[c:c + 1, :]
        for v, grp in enumerate(groups):
            if v > c // sub8:
                beats = row_c >= grp
            elif v < c // sub8:
                beats = row_c > grp
            else:
                beats = (row_c > grp) | ((row_c == grp) & (in_group > c % sub8))
            ranks[v] = ranks[v] + jnp.where(beats, 1.0, 0.0)
    rank = jnp.concatenate(ranks, axis=0)
    sel_t = jnp.where(rank < float(n_top), 1.0, 0.0).astype(BF16)
    sel_ref[0, :, tile] = sel_t

    span = A_WINDOW + tq
    kvw = kvw_ref[0, pl.ds(t0, span), :]
    kpos = t0 - A_WINDOW + lax.broadcasted_iota(jnp.int32, (1, span), 1)
    sw = _dot_nt(q4, kvw) + bias_ref[...] + jnp.where(kpos >= 0, 0.0, NEG_INF)
    mw = jnp.max(sw, axis=1, keepdims=True)
    pw = jnp.exp2(sw - mw)
    o_w = _dot(pw.astype(BF16), kvw) / jnp.sum(pw, axis=1, keepdims=True)

    g = _sigmoid(small_ref[0, tile])
    gated = []
    for h in range(A_HEADS):
        g_c = g[:, LANE_GATE + 3 * h:LANE_GATE + 3 * h + 1]
        g_w = g[:, LANE_GATE + 3 * h + 2:LANE_GATE + 3 * h + 3]
        gated.append(g_c * o_c[h * tq:(h + 1) * tq] + g_w * o_w[h * tq:(h + 1) * tq])
    low = lax.broadcasted_iota(jnp.int32, (tq, LANES), 1) < HEAD_DIM
    for pair in range(A_HEADS // 2):
        part_ref[0, tile, _slot(pair)] = jnp.where(low, pltpu.roll(gated[2 * pair], HEAD_DIM, 1),
                                                   gated[2 * pair + 1])


def _nsa_cmp_win(aq, kvc, kvw_pad, small, ovl, bias_w):
    b, s, _ = aq.shape
    tq = min(TQ_NSA_STEP, s)
    nc = kvc.shape[1]
    n_sel = ovl.shape[0]
    n_top = min(A_SEL_TOPK, n_sel)
    return pl.pallas_call(
        functools.partial(_nsa_cmp_win_kernel, n_top=n_top),
        grid=(b, s // tq),
        in_specs=[pl.BlockSpec((1, tq, A_HEADS * LANES), lambda bi, i: (bi, i, 0)),
                  pl.BlockSpec((1, nc, LANES), lambda bi, i: (bi, 0, 0)),
                  pl.BlockSpec((1, s + A_WINDOW, LANES), lambda bi, i: (bi, 0, 0)),
                  pl.BlockSpec((1, tq, LANES), lambda bi, i: (bi, i, 0)),
                  _const_spec(ovl.shape),
                  _const_spec(bias_w.shape)],
        out_specs=[pl.BlockSpec((1, tq, A_HEADS * HEAD_DIM), lambda bi, i: (bi, i, 0)),
                   pl.BlockSpec((1, n_sel, tq), lambda bi, i: (bi, 0, i))],
        out_shape=[jax.ShapeDtypeStruct((b, s, A_HEADS * HEAD_DIM), F32),
                   jax.ShapeDtypeStruct((b, n_sel, s), BF16)],
        name="nsa_cmp_win",
        compiler_params=_cparams(2),
    )(aq, kvc, kvw_pad, small, ovl, bias_w)


def _flash_kernel(*refs, mode, nh):
    if mode == "sel":
        q_ref, k_ref, v_ref, selt_ref, e_ref, bias_ref, part_ref, small_ref, o_ref, m_ref, acc_ref = refs
    else:
        q_ref, k_ref, v_ref, o_ref, m_ref, acc_ref = refs
    i = pl.program_id(1)
    tq = q_ref.shape[1]
    tk = tq
    shared_kv = mode == "sel"

    m_ref[...] = jnp.full(m_ref.shape, NEG_INF, F32)
    acc_ref[...] = jnp.zeros(acc_ref.shape, F32)

    lag = lax.broadcasted_iota(jnp.int32, (tq, tk), 0) - lax.broadcasted_iota(jnp.int32, (tq, tk), 1)

    def step(j, diag):
        k0 = pl.multiple_of(j * tk, tk)
        keep = (lag >= 0) if diag else None
        if mode == "sel":
            sel_keys = lax.dot_general(selt_ref[0], e_ref[:, pl.ds(k0, tk)], (((0,), (0,)), ((), ())),
                                       preferred_element_type=F32) > 0.5
            keep = (sel_keys & keep) if diag else sel_keys
            n_off, bt = bias_ref.shape[0], bias_ref.shape[2]

            def bias_tile(h):
                strips = []
                for a in range(tq // bt):
                    offs = [jnp.clip(i * (tq // bt) + a - j * (tk // bt) - c, 0, n_off - 1)
                            for c in range(tk // bt)]
                    strips.append(jnp.concatenate([bias_ref[o, h] for o in offs], axis=1))
                return jnp.concatenate(strips, axis=0)
        logits = []
        for h in range(nh):
            s = _dot_nt(q_ref[0, :, _slot(h)], k_ref[0, pl.ds(k0, tk), _slot(0 if shared_kv else h)])
            if mode == "sel":
                s = s + bias_tile(h)
            if keep is not None:
                s = jnp.where(keep, s, NEG_INF)
            logits.append(s)
        probs = []
        for h in range(nh):
            m_prev = m_ref[h]
            m_new = jnp.maximum(m_prev, jnp.max(logits[h], axis=1, keepdims=True))
            p = jnp.exp2(logits[h] - jnp.concatenate([m_new] * (tk // LANES), axis=1))
            alpha = jnp.exp2(m_prev - m_new)
            m_ref[h] = m_new
            probs.append((p.astype(BF16), alpha))
        for h in range(nh):
            p, alpha = probs[h]
            acc_ref[h] = alpha * acc_ref[h] + _dot(p, v_ref[0, pl.ds(k0, tk), _slot(0 if shared_kv else h)])

    def full_tile(j, carry):
        step(j, False)
        return carry

    lax.fori_loop(0, i, full_tile, 0)
    step(i, True)

    low = lax.broadcasted_iota(jnp.int32, (tq, LANES), 1) < HEAD_DIM
    if mode == "sel":
        g = _sigmoid(small_ref[0])
    for pair in range(nh // 2):
        outs = []
        for h in (2 * pair, 2 * pair + 1):
            acc = acc_ref[h]
            outs.append(acc / acc[:, 0:1])
        o_pair = jnp.where(low, pltpu.roll(outs[0], HEAD_DIM, 1), outs[1])
        if mode == "sel":
            gate = [g[:, LANE_GATE + 3 * h + 1:LANE_GATE + 3 * h + 2] for h in (2 * pair, 2 * pair + 1)]
            o_pair = part_ref[0, :, _slot(pair)] + jnp.where(low, gate[0], gate[1]) * o_pair
        o_ref[0, :, _slot(pair)] = o_pair.astype(BF16)


def _flash(mode, q, k, v, sel_t=None, expand=None, bias=None, part=None, small=None):
    b, s, qw = q.shape
    nh = qw // LANES
    tq = min(TQ_FLASH, s)
    ow = nh * HEAD_DIM

    def rows(width):
        return pl.BlockSpec((1, tq, width), lambda bi, i: (bi, i, 0))

    def seq_spec(arr):
        return pl.BlockSpec((1,) + arr.shape[1:], lambda bi, i: (bi, 0, 0))

    args = (q, k, v)
    in_specs = [rows(qw), seq_spec(k), seq_spec(v)]
    if mode == "sel":
        args += (sel_t, expand, bias, part, small)
        in_specs += [pl.BlockSpec((1, sel_t.shape[1], tq), lambda bi, i: (bi, 0, i)),
                     _const_spec(expand.shape), _const_spec(bias.shape), rows(ow), rows(LANES)]
    return pl.pallas_call(
        functools.partial(_flash_kernel, mode=mode, nh=nh),
        grid=(b, s // tq),
        in_specs=in_specs,
        out_specs=rows(ow),
        out_shape=jax.ShapeDtypeStruct((b, s, ow), BF16),
        scratch_shapes=[pltpu.VMEM((nh, tq, LANES), F32)] * 2,
        name="flash_" + mode,
        compiler_params=_cparams(2),
    )(*args)


def _dilated_kernel(q_ref, kvp_ref, kvc_ref, bias_ref, o_ref, o_sc, l_sc):
    i = pl.program_id(1)
    rows = q_ref.shape[2]
    tb = TQ_DIL
    low = lax.broadcasted_iota(jnp.int32, (tb, LANES), 1) < HEAD_DIM
    col = lax.broadcasted_iota(jnp.int32, (tb, 2 * tb), 1)
    in_seq = (col >= tb) | (i > 0)
    for g, (_, dil) in enumerate(B_GROUPS):
        span = dil * tb
        for r in range(dil):
            for c in range(rows // span):
                cur = pl.ds(r + c * span, tb, stride=dil)
                if c > 0:
                    prev_ref, prev = kvc_ref, pl.ds(r + (c - 1) * span, tb, stride=dil)
                else:
                    prev_ref, prev = kvp_ref, pl.ds(rows - span + r, tb, stride=dil)
                q2 = q_ref[0, g, cur, :]
                outs, lses = [], []
                for h in range(B_HPG):
                    q_h = jnp.where(low if h == 0 else ~low, q2, 0.0).astype(BF16)
                    head = B_HPG * g + h
                    kv = jnp.concatenate([prev_ref[0, head, prev, :], kvc_ref[0, head, cur, :]],
                                         axis=0).astype(BF16)
                    s = _dot_nt(q_h, kv) + bias_ref[g, h]
                    if c == 0:
                        s = jnp.where(in_seq, s, NEG_INF)
                    m = jnp.max(s, axis=1, keepdims=True)
                    p = jnp.exp2(s - m)
                    den = jnp.sum(p, axis=1, keepdims=True)
                    outs.append(_dot(p.astype(BF16), kv) / den)
                    lses.append(m + jnp.log2(den))
                o_sc[g, cur, :] = jnp.where(low, outs[1], outs[0])
                l_sc[g, cur, :] = jnp.where(low, lses[1], lses[0])
    step = 2 * tb
    for t in range(rows // step):
        rs = slice(t * step, (t + 1) * step)
        l0, l1, l2 = l_sc[0, rs, :], l_sc[1, rs, :], l_sc[2, rs, :]
        mx = jnp.maximum(jnp.maximum(l0, l1), l2)
        e0, e1, e2 = jnp.exp2(l0 - mx), jnp.exp2(l1 - mx), jnp.exp2(l2 - mx)
        mixed = (e0 * o_sc[0, rs, :] + e1 * o_sc[1, rs, :] + e2 * o_sc[2, rs, :]) / (e0 + e1 + e2)
        o_ref[0, rs, :] = mixed.astype(BF16)


def _dilated(bq, bkv, bias):
    b, _, s, _ = bq.shape
    rows = TQ_DIL * B_GROUPS[-1][1]
    n_groups = len(B_GROUPS)

    def tile(n_slots, prev):
        if prev:
            return pl.BlockSpec((1, n_slots, rows, LANES), lambda bi, i: (bi, 0, jnp.maximum(i - 1, 0), 0))
        return pl.BlockSpec((1, n_slots, rows, LANES), lambda bi, i: (bi, 0, i, 0))

    return pl.pallas_call(
        _dilated_kernel,
        grid=(b, s // rows),
        in_specs=[tile(bq.shape[1], False), tile(bkv.shape[1], True), tile(bkv.shape[1], False),
                  _const_spec(bias.shape)],
        out_specs=pl.BlockSpec((1, rows, LANES), lambda bi, i: (bi, i, 0)),
        out_shape=jax.ShapeDtypeStruct((b, s, LANES), BF16),
        scratch_shapes=[pltpu.VMEM((n_groups, rows, LANES), F32)] * 2,
        name="dilated",
        compiler_params=_cparams(2),
    )(bq, bkv, bkv, bias)


def _merge_kernel(x_ref, oa_ref, ob_ref, oc_ref, od_ref,
                  gpre_ref, gpost_ref, wg_ref, wba_ref, wbb_ref, wbc_ref, wbd_ref, wo_ref, out_ref):
    x = x_ref[0]
    h = _rms(x, gpre_ref[...]).astype(BF16)
    branches = ((oa_ref[0], wba_ref), (ob_ref[0], wbb_ref), (oc_ref[0], wbc_ref), (od_ref[0], wbd_ref))
    merged = None
    for n, (o_n, wb_ref) in enumerate(branches):
        term = _sigmoid(_dot(h, wg_ref[n])) * _dot(o_n, wb_ref[...])
        merged = term if merged is None else merged + term
    z = _dot(merged.astype(BF16), wo_ref[...])
    out_ref[0] = x + _rms(z, gpost_ref[...])


def _merge(x, oa, ob, oc, od, gpre, gpost, wg, wba, wbb, wbc, wbd, wo):
    b, s, d = x.shape
    tm = min(TM_PROJ, s)

    def rows(arr):
        return pl.BlockSpec((1, tm, arr.shape[2]), lambda bi, i: (bi, i, 0))

    acts = (x, oa, ob, oc, od)
    consts = (gpre, gpost, wg, wba, wbb, wbc, wbd, wo)
    return pl.pallas_call(
        _merge_kernel,
        grid=(b, s // tm),
        in_specs=[rows(a) for a in acts] + [_const_spec(c.shape) for c in consts],
        out_specs=rows(x),
        out_shape=jax.ShapeDtypeStruct(x.shape, F32),
        name="merge",
        compiler_params=_cparams(2),
    )(*acts, *consts)


def _ffn_kernel(x_ref, gpre_ref, gpost_ref, wup_ref, cw_ref, cb_ref, wdn_ref, out_ref, tail_ref, *, cuts):
    i = pl.program_id(1)
    tm = x_ref.shape[1]
    d_ff = wdn_ref.shape[0]
    x = x_ref[0]
    h = _rms(x, gpre_ref[...]).astype(BF16)

    @pl.when(i == 0)
    def _():
        tail_ref[...] = jnp.zeros_like(tail_ref)

    def conv(c0, chunk):
        u = _dot(h, wup_ref[:, c0:c0 + chunk])
        ue = jnp.concatenate([tail_ref[:, c0:c0 + chunk], u], axis=0)
        tail_ref[:, c0:c0 + chunk] = u[tm - CONV_HALO:]
        w = cw_ref[:, c0:c0 + chunk]
        out = cb_ref[:, c0:c0 + chunk] + w[0:1] * pltpu.roll(ue, 2, 0)[CONV_HALO:]
        out = out + w[1:2] * pltpu.roll(ue, 1, 0)[CONV_HALO:]
        return out + w[2:3] * u

    y = jnp.zeros((tm, x.shape[1]), F32)
    for c0, c1 in zip(cuts[:-1], cuts[1:]):
        gate = conv(c0, c1 - c0)
        val = conv(d_ff + c0, c1 - c0)
        act = gate * _sigmoid(gate) * val
        y = y + _dot(act.astype(BF16), wdn_ref[c0:c1, :])
    out_ref[0] = x + _rms(y, gpost_ref[...])


def _ffn(x, gpre, gpost, w_up, conv_w, conv_b, w_down):
    b, s, d = x.shape
    tm = min(TM_PROJ, s)
    d_ff = w_down.shape[0]
    n_tiles = d_ff // MXU_WIDTH
    cuts = (0, (n_tiles // 2) * MXU_WIDTH, d_ff) if d_ff % MXU_WIDTH == 0 and n_tiles > 1 else (0, d_ff)
    return pl.pallas_call(
        functools.partial(_ffn_kernel, cuts=cuts),
        grid=(b, s // tm),
        in_specs=[pl.BlockSpec((1, tm, d), lambda bi, i: (bi, i, 0)),
                  _const_spec(gpre.shape), _const_spec(gpost.shape), _const_spec(w_up.shape),
                  _const_spec(conv_w.shape), _const_spec(conv_b.shape), _const_spec(w_down.shape)],
        out_specs=pl.BlockSpec((1, tm, d), lambda bi, i: (bi, i, 0)),
        out_shape=jax.ShapeDtypeStruct(x.shape, F32),
        scratch_shapes=[pltpu.VMEM((CONV_HALO, w_up.shape[1]), F32)],
        name="ffn",
        compiler_params=_cparams(2),
    )(x, gpre, gpost, w_up, conv_w, conv_b, w_down)


def _rel_bucket(dist):
    dist = jnp.maximum(dist, 0)
    d = jnp.maximum(dist, 1).astype(F32)
    large = BUCKET_EXACT + (jnp.log(d / BUCKET_EXACT) / math.log(BUCKET_MAX_DIST / BUCKET_EXACT)
                            * (N_BUCKETS - BUCKET_EXACT)).astype(jnp.int32)
    large = jnp.minimum(large, N_BUCKETS - 1)
    return jnp.where(dist < BUCKET_EXACT, dist, large)


def _in_proj_columns():
    src = np.full((N_SLOTS * LANES,), -1, np.int64)
    scl = np.ones((N_SLOTS * LANES,), np.float32)
    o_aq = 0
    o_akv = o_aq + A_HEADS * HEAD_DIM
    o_ag = o_akv + 6 * HEAD_DIM
    o_b = o_ag + 3 * A_HEADS
    o_c = o_b + 3 * B_HEADS * HEAD_DIM
    o_cf = o_c + 3 * C_HEADS * HEAD_DIM
    o_dq = o_cf + C_HEADS
    o_dkv = o_dq + D_Q_LORA
    o_dkr = o_dkv + D_KV_LORA
    e = np.arange(HEAD_DIM)
    q_scale = HEAD_DIM ** -0.5 * LOG2E

    def put(slot, lane0, cols, scale=1.0):
        dst = slot * LANES + lane0 + np.arange(len(cols))
        src[dst] = cols
        scl[dst] = scale

    for h in range(A_HEADS):
        put(G_AQ + h // 2, (h % 2) * HEAD_DIM, o_aq + h * HEAD_DIM + e, q_scale)
    for n, slot in enumerate((G_CMP, G_SLC, G_WIN)):
        put(slot, 0, o_akv + (2 * n) * HEAD_DIM + e)
        put(slot, HEAD_DIM, o_akv + (2 * n + 1) * HEAD_DIM + e)
    for h in range(B_HEADS):
        odd = h % B_HPG
        put(G_BQ + h // B_HPG, odd * HEAD_DIM, o_b + h * HEAD_DIM + e, q_scale)
        put(G_BKV + h, odd * HEAD_DIM, o_b + (B_HEADS + h) * HEAD_DIM + e)
        put(G_BKV + h, (1 - odd) * HEAD_DIM, o_b + (2 * B_HEADS + h) * HEAD_DIM + e)
    for h in range(C_HEADS):
        put(G_CQ + h // 2, (h % 2) * HEAD_DIM, o_c + h * HEAD_DIM + e, q_scale)
        put(G_CKV + h, 0, o_c + (C_HEADS + h) * HEAD_DIM + e)
        put(G_CKV + h, HEAD_DIM, o_c + (2 * C_HEADS + h) * HEAD_DIM + e)
    put(G_DQL, 0, o_dq + np.arange(D_Q_LORA))
    put(G_DKVL, 0, o_dkv + np.arange(D_KV_LORA))
    put(G_S1, LANE_GATE, o_ag + np.arange(3 * A_HEADS))
    put(G_S1, LANE_FORGET, o_cf + np.repeat(np.arange(C_HEADS), N_TERMS))
    put(G_S1, LANE_ROPE, o_dkr + np.arange(D_ROPE))
    half = D_ROPE // 2
    put(G_S2, LANE_ROPE, o_dkr + half + np.arange(half), -1.0)
    put(G_S2, LANE_ROPE + half, o_dkr + np.arange(half))
    return src, scl


def _gather_cols(w, src, scl):
    pieces = []
    n = len(src)
    a = 0
    while a < n:
        e = a + 1
        if src[a] < 0:
            while e < n and src[e] < 0:
                e += 1
            pieces.append(jnp.zeros((w.shape[0], e - a), w.dtype))
        else:
            while e < n and src[e] == src[e - 1] + 1 and scl[e] == scl[a]:
                e += 1
            run = w[:, int(src[a]):int(src[a]) + (e - a)]
            pieces.append(run if scl[a] == 1.0 else run * float(scl[a]))
        a = e
    return jnp.concatenate(pieces, axis=1)


def _mla_weight_columns():
    per_q = D_NOPE + D_ROPE
    half = D_ROPE // 2
    qa = np.full((D_HEADS * LANES,), -1, np.int64)
    qb = np.full((D_HEADS * LANES,), -1, np.int64)
    qb_s = np.ones((D_HEADS * LANES,), np.float32)
    for h in range(D_HEADS):
        qa[h * LANES + np.arange(per_q)] = h * per_q + np.arange(per_q)
        rot = h * per_q + D_NOPE
        qb[h * LANES + D_NOPE + np.arange(half)] = rot + half + np.arange(half)
        qb_s[h * LANES + D_NOPE + np.arange(half)] = -1.0
        qb[h * LANES + D_NOPE + half + np.arange(half)] = rot + np.arange(half)
    ones = np.ones_like(qb_s)
    return (qa, ones), (qb, qb_s)


def _rope_tables(s):
    inv_freq = ROPE_THETA ** (-jnp.arange(0, D_ROPE, 2, dtype=F32) / D_ROPE)
    ang = jnp.arange(s, dtype=F32)[:, None] * inv_freq[None, :]
    cos2 = jnp.concatenate([jnp.cos(ang)] * 2, axis=1)
    sin2 = jnp.concatenate([jnp.sin(ang)] * 2, axis=1)
    scale = (D_NOPE + D_ROPE) ** -0.5 * LOG2E
    z_lo = jnp.zeros((s, D_NOPE), F32)
    z_hi = jnp.zeros((s, LANES - D_NOPE - D_ROPE), F32)
    c_q = jnp.concatenate([jnp.full((s, D_NOPE), scale, F32), scale * cos2, z_hi], axis=1)
    s_q = jnp.concatenate([z_lo, scale * sin2, z_hi], axis=1)
    c_k = jnp.concatenate([z_lo, cos2, z_hi], axis=1)
    s_k = jnp.concatenate([z_lo, sin2, z_hi], axis=1)
    return jnp.concatenate([c_q, s_q, c_k, s_k], axis=1)


def _toeplitz(vec, n_rows, n_cols, off, step=1):
    base = vec[:, ::step]
    last = base.shape[1] - 1
    lo, hi = off - (n_cols - 1), off + n_rows - 1
    core = base[:, max(lo, 0):min(hi, last) + 1]
    front = jnp.repeat(base[:, :1], max(0, -lo), axis=1)
    back = jnp.repeat(base[:, last:], max(0, hi - last), axis=1)
    g = jnp.concatenate([front, core, back], axis=1)
    lg = n_rows + n_cols - 1
    t = jnp.tile(g, (1, n_rows + 1))[:, :n_rows * (lg + 1)].reshape(-1, n_rows, lg + 1)
    return t[:, :, :n_cols][:, :, ::-1]


def _bias_tables(rel_table, s):
    bucket = _rel_bucket(jnp.arange(s))
    hit = bucket[None, :, None] == jnp.arange(N_BUCKETS)[None, None, :]
    by_dist = jnp.sum(jnp.where(hit, LOG2E * rel_table.T[:, None, :], 0.0), axis=-1)
    tqa = min(TQ_NSA, s)
    dist = jnp.arange(tqa)[:, None] + A_WINDOW - jnp.arange(A_WINDOW + tqa)[None, :]
    ok = (dist >= 0) & (dist < A_WINDOW)
    bias_w = jnp.where(ok[None], _toeplitz(by_dist[:A_HEADS], tqa, A_WINDOW + tqa, A_WINDOW), NEG_INF)
    bias_w = bias_w.reshape(A_HEADS * tqa, A_WINDOW + tqa)
    tf = min(BIAS_TILE, s)
    n_off = min(s // tf, -(-(BUCKET_MAX_DIST + tf - 1) // tf) + 1)
    bias_s = jnp.stack([_toeplitz(by_dist[:A_HEADS], tf, tf, n * tf) for n in range(n_off)])
    m = jnp.arange(TQ_DIL)[:, None] + TQ_DIL - jnp.arange(2 * TQ_DIL)[None, :]
    bias_d = []
    for g, (window, dil) in enumerate(B_GROUPS):
        ok = (m >= 0) & (m <= window // dil)
        heads = by_dist[A_HEADS + g * B_HPG:A_HEADS + (g + 1) * B_HPG]
        bias_d.append(jnp.where(ok[None], _toeplitz(heads, TQ_DIL, 2 * TQ_DIL, TQ_DIL, dil), NEG_INF))
    return bias_w, bias_s, jnp.stack(bias_d)


def _selection_constants(s):
    nc = s // A_CMP_STRIDE
    n_cmp = (s - A_CMP_LEN) // A_CMP_STRIDE + 1
    n_sel = s // A_SEL_BLOCK
    c = np.arange(nc)[:, None]
    j = np.arange(n_sel)[None, :]
    c_start = c * A_CMP_STRIDE
    overlap = ((c_start <= j * A_SEL_BLOCK + A_SEL_BLOCK - 1) & (c_start + A_CMP_LEN - 1 >= j * A_SEL_BLOCK)
               & (c < n_cmp))
    expand = (np.arange(s)[None, :] // A_SEL_BLOCK) == np.arange(n_sel)[:, None]
    return jnp.asarray(overlap.T, BF16), jnp.asarray(expand, BF16)


def kernel(x, rel_bias_table, norm_attn_pre, norm_attn_post, norm_ffn_pre, norm_ffn_post, w_in, nsa_cmp_pos, nsa_phi_k_w1, nsa_phi_k_w2, nsa_phi_v_w1, nsa_phi_v_w2, fox_forget_bias, mla_q_norm, mla_kv_norm, mla_w_uq, mla_w_ukv, w_branch_a, w_branch_b, w_branch_c, w_branch_d, w_merge_gate, w_o, ffn_w_up, ffn_conv_w, ffn_conv_b, ffn_w_down):
    b, s, d = x.shape
    depth = w_in.shape[0]
    assert s % (TQ_DIL * B_GROUPS[-1][1]) == 0, "every dilation class needs whole 128-row tiles"

    in_src, in_scl = _in_proj_columns()
    assert D_NOPE + D_VDIM == LANES, "an MLA head's [k_nope | v] up-projection fills one slot"
    (qa_i, qa_s), (qb_i, qb_s) = _mla_weight_columns()
    rope_tab = _rope_tables(s)
    bias_w, bias_s, bias_d = _bias_tables(rel_bias_table, s)
    overlap, expand = _selection_constants(s)
    half = A_CMP_STRIDE * HEAD_DIM

    for l in range(depth):
        w_ext = _gather_cols(w_in[l], in_src, in_scl).astype(BF16)
        fbias = jnp.zeros((1, LANES), F32).at[0, LANE_FORGET:LANE_FORGET + N_TERMS * C_HEADS].set(
            jnp.repeat(fox_forget_bias[l], N_TERMS))
        wq = jnp.concatenate([_gather_cols(mla_w_uq[l], qa_i, qa_s), _gather_cols(mla_w_uq[l], qb_i, qb_s)],
                             axis=1).astype(BF16)
        (aq, cmp, slc, slc_v, win, bq, bkv, cq, ck, c_v, dq, dk, d_v, small) = _inproj(
            x, norm_attn_pre[l][None], w_ext, rope_tab, fbias,
            mla_q_norm[l][None], mla_kv_norm[l][None], wq, mla_w_ukv[l].astype(BF16))

        nc = s // A_CMP_STRIDE
        w2k_pad = jnp.pad(nsa_phi_k_w2[l], ((0, 0), (0, LANES - HEAD_DIM))).astype(BF16)
        w2v_pad = jnp.pad(nsa_phi_v_w2[l], ((0, 0), (LANES - HEAD_DIM, 0))).astype(BF16)
        kvc = _compress(cmp[..., :HEAD_DIM].reshape(b, nc, half), cmp[..., HEAD_DIM:].reshape(b, nc, half),
                        nsa_cmp_pos[l].reshape(2, half),
                        nsa_phi_k_w1[l].astype(BF16), nsa_phi_v_w1[l].astype(BF16), w2k_pad, w2v_pad)
        kvw_pad = jnp.pad(win, ((0, 0), (A_WINDOW, 0), (0, 0)))
        part, sel_t = _nsa_cmp_win(aq, kvc, kvw_pad, small, overlap, bias_w)
        o_a = _flash("sel", aq, slc, slc_v, sel_t=sel_t, expand=expand, bias=bias_s, part=part, small=small)

        o_b = _dilated(bq, bkv, bias_d)

        o_c = _flash("fox", cq, ck, c_v)
        o_d = _flash("mla", dq, dk, d_v)

        x = _merge(x, o_a, o_b, o_c, o_d,
                   norm_attn_pre[l][None], norm_attn_post[l][None],
                   w_merge_gate[l].astype(BF16),
                   w_branch_a[l].astype(BF16),
                   jnp.concatenate([w_branch_b[l][HEAD_DIM:], w_branch_b[l][:HEAD_DIM]]).astype(BF16),
                   w_branch_c[l].astype(BF16),
                   w_branch_d[l].astype(BF16),
                   w_o[l].astype(BF16))
        x = _ffn(x, norm_ffn_pre[l][None], norm_ffn_post[l][None], ffn_w_up[l].astype(BF16),
                 ffn_conv_w[l], ffn_conv_b[l][None], ffn_w_down[l].astype(BF16))
    return x
```

```python
import functools
import math

import numpy as np
import jax
import jax.numpy as jnp
from jax import lax
from jax.experimental import pallas as pl
from jax.experimental.pallas import tpu as pltpu

F32 = jnp.float32
BF16 = jnp.bfloat16

LANES = 128
MXU_WIDTH = 256
HEAD_DIM = 64
RMS_EPS = 1e-6
NEG_INF = -1e30
FORCE_SCORE = 1e9
LOG2E = math.log2(math.e)
VMEM_LIMIT_BYTES = 56 * 1024 * 1024

N_BUCKETS = 32
BUCKET_EXACT = 16
BUCKET_MAX_DIST = 2048

A_HEADS = 4
A_CMP_LEN = 32
A_CMP_STRIDE = 16
A_PHI_HIDDEN = 128
A_SEL_BLOCK = 64
A_SEL_TOPK = 16
A_WINDOW = 512
B_GROUPS = ((128, 1), (512, 4), (2048, 16))
B_HPG = 2
B_HEADS = B_HPG * len(B_GROUPS)
C_HEADS = 4
D_HEADS = 4
D_Q_LORA = 256
D_KV_LORA = 128
D_NOPE = 64
D_ROPE = 32
D_VDIM = 64
ROPE_THETA = 10000.0
N_BRANCHES = 4

G_AQ = 0
G_CMP = G_AQ + A_HEADS // 2
G_SLC = G_CMP + 1
G_WIN = G_SLC + 1
G_BQ = G_WIN + 1
G_BKV = G_BQ + len(B_GROUPS)
G_CQ = G_BKV + B_HEADS
G_CKV = G_CQ + C_HEADS // 2
G_DQL = G_CKV + C_HEADS
G_DKVL = G_DQL + 2
G_S1 = G_DKVL + 1
G_S2 = G_S1 + 1
N_SLOTS = G_S2 + 1
LANE_GATE = 0
LANE_FORGET = 96
N_TERMS = 3
LANE_ROPE = 64

TM_PROJ = 512
TQ_NSA_STEP = 512
TQ_FLASH = 512
CHUNK_SLOTS = 8
BIAS_TILE = 256
TQ_NSA = 128
TQ_DIL = 128
CONV_HALO = 8


def _cparams(n_grid):
    return pltpu.CompilerParams(dimension_semantics=("arbitrary",) * n_grid,
                                vmem_limit_bytes=VMEM_LIMIT_BYTES)


def _const_spec(shape):
    nd = len(shape)
    return pl.BlockSpec(shape, lambda *_: (0,) * nd, pipeline_mode=pl.Buffered(1))


def _rms(x, gain):
    return x * lax.rsqrt(jnp.mean(x * x, axis=-1, keepdims=True) + RMS_EPS) * gain


def _sigmoid(x):
    return 1.0 / (1.0 + jnp.exp(-x))


def _dot(a, b):
    return jnp.dot(a, b, preferred_element_type=F32)


def _dot_nt(a, b):
    return lax.dot_general(a, b, (((1,), (1,)), ((), ())), preferred_element_type=F32)


def _slot(h):
    return slice(h * LANES, (h + 1) * LANES)


def _inproj_kernel(x_ref, g_ref, w_ref, tab_ref, fb_ref, qn_ref, kvn_ref, wq_ref, wkv_ref,
                   aq_ref, cmp_ref, slc_ref, slcv_ref, win_ref, bq_ref, bkv_ref, cq_ref, ck_ref, cv_ref,
                   dq_ref, dk_ref, dv_ref, small_ref, carry_ref):
    i = pl.program_id(1)
    tm = x_ref.shape[1]
    h = _rms(x_ref[0], g_ref[...]).astype(BF16)

    n_slots = w_ref.shape[1] // LANES
    wide = [_dot(h, w_ref[:, c * LANES:min(c + CHUNK_SLOTS, n_slots) * LANES])
            for c in range(0, n_slots, CHUNK_SLOTS)]

    def proj(g0, n):
        pieces = []
        g = g0
        while g < g0 + n:
            c, off = divmod(g, CHUNK_SLOTS)
            take = min(g0 + n - g, CHUNK_SLOTS - off)
            pieces.append(wide[c][:, off * LANES:(off + take) * LANES])
            g += take
        return pieces[0] if len(pieces) == 1 else jnp.concatenate(pieces, axis=1)

    def key_half(n):
        return (lax.broadcasted_iota(jnp.int32, (tm, n * LANES), 1) & (LANES - 1)) < HEAD_DIM

    def padded_heads(pairs):
        slots = []
        for n in range(pairs.shape[1] // LANES):
            pair = pairs[:, _slot(n)]
            slots += [jnp.where(key_half(1), pair, 0.0),
                      jnp.where(key_half(1), pltpu.roll(pair, HEAD_DIM, 1), 0.0)]
        return jnp.concatenate(slots, axis=1)

    aq_ref[0] = padded_heads(proj(G_AQ, A_HEADS // 2)).astype(BF16)
    cmp_ref[0] = proj(G_CMP, 1)
    slc = proj(G_SLC, 1)
    slc_ref[0] = slc.astype(BF16)
    slcv_ref[0] = jnp.where(key_half(1), 1.0, slc).astype(BF16)
    win_ref[0] = proj(G_WIN, 1).astype(BF16)
    b_all = proj(G_BQ, len(B_GROUPS) + B_HEADS)
    for n in range(len(B_GROUPS)):
        bq_ref[0, n] = b_all[:, _slot(n)]
    for n in range(B_HEADS):
        bkv_ref[0, n] = b_all[:, _slot(len(B_GROUPS) + n)]

    s1 = proj(G_S1, 1)
    s2 = proj(G_S2, 1)

    @pl.when(i == 0)
    def _():
        carry_ref[...] = jnp.zeros_like(carry_ref)

    def split3(v):
        hi = v.astype(BF16)
        r1 = v - hi.astype(F32)
        mid = r1.astype(BF16)
        return hi, mid, (r1 - mid.astype(F32)).astype(BF16)

    z = s1 + fb_ref[...]
    logf = jnp.minimum(z, 0.0) - jnp.log(1.0 + jnp.exp(-jnp.abs(z)))
    blk = min(LANES, tm)
    row = lax.broadcasted_iota(jnp.int32, (blk, blk), 0)
    col = lax.broadcasted_iota(jnp.int32, (blk, blk), 1)
    tri = (row >= col).astype(BF16)
    parts = split3(logf)
    running = carry_ref[0:1, :]
    blocks = []
    for b0 in range(0, tm, blk):
        blocks.append(sum(_dot(tri, part[b0:b0 + blk]) for part in parts) + running)
        running = blocks[-1][blk - 1:blk, :]
    cum = jnp.concatenate(blocks, axis=0)
    carry_ref[0:1, :] = running
    small_ref[0] = s1

    hi, mid, lo = split3(-LOG2E * cum)
    r = lax.broadcasted_iota(jnp.int32, (1, LANES), 1) - LANE_FORGET
    in_terms = (r >= 0) & (r < N_TERMS * C_HEADS)

    def is_term(n):
        hit = r == n
        for hd in range(1, C_HEADS):
            hit = hit | (r == N_TERMS * hd + n)
        return hit

    terms = jnp.where(is_term(0), hi.astype(F32), jnp.where(is_term(1), mid.astype(F32), lo.astype(F32)))
    lane4 = lax.broadcasted_iota(jnp.int32, (1, C_HEADS * LANES), 1)
    own = (lane4 & (LANES - 1)) - LANE_FORGET - N_TERMS * lax.shift_right_logical(lane4, 7)
    own_terms = (own >= 0) & (own < N_TERMS)
    cq_ref[0] = jnp.where(own_terms, 1.0, padded_heads(proj(G_CQ, C_HEADS // 2))).astype(BF16)
    ckv = proj(G_CKV, C_HEADS)
    ck_ref[0] = jnp.where(jnp.concatenate([in_terms] * C_HEADS, axis=1),
                          jnp.concatenate([terms] * C_HEADS, axis=1), ckv).astype(BF16)
    cv_ref[0] = jnp.where(key_half(C_HEADS), 1.0, ckv).astype(BF16)

    tab = tab_ref[...]
    c_q, s_q, c_k, s_k = (tab[:, _slot(n)] for n in range(4))
    qn = _rms(proj(G_DQL, 2), qn_ref[...]).astype(BF16)
    qab = _dot(qn, wq_ref[...])
    kvn = _rms(proj(G_DKVL, 1), kvn_ref[...]).astype(BF16)
    kv_up = _dot(kvn, wkv_ref[...])
    k_rot = s1 * c_k + s2 * s_k
    low1 = key_half(1)
    for hd in range(D_HEADS):
        q_rot = qab[:, _slot(hd)] * c_q + qab[:, _slot(D_HEADS + hd)] * s_q
        dq_ref[0, :, _slot(hd)] = q_rot.astype(BF16)
        dk_ref[0, :, _slot(hd)] = jnp.where(low1, kv_up[:, _slot(hd)], k_rot).astype(BF16)
    dv_ref[0] = jnp.where(key_half(D_HEADS), 1.0, kv_up).astype(BF16)


def _inproj(x, gain, w_ext, tab, fbias, qn_g, kvn_g, wq, wkv):
    b, s, d = x.shape
    tm = min(TM_PROJ, s)

    def out(kind, n, dtype):
        if kind == "rows":
            return (pl.BlockSpec((1, tm, n * LANES), lambda bi, i: (bi, i, 0)),
                    jax.ShapeDtypeStruct((b, s, n * LANES), dtype))
        return (pl.BlockSpec((1, n, tm, LANES), lambda bi, i: (bi, 0, i, 0)),
                jax.ShapeDtypeStruct((b, n, s, LANES), dtype))

    outs = [out("rows", A_HEADS, BF16), out("rows", 1, F32), out("rows", 1, BF16), out("rows", 1, BF16),
            out("rows", 1, BF16), out("slots", len(B_GROUPS), F32), out("slots", B_HEADS, F32),
            out("rows", C_HEADS, BF16), out("rows", C_HEADS, BF16), out("rows", C_HEADS, BF16),
            out("rows", D_HEADS, BF16), out("rows", D_HEADS, BF16), out("rows", D_HEADS, BF16),
            out("rows", 1, F32)]
    consts = (gain, w_ext)
    consts2 = (fbias, qn_g, kvn_g, wq, wkv)
    return pl.pallas_call(
        _inproj_kernel,
        grid=(b, s // tm),
        in_specs=[pl.BlockSpec((1, tm, d), lambda bi, i: (bi, i, 0))]
        + [_const_spec(c.shape) for c in consts]
        + [pl.BlockSpec((tm, 4 * LANES), lambda bi, i: (i, 0))]
        + [_const_spec(c.shape) for c in consts2],
        out_specs=[o[0] for o in outs],
        out_shape=[o[1] for o in outs],
        scratch_shapes=[pltpu.VMEM((8, LANES), F32)],
        name="inproj",
        compiler_params=_cparams(2),
    )(x, *consts, tab, *consts2)


def _gelu_tanh(x):
    return 0.5 * x * (1.0 + jnp.tanh(math.sqrt(2.0 / math.pi) * (x + 0.044715 * (x * x * x))))


def _compress_kernel(kc_ref, vc_ref, pos_ref, w1k_ref, w1v_ref, w2k_ref, w2v_ref, out_ref):
    nc = kc_ref.shape[1]
    half = A_CMP_STRIDE * HEAD_DIM
    pos_a = pos_ref[0:1, :]
    pos_b = pos_ref[1:2, :]

    def phi(chunks, w1_ref, w2_ref):
        u = _dot((chunks + pos_a).astype(BF16), w1_ref[0:half, :])
        v = _dot((chunks + pos_b).astype(BF16), w1_ref[half:2 * half, :])
        hid = u + pltpu.roll(v, nc - 1, 0)
        return _dot(_gelu_tanh(hid).astype(BF16), w2_ref[...])

    out_ref[0] = (phi(kc_ref[0], w1k_ref, w2k_ref) + phi(vc_ref[0], w1v_ref, w2v_ref)).astype(BF16)


def _compress(kc_in, vc_in, pos2, w1k, w1v, w2k_pad, w2v_pad):
    b, nc, width = kc_in.shape
    blk = pl.BlockSpec((1, nc, width), lambda bi: (bi, 0, 0))
    return pl.pallas_call(
        _compress_kernel,
        grid=(b,),
        in_specs=[blk, blk, _const_spec(pos2.shape), _const_spec(w1k.shape), _const_spec(w1v.shape),
                  _const_spec(w2k_pad.shape), _const_spec(w2v_pad.shape)],
        out_specs=pl.BlockSpec((1, nc, LANES), lambda bi: (bi, 0, 0)),
        out_shape=jax.ShapeDtypeStruct((b, nc, LANES), BF16),
        name="compress",
        compiler_params=_cparams(1),
    )(kc_in, vc_in, pos2, w1k, w1v, w2k_pad, w2v_pad)


def _nsa_cmp_win_kernel(q_ref, kvc_ref, kvw_ref, small_ref, ovl_ref, bias_ref,
                        part_ref, sel_ref, *, n_top):
    tq = min(TQ_NSA, q_ref.shape[1])
    for u in range(q_ref.shape[1] // tq):
        _nsa_cmp_win_tile(u * tq, tq, q_ref, kvc_ref, kvw_ref, small_ref, ovl_ref, bias_ref,
                          part_ref, sel_ref, n_top)


def _nsa_cmp_win_tile(r0, tq, q_ref, kvc_ref, kvw_ref, small_ref, ovl_ref, bias_ref, part_ref, sel_ref, n_top):
    nc = kvc_ref.shape[1]
    n_sel = ovl_ref.shape[0]
    t0 = pl.multiple_of(pl.program_id(1) * q_ref.shape[1] + r0, tq)
    tile = slice(r0, r0 + tq)
    q4 = jnp.concatenate([q_ref[0, tile, _slot(h)] for h in range(A_HEADS)], axis=0)
    rows = A_HEADS * tq

    kvc = kvc_ref[0]
    s = _dot_nt(q4, kvc)
    t_row = t0 + lax.broadcasted_iota(jnp.int32, (tq, nc), 0)
    c_end = lax.broadcasted_iota(jnp.int32, (tq, nc), 1) * A_CMP_STRIDE + (A_CMP_LEN - 1)
    valid = jnp.concatenate([c_end <= t_row] * A_HEADS, axis=0)
    s = jnp.where(valid, s, NEG_INF)
    m = jnp.max(s, axis=1, keepdims=True)
    p = jnp.where(valid, jnp.exp2(s - m), 0.0)
    den = jnp.maximum(jnp.sum(p, axis=1, keepdims=True), 1e-30)
    p = p / den
    o_c = _dot(p.astype(BF16), kvc)

    p_sum = p[0:tq] + p[tq:2 * tq] + p[2 * tq:3 * tq] + p[3 * tq:4 * tq]
    p_hi = p_sum.astype(BF16)
    p_lo = (p_sum - p_hi.astype(F32)).astype(BF16)
    imp_t = _dot_nt(ovl_ref[...], p_hi) + _dot_nt(ovl_ref[...], p_lo)

    j_idx = lax.broadcasted_iota(jnp.int32, (n_sel, tq), 0)
    cur = lax.shift_right_logical(t0 + lax.broadcasted_iota(jnp.int32, (n_sel, tq), 1), 6)
    forced = (j_idx == 0) | (j_idx == cur) | (j_idx == cur - 1)
    imp_t = jnp.where(forced, FORCE_SCORE, jnp.where(j_idx > cur, -FORCE_SCORE, imp_t))
    sub8 = 8
    groups = [imp_t[v * sub8:(v + 1) * sub8] for v in range(n_sel // sub8)]
    ranks = [jnp.zeros((sub8, tq), F32) for _ in groups]
    in_group = lax.broadcasted_iota(jnp.int32, (sub8, tq), 0)
    for c in range(n_sel):
        row_c = imp_t[c:c + 1, :]
        for v, grp in enumerate(groups):
            if v > c // sub8:
                beats = row_c >= grp
            elif v < c // sub8:
                beats = row_c > grp
            else:
                beats = (row_c > grp) | ((row_c == grp) & (in_group > c % sub8))
            ranks[v] = ranks[v] + jnp.where(beats, 1.0, 0.0)
    rank = jnp.concatenate(ranks, axis=0)
    sel_t = jnp.where(rank < float(n_top), 1.0, 0.0).astype(BF16)
    sel_ref[0, :, tile] = sel_t

    span = A_WINDOW + tq
    kvw = kvw_ref[0, pl.ds(t0, span), :]
    kpos = t0 - A_WINDOW + lax.broadcasted_iota(jnp.int32, (1, span), 1)
    sw = _dot_nt(q4, kvw) + bias_ref[...] + jnp.where(kpos >= 0, 0.0, NEG_INF)
    mw = jnp.max(sw, axis=1, keepdims=True)
    pw = jnp.exp2(sw - mw)
    o_w = _dot(pw.astype(BF16), kvw) / jnp.sum(pw, axis=1, keepdims=True)

    g = _sigmoid(small_ref[0, tile])
    gated = []
    for h in range(A_HEADS):
        g_c = g[:, LANE_GATE + 3 * h:LANE_GATE + 3 * h + 1]
        g_w = g[:, LANE_GATE + 3 * h + 2:LANE_GATE + 3 * h + 3]
        gated.append(g_c * o_c[h * tq:(h + 1) * tq] + g_w * o_w[h * tq:(h + 1) * tq])
    low = lax.broadcasted_iota(jnp.int32, (tq, LANES), 1) < HEAD_DIM
    for pair in range(A_HEADS // 2):
        part_ref[0, tile, _slot(pair)] = jnp.where(low, pltpu.roll(gated[2 * pair], HEAD_DIM, 1),
                                                   gated[2 * pair + 1])


def _nsa_cmp_win(aq, kvc, kvw_pad, small, ovl, bias_w):
    b, s, _ = aq.shape
    tq = min(TQ_NSA_STEP, s)
    nc = kvc.shape[1]
    n_sel = ovl.shape[0]
    n_top = min(A_SEL_TOPK, n_sel)
    return pl.pallas_call(
        functools.partial(_nsa_cmp_win_kernel, n_top=n_top),
        grid=(b, s // tq),
        in_specs=[pl.BlockSpec((1, tq, A_HEADS * LANES), lambda bi, i: (bi, i, 0)),
                  pl.BlockSpec((1, nc, LANES), lambda bi, i: (bi, 0, 0)),
                  pl.BlockSpec((1, s + A_WINDOW, LANES), lambda bi, i: (bi, 0, 0)),
                  pl.BlockSpec((1, tq, LANES), lambda bi, i: (bi, i, 0)),
                  _const_spec(ovl.shape),
                  _const_spec(bias_w.shape)],
        out_specs=[pl.BlockSpec((1, tq, A_HEADS * HEAD_DIM), lambda bi, i: (bi, i, 0)),
                   pl.BlockSpec((1, n_sel, tq), lambda bi, i: (bi, 0, i))],
        out_shape=[jax.ShapeDtypeStruct((b, s, A_HEADS * HEAD_DIM), F32),
                   jax.ShapeDtypeStruct((b, n_sel, s), BF16)],
        name="nsa_cmp_win",
        compiler_params=_cparams(2),
    )(aq, kvc, kvw_pad, small, ovl, bias_w)


def _flash_kernel(*refs, mode, nh):
    if mode == "sel":
        q_ref, k_ref, v_ref, selt_ref, e_ref, bias_ref, part_ref, small_ref, o_ref, m_ref, acc_ref = refs
    else:
        q_ref, k_ref, v_ref, o_ref, m_ref, acc_ref = refs
    i = pl.program_id(1)
    tq = q_ref.shape[1]
    tk = tq
    shared_kv = mode == "sel"

    m_ref[...] = jnp.full(m_ref.shape, NEG_INF, F32)
    acc_ref[...] = jnp.zeros(acc_ref.shape, F32)

    lag = lax.broadcasted_iota(jnp.int32, (tq, tk), 0) - lax.broadcasted_iota(jnp.int32, (tq, tk), 1)

    def step(j, diag):
        k0 = pl.multiple_of(j * tk, tk)
        keep = (lag >= 0) if diag else None
        if mode == "sel":
            sel_keys = lax.dot_general(selt_ref[0], e_ref[:, pl.ds(k0, tk)], (((0,), (0,)), ((), ())),
                                       preferred_element_type=F32) > 0.5
            keep = (sel_keys & keep) if diag else sel_keys
        penalty = None if keep is None else jnp.where(keep, 0.0, NEG_INF)
        if mode == "sel":
            n_off, bt = bias_ref.shape[0], bias_ref.shape[2]

            def bias_tile(h):
                strips = []
                for a in range(tq // bt):
                    offs = [jnp.clip(i * (tq // bt) + a - j * (tk // bt) - c, 0, n_off - 1)
                            for c in range(tk // bt)]
                    strips.append(jnp.concatenate([bias_ref[o, h] for o in offs], axis=1))
                return jnp.concatenate(strips, axis=0)
        logits = []
        for h in range(nh):
            s = _dot_nt(q_ref[0, :, _slot(h)], k_ref[0, pl.ds(k0, tk), _slot(0 if shared_kv else h)])
            if mode == "sel":
                s = s + bias_tile(h)
            if penalty is not None:
                s = s + penalty
            logits.append(s)
        probs = []
        for h in range(nh):
            m_prev = m_ref[h]
            m_new = jnp.maximum(m_prev, jnp.max(logits[h], axis=1, keepdims=True))
            p = jnp.exp2(logits[h] - jnp.concatenate([m_new] * (tk // LANES), axis=1))
            alpha = jnp.exp2(m_prev - m_new)
            m_ref[h] = m_new
            probs.append((p.astype(BF16), alpha))
        for h in range(nh):
            p, alpha = probs[h]
            acc_ref[h] = alpha * acc_ref[h] + _dot(p, v_ref[0, pl.ds(k0, tk), _slot(0 if shared_kv else h)])

    def full_tile(j, carry):
        step(j, False)
        return carry

    lax.fori_loop(0, i, full_tile, 0)
    step(i, True)

    low = lax.broadcasted_iota(jnp.int32, (tq, LANES), 1) < HEAD_DIM
    if mode == "sel":
        g = _sigmoid(small_ref[0])
    for pair in range(nh // 2):
        outs = []
        for h in (2 * pair, 2 * pair + 1):
            acc = acc_ref[h]
            outs.append(acc / acc[:, 0:1])
        o_pair = jnp.where(low, pltpu.roll(outs[0], HEAD_DIM, 1), outs[1])
        if mode == "sel":
            gate = [g[:, LANE_GATE + 3 * h + 1:LANE_GATE + 3 * h + 2] for h in (2 * pair, 2 * pair + 1)]
            o_pair = part_ref[0, :, _slot(pair)] + jnp.where(low, gate[0], gate[1]) * o_pair
        o_ref[0, :, _slot(pair)] = o_pair.astype(BF16)


def _flash(mode, q, k, v, sel_t=None, expand=None, bias=None, part=None, small=None):
    b, s, qw = q.shape
    nh = qw // LANES
    tq = min(TQ_FLASH, s)
    ow = nh * HEAD_DIM

    def rows(width):
        return pl.BlockSpec((1, tq, width), lambda bi, i: (bi, i, 0))

    def seq_spec(arr):
        return pl.BlockSpec((1,) + arr.shape[1:], lambda bi, i: (bi, 0, 0))

    args = (q, k, v)
    in_specs = [rows(qw), seq_spec(k), seq_spec(v)]
    if mode == "sel":
        args += (sel_t, expand, bias, part, small)
        in_specs += [pl.BlockSpec((1, sel_t.shape[1], tq), lambda bi, i: (bi, 0, i)),
                     _const_spec(expand.shape), _const_spec(bias.shape), rows(ow), rows(LANES)]
    return pl.pallas_call(
        functools.partial(_flash_kernel, mode=mode, nh=nh),
        grid=(b, s // tq),
        in_specs=in_specs,
        out_specs=rows(ow),
        out_shape=jax.ShapeDtypeStruct((b, s, ow), BF16),
        scratch_shapes=[pltpu.VMEM((nh, tq, LANES), F32)] * 2,
        name="flash_" + mode,
        compiler_params=_cparams(2),
    )(*args)


def _dilated_kernel(q_ref, kvp_ref, kvc_ref, bias_ref, o_ref, o_sc, l_sc):
    i = pl.program_id(1)
    rows = q_ref.shape[2]
    tb = TQ_DIL
    low = lax.broadcasted_iota(jnp.int32, (tb, LANES), 1) < HEAD_DIM
    col = lax.broadcasted_iota(jnp.int32, (tb, 2 * tb), 1)
    in_seq = (col >= tb) | (i > 0)
    for g, (_, dil) in enumerate(B_GROUPS):
        span = dil * tb
        for r in range(dil):
            for c in range(rows // span):
                cur = pl.ds(r + c * span, tb, stride=dil)
                if c > 0:
                    prev_ref, prev = kvc_ref, pl.ds(r + (c - 1) * span, tb, stride=dil)
                else:
                    prev_ref, prev = kvp_ref, pl.ds(rows - span + r, tb, stride=dil)
                q2 = q_ref[0, g, cur, :]
                outs, lses = [], []
                for h in range(B_HPG):
                    q_h = jnp.where(low if h == 0 else ~low, q2, 0.0).astype(BF16)
                    head = B_HPG * g + h
                    kv = jnp.concatenate([prev_ref[0, head, prev, :], kvc_ref[0, head, cur, :]],
                                         axis=0).astype(BF16)
                    s = _dot_nt(q_h, kv) + bias_ref[g, h]
                    if c == 0:
                        s = jnp.where(in_seq, s, NEG_INF)
                    m = jnp.max(s, axis=1, keepdims=True)
                    p = jnp.exp2(s - m)
                    den = jnp.sum(p, axis=1, keepdims=True)
                    outs.append(_dot(p.astype(BF16), kv) / den)
                    lses.append(m + jnp.log2(den))
                o_sc[g, cur, :] = jnp.where(low, outs[1], outs[0])
                l_sc[g, cur, :] = jnp.where(low, lses[1], lses[0])
    step = 2 * tb
    for t in range(rows // step):
        rs = slice(t * step, (t + 1) * step)
        l0, l1, l2 = l_sc[0, rs, :], l_sc[1, rs, :], l_sc[2, rs, :]
        mx = jnp.maximum(jnp.maximum(l0, l1), l2)
        e0, e1, e2 = jnp.exp2(l0 - mx), jnp.exp2(l1 - mx), jnp.exp2(l2 - mx)
        mixed = (e0 * o_sc[0, rs, :] + e1 * o_sc[1, rs, :] + e2 * o_sc[2, rs, :]) / (e0 + e1 + e2)
        o_ref[0, rs, :] = mixed.astype(BF16)


def _dilated(bq, bkv, bias):
    b, _, s, _ = bq.shape
    rows = TQ_DIL * B_GROUPS[-1][1]
    n_groups = len(B_GROUPS)

    def tile(n_slots, prev):
        if prev:
            return pl.BlockSpec((1, n_slots, rows, LANES), lambda bi, i: (bi, 0, jnp.maximum(i - 1, 0), 0))
        return pl.BlockSpec((1, n_slots, rows, LANES), lambda bi, i: (bi, 0, i, 0))

    return pl.pallas_call(
        _dilated_kernel,
        grid=(b, s // rows),
        in_specs=[tile(bq.shape[1], False), tile(bkv.shape[1], True), tile(bkv.shape[1], False),
                  _const_spec(bias.shape)],
        out_specs=pl.BlockSpec((1, rows, LANES), lambda bi, i: (bi, i, 0)),
        out_shape=jax.ShapeDtypeStruct((b, s, LANES), BF16),
        scratch_shapes=[pltpu.VMEM((n_groups, rows, LANES), F32)] * 2,
        name="dilated",
        compiler_params=_cparams(2),
    )(bq, bkv, bkv, bias)


def _merge_kernel(x_ref, oa_ref, ob_ref, oc_ref, od_ref,
                  gpre_ref, gpost_ref, wg_ref, wba_ref, wbb_ref, wbc_ref, wbd_ref, wo_ref, out_ref):
    x = x_ref[0]
    h = _rms(x, gpre_ref[...]).astype(BF16)
    branches = ((oa_ref[0], wba_ref), (ob_ref[0], wbb_ref), (oc_ref[0], wbc_ref), (od_ref[0], wbd_ref))
    merged = None
    for n, (o_n, wb_ref) in enumerate(branches):
        term = _sigmoid(_dot(h, wg_ref[n])) * _dot(o_n, wb_ref[...])
        merged = term if merged is None else merged + term
    z = _dot(merged.astype(BF16), wo_ref[...])
    out_ref[0] = x + _rms(z, gpost_ref[...])


def _merge(x, oa, ob, oc, od, gpre, gpost, wg, wba, wbb, wbc, wbd, wo):
    b, s, d = x.shape
    tm = min(TM_PROJ, s)

    def rows(arr):
        return pl.BlockSpec((1, tm, arr.shape[2]), lambda bi, i: (bi, i, 0))

    acts = (x, oa, ob, oc, od)
    consts = (gpre, gpost, wg, wba, wbb, wbc, wbd, wo)
    return pl.pallas_call(
        _merge_kernel,
        grid=(b, s // tm),
        in_specs=[rows(a) for a in acts] + [_const_spec(c.shape) for c in consts],
        out_specs=rows(x),
        out_shape=jax.ShapeDtypeStruct(x.shape, F32),
        name="merge",
        compiler_params=_cparams(2),
    )(*acts, *consts)


def _ffn_kernel(x_ref, gpre_ref, gpost_ref, wup_ref, cw_ref, cb_ref, wdn_ref, out_ref, tail_ref, *, cuts):
    i = pl.program_id(1)
    tm = x_ref.shape[1]
    d_ff = wdn_ref.shape[0]
    x = x_ref[0]
    h = _rms(x, gpre_ref[...]).astype(BF16)

    @pl.when(i == 0)
    def _():
        tail_ref[...] = jnp.zeros_like(tail_ref)

    def conv(c0, chunk):
        u = _dot(h, wup_ref[:, c0:c0 + chunk])
        ue = jnp.concatenate([tail_ref[:, c0:c0 + chunk], u], axis=0)
        tail_ref[:, c0:c0 + chunk] = u[tm - CONV_HALO:]
        w = cw_ref[:, c0:c0 + chunk]
        out = cb_ref[:, c0:c0 + chunk] + w[0:1] * pltpu.roll(ue, 2, 0)[CONV_HALO:]
        out = out + w[1:2] * pltpu.roll(ue, 1, 0)[CONV_HALO:]
        return out + w[2:3] * u

    y = jnp.zeros((tm, x.shape[1]), F32)
    for c0, c1 in zip(cuts[:-1], cuts[1:]):
        gate = conv(c0, c1 - c0)
        val = conv(d_ff + c0, c1 - c0)
        act = gate * _sigmoid(gate) * val
        y = y + _dot(act.astype(BF16), wdn_ref[c0:c1, :])
    out_ref[0] = x + _rms(y, gpost_ref[...])


def _ffn(x, gpre, gpost, w_up, conv_w, conv_b, w_down):
    b, s, d = x.shape
    tm = min(TM_PROJ, s)
    d_ff = w_down.shape[0]
    n_tiles = d_ff // MXU_WIDTH
    cuts = (0, (n_tiles // 2) * MXU_WIDTH, d_ff) if d_ff % MXU_WIDTH == 0 and n_tiles > 1 else (0, d_ff)
    return pl.pallas_call(
        functools.partial(_ffn_kernel, cuts=cuts),
        grid=(b, s // tm),
        in_specs=[pl.BlockSpec((1, tm, d), lambda bi, i: (bi, i, 0)),
                  _const_spec(gpre.shape), _const_spec(gpost.shape), _const_spec(w_up.shape),
                  _const_spec(conv_w.shape), _const_spec(conv_b.shape), _const_spec(w_down.shape)],
        out_specs=pl.BlockSpec((1, tm, d), lambda bi, i: (bi, i, 0)),
        out_shape=jax.ShapeDtypeStruct(x.shape, F32),
        scratch_shapes=[pltpu.VMEM((CONV_HALO, w_up.shape[1]), F32)],
        name="ffn",
        compiler_params=_cparams(2),
    )(x, gpre, gpost, w_up, conv_w, conv_b, w_down)


def _rel_bucket(dist):
    dist = jnp.maximum(dist, 0)
    d = jnp.maximum(dist, 1).astype(F32)
    large = BUCKET_EXACT + (jnp.log(d / BUCKET_EXACT) / math.log(BUCKET_MAX_DIST / BUCKET_EXACT)
                            * (N_BUCKETS - BUCKET_EXACT)).astype(jnp.int32)
    large = jnp.minimum(large, N_BUCKETS - 1)
    return jnp.where(dist < BUCKET_EXACT, dist, large)


def _in_proj_columns():
    src = np.full((N_SLOTS * LANES,), -1, np.int64)
    scl = np.ones((N_SLOTS * LANES,), np.float32)
    o_aq = 0
    o_akv = o_aq + A_HEADS * HEAD_DIM
    o_ag = o_akv + 6 * HEAD_DIM
    o_b = o_ag + 3 * A_HEADS
    o_c = o_b + 3 * B_HEADS * HEAD_DIM
    o_cf = o_c + 3 * C_HEADS * HEAD_DIM
    o_dq = o_cf + C_HEADS
    o_dkv = o_dq + D_Q_LORA
    o_dkr = o_dkv + D_KV_LORA
    e = np.arange(HEAD_DIM)
    q_scale = HEAD_DIM ** -0.5 * LOG2E

    def put(slot, lane0, cols, scale=1.0):
        dst = slot * LANES + lane0 + np.arange(len(cols))
        src[dst] = cols
        scl[dst] = scale

    for h in range(A_HEADS):
        put(G_AQ + h // 2, (h % 2) * HEAD_DIM, o_aq + h * HEAD_DIM + e, q_scale)
    for n, slot in enumerate((G_CMP, G_SLC, G_WIN)):
        put(slot, 0, o_akv + (2 * n) * HEAD_DIM + e)
        put(slot, HEAD_DIM, o_akv + (2 * n + 1) * HEAD_DIM + e)
    for h in range(B_HEADS):
        odd = h % B_HPG
        put(G_BQ + h // B_HPG, odd * HEAD_DIM, o_b + h * HEAD_DIM + e, q_scale)
        put(G_BKV + h, odd * HEAD_DIM, o_b + (B_HEADS + h) * HEAD_DIM + e)
        put(G_BKV + h, (1 - odd) * HEAD_DIM, o_b + (2 * B_HEADS + h) * HEAD_DIM + e)
    for h in range(C_HEADS):
        put(G_CQ + h // 2, (h % 2) * HEAD_DIM, o_c + h * HEAD_DIM + e, q_scale)
        put(G_CKV + h, 0, o_c + (C_HEADS + h) * HEAD_DIM + e)
        put(G_CKV + h, HEAD_DIM, o_c + (2 * C_HEADS + h) * HEAD_DIM + e)
    put(G_DQL, 0, o_dq + np.arange(D_Q_LORA))
    put(G_DKVL, 0, o_dkv + np.arange(D_KV_LORA))
    put(G_S1, LANE_GATE, o_ag + np.arange(3 * A_HEADS))
    put(G_S1, LANE_FORGET, o_cf + np.repeat(np.arange(C_HEADS), N_TERMS))
    put(G_S1, LANE_ROPE, o_dkr + np.arange(D_ROPE))
    half = D_ROPE // 2
    put(G_S2, LANE_ROPE, o_dkr + half + np.arange(half), -1.0)
    put(G_S2, LANE_ROPE + half, o_dkr + np.arange(half))
    return src, scl


def _gather_cols(w, src, scl):
    pieces = []
    n = len(src)
    a = 0
    while a < n:
        e = a + 1
        if src[a] < 0:
            while e < n and src[e] < 0:
                e += 1
            pieces.append(jnp.zeros((w.shape[0], e - a), w.dtype))
        else:
            while e < n and src[e] == src[e - 1] + 1 and scl[e] == scl[a]:
                e += 1
            run = w[:, int(src[a]):int(src[a]) + (e - a)]
            pieces.append(run if scl[a] == 1.0 else run * float(scl[a]))
        a = e
    return jnp.concatenate(pieces, axis=1)


def _mla_weight_columns():
    per_q = D_NOPE + D_ROPE
    half = D_ROPE // 2
    qa = np.full((D_HEADS * LANES,), -1, np.int64)
    qb = np.full((D_HEADS * LANES,), -1, np.int64)
    qb_s = np.ones((D_HEADS * LANES,), np.float32)
    for h in range(D_HEADS):
        qa[h * LANES + np.arange(per_q)] = h * per_q + np.arange(per_q)
        rot = h * per_q + D_NOPE
        qb[h * LANES + D_NOPE + np.arange(half)] = rot + half + np.arange(half)
        qb_s[h * LANES + D_NOPE + np.arange(half)] = -1.0
        qb[h * LANES + D_NOPE + half + np.arange(half)] = rot + np.arange(half)
    ones = np.ones_like(qb_s)
    return (qa, ones), (qb, qb_s)


def _rope_tables(s):
    inv_freq = ROPE_THETA ** (-jnp.arange(0, D_ROPE, 2, dtype=F32) / D_ROPE)
    ang = jnp.arange(s, dtype=F32)[:, None] * inv_freq[None, :]
    cos2 = jnp.concatenate([jnp.cos(ang)] * 2, axis=1)
    sin2 = jnp.concatenate([jnp.sin(ang)] * 2, axis=1)
    scale = (D_NOPE + D_ROPE) ** -0.5 * LOG2E
    z_lo = jnp.zeros((s, D_NOPE), F32)
    z_hi = jnp.zeros((s, LANES - D_NOPE - D_ROPE), F32)
    c_q = jnp.concatenate([jnp.full((s, D_NOPE), scale, F32), scale * cos2, z_hi], axis=1)
    s_q = jnp.concatenate([z_lo, scale * sin2, z_hi], axis=1)
    c_k = jnp.concatenate([z_lo, cos2, z_hi], axis=1)
    s_k = jnp.concatenate([z_lo, sin2, z_hi], axis=1)
    return jnp.concatenate([c_q, s_q, c_k, s_k], axis=1)


def _toeplitz(vec, n_rows, n_cols, off, step=1):
    base = vec[:, ::step]
    last = base.shape[1] - 1
    lo, hi = off - (n_cols - 1), off + n_rows - 1
    core = base[:, max(lo, 0):min(hi, last) + 1]
    front = jnp.repeat(base[:, :1], max(0, -lo), axis=1)
    back = jnp.repeat(base[:, last:], max(0, hi - last), axis=1)
    g = jnp.concatenate([front, core, back], axis=1)
    lg = n_rows + n_cols - 1
    t = jnp.tile(g, (1, n_rows + 1))[:, :n_rows * (lg + 1)].reshape(-1, n_rows, lg + 1)
    return t[:, :, :n_cols][:, :, ::-1]


def _bias_tables(rel_table, s):
    bucket = _rel_bucket(jnp.arange(s))
    hit = bucket[None, :, None] == jnp.arange(N_BUCKETS)[None, None, :]
    by_dist = jnp.sum(jnp.where(hit, LOG2E * rel_table.T[:, None, :], 0.0), axis=-1)
    tqa = min(TQ_NSA, s)
    dist = jnp.arange(tqa)[:, None] + A_WINDOW - jnp.arange(A_WINDOW + tqa)[None, :]
    ok = (dist >= 0) & (dist < A_WINDOW)
    bias_w = jnp.where(ok[None], _toeplitz(by_dist[:A_HEADS], tqa, A_WINDOW + tqa, A_WINDOW), NEG_INF)
    bias_w = bias_w.reshape(A_HEADS * tqa, A_WINDOW + tqa)
    tf = min(BIAS_TILE, s)
    n_off = min(s // tf, -(-(BUCKET_MAX_DIST + tf - 1) // tf) + 1)
    bias_s = jnp.stack([_toeplitz(by_dist[:A_HEADS], tf, tf, n * tf) for n in range(n_off)])
    m = jnp.arange(TQ_DIL)[:, None] + TQ_DIL - jnp.arange(2 * TQ_DIL)[None, :]
    bias_d = []
    for g, (window, dil) in enumerate(B_GROUPS):
        ok = (m >= 0) & (m <= window // dil)
        heads = by_dist[A_HEADS + g * B_HPG:A_HEADS + (g + 1) * B_HPG]
        bias_d.append(jnp.where(ok[None], _toeplitz(heads, TQ_DIL, 2 * TQ_DIL, TQ_DIL, dil), NEG_INF))
    return bias_w, bias_s, jnp.stack(bias_d)


def _selection_constants(s):
    nc = s // A_CMP_STRIDE
    n_cmp = (s - A_CMP_LEN) // A_CMP_STRIDE + 1
    n_sel = s // A_SEL_BLOCK
    c = np.arange(nc)[:, None]
    j = np.arange(n_sel)[None, :]
    c_start = c * A_CMP_STRIDE
    overlap = ((c_start <= j * A_SEL_BLOCK + A_SEL_BLOCK - 1) & (c_start + A_CMP_LEN - 1 >= j * A_SEL_BLOCK)
               & (c < n_cmp))
    expand = (np.arange(s)[None, :] // A_SEL_BLOCK) == np.arange(n_sel)[:, None]
    return jnp.asarray(overlap.T, BF16), jnp.asarray(expand, BF16)


def kernel(x, rel_bias_table, norm_attn_pre, norm_attn_post, norm_ffn_pre, norm_ffn_post, w_in, nsa_cmp_pos, nsa_phi_k_w1, nsa_phi_k_w2, nsa_phi_v_w1, nsa_phi_v_w2, fox_forget_bias, mla_q_norm, mla_kv_norm, mla_w_uq, mla_w_ukv, w_branch_a, w_branch_b, w_branch_c, w_branch_d, w_merge_gate, w_o, ffn_w_up, ffn_conv_w, ffn_conv_b, ffn_w_down):
    b, s, d = x.shape
    depth = w_in.shape[0]
    assert s % (TQ_DIL * B_GROUPS[-1][1]) == 0, "every dilation class needs whole 128-row tiles"

    in_src, in_scl = _in_proj_columns()
    assert D_NOPE + D_VDIM == LANES, "an MLA head's [k_nope | v] up-projection fills one slot"
    (qa_i, qa_s), (qb_i, qb_s) = _mla_weight_columns()
    rope_tab = _rope_tables(s)
    bias_w, bias_s, bias_d = _bias_tables(rel_bias_table, s)
    overlap, expand = _selection_constants(s)
    half = A_CMP_STRIDE * HEAD_DIM

    for l in range(depth):
        w_ext = _gather_cols(w_in[l], in_src, in_scl).astype(BF16)
        fbias = jnp.zeros((1, LANES), F32).at[0, LANE_FORGET:LANE_FORGET + N_TERMS * C_HEADS].set(
            jnp.repeat(fox_forget_bias[l], N_TERMS))
        wq = jnp.concatenate([_gather_cols(mla_w_uq[l], qa_i, qa_s), _gather_cols(mla_w_uq[l], qb_i, qb_s)],
                             axis=1).astype(BF16)
        (aq, cmp, slc, slc_v, win, bq, bkv, cq, ck, c_v, dq, dk, d_v, small) = _inproj(
            x, norm_attn_pre[l][None], w_ext, rope_tab, fbias,
            mla_q_norm[l][None], mla_kv_norm[l][None], wq, mla_w_ukv[l].astype(BF16))

        nc = s // A_CMP_STRIDE
        w2k_pad = jnp.pad(nsa_phi_k_w2[l], ((0, 0), (0, LANES - HEAD_DIM))).astype(BF16)
        w2v_pad = jnp.pad(nsa_phi_v_w2[l], ((0, 0), (LANES - HEAD_DIM, 0))).astype(BF16)
        kvc = _compress(cmp[..., :HEAD_DIM].reshape(b, nc, half), cmp[..., HEAD_DIM:].reshape(b, nc, half),
                        nsa_cmp_pos[l].reshape(2, half),
                        nsa_phi_k_w1[l].astype(BF16), nsa_phi_v_w1[l].astype(BF16), w2k_pad, w2v_pad)
        kvw_pad = jnp.pad(win, ((0, 0), (A_WINDOW, 0), (0, 0)))
        part, sel_t = _nsa_cmp_win(aq, kvc, kvw_pad, small, overlap, bias_w)
        o_a = _flash("sel", aq, slc, slc_v, sel_t=sel_t, expand=expand, bias=bias_s, part=part, small=small)

        o_b = _dilated(bq, bkv, bias_d)

        o_c = _flash("fox", cq, ck, c_v)
        o_d = _flash("mla", dq, dk, d_v)

        x = _merge(x, o_a, o_b, o_c, o_d,
                   norm_attn_pre[l][None], norm_attn_post[l][None],
                   w_merge_gate[l].astype(BF16),
                   w_branch_a[l].astype(BF16),
                   jnp.concatenate([w_branch_b[l][HEAD_DIM:], w_branch_b[l][:HEAD_DIM]]).astype(BF16),
                   w_branch_c[l].astype(BF16),
                   w_branch_d[l].astype(BF16),
                   w_o[l].astype(BF16))
        x = _ffn(x, norm_ffn_pre[l][None], norm_ffn_post[l][None], ffn_w_up[l].astype(BF16),
                 ffn_conv_w[l], ffn_conv_b[l][None], ffn_w_down[l].astype(BF16))
    return x
```

```python
import functools
import math

import numpy as np
import jax
import jax.numpy as jnp
from jax import lax
from jax.experimental import pallas as pl
from jax.experimental.pallas import tpu as pltpu

F32 = jnp.float32
BF16 = jnp.bfloat16

LANES = 128
MXU_WIDTH = 256
HEAD_DIM = 64
RMS_EPS = 1e-6
NEG_INF = -1e30
FORCE_SCORE = 1e9
LOG2E = math.log2(math.e)
VMEM_LIMIT_BYTES = 56 * 1024 * 1024

N_BUCKETS = 32
BUCKET_EXACT = 16
BUCKET_MAX_DIST = 2048

A_HEADS = 4
A_CMP_LEN = 32
A_CMP_STRIDE = 16
A_PHI_HIDDEN = 128
A_SEL_BLOCK = 64
A_SEL_TOPK = 16
A_WINDOW = 512
B_GROUPS = ((128, 1), (512, 4), (2048, 16))
B_HPG = 2
B_HEADS = B_HPG * len(B_GROUPS)
C_HEADS = 4
D_HEADS = 4
D_Q_LORA = 256
D_KV_LORA = 128
D_NOPE = 64
D_ROPE = 32
D_VDIM = 64
ROPE_THETA = 10000.0
N_BRANCHES = 4

G_AQ = 0
G_CMP = G_AQ + A_HEADS // 2
G_SLC = G_CMP + 1
G_WIN = G_SLC + 1
G_BQ = G_WIN + 1
G_BKV = G_BQ + len(B_GROUPS)
G_CQ = G_BKV + B_HEADS
G_CKV = G_CQ + C_HEADS // 2
G_DQL = G_CKV + C_HEADS
G_DKVL = G_DQL + 2
G_S1 = G_DKVL + 1
G_S2 = G_S1 + 1
N_SLOTS = G_S2 + 1
LANE_GATE = 0
LANE_FORGET = 96
N_TERMS = 3
LANE_ROPE = 64

TM_PROJ = 512
TQ_NSA_STEP = 512
TQ_FLASH = 512
CHUNK_SLOTS = 8
BIAS_TILE = 256
TQ_NSA = 128
TQ_DIL = 128
CONV_HALO = 8


def _cparams(n_grid):
    return pltpu.CompilerParams(dimension_semantics=("arbitrary",) * n_grid,
                                vmem_limit_bytes=VMEM_LIMIT_BYTES)


def _const_spec(shape):
    nd = len(shape)
    return pl.BlockSpec(shape, lambda *_: (0,) * nd, pipeline_mode=pl.Buffered(1))


def _rms(x, gain):
    return x * lax.rsqrt(jnp.mean(x * x, axis=-1, keepdims=True) + RMS_EPS) * gain


def _sigmoid(x):
    return 1.0 / (1.0 + jnp.exp(-x))


def _dot(a, b):
    return jnp.dot(a, b, preferred_element_type=F32)


def _dot_nt(a, b):
    return lax.dot_general(a, b, (((1,), (1,)), ((), ())), preferred_element_type=F32)


def _slot(h):
    return slice(h * LANES, (h + 1) * LANES)


def _inproj_kernel(x_ref, g_ref, w_ref, tab_ref, fb_ref, qn_ref, kvn_ref, wq_ref, wkv_ref,
                   aq_ref, cmp_ref, slc_ref, slcv_ref, win_ref, bq_ref, bkv_ref, cq_ref, ck_ref, cv_ref,
                   dq_ref, dk_ref, dv_ref, small_ref, carry_ref):
    i = pl.program_id(1)
    tm = x_ref.shape[1]
    h = _rms(x_ref[0], g_ref[...]).astype(BF16)

    n_slots = w_ref.shape[1] // LANES
    wide = {}

    def chunk(c):
        if c not in wide:
            c0 = c * CHUNK_SLOTS
            wide[c] = _dot(h, w_ref[:, c0 * LANES:min(c0 + CHUNK_SLOTS, n_slots) * LANES])
        return wide[c]

    def proj(g0, n):
        pieces = []
        g = g0
        while g < g0 + n:
            c, off = divmod(g, CHUNK_SLOTS)
            take = min(g0 + n - g, CHUNK_SLOTS - off)
            pieces.append(chunk(c)[:, off * LANES:(off + take) * LANES])
            g += take
        return pieces[0] if len(pieces) == 1 else jnp.concatenate(pieces, axis=1)

    def key_half(n):
        return (lax.broadcasted_iota(jnp.int32, (tm, n * LANES), 1) & (LANES - 1)) < HEAD_DIM

    def padded_heads(pairs):
        slots = []
        for n in range(pairs.shape[1] // LANES):
            pair = pairs[:, _slot(n)]
            slots += [jnp.where(key_half(1), pair, 0.0),
                      jnp.where(key_half(1), pltpu.roll(pair, HEAD_DIM, 1), 0.0)]
        return jnp.concatenate(slots, axis=1)

    def split3(v):
        hi = v.astype(BF16)
        r1 = v - hi.astype(F32)
        mid = r1.astype(BF16)
        return hi, mid, (r1 - mid.astype(F32)).astype(BF16)

    s1 = proj(G_S1, 1)
    s2 = proj(G_S2, 1)
    ckv = proj(G_CKV, C_HEADS)
    qn = _rms(proj(G_DQL, 2), qn_ref[...]).astype(BF16)
    kvn = _rms(proj(G_DKVL, 1), kvn_ref[...]).astype(BF16)
    z = s1 + fb_ref[...]
    logf_parts = split3(jnp.minimum(z, 0.0) - jnp.log(1.0 + jnp.exp(-jnp.abs(z))))
    small_ref[0] = s1
    cv_ref[0] = jnp.where(key_half(C_HEADS), 1.0, ckv).astype(BF16)

    aq_ref[0] = padded_heads(proj(G_AQ, A_HEADS // 2)).astype(BF16)
    cmp_ref[0] = proj(G_CMP, 1)
    slc = proj(G_SLC, 1)
    slc_ref[0] = slc.astype(BF16)
    slcv_ref[0] = jnp.where(key_half(1), 1.0, slc).astype(BF16)
    win_ref[0] = proj(G_WIN, 1).astype(BF16)
    bq = proj(G_BQ, len(B_GROUPS))
    for n in range(len(B_GROUPS)):
        bq_ref[0, n] = bq[:, _slot(n)]

    @pl.when(i == 0)
    def _():
        carry_ref[...] = jnp.zeros_like(carry_ref)

    blk = min(LANES, tm)
    row = lax.broadcasted_iota(jnp.int32, (blk, blk), 0)
    col = lax.broadcasted_iota(jnp.int32, (blk, blk), 1)
    tri = (row >= col).astype(BF16)
    running = carry_ref[0:1, :]
    blocks = []
    for b0 in range(0, tm, blk):
        blocks.append(sum(_dot(tri, part[b0:b0 + blk]) for part in logf_parts) + running)
        running = blocks[-1][blk - 1:blk, :]
    cum = jnp.concatenate(blocks, axis=0)
    carry_ref[0:1, :] = running

    hi, mid, lo = split3(-LOG2E * cum)
    r = lax.broadcasted_iota(jnp.int32, (1, LANES), 1) - LANE_FORGET
    in_terms = (r >= 0) & (r < N_TERMS * C_HEADS)

    def is_term(n):
        hit = r == n
        for hd in range(1, C_HEADS):
            hit = hit | (r == N_TERMS * hd + n)
        return hit

    terms = jnp.where(is_term(0), hi.astype(F32), jnp.where(is_term(1), mid.astype(F32), lo.astype(F32)))
    lane4 = lax.broadcasted_iota(jnp.int32, (1, C_HEADS * LANES), 1)
    own = (lane4 & (LANES - 1)) - LANE_FORGET - N_TERMS * lax.shift_right_logical(lane4, 7)
    own_terms = (own >= 0) & (own < N_TERMS)
    ck_ref[0] = jnp.where(jnp.concatenate([in_terms] * C_HEADS, axis=1),
                          jnp.concatenate([terms] * C_HEADS, axis=1), ckv).astype(BF16)

    tab = tab_ref[...]
    c_q, s_q, c_k, s_k = (tab[:, _slot(n)] for n in range(4))
    qab = _dot(qn, wq_ref[...])
    kv_up = _dot(kvn, wkv_ref[...])
    k_rot = s1 * c_k + s2 * s_k
    low1 = key_half(1)
    for hd in range(D_HEADS):
        q_rot = qab[:, _slot(hd)] * c_q + qab[:, _slot(D_HEADS + hd)] * s_q
        dq_ref[0, :, _slot(hd)] = q_rot.astype(BF16)
        dk_ref[0, :, _slot(hd)] = jnp.where(low1, kv_up[:, _slot(hd)], k_rot).astype(BF16)
    dv_ref[0] = jnp.where(key_half(D_HEADS), 1.0, kv_up).astype(BF16)

    bkv = proj(G_BKV, B_HEADS)
    for n in range(B_HEADS):
        bkv_ref[0, n] = bkv[:, _slot(n)]
    cq_ref[0] = jnp.where(own_terms, 1.0, padded_heads(proj(G_CQ, C_HEADS // 2))).astype(BF16)


def _inproj(x, gain, w_ext, tab, fbias, qn_g, kvn_g, wq, wkv):
    b, s, d = x.shape
    tm = min(TM_PROJ, s)

    def out(kind, n, dtype):
        if kind == "rows":
            return (pl.BlockSpec((1, tm, n * LANES), lambda bi, i: (bi, i, 0)),
                    jax.ShapeDtypeStruct((b, s, n * LANES), dtype))
        return (pl.BlockSpec((1, n, tm, LANES), lambda bi, i: (bi, 0, i, 0)),
                jax.ShapeDtypeStruct((b, n, s, LANES), dtype))

    outs = [out("rows", A_HEADS, BF16), out("rows", 1, F32), out("rows", 1, BF16), out("rows", 1, BF16),
            out("rows", 1, BF16), out("slots", len(B_GROUPS), F32), out("slots", B_HEADS, F32),
            out("rows", C_HEADS, BF16), out("rows", C_HEADS, BF16), out("rows", C_HEADS, BF16),
            out("rows", D_HEADS, BF16), out("rows", D_HEADS, BF16), out("rows", D_HEADS, BF16),
            out("rows", 1, F32)]
    consts = (gain, w_ext)
    consts2 = (fbias, qn_g, kvn_g, wq, wkv)
    return pl.pallas_call(
        _inproj_kernel,
        grid=(b, s // tm),
        in_specs=[pl.BlockSpec((1, tm, d), lambda bi, i: (bi, i, 0))]
        + [_const_spec(c.shape) for c in consts]
        + [pl.BlockSpec((tm, 4 * LANES), lambda bi, i: (i, 0))]
        + [_const_spec(c.shape) for c in consts2],
        out_specs=[o[0] for o in outs],
        out_shape=[o[1] for o in outs],
        scratch_shapes=[pltpu.VMEM((8, LANES), F32)],
        name="inproj",
        compiler_params=_cparams(2),
    )(x, *consts, tab, *consts2)


def _gelu_tanh(x):
    return 0.5 * x * (1.0 + jnp.tanh(math.sqrt(2.0 / math.pi) * (x + 0.044715 * (x * x * x))))


def _compress_kernel(kc_ref, vc_ref, pos_ref, w1k_ref, w1v_ref, w2k_ref, w2v_ref, out_ref):
    nc = kc_ref.shape[1]
    half = A_CMP_STRIDE * HEAD_DIM
    pos_a = pos_ref[0:1, :]
    pos_b = pos_ref[1:2, :]

    def phi(chunks, w1_ref, w2_ref):
        u = _dot((chunks + pos_a).astype(BF16), w1_ref[0:half, :])
        v = _dot((chunks + pos_b).astype(BF16), w1_ref[half:2 * half, :])
        hid = u + pltpu.roll(v, nc - 1, 0)
        return _dot(_gelu_tanh(hid).astype(BF16), w2_ref[...])

    out_ref[0] = (phi(kc_ref[0], w1k_ref, w2k_ref) + phi(vc_ref[0], w1v_ref, w2v_ref)).astype(BF16)


def _compress(kc_in, vc_in, pos2, w1k, w1v, w2k_pad, w2v_pad):
    b, nc, width = kc_in.shape
    blk = pl.BlockSpec((1, nc, width), lambda bi: (bi, 0, 0))
    return pl.pallas_call(
        _compress_kernel,
        grid=(b,),
        in_specs=[blk, blk, _const_spec(pos2.shape), _const_spec(w1k.shape), _const_spec(w1v.shape),
                  _const_spec(w2k_pad.shape), _const_spec(w2v_pad.shape)],
        out_specs=pl.BlockSpec((1, nc, LANES), lambda bi: (bi, 0, 0)),
        out_shape=jax.ShapeDtypeStruct((b, nc, LANES), BF16),
        name="compress",
        compiler_params=_cparams(1),
    )(kc_in, vc_in, pos2, w1k, w1v, w2k_pad, w2v_pad)


def _nsa_cmp_win_kernel(q_ref, kvc_ref, kvw_ref, small_ref, ovl_ref, bias_ref,
                        part_ref, sel_ref, *, n_top):
    tq = min(TQ_NSA, q_ref.shape[1])
    for u in range(q_ref.shape[1] // tq):
        _nsa_cmp_win_tile(u * tq, tq, q_ref, kvc_ref, kvw_ref, small_ref, ovl_ref, bias_ref,
                          part_ref, sel_ref, n_top)


def _nsa_cmp_win_tile(r0, tq, q_ref, kvc_ref, kvw_ref, small_ref, ovl_ref, bias_ref, part_ref, sel_ref, n_top):
    nc = kvc_ref.shape[1]
    n_sel = ovl_ref.shape[0]
    t0 = pl.multiple_of(pl.program_id(1) * q_ref.shape[1] + r0, tq)
    tile = slice(r0, r0 + tq)
    q4 = jnp.concatenate([q_ref[0, tile, _slot(h)] for h in range(A_HEADS)], axis=0)
    rows = A_HEADS * tq

    kvc = kvc_ref[0]
    s = _dot_nt(q4, kvc)
    t_row = t0 + lax.broadcasted_iota(jnp.int32, (tq, nc), 0)
    c_end = lax.broadcasted_iota(jnp.int32, (tq, nc), 1) * A_CMP_STRIDE + (A_CMP_LEN - 1)
    valid = jnp.concatenate([c_end <= t_row] * A_HEADS, axis=0)
    s = jnp.where(valid, s, NEG_INF)
    m = jnp.max(s, axis=1, keepdims=True)
    p = jnp.where(valid, jnp.exp2(s - m), 0.0)
    den = jnp.maximum(jnp.sum(p, axis=1, keepdims=True), 1e-30)
    p = p / den
    o_c = _dot(p.astype(BF16), kvc)

    p_sum = p[0:tq] + p[tq:2 * tq] + p[2 * tq:3 * tq] + p[3 * tq:4 * tq]
    p_hi = p_sum.astype(BF16)
    p_lo = (p_sum - p_hi.astype(F32)).astype(BF16)
    imp_t = _dot_nt(ovl_ref[...], p_hi) + _dot_nt(ovl_ref[...], p_lo)

    j_idx = lax.broadcasted_iota(jnp.int32, (n_sel, tq), 0)
    cur = lax.shift_right_logical(t0 + lax.broadcasted_iota(jnp.int32, (n_sel, tq), 1), 6)
    forced = (j_idx == 0) | (j_idx == cur) | (j_idx == cur - 1)
    imp_t = jnp.where(forced, FORCE_SCORE, jnp.where(j_idx > cur, -FORCE_SCORE, imp_t))
    sub8 = 8
    groups = [imp_t[v * sub8:(v + 1) * sub8] for v in range(n_sel // sub8)]
    ranks = [jnp.zeros((sub8, tq), F32) for _ in groups]
    in_group = lax.broadcasted_iota(jnp.int32, (sub8, tq), 0)
    for c in range(n_sel):
        row_c = imp_t[c:c + 1, :]
        for v, grp in enumerate(groups):
            if v > c // sub8:
                beats = row_c >= grp
            elif v < c // sub8:
                beats = row_c > grp
            else:
                beats = (row_c > grp) | ((row_c == grp) & (in_group > c % sub8))
            ranks[v] = ranks[v] + jnp.where(beats, 1.0, 0.0)
    rank = jnp.concatenate(ranks, axis=0)
    sel_t = jnp.where(rank < float(n_top), 1.0, 0.0).astype(BF16)
    sel_ref[0, :, tile] = sel_t

    span = A_WINDOW + tq
    kvw = kvw_ref[0, pl.ds(t0, span), :]
    kpos = t0 - A_WINDOW + lax.broadcasted_iota(jnp.int32, (1, span), 1)
    sw = _dot_nt(q4, kvw) + bias_ref[...] + jnp.where(kpos >= 0, 0.0, NEG_INF)
    mw = jnp.max(sw, axis=1, keepdims=True)
    pw = jnp.exp2(sw - mw)
    o_w = _dot(pw.astype(BF16), kvw) / jnp.sum(pw, axis=1, keepdims=True)

    g = _sigmoid(small_ref[0, tile])
    gated = []
    for h in range(A_HEADS):
        g_c = g[:, LANE_GATE + 3 * h:LANE_GATE + 3 * h + 1]
        g_w = g[:, LANE_GATE + 3 * h + 2:LANE_GATE + 3 * h + 3]
        gated.append(g_c * o_c[h * tq:(h + 1) * tq] + g_w * o_w[h * tq:(h + 1) * tq])
    low = lax.broadcasted_iota(jnp.int32, (tq, LANES), 1) < HEAD_DIM
    for pair in range(A_HEADS // 2):
        part_ref[0, tile, _slot(pair)] = jnp.where(low, pltpu.roll(gated[2 * pair], HEAD_DIM, 1),
                                                   gated[2 * pair + 1])


def _nsa_cmp_win(aq, kvc, kvw_pad, small, ovl, bias_w):
    b, s, _ = aq.shape
    tq = min(TQ_NSA_STEP, s)
    nc = kvc.shape[1]
    n_sel = ovl.shape[0]
    n_top = min(A_SEL_TOPK, n_sel)
    return pl.pallas_call(
        functools.partial(_nsa_cmp_win_kernel, n_top=n_top),
        grid=(b, s // tq),
        in_specs=[pl.BlockSpec((1, tq, A_HEADS * LANES), lambda bi, i: (bi, i, 0)),
                  pl.BlockSpec((1, nc, LANES), lambda bi, i: (bi, 0, 0)),
                  pl.BlockSpec((1, s + A_WINDOW, LANES), lambda bi, i: (bi, 0, 0)),
                  pl.BlockSpec((1, tq, LANES), lambda bi, i: (bi, i, 0)),
                  _const_spec(ovl.shape),
                  _const_spec(bias_w.shape)],
        out_specs=[pl.BlockSpec((1, tq, A_HEADS * HEAD_DIM), lambda bi, i: (bi, i, 0)),
                   pl.BlockSpec((1, n_sel, tq), lambda bi, i: (bi, 0, i))],
        out_shape=[jax.ShapeDtypeStruct((b, s, A_HEADS * HEAD_DIM), F32),
                   jax.ShapeDtypeStruct((b, n_sel, s), BF16)],
        name="nsa_cmp_win",
        compiler_params=_cparams(2),
    )(aq, kvc, kvw_pad, small, ovl, bias_w)


def _flash_kernel(*refs, mode, nh):
    if mode == "sel":
        q_ref, k_ref, v_ref, selt_ref, e_ref, bias_ref, part_ref, small_ref, o_ref, m_ref, acc_ref = refs
    else:
        q_ref, k_ref, v_ref, o_ref, m_ref, acc_ref = refs
    i = pl.program_id(1)
    tq = q_ref.shape[1]
    tk = tq
    shared_kv = mode == "sel"

    m_ref[...] = jnp.full(m_ref.shape, NEG_INF, F32)
    acc_ref[...] = jnp.zeros(acc_ref.shape, F32)

    lag = lax.broadcasted_iota(jnp.int32, (tq, tk), 0) - lax.broadcasted_iota(jnp.int32, (tq, tk), 1)

    def step(j, diag):
        k0 = pl.multiple_of(j * tk, tk)
        keep = (lag >= 0) if diag else None
        if mode == "sel":
            sel_keys = lax.dot_general(selt_ref[0], e_ref[:, pl.ds(k0, tk)], (((0,), (0,)), ((), ())),
                                       preferred_element_type=F32) > 0.5
            keep = (sel_keys & keep) if diag else sel_keys
        penalty = None if keep is None else jnp.where(keep, 0.0, NEG_INF)
        if mode == "sel":
            n_off, bt = bias_ref.shape[0], bias_ref.shape[2]

            def bias_tile(h):
                strips = []
                for a in range(tq // bt):
                    offs = [jnp.clip(i * (tq // bt) + a - j * (tk // bt) - c, 0, n_off - 1)
                            for c in range(tk // bt)]
                    strips.append(jnp.concatenate([bias_ref[o, h] for o in offs], axis=1))
                return jnp.concatenate(strips, axis=0)
        logits = []
        for h in range(nh):
            s = _dot_nt(q_ref[0, :, _slot(h)], k_ref[0, pl.ds(k0, tk), _slot(0 if shared_kv else h)])
            if mode == "sel":
                s = s + bias_tile(h)
            if penalty is not None:
                s = s + penalty
            logits.append(s)
        probs = []
        for h in range(nh):
            m_prev = m_ref[h]
            m_new = jnp.maximum(m_prev, jnp.max(logits[h], axis=1, keepdims=True))
            p = jnp.exp2(logits[h] - jnp.concatenate([m_new] * (tk // LANES), axis=1))
            alpha = jnp.exp2(m_prev - m_new)
            m_ref[h] = m_new
            probs.append((p.astype(BF16), alpha))
        for h in range(nh):
            p, alpha = probs[h]
            acc_ref[h] = alpha * acc_ref[h] + _dot(p, v_ref[0, pl.ds(k0, tk), _slot(0 if shared_kv else h)])

    def full_tile(j, carry):
        step(j, False)
        return carry

    lax.fori_loop(0, i, full_tile, 0)
    step(i, True)

    low = lax.broadcasted_iota(jnp.int32, (tq, LANES), 1) < HEAD_DIM
    if mode == "sel":
        g = _sigmoid(small_ref[0])
    for pair in range(nh // 2):
        outs = []
        for h in (2 * pair, 2 * pair + 1):
            acc = acc_ref[h]
            outs.append(acc / acc[:, 0:1])
        o_pair = jnp.where(low, pltpu.roll(outs[0], HEAD_DIM, 1), outs[1])
        if mode == "sel":
            gate = [g[:, LANE_GATE + 3 * h + 1:LANE_GATE + 3 * h + 2] for h in (2 * pair, 2 * pair + 1)]
            o_pair = part_ref[0, :, _slot(pair)] + jnp.where(low, gate[0], gate[1]) * o_pair
        o_ref[0, :, _slot(pair)] = o_pair.astype(BF16)


def _flash(mode, q, k, v, sel_t=None, expand=None, bias=None, part=None, small=None):
    b, s, qw = q.shape
    nh = qw // LANES
    tq = min(TQ_FLASH, s)
    ow = nh * HEAD_DIM

    def rows(width):
        return pl.BlockSpec((1, tq, width), lambda bi, i: (bi, i, 0))

    def seq_spec(arr):
        return pl.BlockSpec((1,) + arr.shape[1:], lambda bi, i: (bi, 0, 0))

    args = (q, k, v)
    in_specs = [rows(qw), seq_spec(k), seq_spec(v)]
    if mode == "sel":
        args += (sel_t, expand, bias, part, small)
        in_specs += [pl.BlockSpec((1, sel_t.shape[1], tq), lambda bi, i: (bi, 0, i)),
                     _const_spec(expand.shape), _const_spec(bias.shape), rows(ow), rows(LANES)]
    return pl.pallas_call(
        functools.partial(_flash_kernel, mode=mode, nh=nh),
        grid=(b, s // tq),
        in_specs=in_specs,
        out_specs=rows(ow),
        out_shape=jax.ShapeDtypeStruct((b, s, ow), BF16),
        scratch_shapes=[pltpu.VMEM((nh, tq, LANES), F32)] * 2,
        name="flash_" + mode,
        compiler_params=_cparams(2),
    )(*args)


def _dilated_kernel(q_ref, kvp_ref, kvc_ref, bias_ref, o_ref, o_sc, l_sc):
    i = pl.program_id(1)
    rows = q_ref.shape[2]
    tb = TQ_DIL
    low = lax.broadcasted_iota(jnp.int32, (tb, LANES), 1) < HEAD_DIM
    col = lax.broadcasted_iota(jnp.int32, (tb, 2 * tb), 1)
    in_seq = (col >= tb) | (i > 0)
    for g, (_, dil) in enumerate(B_GROUPS):
        span = dil * tb
        for r in range(dil):
            for c in range(rows // span):
                cur = pl.ds(r + c * span, tb, stride=dil)
                if c > 0:
                    prev_ref, prev = kvc_ref, pl.ds(r + (c - 1) * span, tb, stride=dil)
                else:
                    prev_ref, prev = kvp_ref, pl.ds(rows - span + r, tb, stride=dil)
                q2 = q_ref[0, g, cur, :]
                outs, lses = [], []
                for h in range(B_HPG):
                    q_h = jnp.where(low if h == 0 else ~low, q2, 0.0).astype(BF16)
                    head = B_HPG * g + h
                    kv = jnp.concatenate([prev_ref[0, head, prev, :], kvc_ref[0, head, cur, :]],
                                         axis=0).astype(BF16)
                    s = _dot_nt(q_h, kv) + bias_ref[g, h]
                    if c == 0:
                        s = jnp.where(in_seq, s, NEG_INF)
                    m = jnp.max(s, axis=1, keepdims=True)
                    p = jnp.exp2(s - m)
                    den = jnp.sum(p, axis=1, keepdims=True)
                    outs.append(_dot(p.astype(BF16), kv) / den)
                    lses.append(m + jnp.log2(den))
                o_sc[g, cur, :] = jnp.where(low, outs[1], outs[0])
                l_sc[g, cur, :] = jnp.where(low, lses[1], lses[0])
    step = 2 * tb
    for t in range(rows // step):
        rs = slice(t * step, (t + 1) * step)
        l0, l1, l2 = l_sc[0, rs, :], l_sc[1, rs, :], l_sc[2, rs, :]
        mx = jnp.maximum(jnp.maximum(l0, l1), l2)
        e0, e1, e2 = jnp.exp2(l0 - mx), jnp.exp2(l1 - mx), jnp.exp2(l2 - mx)
        mixed = (e0 * o_sc[0, rs, :] + e1 * o_sc[1, rs, :] + e2 * o_sc[2, rs, :]) / (e0 + e1 + e2)
        o_ref[0, rs, :] = mixed.astype(BF16)


def _dilated(bq, bkv, bias):
    b, _, s, _ = bq.shape
    rows = TQ_DIL * B_GROUPS[-1][1]
    n_groups = len(B_GROUPS)

    def tile(n_slots, prev):
        if prev:
            return pl.BlockSpec((1, n_slots, rows, LANES), lambda bi, i: (bi, 0, jnp.maximum(i - 1, 0), 0))
        return pl.BlockSpec((1, n_slots, rows, LANES), lambda bi, i: (bi, 0, i, 0))

    return pl.pallas_call(
        _dilated_kernel,
        grid=(b, s // rows),
        in_specs=[tile(bq.shape[1], False), tile(bkv.shape[1], True), tile(bkv.shape[1], False),
                  _const_spec(bias.shape)],
        out_specs=pl.BlockSpec((1, rows, LANES), lambda bi, i: (bi, i, 0)),
        out_shape=jax.ShapeDtypeStruct((b, s, LANES), BF16),
        scratch_shapes=[pltpu.VMEM((n_groups, rows, LANES), F32)] * 2,
        name="dilated",
        compiler_params=_cparams(2),
    )(bq, bkv, bkv, bias)


def _merge_kernel(x_ref, oa_ref, ob_ref, oc_ref, od_ref,
                  gpre_ref, gpost_ref, wg_ref, wba_ref, wbb_ref, wbc_ref, wbd_ref, wo_ref, out_ref):
    x = x_ref[0]
    h = _rms(x, gpre_ref[...]).astype(BF16)
    branches = ((oa_ref[0], wba_ref), (ob_ref[0], wbb_ref), (oc_ref[0], wbc_ref), (od_ref[0], wbd_ref))
    merged = None
    for n, (o_n, wb_ref) in enumerate(branches):
        term = _sigmoid(_dot(h, wg_ref[n])) * _dot(o_n, wb_ref[...])
        merged = term if merged is None else merged + term
    z = _dot(merged.astype(BF16), wo_ref[...])
    out_ref[0] = x + _rms(z, gpost_ref[...])


def _merge(x, oa, ob, oc, od, gpre, gpost, wg, wba, wbb, wbc, wbd, wo):
    b, s, d = x.shape
    tm = min(TM_PROJ, s)

    def rows(arr):
        return pl.BlockSpec((1, tm, arr.shape[2]), lambda bi, i: (bi, i, 0))

    acts = (x, oa, ob, oc, od)
    consts = (gpre, gpost, wg, wba, wbb, wbc, wbd, wo)
    return pl.pallas_call(
        _merge_kernel,
        grid=(b, s // tm),
        in_specs=[rows(a) for a in acts] + [_const_spec(c.shape) for c in consts],
        out_specs=rows(x),
        out_shape=jax.ShapeDtypeStruct(x.shape, F32),
        name="merge",
        compiler_params=_cparams(2),
    )(*acts, *consts)


def _ffn_kernel(x_ref, gpre_ref, gpost_ref, wup_ref, cw_ref, cb_ref, wdn_ref, out_ref, tail_ref, *, cuts):
    i = pl.program_id(1)
    tm = x_ref.shape[1]
    d_ff = wdn_ref.shape[0]
    x = x_ref[0]
    h = _rms(x, gpre_ref[...]).astype(BF16)

    @pl.when(i == 0)
    def _():
        tail_ref[...] = jnp.zeros_like(tail_ref)

    def conv(c0, chunk):
        u = _dot(h, wup_ref[:, c0:c0 + chunk])
        ue = jnp.concatenate([tail_ref[:, c0:c0 + chunk], u], axis=0)
        tail_ref[:, c0:c0 + chunk] = u[tm - CONV_HALO:]
        w = cw_ref[:, c0:c0 + chunk]
        out = cb_ref[:, c0:c0 + chunk] + w[0:1] * pltpu.roll(ue, 2, 0)[CONV_HALO:]
        out = out + w[1:2] * pltpu.roll(ue, 1, 0)[CONV_HALO:]
        return out + w[2:3] * u

    y = jnp.zeros((tm, x.shape[1]), F32)
    for c0, c1 in zip(cuts[:-1], cuts[1:]):
        gate = conv(c0, c1 - c0)
        val = conv(d_ff + c0, c1 - c0)
        act = gate * _sigmoid(gate) * val
        y = y + _dot(act.astype(BF16), wdn_ref[c0:c1, :])
    out_ref[0] = x + _rms(y, gpost_ref[...])


def _ffn(x, gpre, gpost, w_up, conv_w, conv_b, w_down):
    b, s, d = x.shape
    tm = min(TM_PROJ, s)
    d_ff = w_down.shape[0]
    n_tiles = d_ff // MXU_WIDTH
    cuts = (0, (n_tiles // 2) * MXU_WIDTH, d_ff) if d_ff % MXU_WIDTH == 0 and n_tiles > 1 else (0, d_ff)
    return pl.pallas_call(
        functools.partial(_ffn_kernel, cuts=cuts),
        grid=(b, s // tm),
        in_specs=[pl.BlockSpec((1, tm, d), lambda bi, i: (bi, i, 0)),
                  _const_spec(gpre.shape), _const_spec(gpost.shape), _const_spec(w_up.shape),
                  _const_spec(conv_w.shape), _const_spec(conv_b.shape), _const_spec(w_down.shape)],
        out_specs=pl.BlockSpec((1, tm, d), lambda bi, i: (bi, i, 0)),
        out_shape=jax.ShapeDtypeStruct(x.shape, F32),
        scratch_shapes=[pltpu.VMEM((CONV_HALO, w_up.shape[1]), F32)],
        name="ffn",
        compiler_params=_cparams(2),
    )(x, gpre, gpost, w_up, conv_w, conv_b, w_down)


def _rel_bucket(dist):
    dist = jnp.maximum(dist, 0)
    d = jnp.maximum(dist, 1).astype(F32)
    large = BUCKET_EXACT + (jnp.log(d / BUCKET_EXACT) / math.log(BUCKET_MAX_DIST / BUCKET_EXACT)
                            * (N_BUCKETS - BUCKET_EXACT)).astype(jnp.int32)
    large = jnp.minimum(large, N_BUCKETS - 1)
    return jnp.where(dist < BUCKET_EXACT, dist, large)


def _in_proj_columns():
    src = np.full((N_SLOTS * LANES,), -1, np.int64)
    scl = np.ones((N_SLOTS * LANES,), np.float32)
    o_aq = 0
    o_akv = o_aq + A_HEADS * HEAD_DIM
    o_ag = o_akv + 6 * HEAD_DIM
    o_b = o_ag + 3 * A_HEADS
    o_c = o_b + 3 * B_HEADS * HEAD_DIM
    o_cf = o_c + 3 * C_HEADS * HEAD_DIM
    o_dq = o_cf + C_HEADS
    o_dkv = o_dq + D_Q_LORA
    o_dkr = o_dkv + D_KV_LORA
    e = np.arange(HEAD_DIM)
    q_scale = HEAD_DIM ** -0.5 * LOG2E

    def put(slot, lane0, cols, scale=1.0):
        dst = slot * LANES + lane0 + np.arange(len(cols))
        src[dst] = cols
        scl[dst] = scale

    for h in range(A_HEADS):
        put(G_AQ + h // 2, (h % 2) * HEAD_DIM, o_aq + h * HEAD_DIM + e, q_scale)
    for n, slot in enumerate((G_CMP, G_SLC, G_WIN)):
        put(slot, 0, o_akv + (2 * n) * HEAD_DIM + e)
        put(slot, HEAD_DIM, o_akv + (2 * n + 1) * HEAD_DIM + e)
    for h in range(B_HEADS):
        odd = h % B_HPG
        put(G_BQ + h // B_HPG, odd * HEAD_DIM, o_b + h * HEAD_DIM + e, q_scale)
        put(G_BKV + h, odd * HEAD_DIM, o_b + (B_HEADS + h) * HEAD_DIM + e)
        put(G_BKV + h, (1 - odd) * HEAD_DIM, o_b + (2 * B_HEADS + h) * HEAD_DIM + e)
    for h in range(C_HEADS):
        put(G_CQ + h // 2, (h % 2) * HEAD_DIM, o_c + h * HEAD_DIM + e, q_scale)
        put(G_CKV + h, 0, o_c + (C_HEADS + h) * HEAD_DIM + e)
        put(G_CKV + h, HEAD_DIM, o_c + (2 * C_HEADS + h) * HEAD_DIM + e)
    put(G_DQL, 0, o_dq + np.arange(D_Q_LORA))
    put(G_DKVL, 0, o_dkv + np.arange(D_KV_LORA))
    put(G_S1, LANE_GATE, o_ag + np.arange(3 * A_HEADS))
    put(G_S1, LANE_FORGET, o_cf + np.repeat(np.arange(C_HEADS), N_TERMS))
    put(G_S1, LANE_ROPE, o_dkr + np.arange(D_ROPE))
    half = D_ROPE // 2
    put(G_S2, LANE_ROPE, o_dkr + half + np.arange(half), -1.0)
    put(G_S2, LANE_ROPE + half, o_dkr + np.arange(half))
    return src, scl


def _gather_cols(w, src, scl):
    pieces = []
    n = len(src)
    a = 0
    while a < n:
        e = a + 1
        if src[a] < 0:
            while e < n and src[e] < 0:
                e += 1
            pieces.append(jnp.zeros((w.shape[0], e - a), w.dtype))
        else:
            while e < n and src[e] == src[e - 1] + 1 and scl[e] == scl[a]:
                e += 1
            run = w[:, int(src[a]):int(src[a]) + (e - a)]
            pieces.append(run if scl[a] == 1.0 else run * float(scl[a]))
        a = e
    return jnp.concatenate(pieces, axis=1)


def _mla_weight_columns():
    per_q = D_NOPE + D_ROPE
    half = D_ROPE // 2
    qa = np.full((D_HEADS * LANES,), -1, np.int64)
    qb = np.full((D_HEADS * LANES,), -1, np.int64)
    qb_s = np.ones((D_HEADS * LANES,), np.float32)
    for h in range(D_HEADS):
        qa[h * LANES + np.arange(per_q)] = h * per_q + np.arange(per_q)
        rot = h * per_q + D_NOPE
        qb[h * LANES + D_NOPE + np.arange(half)] = rot + half + np.arange(half)
        qb_s[h * LANES + D_NOPE + np.arange(half)] = -1.0
        qb[h * LANES + D_NOPE + half + np.arange(half)] = rot + np.arange(half)
    ones = np.ones_like(qb_s)
    return (qa, ones), (qb, qb_s)


def _rope_tables(s):
    inv_freq = ROPE_THETA ** (-jnp.arange(0, D_ROPE, 2, dtype=F32) / D_ROPE)
    ang = jnp.arange(s, dtype=F32)[:, None] * inv_freq[None, :]
    cos2 = jnp.concatenate([jnp.cos(ang)] * 2, axis=1)
    sin2 = jnp.concatenate([jnp.sin(ang)] * 2, axis=1)
    scale = (D_NOPE + D_ROPE) ** -0.5 * LOG2E
    z_lo = jnp.zeros((s, D_NOPE), F32)
    z_hi = jnp.zeros((s, LANES - D_NOPE - D_ROPE), F32)
    c_q = jnp.concatenate([jnp.full((s, D_NOPE), scale, F32), scale * cos2, z_hi], axis=1)
    s_q = jnp.concatenate([z_lo, scale * sin2, z_hi], axis=1)
    c_k = jnp.concatenate([z_lo, cos2, z_hi], axis=1)
    s_k = jnp.concatenate([z_lo, sin2, z_hi], axis=1)
    return jnp.concatenate([c_q, s_q, c_k, s_k], axis=1)


def _toeplitz(vec, n_rows, n_cols, off, step=1):
    base = vec[:, ::step]
    last = base.shape[1] - 1
    lo, hi = off - (n_cols - 1), off + n_rows - 1
    core = base[:, max(lo, 0):min(hi, last) + 1]
    front = jnp.repeat(base[:, :1], max(0, -lo), axis=1)
    back = jnp.repeat(base[:, last:], max(0, hi - last), axis=1)
    g = jnp.concatenate([front, core, back], axis=1)
    lg = n_rows + n_cols - 1
    t = jnp.tile(g, (1, n_rows + 1))[:, :n_rows * (lg + 1)].reshape(-1, n_rows, lg + 1)
    return t[:, :, :n_cols][:, :, ::-1]


def _bias_tables(rel_table, s):
    bucket = _rel_bucket(jnp.arange(s))
    hit = bucket[None, :, None] == jnp.arange(N_BUCKETS)[None, None, :]
    by_dist = jnp.sum(jnp.where(hit, LOG2E * rel_table.T[:, None, :], 0.0), axis=-1)
    tqa = min(TQ_NSA, s)
    dist = jnp.arange(tqa)[:, None] + A_WINDOW - jnp.arange(A_WINDOW + tqa)[None, :]
    ok = (dist >= 0) & (dist < A_WINDOW)
    bias_w = jnp.where(ok[None], _toeplitz(by_dist[:A_HEADS], tqa, A_WINDOW + tqa, A_WINDOW), NEG_INF)
    bias_w = bias_w.reshape(A_HEADS * tqa, A_WINDOW + tqa)
    tf = min(BIAS_TILE, s)
    n_off = min(s // tf, -(-(BUCKET_MAX_DIST + tf - 1) // tf) + 1)
    bias_s = jnp.stack([_toeplitz(by_dist[:A_HEADS], tf, tf, n * tf) for n in range(n_off)])
    m = jnp.arange(TQ_DIL)[:, None] + TQ_DIL - jnp.arange(2 * TQ_DIL)[None, :]
    bias_d = []
    for g, (window, dil) in enumerate(B_GROUPS):
        ok = (m >= 0) & (m <= window // dil)
        heads = by_dist[A_HEADS + g * B_HPG:A_HEADS + (g + 1) * B_HPG]
        bias_d.append(jnp.where(ok[None], _toeplitz(heads, TQ_DIL, 2 * TQ_DIL, TQ_DIL, dil), NEG_INF))
    return bias_w, bias_s, jnp.stack(bias_d)


def _selection_constants(s):
    nc = s // A_CMP_STRIDE
    n_cmp = (s - A_CMP_LEN) // A_CMP_STRIDE + 1
    n_sel = s // A_SEL_BLOCK
    c = np.arange(nc)[:, None]
    j = np.arange(n_sel)[None, :]
    c_start = c * A_CMP_STRIDE
    overlap = ((c_start <= j * A_SEL_BLOCK + A_SEL_BLOCK - 1) & (c_start + A_CMP_LEN - 1 >= j * A_SEL_BLOCK)
               & (c < n_cmp))
    expand = (np.arange(s)[None, :] // A_SEL_BLOCK) == np.arange(n_sel)[:, None]
    return jnp.asarray(overlap.T, BF16), jnp.asarray(expand, BF16)


def kernel(x, rel_bias_table, norm_attn_pre, norm_attn_post, norm_ffn_pre, norm_ffn_post, w_in, nsa_cmp_pos, nsa_phi_k_w1, nsa_phi_k_w2, nsa_phi_v_w1, nsa_phi_v_w2, fox_forget_bias, mla_q_norm, mla_kv_norm, mla_w_uq, mla_w_ukv, w_branch_a, w_branch_b, w_branch_c, w_branch_d, w_merge_gate, w_o, ffn_w_up, ffn_conv_w, ffn_conv_b, ffn_w_down):
    b, s, d = x.shape
    depth = w_in.shape[0]
    assert s % (TQ_DIL * B_GROUPS[-1][1]) == 0, "every dilation class needs whole 128-row tiles"

    in_src, in_scl = _in_proj_columns()
    assert D_NOPE + D_VDIM == LANES, "an MLA head's [k_nope | v] up-projection fills one slot"
    (qa_i, qa_s), (qb_i, qb_s) = _mla_weight_columns()
    rope_tab = _rope_tables(s)
    bias_w, bias_s, bias_d = _bias_tables(rel_bias_table, s)
    overlap, expand = _selection_constants(s)
    half = A_CMP_STRIDE * HEAD_DIM

    for l in range(depth):
        w_ext = _gather_cols(w_in[l], in_src, in_scl).astype(BF16)
        fbias = jnp.zeros((1, LANES), F32).at[0, LANE_FORGET:LANE_FORGET + N_TERMS * C_HEADS].set(
            jnp.repeat(fox_forget_bias[l], N_TERMS))
        wq = jnp.concatenate([_gather_cols(mla_w_uq[l], qa_i, qa_s), _gather_cols(mla_w_uq[l], qb_i, qb_s)],
                             axis=1).astype(BF16)
        (aq, cmp, slc, slc_v, win, bq, bkv, cq, ck, c_v, dq, dk, d_v, small) = _inproj(
            x, norm_attn_pre[l][None], w_ext, rope_tab, fbias,
            mla_q_norm[l][None], mla_kv_norm[l][None], wq, mla_w_ukv[l].astype(BF16))

        nc = s // A_CMP_STRIDE
        w2k_pad = jnp.pad(nsa_phi_k_w2[l], ((0, 0), (0, LANES - HEAD_DIM))).astype(BF16)
        w2v_pad = jnp.pad(nsa_phi_v_w2[l], ((0, 0), (LANES - HEAD_DIM, 0))).astype(BF16)
        kvc = _compress(cmp[..., :HEAD_DIM].reshape(b, nc, half), cmp[..., HEAD_DIM:].reshape(b, nc, half),
                        nsa_cmp_pos[l].reshape(2, half),
                        nsa_phi_k_w1[l].astype(BF16), nsa_phi_v_w1[l].astype(BF16), w2k_pad, w2v_pad)
        kvw_pad = jnp.pad(win, ((0, 0), (A_WINDOW, 0), (0, 0)))
        part, sel_t = _nsa_cmp_win(aq, kvc, kvw_pad, small, overlap, bias_w)
        o_a = _flash("sel", aq, slc, slc_v, sel_t=sel_t, expand=expand, bias=bias_s, part=part, small=small)

        o_b = _dilated(bq, bkv, bias_d)

        o_c = _flash("fox", cq, ck, c_v)
        o_d = _flash("mla", dq, dk, d_v)

        x = _merge(x, o_a, o_b, o_c, o_d,
                   norm_attn_pre[l][None], norm_attn_post[l][None],
                   w_merge_gate[l].astype(BF16),
                   w_branch_a[l].astype(BF16),
                   jnp.concatenate([w_branch_b[l][HEAD_DIM:], w_branch_b[l][:HEAD_DIM]]).astype(BF16),
                   w_branch_c[l].astype(BF16),
                   w_branch_d[l].astype(BF16),
                   w_o[l].astype(BF16))
        x = _ffn(x, norm_ffn_pre[l][None], norm_ffn_post[l][None], ffn_w_up[l].astype(BF16),
                 ffn_conv_w[l], ffn_conv_b[l][None], ffn_w_down[l].astype(BF16))
    return x
```

```python
import functools
import math

import numpy as np
import jax
import jax.numpy as jnp
from jax import lax
from jax.experimental import pallas as pl
from jax.experimental.pallas import tpu as pltpu

F32 = jnp.float32
BF16 = jnp.bfloat16

LANES = 128
MXU_WIDTH = 256
HEAD_DIM = 64
RMS_EPS = 1e-6
NEG_INF = -1e30
FORCE_SCORE = 1e9
LOG2E = math.log2(math.e)
VMEM_LIMIT_BYTES = 56 * 1024 * 1024

N_BUCKETS = 32
BUCKET_EXACT = 16
BUCKET_MAX_DIST = 2048

A_HEADS = 4
A_CMP_LEN = 32
A_CMP_STRIDE = 16
A_SEL_BLOCK = 64
A_SEL_TOPK = 16
A_WINDOW = 512
B_GROUPS = ((128, 1), (512, 4), (2048, 16))
B_HPG = 2
B_HEADS = B_HPG * len(B_GROUPS)
C_HEADS = 4
D_HEADS = 4
D_Q_LORA = 256
D_KV_LORA = 128
D_NOPE = 64
D_ROPE = 32
D_VDIM = 64
ROPE_THETA = 10000.0

G_AQ = 0
G_CMP = G_AQ + A_HEADS // 2
G_SLC = G_CMP + 1
G_WIN = G_SLC + 1
G_BQ = G_WIN + 1
G_BKV = G_BQ + len(B_GROUPS)
G_CQ = G_BKV + B_HEADS
G_CKV = G_CQ + C_HEADS // 2
G_DQL = G_CKV + C_HEADS
G_DKVL = G_DQL + 2
G_S1 = G_DKVL + 1
G_S2 = G_S1 + 1
N_SLOTS = G_S2 + 1
LANE_GATE = 0
LANE_FORGET = 96
N_TERMS = 3
LANE_ROPE = 64

TM_PROJ = 512
TQ_NSA_STEP = 512
TQ_FLASH = 512
CHUNK_SLOTS = 8
BIAS_TILE = 256
TQ_NSA = 128
TQ_DIL = 128
CONV_HALO = 8


def _cparams(n_grid):
    return pltpu.CompilerParams(dimension_semantics=("arbitrary",) * n_grid,
                                vmem_limit_bytes=VMEM_LIMIT_BYTES)


def _const_spec(shape):
    nd = len(shape)
    return pl.BlockSpec(shape, lambda *_: (0,) * nd, pipeline_mode=pl.Buffered(1))


def _rms(x, gain):
    return x * lax.rsqrt(jnp.mean(x * x, axis=-1, keepdims=True) + RMS_EPS) * gain


def _sigmoid(x):
    return 1.0 / (1.0 + jnp.exp(-x))


def _dot(a, b):
    return jnp.dot(a, b, preferred_element_type=F32)


def _dot_nt(a, b):
    return lax.dot_general(a, b, (((1,), (1,)), ((), ())), preferred_element_type=F32)


def _slot(h):
    return slice(h * LANES, (h + 1) * LANES)


def _inproj_kernel(x_ref, g_ref, w_ref, tab_ref, fb_ref, qn_ref, kvn_ref, wq_ref, wkv_ref,
                   aq_ref, cmp_ref, slc_ref, slcv_ref, win_ref, bq_ref, bkv_ref, cq_ref, ck_ref, cv_ref,
                   dq_ref, dk_ref, dv_ref, small_ref, carry_ref):
    i = pl.program_id(1)
    tm = x_ref.shape[1]
    h = _rms(x_ref[0], g_ref[...]).astype(BF16)

    n_slots = w_ref.shape[1] // LANES
    wide = {}

    def chunk(c):
        if c not in wide:
            c0 = c * CHUNK_SLOTS
            wide[c] = _dot(h, w_ref[:, c0 * LANES:min(c0 + CHUNK_SLOTS, n_slots) * LANES])
        return wide[c]

    def proj(g0, n):
        pieces = []
        g = g0
        while g < g0 + n:
            c, off = divmod(g, CHUNK_SLOTS)
            take = min(g0 + n - g, CHUNK_SLOTS - off)
            pieces.append(chunk(c)[:, off * LANES:(off + take) * LANES])
            g += take
        return pieces[0] if len(pieces) == 1 else jnp.concatenate(pieces, axis=1)

    def key_half(n):
        return (lax.broadcasted_iota(jnp.int32, (tm, n * LANES), 1) & (LANES - 1)) < HEAD_DIM

    def padded_heads(pairs):
        slots = []
        for n in range(pairs.shape[1] // LANES):
            pair = pairs[:, _slot(n)]
            slots += [jnp.where(key_half(1), pair, 0.0),
                      jnp.where(key_half(1), pltpu.roll(pair, HEAD_DIM, 1), 0.0)]
        return jnp.concatenate(slots, axis=1)

    def split3(v):
        hi = v.astype(BF16)
        r1 = v - hi.astype(F32)
        mid = r1.astype(BF16)
        return hi, mid, (r1 - mid.astype(F32)).astype(BF16)

    s1 = proj(G_S1, 1)
    s2 = proj(G_S2, 1)
    ckv = proj(G_CKV, C_HEADS)
    qn = _rms(proj(G_DQL, 2), qn_ref[...]).astype(BF16)
    kvn = _rms(proj(G_DKVL, 1), kvn_ref[...]).astype(BF16)
    z = s1 + fb_ref[...]
    logf_parts = split3(jnp.minimum(z, 0.0) - jnp.log(1.0 + jnp.exp(-jnp.abs(z))))
    small_ref[0] = s1
    cv_ref[0] = jnp.where(key_half(C_HEADS), 1.0, ckv).astype(BF16)

    aq_ref[0] = padded_heads(proj(G_AQ, A_HEADS // 2)).astype(BF16)
    cmp_ref[0] = proj(G_CMP, 1)
    slc = proj(G_SLC, 1)
    slc_ref[0] = slc.astype(BF16)
    slcv_ref[0] = jnp.where(key_half(1), 1.0, slc).astype(BF16)
    win_ref[0] = proj(G_WIN, 1).astype(BF16)
    bq = proj(G_BQ, len(B_GROUPS))
    for n in range(len(B_GROUPS)):
        bq_ref[0, n] = bq[:, _slot(n)]

    @pl.when(i == 0)
    def _():
        carry_ref[...] = jnp.zeros_like(carry_ref)

    blk = min(LANES, tm)
    row = lax.broadcasted_iota(jnp.int32, (blk, blk), 0)
    col = lax.broadcasted_iota(jnp.int32, (blk, blk), 1)
    tri = (row >= col).astype(BF16)
    running = carry_ref[0:1, :]
    blocks = []
    for b0 in range(0, tm, blk):
        blocks.append(sum(_dot(tri, part[b0:b0 + blk]) for part in logf_parts) + running)
        running = blocks[-1][blk - 1:blk, :]
    cum = jnp.concatenate(blocks, axis=0)
    carry_ref[0:1, :] = running

    hi, mid, lo = split3(-LOG2E * cum)
    r = lax.broadcasted_iota(jnp.int32, (1, LANES), 1) - LANE_FORGET
    in_terms = (r >= 0) & (r < N_TERMS * C_HEADS)

    def is_term(n):
        hit = r == n
        for hd in range(1, C_HEADS):
            hit = hit | (r == N_TERMS * hd + n)
        return hit

    terms = jnp.where(is_term(0), hi.astype(F32), jnp.where(is_term(1), mid.astype(F32), lo.astype(F32)))
    lane4 = lax.broadcasted_iota(jnp.int32, (1, C_HEADS * LANES), 1)
    own = (lane4 & (LANES - 1)) - LANE_FORGET - N_TERMS * lax.shift_right_logical(lane4, 7)
    own_terms = (own >= 0) & (own < N_TERMS)
    ck_ref[0] = jnp.where(jnp.concatenate([in_terms] * C_HEADS, axis=1),
                          jnp.concatenate([terms] * C_HEADS, axis=1), ckv).astype(BF16)

    tab = tab_ref[...]
    c_q, s_q, c_k, s_k = (tab[:, _slot(n)] for n in range(4))
    qab = _dot(qn, wq_ref[...])
    kv_up = _dot(kvn, wkv_ref[...])
    k_rot = s1 * c_k + s2 * s_k
    low1 = key_half(1)
    for hd in range(D_HEADS):
        q_rot = qab[:, _slot(hd)] * c_q + qab[:, _slot(D_HEADS + hd)] * s_q
        dq_ref[0, :, _slot(hd)] = q_rot.astype(BF16)
        dk_ref[0, :, _slot(hd)] = jnp.where(low1, kv_up[:, _slot(hd)], k_rot).astype(BF16)
    dv_ref[0] = jnp.where(key_half(D_HEADS), 1.0, kv_up).astype(BF16)

    bkv = proj(G_BKV, B_HEADS)
    for n in range(B_HEADS):
        bkv_ref[0, n] = bkv[:, _slot(n)]
    cq_ref[0] = jnp.where(own_terms, 1.0, padded_heads(proj(G_CQ, C_HEADS // 2))).astype(BF16)


def _inproj(x, gain, w_ext, tab, fbias, qn_g, kvn_g, wq, wkv):
    b, s, d = x.shape
    tm = min(TM_PROJ, s)

    def out(kind, n, dtype):
        if kind == "rows":
            return (pl.BlockSpec((1, tm, n * LANES), lambda bi, i: (bi, i, 0)),
                    jax.ShapeDtypeStruct((b, s, n * LANES), dtype))
        return (pl.BlockSpec((1, n, tm, LANES), lambda bi, i: (bi, 0, i, 0)),
                jax.ShapeDtypeStruct((b, n, s, LANES), dtype))

    outs = [out("rows", A_HEADS, BF16), out("rows", 1, F32), out("rows", 1, BF16), out("rows", 1, BF16),
            out("rows", 1, BF16), out("slots", len(B_GROUPS), F32), out("slots", B_HEADS, F32),
            out("rows", C_HEADS, BF16), out("rows", C_HEADS, BF16), out("rows", C_HEADS, BF16),
            out("rows", D_HEADS, BF16), out("rows", D_HEADS, BF16), out("rows", D_HEADS, BF16),
            out("rows", 1, F32)]
    consts = (gain, w_ext)
    consts2 = (fbias, qn_g, kvn_g, wq, wkv)
    return pl.pallas_call(
        _inproj_kernel,
        grid=(b, s // tm),
        in_specs=[pl.BlockSpec((1, tm, d), lambda bi, i: (bi, i, 0))]
        + [_const_spec(c.shape) for c in consts]
        + [pl.BlockSpec((tm, 4 * LANES), lambda bi, i: (i, 0))]
        + [_const_spec(c.shape) for c in consts2],
        out_specs=[o[0] for o in outs],
        out_shape=[o[1] for o in outs],
        scratch_shapes=[pltpu.VMEM((8, LANES), F32)],
        name="inproj",
        compiler_params=_cparams(2),
    )(x, *consts, tab, *consts2)


def _gelu_tanh(x):
    return 0.5 * x * (1.0 + jnp.tanh(math.sqrt(2.0 / math.pi) * (x + 0.044715 * (x * x * x))))


def _compress_kernel(kc_ref, vc_ref, pos_ref, w1k_ref, w1v_ref, w2k_ref, w2v_ref, out_ref):
    nc = kc_ref.shape[1]
    half = A_CMP_STRIDE * HEAD_DIM
    pos_a = pos_ref[0:1, :]
    pos_b = pos_ref[1:2, :]

    def phi(chunks, w1_ref, w2_ref):
        u = _dot((chunks + pos_a).astype(BF16), w1_ref[0:half, :])
        v = _dot((chunks + pos_b).astype(BF16), w1_ref[half:2 * half, :])
        hid = u + pltpu.roll(v, nc - 1, 0)
        return _dot(_gelu_tanh(hid).astype(BF16), w2_ref[...])

    out_ref[0] = (phi(kc_ref[0], w1k_ref, w2k_ref) + phi(vc_ref[0], w1v_ref, w2v_ref)).astype(BF16)


def _compress(kc_in, vc_in, pos2, w1k, w1v, w2k_pad, w2v_pad):
    b, nc, width = kc_in.shape
    blk = pl.BlockSpec((1, nc, width), lambda bi: (bi, 0, 0))
    return pl.pallas_call(
        _compress_kernel,
        grid=(b,),
        in_specs=[blk, blk, _const_spec(pos2.shape), _const_spec(w1k.shape), _const_spec(w1v.shape),
                  _const_spec(w2k_pad.shape), _const_spec(w2v_pad.shape)],
        out_specs=pl.BlockSpec((1, nc, LANES), lambda bi: (bi, 0, 0)),
        out_shape=jax.ShapeDtypeStruct((b, nc, LANES), BF16),
        name="compress",
        compiler_params=_cparams(1),
    )(kc_in, vc_in, pos2, w1k, w1v, w2k_pad, w2v_pad)


def _nsa_cmp_win_kernel(q_ref, kvc_ref, kvw_ref, small_ref, ovl_ref, bias_ref,
                        part_ref, sel_ref, *, n_top):
    tq = min(TQ_NSA, q_ref.shape[1])
    for u in range(q_ref.shape[1] // tq):
        _nsa_cmp_win_tile(u * tq, tq, q_ref, kvc_ref, kvw_ref, small_ref, ovl_ref, bias_ref,
                          part_ref, sel_ref, n_top)


def _nsa_cmp_win_tile(r0, tq, q_ref, kvc_ref, kvw_ref, small_ref, ovl_ref, bias_ref, part_ref, sel_ref, n_top):
    nc = kvc_ref.shape[1]
    n_sel = ovl_ref.shape[0]
    t0 = pl.multiple_of(pl.program_id(1) * q_ref.shape[1] + r0, tq)
    tile = slice(r0, r0 + tq)
    q4 = jnp.concatenate([q_ref[0, tile, _slot(h)] for h in range(A_HEADS)], axis=0)
    rows = A_HEADS * tq

    kvc = kvc_ref[0]
    s = _dot_nt(q4, kvc)
    t_row = t0 + lax.broadcasted_iota(jnp.int32, (tq, nc), 0)
    c_end = lax.broadcasted_iota(jnp.int32, (tq, nc), 1) * A_CMP_STRIDE + (A_CMP_LEN - 1)
    valid = jnp.concatenate([c_end <= t_row] * A_HEADS, axis=0)
    s = jnp.where(valid, s, NEG_INF)
    m = jnp.max(s, axis=1, keepdims=True)
    p = jnp.where(valid, jnp.exp2(s - m), 0.0)
    den = jnp.maximum(jnp.sum(p, axis=1, keepdims=True), 1e-30)
    p = p / den
    o_c = _dot(p.astype(BF16), kvc)

    p_sum = p[0:tq] + p[tq:2 * tq] + p[2 * tq:3 * tq] + p[3 * tq:4 * tq]
    p_hi = p_sum.astype(BF16)
    p_lo = (p_sum - p_hi.astype(F32)).astype(BF16)
    imp_t = _dot_nt(ovl_ref[...], p_hi) + _dot_nt(ovl_ref[...], p_lo)

    j_idx = lax.broadcasted_iota(jnp.int32, (n_sel, tq), 0)
    cur = lax.shift_right_logical(t0 + lax.broadcasted_iota(jnp.int32, (n_sel, tq), 1), 6)
    forced = (j_idx == 0) | (j_idx == cur) | (j_idx == cur - 1)
    imp_t = jnp.where(forced, FORCE_SCORE, jnp.where(j_idx > cur, -FORCE_SCORE, imp_t))
    sub8 = 8
    groups = [imp_t[v * sub8:(v + 1) * sub8] for v in range(n_sel // sub8)]
    ranks = [jnp.zeros((sub8, tq), F32) for _ in groups]
    in_group = lax.broadcasted_iota(jnp.int32, (sub8, tq), 0)
    for c in range(n_sel):
        row_c = imp_t[c:c + 1, :]
        for v, grp in enumerate(groups):
            if v > c // sub8:
                beats = row_c >= grp
            elif v < c // sub8:
                beats = row_c > grp
            else:
                beats = (row_c > grp) | ((row_c == grp) & (in_group > c % sub8))
            ranks[v] = ranks[v] + jnp.where(beats, 1.0, 0.0)
    rank = jnp.concatenate(ranks, axis=0)
    sel_t = jnp.where(rank < float(n_top), 1.0, 0.0).astype(BF16)
    sel_ref[0, :, tile] = sel_t

    span = A_WINDOW + tq
    kvw = kvw_ref[0, pl.ds(t0, span), :]
    kpos = t0 - A_WINDOW + lax.broadcasted_iota(jnp.int32, (1, span), 1)
    sw = _dot_nt(q4, kvw) + bias_ref[...] + jnp.where(kpos >= 0, 0.0, NEG_INF)
    mw = jnp.max(sw, axis=1, keepdims=True)
    pw = jnp.exp2(sw - mw)
    o_w = _dot(pw.astype(BF16), kvw) / jnp.sum(pw, axis=1, keepdims=True)

    g = _sigmoid(small_ref[0, tile])
    gated = []
    for h in range(A_HEADS):
        g_c = g[:, LANE_GATE + 3 * h:LANE_GATE + 3 * h + 1]
        g_w = g[:, LANE_GATE + 3 * h + 2:LANE_GATE + 3 * h + 3]
        gated.append(g_c * o_c[h * tq:(h + 1) * tq] + g_w * o_w[h * tq:(h + 1) * tq])
    low = lax.broadcasted_iota(jnp.int32, (tq, LANES), 1) < HEAD_DIM
    for pair in range(A_HEADS // 2):
        part_ref[0, tile, _slot(pair)] = jnp.where(low, pltpu.roll(gated[2 * pair], HEAD_DIM, 1),
                                                   gated[2 * pair + 1])


def _nsa_cmp_win(aq, kvc, kvw_pad, small, ovl, bias_w):
    b, s, _ = aq.shape
    tq = min(TQ_NSA_STEP, s)
    nc = kvc.shape[1]
    n_sel = ovl.shape[0]
    n_top = min(A_SEL_TOPK, n_sel)
    return pl.pallas_call(
        functools.partial(_nsa_cmp_win_kernel, n_top=n_top),
        grid=(b, s // tq),
        in_specs=[pl.BlockSpec((1, tq, A_HEADS * LANES), lambda bi, i: (bi, i, 0)),
                  pl.BlockSpec((1, nc, LANES), lambda bi, i: (bi, 0, 0)),
                  pl.BlockSpec((1, s + A_WINDOW, LANES), lambda bi, i: (bi, 0, 0)),
                  pl.BlockSpec((1, tq, LANES), lambda bi, i: (bi, i, 0)),
                  _const_spec(ovl.shape),
                  _const_spec(bias_w.shape)],
        out_specs=[pl.BlockSpec((1, tq, A_HEADS * HEAD_DIM), lambda bi, i: (bi, i, 0)),
                   pl.BlockSpec((1, n_sel, tq), lambda bi, i: (bi, 0, i))],
        out_shape=[jax.ShapeDtypeStruct((b, s, A_HEADS * HEAD_DIM), F32),
                   jax.ShapeDtypeStruct((b, n_sel, s), BF16)],
        name="nsa_cmp_win",
        compiler_params=_cparams(2),
    )(aq, kvc, kvw_pad, small, ovl, bias_w)


def _flash_kernel(*refs, mode, nh):
    if mode == "sel":
        q_ref, k_ref, v_ref, selt_ref, e_ref, bias_ref, part_ref, small_ref, o_ref, m_ref, acc_ref = refs
    else:
        q_ref, k_ref, v_ref, o_ref, m_ref, acc_ref = refs
    i = pl.program_id(1)
    tq = q_ref.shape[1]
    tk = tq
    shared_kv = mode == "sel"

    m_ref[...] = jnp.full(m_ref.shape, NEG_INF, F32)
    acc_ref[...] = jnp.zeros(acc_ref.shape, F32)

    lag = lax.broadcasted_iota(jnp.int32, (tq, tk), 0) - lax.broadcasted_iota(jnp.int32, (tq, tk), 1)

    def step(j, diag):
        k0 = pl.multiple_of(j * tk, tk)
        keep = (lag >= 0) if diag else None
        if mode == "sel":
            sel_keys = lax.dot_general(selt_ref[0], e_ref[:, pl.ds(k0, tk)], (((0,), (0,)), ((), ())),
                                       preferred_element_type=F32) > 0.5
            keep = (sel_keys & keep) if diag else sel_keys
        penalty = None if keep is None else jnp.where(keep, 0.0, NEG_INF)
        if mode == "sel":
            n_off, bt = bias_ref.shape[0], bias_ref.shape[2]

            def bias_tile(h):
                strips = []
                for a in range(tq // bt):
                    offs = [jnp.clip(i * (tq // bt) + a - j * (tk // bt) - c, 0, n_off - 1)
                            for c in range(tk // bt)]
                    strips.append(jnp.concatenate([bias_ref[o, h] for o in offs], axis=1))
                return jnp.concatenate(strips, axis=0)
        logits = []
        for h in range(nh):
            s = _dot_nt(q_ref[0, :, _slot(h)], k_ref[0, pl.ds(k0, tk), _slot(0 if shared_kv else h)])
            if mode == "sel":
                s = s + bias_tile(h)
            if penalty is not None:
                s = s + penalty
            logits.append(s)
        probs = []
        for h in range(nh):
            m_prev = m_ref[h]
            m_new = jnp.maximum(m_prev, jnp.max(logits[h], axis=1, keepdims=True))
            p = jnp.exp2(logits[h] - jnp.concatenate([m_new] * (tk // LANES), axis=1))
            alpha = jnp.exp2(m_prev - m_new)
            m_ref[h] = m_new
            probs.append((p.astype(BF16), alpha))
        for h in range(nh):
            p, alpha = probs[h]
            acc_ref[h] = alpha * acc_ref[h] + _dot(p, v_ref[0, pl.ds(k0, tk), _slot(0 if shared_kv else h)])

    def full_tile(j, carry):
        step(j, False)
        return carry

    lax.fori_loop(0, i, full_tile, 0)
    step(i, True)

    low = lax.broadcasted_iota(jnp.int32, (tq, LANES), 1) < HEAD_DIM
    if mode == "sel":
        g = _sigmoid(small_ref[0])
    for pair in range(nh // 2):
        outs = []
        for h in (2 * pair, 2 * pair + 1):
            acc = acc_ref[h]
            outs.append(acc / acc[:, 0:1])
        o_pair = jnp.where(low, pltpu.roll(outs[0], HEAD_DIM, 1), outs[1])
        if mode == "sel":
            gate = [g[:, LANE_GATE + 3 * h + 1:LANE_GATE + 3 * h + 2] for h in (2 * pair, 2 * pair + 1)]
            o_pair = part_ref[0, :, _slot(pair)] + jnp.where(low, gate[0], gate[1]) * o_pair
        o_ref[0, :, _slot(pair)] = o_pair.astype(BF16)


def _flash(mode, q, k, v, sel_t=None, expand=None, bias=None, part=None, small=None):
    b, s, qw = q.shape
    nh = qw // LANES
    tq = min(TQ_FLASH, s)
    ow = nh * HEAD_DIM

    def rows(width):
        return pl.BlockSpec((1, tq, width), lambda bi, i: (bi, i, 0))

    def seq_spec(arr):
        return pl.BlockSpec((1,) + arr.shape[1:], lambda bi, i: (bi, 0, 0))

    args = (q, k, v)
    in_specs = [rows(qw), seq_spec(k), seq_spec(v)]
    if mode == "sel":
        args += (sel_t, expand, bias, part, small)
        in_specs += [pl.BlockSpec((1, sel_t.shape[1], tq), lambda bi, i: (bi, 0, i)),
                     _const_spec(expand.shape), _const_spec(bias.shape), rows(ow), rows(LANES)]
    return pl.pallas_call(
        functools.partial(_flash_kernel, mode=mode, nh=nh),
        grid=(b, s // tq),
        in_specs=in_specs,
        out_specs=rows(ow),
        out_shape=jax.ShapeDtypeStruct((b, s, ow), BF16),
        scratch_shapes=[pltpu.VMEM((nh, tq, LANES), F32)] * 2,
        name="flash_" + mode,
        compiler_params=_cparams(2),
    )(*args)


def _dilated_kernel(q_ref, kvp_ref, kvc_ref, bias_ref, o_ref, o_sc, l_sc):
    i = pl.program_id(1)
    rows = q_ref.shape[2]
    tb = TQ_DIL
    low = lax.broadcasted_iota(jnp.int32, (tb, LANES), 1) < HEAD_DIM
    col = lax.broadcasted_iota(jnp.int32, (tb, 2 * tb), 1)
    in_seq = (col >= tb) | (i > 0)
    for g, (_, dil) in enumerate(B_GROUPS):
        span = dil * tb
        for r in range(dil):
            for c in range(rows // span):
                cur = pl.ds(r + c * span, tb, stride=dil)
                if c > 0:
                    prev_ref, prev = kvc_ref, pl.ds(r + (c - 1) * span, tb, stride=dil)
                else:
                    prev_ref, prev = kvp_ref, pl.ds(rows - span + r, tb, stride=dil)
                q2 = q_ref[0, g, cur, :]
                outs, lses = [], []
                for h in range(B_HPG):
                    q_h = jnp.where(low if h == 0 else ~low, q2, 0.0).astype(BF16)
                    head = B_HPG * g + h
                    kv = jnp.concatenate([prev_ref[0, head, prev, :], kvc_ref[0, head, cur, :]],
                                         axis=0).astype(BF16)
                    s = _dot_nt(q_h, kv) + bias_ref[g, h]
                    if c == 0:
                        s = jnp.where(in_seq, s, NEG_INF)
                    m = jnp.max(s, axis=1, keepdims=True)
                    p = jnp.exp2(s - m)
                    den = jnp.sum(p, axis=1, keepdims=True)
                    outs.append(_dot(p.astype(BF16), kv) / den)
                    lses.append(m + jnp.log2(den))
                o_sc[g, cur, :] = jnp.where(low, outs[1], outs[0])
                l_sc[g, cur, :] = jnp.where(low, lses[1], lses[0])
    step = 2 * tb
    for t in range(rows // step):
        rs = slice(t * step, (t + 1) * step)
        l0, l1, l2 = l_sc[0, rs, :], l_sc[1, rs, :], l_sc[2, rs, :]
        mx = jnp.maximum(jnp.maximum(l0, l1), l2)
        e0, e1, e2 = jnp.exp2(l0 - mx), jnp.exp2(l1 - mx), jnp.exp2(l2 - mx)
        mixed = (e0 * o_sc[0, rs, :] + e1 * o_sc[1, rs, :] + e2 * o_sc[2, rs, :]) / (e0 + e1 + e2)
        o_ref[0, rs, :] = mixed.astype(BF16)


def _dilated(bq, bkv, bias):
    b, _, s, _ = bq.shape
    rows = TQ_DIL * B_GROUPS[-1][1]
    n_groups = len(B_GROUPS)

    def tile(n_slots, prev):
        if prev:
            return pl.BlockSpec((1, n_slots, rows, LANES), lambda bi, i: (bi, 0, jnp.maximum(i - 1, 0), 0))
        return pl.BlockSpec((1, n_slots, rows, LANES), lambda bi, i: (bi, 0, i, 0))

    return pl.pallas_call(
        _dilated_kernel,
        grid=(b, s // rows),
        in_specs=[tile(bq.shape[1], False), tile(bkv.shape[1], True), tile(bkv.shape[1], False),
                  _const_spec(bias.shape)],
        out_specs=pl.BlockSpec((1, rows, LANES), lambda bi, i: (bi, i, 0)),
        out_shape=jax.ShapeDtypeStruct((b, s, LANES), BF16),
        scratch_shapes=[pltpu.VMEM((n_groups, rows, LANES), F32)] * 2,
        name="dilated",
        compiler_params=_cparams(2),
    )(bq, bkv, bkv, bias)


def _merge_kernel(x_ref, oa_ref, ob_ref, oc_ref, od_ref,
                  gpre_ref, gpost_ref, wg_ref, wba_ref, wbb_ref, wbc_ref, wbd_ref, wo_ref, out_ref):
    branches = ((oa_ref[0], wba_ref), (ob_ref[0], wbb_ref), (oc_ref[0], wbc_ref), (od_ref[0], wbd_ref))
    projected = [_dot(o_n, wb_ref[...]) for o_n, wb_ref in branches]
    x = x_ref[0]
    h = _rms(x, gpre_ref[...]).astype(BF16)
    merged = None
    for n, proj_n in enumerate(projected):
        term = _sigmoid(_dot(h, wg_ref[n])) * proj_n
        merged = term if merged is None else merged + term
    z = _dot(merged.astype(BF16), wo_ref[...])
    out_ref[0] = x + _rms(z, gpost_ref[...])


def _merge(x, oa, ob, oc, od, gpre, gpost, wg, wba, wbb, wbc, wbd, wo):
    b, s, d = x.shape
    tm = min(TM_PROJ, s)

    def rows(arr):
        return pl.BlockSpec((1, tm, arr.shape[2]), lambda bi, i: (bi, i, 0))

    acts = (x, oa, ob, oc, od)
    consts = (gpre, gpost, wg, wba, wbb, wbc, wbd, wo)
    return pl.pallas_call(
        _merge_kernel,
        grid=(b, s // tm),
        in_specs=[rows(a) for a in acts] + [_const_spec(c.shape) for c in consts],
        out_specs=rows(x),
        out_shape=jax.ShapeDtypeStruct(x.shape, F32),
        name="merge",
        compiler_params=_cparams(2),
    )(*acts, *consts)


def _ffn_kernel(x_ref, gpre_ref, gpost_ref, wup_ref, cw_ref, cb_ref, wdn_ref, out_ref, tail_ref, *, cuts):
    i = pl.program_id(1)
    tm = x_ref.shape[1]
    d_ff = wdn_ref.shape[0]
    x = x_ref[0]
    h = _rms(x, gpre_ref[...]).astype(BF16)

    @pl.when(i == 0)
    def _():
        tail_ref[...] = jnp.zeros_like(tail_ref)

    def conv(c0, chunk):
        u = _dot(h, wup_ref[:, c0:c0 + chunk])
        ue = jnp.concatenate([tail_ref[:, c0:c0 + chunk], u], axis=0)
        tail_ref[:, c0:c0 + chunk] = u[tm - CONV_HALO:]
        w = cw_ref[:, c0:c0 + chunk]
        out = cb_ref[:, c0:c0 + chunk] + w[0:1] * pltpu.roll(ue, 2, 0)[CONV_HALO:]
        out = out + w[1:2] * pltpu.roll(ue, 1, 0)[CONV_HALO:]
        return out + w[2:3] * u

    y = jnp.zeros((tm, x.shape[1]), F32)
    for c0, c1 in zip(cuts[:-1], cuts[1:]):
        gate = conv(c0, c1 - c0)
        val = conv(d_ff + c0, c1 - c0)
        act = gate * _sigmoid(gate) * val
        y = y + _dot(act.astype(BF16), wdn_ref[c0:c1, :])
    out_ref[0] = x + _rms(y, gpost_ref[...])


def _ffn(x, gpre, gpost, w_up, conv_w, conv_b, w_down):
    b, s, d = x.shape
    tm = min(TM_PROJ, s)
    d_ff = w_down.shape[0]
    n_tiles = d_ff // MXU_WIDTH
    cuts = (0, (n_tiles // 2) * MXU_WIDTH, d_ff) if d_ff % MXU_WIDTH == 0 and n_tiles > 1 else (0, d_ff)
    return pl.pallas_call(
        functools.partial(_ffn_kernel, cuts=cuts),
        grid=(b, s // tm),
        in_specs=[pl.BlockSpec((1, tm, d), lambda bi, i: (bi, i, 0)),
                  _const_spec(gpre.shape), _const_spec(gpost.shape), _const_spec(w_up.shape),
                  _const_spec(conv_w.shape), _const_spec(conv_b.shape), _const_spec(w_down.shape)],
        out_specs=pl.BlockSpec((1, tm, d), lambda bi, i: (bi, i, 0)),
        out_shape=jax.ShapeDtypeStruct(x.shape, F32),
        scratch_shapes=[pltpu.VMEM((CONV_HALO, w_up.shape[1]), F32)],
        name="ffn",
        compiler_params=_cparams(2),
    )(x, gpre, gpost, w_up, conv_w, conv_b, w_down)


def _rel_bucket(dist):
    dist = jnp.maximum(dist, 0)
    d = jnp.maximum(dist, 1).astype(F32)
    large = BUCKET_EXACT + (jnp.log(d / BUCKET_EXACT) / math.log(BUCKET_MAX_DIST / BUCKET_EXACT)
                            * (N_BUCKETS - BUCKET_EXACT)).astype(jnp.int32)
    large = jnp.minimum(large, N_BUCKETS - 1)
    return jnp.where(dist < BUCKET_EXACT, dist, large)


def _in_proj_columns():
    src = np.full((N_SLOTS * LANES,), -1, np.int64)
    scl = np.ones((N_SLOTS * LANES,), np.float32)
    o_aq = 0
    o_akv = o_aq + A_HEADS * HEAD_DIM
    o_ag = o_akv + 6 * HEAD_DIM
    o_b = o_ag + 3 * A_HEADS
    o_c = o_b + 3 * B_HEADS * HEAD_DIM
    o_cf = o_c + 3 * C_HEADS * HEAD_DIM
    o_dq = o_cf + C_HEADS
    o_dkv = o_dq + D_Q_LORA
    o_dkr = o_dkv + D_KV_LORA
    e = np.arange(HEAD_DIM)
    q_scale = HEAD_DIM ** -0.5 * LOG2E

    def put(slot, lane0, cols, scale=1.0):
        dst = slot * LANES + lane0 + np.arange(len(cols))
        src[dst] = cols
        scl[dst] = scale

    for h in range(A_HEADS):
        put(G_AQ + h // 2, (h % 2) * HEAD_DIM, o_aq + h * HEAD_DIM + e, q_scale)
    for n, slot in enumerate((G_CMP, G_SLC, G_WIN)):
        put(slot, 0, o_akv + (2 * n) * HEAD_DIM + e)
        put(slot, HEAD_DIM, o_akv + (2 * n + 1) * HEAD_DIM + e)
    for h in range(B_HEADS):
        odd = h % B_HPG
        put(G_BQ + h // B_HPG, odd * HEAD_DIM, o_b + h * HEAD_DIM + e, q_scale)
        put(G_BKV + h, odd * HEAD_DIM, o_b + (B_HEADS + h) * HEAD_DIM + e)
        put(G_BKV + h, (1 - odd) * HEAD_DIM, o_b + (2 * B_HEADS + h) * HEAD_DIM + e)
    for h in range(C_HEADS):
        put(G_CQ + h // 2, (h % 2) * HEAD_DIM, o_c + h * HEAD_DIM + e, q_scale)
        put(G_CKV + h, 0, o_c + (C_HEADS + h) * HEAD_DIM + e)
        put(G_CKV + h, HEAD_DIM, o_c + (2 * C_HEADS + h) * HEAD_DIM + e)
    put(G_DQL, 0, o_dq + np.arange(D_Q_LORA))
    put(G_DKVL, 0, o_dkv + np.arange(D_KV_LORA))
    put(G_S1, LANE_GATE, o_ag + np.arange(3 * A_HEADS))
    put(G_S1, LANE_FORGET, o_cf + np.repeat(np.arange(C_HEADS), N_TERMS))
    put(G_S1, LANE_ROPE, o_dkr + np.arange(D_ROPE))
    half = D_ROPE // 2
    put(G_S2, LANE_ROPE, o_dkr + half + np.arange(half), -1.0)
    put(G_S2, LANE_ROPE + half, o_dkr + np.arange(half))
    return src, scl


def _gather_cols(w, src, scl):
    pieces = []
    n = len(src)
    a = 0
    while a < n:
        e = a + 1
        if src[a] < 0:
            while e < n and src[e] < 0:
                e += 1
            pieces.append(jnp.zeros((w.shape[0], e - a), w.dtype))
        else:
            while e < n and src[e] == src[e - 1] + 1 and scl[e] == scl[a]:
                e += 1
            run = w[:, int(src[a]):int(src[a]) + (e - a)]
            pieces.append(run if scl[a] == 1.0 else run * float(scl[a]))
        a = e
    return jnp.concatenate(pieces, axis=1)


def _mla_weight_columns():
    per_q = D_NOPE + D_ROPE
    half = D_ROPE // 2
    qa = np.full((D_HEADS * LANES,), -1, np.int64)
    qb = np.full((D_HEADS * LANES,), -1, np.int64)
    qb_s = np.ones((D_HEADS * LANES,), np.float32)
    for h in range(D_HEADS):
        qa[h * LANES + np.arange(per_q)] = h * per_q + np.arange(per_q)
        rot = h * per_q + D_NOPE
        qb[h * LANES + D_NOPE + np.arange(half)] = rot + half + np.arange(half)
        qb_s[h * LANES + D_NOPE + np.arange(half)] = -1.0
        qb[h * LANES + D_NOPE + half + np.arange(half)] = rot + np.arange(half)
    ones = np.ones_like(qb_s)
    return (qa, ones), (qb, qb_s)


def _rope_tables(s):
    inv_freq = ROPE_THETA ** (-jnp.arange(0, D_ROPE, 2, dtype=F32) / D_ROPE)
    ang = jnp.arange(s, dtype=F32)[:, None] * inv_freq[None, :]
    cos2 = jnp.concatenate([jnp.cos(ang)] * 2, axis=1)
    sin2 = jnp.concatenate([jnp.sin(ang)] * 2, axis=1)
    scale = (D_NOPE + D_ROPE) ** -0.5 * LOG2E
    z_lo = jnp.zeros((s, D_NOPE), F32)
    z_hi = jnp.zeros((s, LANES - D_NOPE - D_ROPE), F32)
    c_q = jnp.concatenate([jnp.full((s, D_NOPE), scale, F32), scale * cos2, z_hi], axis=1)
    s_q = jnp.concatenate([z_lo, scale * sin2, z_hi], axis=1)
    c_k = jnp.concatenate([z_lo, cos2, z_hi], axis=1)
    s_k = jnp.concatenate([z_lo, sin2, z_hi], axis=1)
    return jnp.concatenate([c_q, s_q, c_k, s_k], axis=1)


def _toeplitz(vec, n_rows, n_cols, off, step=1):
    base = vec[:, ::step]
    last = base.shape[1] - 1
    lo, hi = off - (n_cols - 1), off + n_rows - 1
    core = base[:, max(lo, 0):min(hi, last) + 1]
    front = jnp.repeat(base[:, :1], max(0, -lo), axis=1)
    back = jnp.repeat(base[:, last:], max(0, hi - last), axis=1)
    g = jnp.concatenate([front, core, back], axis=1)
    lg = n_rows + n_cols - 1
    t = jnp.tile(g, (1, n_rows + 1))[:, :n_rows * (lg + 1)].reshape(-1, n_rows, lg + 1)
    return t[:, :, :n_cols][:, :, ::-1]


def _bias_tables(rel_table, s):
    bucket = _rel_bucket(jnp.arange(s))
    hit = bucket[None, :, None] == jnp.arange(N_BUCKETS)[None, None, :]
    by_dist = jnp.sum(jnp.where(hit, LOG2E * rel_table.T[:, None, :], 0.0), axis=-1)
    tqa = min(TQ_NSA, s)
    dist = jnp.arange(tqa)[:, None] + A_WINDOW - jnp.arange(A_WINDOW + tqa)[None, :]
    ok = (dist >= 0) & (dist < A_WINDOW)
    bias_w = jnp.where(ok[None], _toeplitz(by_dist[:A_HEADS], tqa, A_WINDOW + tqa, A_WINDOW), NEG_INF)
    bias_w = bias_w.reshape(A_HEADS * tqa, A_WINDOW + tqa)
    tf = min(BIAS_TILE, s)
    n_off = min(s // tf, -(-(BUCKET_MAX_DIST + tf - 1) // tf) + 1)
    bias_s = jnp.stack([_toeplitz(by_dist[:A_HEADS], tf, tf, n * tf) for n in range(n_off)])
    m = jnp.arange(TQ_DIL)[:, None] + TQ_DIL - jnp.arange(2 * TQ_DIL)[None, :]
    bias_d = []
    for g, (window, dil) in enumerate(B_GROUPS):
        ok = (m >= 0) & (m <= window // dil)
        heads = by_dist[A_HEADS + g * B_HPG:A_HEADS + (g + 1) * B_HPG]
        bias_d.append(jnp.where(ok[None], _toeplitz(heads, TQ_DIL, 2 * TQ_DIL, TQ_DIL, dil), NEG_INF))
    return bias_w, bias_s, jnp.stack(bias_d)


def _selection_constants(s):
    nc = s // A_CMP_STRIDE
    n_cmp = (s - A_CMP_LEN) // A_CMP_STRIDE + 1
    n_sel = s // A_SEL_BLOCK
    c = np.arange(nc)[:, None]
    j = np.arange(n_sel)[None, :]
    c_start = c * A_CMP_STRIDE
    overlap = ((c_start <= j * A_SEL_BLOCK + A_SEL_BLOCK - 1) & (c_start + A_CMP_LEN - 1 >= j * A_SEL_BLOCK)
               & (c < n_cmp))
    expand = (np.arange(s)[None, :] // A_SEL_BLOCK) == np.arange(n_sel)[:, None]
    return jnp.asarray(overlap.T, BF16), jnp.asarray(expand, BF16)


def kernel(x, rel_bias_table, norm_attn_pre, norm_attn_post, norm_ffn_pre, norm_ffn_post, w_in, nsa_cmp_pos, nsa_phi_k_w1, nsa_phi_k_w2, nsa_phi_v_w1, nsa_phi_v_w2, fox_forget_bias, mla_q_norm, mla_kv_norm, mla_w_uq, mla_w_ukv, w_branch_a, w_branch_b, w_branch_c, w_branch_d, w_merge_gate, w_o, ffn_w_up, ffn_conv_w, ffn_conv_b, ffn_w_down):
    b, s, d = x.shape
    depth = w_in.shape[0]
    assert s % (TQ_DIL * B_GROUPS[-1][1]) == 0, "every dilation class needs whole 128-row tiles"

    in_src, in_scl = _in_proj_columns()
    assert D_NOPE + D_VDIM == LANES, "an MLA head's [k_nope | v] up-projection fills one slot"
    (qa_i, qa_s), (qb_i, qb_s) = _mla_weight_columns()
    rope_tab = _rope_tables(s)
    bias_w, bias_s, bias_d = _bias_tables(rel_bias_table, s)
    overlap, expand = _selection_constants(s)
    half = A_CMP_STRIDE * HEAD_DIM

    for l in range(depth):
        w_ext = _gather_cols(w_in[l], in_src, in_scl).astype(BF16)
        fbias = jnp.zeros((1, LANES), F32).at[0, LANE_FORGET:LANE_FORGET + N_TERMS * C_HEADS].set(
            jnp.repeat(fox_forget_bias[l], N_TERMS))
        wq = jnp.concatenate([_gather_cols(mla_w_uq[l], qa_i, qa_s), _gather_cols(mla_w_uq[l], qb_i, qb_s)],
                             axis=1).astype(BF16)
        (aq, cmp, slc, slc_v, win, bq, bkv, cq, ck, c_v, dq, dk, d_v, small) = _inproj(
            x, norm_attn_pre[l][None], w_ext, rope_tab, fbias,
            mla_q_norm[l][None], mla_kv_norm[l][None], wq, mla_w_ukv[l].astype(BF16))

        nc = s // A_CMP_STRIDE
        w2k_pad = jnp.pad(nsa_phi_k_w2[l], ((0, 0), (0, LANES - HEAD_DIM))).astype(BF16)
        w2v_pad = jnp.pad(nsa_phi_v_w2[l], ((0, 0), (LANES - HEAD_DIM, 0))).astype(BF16)
        kvc = _compress(cmp[..., :HEAD_DIM].reshape(b, nc, half), cmp[..., HEAD_DIM:].reshape(b, nc, half),
                        nsa_cmp_pos[l].reshape(2, half),
                        nsa_phi_k_w1[l].astype(BF16), nsa_phi_v_w1[l].astype(BF16), w2k_pad, w2v_pad)
        kvw_pad = jnp.pad(win, ((0, 0), (A_WINDOW, 0), (0, 0)))
        part, sel_t = _nsa_cmp_win(aq, kvc, kvw_pad, small, overlap, bias_w)
        o_a = _flash("sel", aq, slc, slc_v, sel_t=sel_t, expand=expand, bias=bias_s, part=part, small=small)

        o_b = _dilated(bq, bkv, bias_d)

        o_c = _flash("fox", cq, ck, c_v)
        o_d = _flash("mla", dq, dk, d_v)

        x = _merge(x, o_a, o_b, o_c, o_d,
                   norm_attn_pre[l][None], norm_attn_post[l][None],
                   w_merge_gate[l].astype(BF16),
                   w_branch_a[l].astype(BF16),
                   jnp.concatenate([w_branch_b[l][HEAD_DIM:], w_branch_b[l][:HEAD_DIM]]).astype(BF16),
                   w_branch_c[l].astype(BF16),
                   w_branch_d[l].astype(BF16),
                   w_o[l].astype(BF16))
        x = _ffn(x, norm_ffn_pre[l][None], norm_ffn_post[l][None], ffn_w_up[l].astype(BF16),
                 ffn_conv_w[l], ffn_conv_b[l][None], ffn_w_down[l].astype(BF16))
    return x
```

```python
import functools
import math

import numpy as np
import jax
import jax.numpy as jnp
from jax import lax
from jax.experimental import pallas as pl
from jax.experimental.pallas import tpu as pltpu

F32 = jnp.float32
BF16 = jnp.bfloat16

LANES = 128
MXU_WIDTH = 256
HEAD_DIM = 64
RMS_EPS = 1e-6
NEG_INF = -1e30
FORCE_SCORE = 1e9
LOG2E = math.log2(math.e)
VMEM_LIMIT_BYTES = 56 * 1024 * 1024

N_BUCKETS = 32
BUCKET_EXACT = 16
BUCKET_MAX_DIST = 2048

A_HEADS = 4
A_CMP_LEN = 32
A_CMP_STRIDE = 16
A_SEL_BLOCK = 64
A_SEL_TOPK = 16
A_WINDOW = 512
B_GROUPS = ((128, 1), (512, 4), (2048, 16))
B_HPG = 2
B_HEADS = B_HPG * len(B_GROUPS)
C_HEADS = 4
D_HEADS = 4
D_Q_LORA = 256
D_KV_LORA = 128
D_NOPE = 64
D_ROPE = 32
D_VDIM = 64
ROPE_THETA = 10000.0

G_AQ = 0
G_CMP = G_AQ + A_HEADS // 2
G_SLC = G_CMP + 1
G_WIN = G_SLC + 1
G_BQ = G_WIN + 1
G_BKV = G_BQ + len(B_GROUPS)
G_CQ = G_BKV + B_HEADS
G_CKV = G_CQ + C_HEADS // 2
G_DQL = G_CKV + C_HEADS
G_DKVL = G_DQL + 2
G_S1 = G_DKVL + 1
G_S2 = G_S1 + 1
N_SLOTS = G_S2 + 1
LANE_GATE = 0
LANE_FORGET = 96
N_TERMS = 3
LANE_ROPE = 64

TM_PROJ = 512
TQ_NSA_STEP = 512
TQ_FLASH = 512
CHUNK_SLOTS = 8
BIAS_TILE = 256
TQ_NSA = 128
TQ_DIL = 128
CONV_HALO = 8


def _cparams(n_grid):
    return pltpu.CompilerParams(dimension_semantics=("arbitrary",) * n_grid,
                                vmem_limit_bytes=VMEM_LIMIT_BYTES)


def _const_spec(shape):
    nd = len(shape)
    return pl.BlockSpec(shape, lambda *_: (0,) * nd, pipeline_mode=pl.Buffered(1))


def _rms(x, gain):
    return x * lax.rsqrt(jnp.mean(x * x, axis=-1, keepdims=True) + RMS_EPS) * gain


def _sigmoid(x):
    return 1.0 / (1.0 + jnp.exp(-x))


def _dot(a, b):
    return jnp.dot(a, b, preferred_element_type=F32)


def _dot_nt(a, b):
    return lax.dot_general(a, b, (((1,), (1,)), ((), ())), preferred_element_type=F32)


def _slot(h):
    return slice(h * LANES, (h + 1) * LANES)


def _inproj_kernel(x_ref, g_ref, w_ref, tab_ref, fb_ref, qn_ref, kvn_ref, wq_ref, wkv_ref,
                   aq_ref, cmp_ref, slc_ref, slcv_ref, win_ref, bq_ref, bkv_ref, cq_ref, ck_ref, cv_ref,
                   dq_ref, dk_ref, dv_ref, small_ref, carry_ref):
    i = pl.program_id(1)
    tm = x_ref.shape[1]
    h = _rms(x_ref[0], g_ref[...]).astype(BF16)

    n_slots = w_ref.shape[1] // LANES
    wide = {}

    def chunk(c):
        if c not in wide:
            c0 = c * CHUNK_SLOTS
            wide[c] = _dot(h, w_ref[:, c0 * LANES:min(c0 + CHUNK_SLOTS, n_slots) * LANES])
        return wide[c]

    def proj(g0, n):
        pieces = []
        g = g0
        while g < g0 + n:
            c, off = divmod(g, CHUNK_SLOTS)
            take = min(g0 + n - g, CHUNK_SLOTS - off)
            pieces.append(chunk(c)[:, off * LANES:(off + take) * LANES])
            g += take
        return pieces[0] if len(pieces) == 1 else jnp.concatenate(pieces, axis=1)

    def key_half(n):
        return (lax.broadcasted_iota(jnp.int32, (tm, n * LANES), 1) & (LANES - 1)) < HEAD_DIM

    def padded_heads(pairs):
        slots = []
        for n in range(pairs.shape[1] // LANES):
            pair = pairs[:, _slot(n)]
            slots += [jnp.where(key_half(1), pair, 0.0),
                      jnp.where(key_half(1), pltpu.roll(pair, HEAD_DIM, 1), 0.0)]
        return jnp.concatenate(slots, axis=1)

    def split3(v):
        hi = v.astype(BF16)
        r1 = v - hi.astype(F32)
        mid = r1.astype(BF16)
        return hi, mid, (r1 - mid.astype(F32)).astype(BF16)

    s1 = proj(G_S1, 1)
    s2 = proj(G_S2, 1)
    ckv = proj(G_CKV, C_HEADS)
    qn = _rms(proj(G_DQL, 2), qn_ref[...]).astype(BF16)
    kvn = _rms(proj(G_DKVL, 1), kvn_ref[...]).astype(BF16)
    z = s1 + fb_ref[...]
    logf_parts = split3(jnp.minimum(z, 0.0) - jnp.log(1.0 + jnp.exp(-jnp.abs(z))))
    small_ref[0] = s1
    cv_ref[0] = jnp.where(key_half(C_HEADS), 1.0, ckv).astype(BF16)

    aq_ref[0] = padded_heads(proj(G_AQ, A_HEADS // 2)).astype(BF16)
    cmp_ref[0] = proj(G_CMP, 1)
    slc = proj(G_SLC, 1)
    slc_ref[0] = slc.astype(BF16)
    slcv_ref[0] = jnp.where(key_half(1), 1.0, slc).astype(BF16)
    win_ref[0] = proj(G_WIN, 1).astype(BF16)
    bq = proj(G_BQ, len(B_GROUPS))
    for n in range(len(B_GROUPS)):
        bq_ref[0, n] = bq[:, _slot(n)]

    @pl.when(i == 0)
    def _():
        carry_ref[...] = jnp.zeros_like(carry_ref)

    blk = min(LANES, tm)
    row = lax.broadcasted_iota(jnp.int32, (blk, blk), 0)
    col = lax.broadcasted_iota(jnp.int32, (blk, blk), 1)
    tri = (row >= col).astype(BF16)
    running = carry_ref[0:1, :]
    blocks = []
    for b0 in range(0, tm, blk):
        blocks.append(sum(_dot(tri, part[b0:b0 + blk]) for part in logf_parts) + running)
        running = blocks[-1][blk - 1:blk, :]
    cum = jnp.concatenate(blocks, axis=0)
    carry_ref[0:1, :] = running

    hi, mid, lo = split3(-LOG2E * cum)
    r = lax.broadcasted_iota(jnp.int32, (1, LANES), 1) - LANE_FORGET
    in_terms = (r >= 0) & (r < N_TERMS * C_HEADS)

    def is_term(n):
        hit = r == n
        for hd in range(1, C_HEADS):
            hit = hit | (r == N_TERMS * hd + n)
        return hit

    terms = jnp.where(is_term(0), hi.astype(F32), jnp.where(is_term(1), mid.astype(F32), lo.astype(F32)))
    lane4 = lax.broadcasted_iota(jnp.int32, (1, C_HEADS * LANES), 1)
    own = (lane4 & (LANES - 1)) - LANE_FORGET - N_TERMS * lax.shift_right_logical(lane4, 7)
    own_terms = (own >= 0) & (own < N_TERMS)
    ck_ref[0] = jnp.where(jnp.concatenate([in_terms] * C_HEADS, axis=1),
                          jnp.concatenate([terms] * C_HEADS, axis=1), ckv).astype(BF16)

    tab = tab_ref[...]
    c_q, s_q, c_k, s_k = (tab[:, _slot(n)] for n in range(4))
    qab = _dot(qn, wq_ref[...])
    kv_up = _dot(kvn, wkv_ref[...])
    k_rot = s1 * c_k + s2 * s_k
    low1 = key_half(1)
    for hd in range(D_HEADS):
        q_rot = qab[:, _slot(hd)] * c_q + qab[:, _slot(D_HEADS + hd)] * s_q
        dq_ref[0, :, _slot(hd)] = q_rot.astype(BF16)
        dk_ref[0, :, _slot(hd)] = jnp.where(low1, kv_up[:, _slot(hd)], k_rot).astype(BF16)
    dv_ref[0] = jnp.where(key_half(D_HEADS), 1.0, kv_up).astype(BF16)

    bkv = proj(G_BKV, B_HEADS)
    for n in range(B_HEADS):
        bkv_ref[0, n] = bkv[:, _slot(n)]
    cq_ref[0] = jnp.where(own_terms, 1.0, padded_heads(proj(G_CQ, C_HEADS // 2))).astype(BF16)


def _inproj(x, gain, w_ext, tab, fbias, qn_g, kvn_g, wq, wkv):
    b, s, d = x.shape
    tm = min(TM_PROJ, s)

    def out(kind, n, dtype):
        if kind == "rows":
            return (pl.BlockSpec((1, tm, n * LANES), lambda bi, i: (bi, i, 0)),
                    jax.ShapeDtypeStruct((b, s, n * LANES), dtype))
        return (pl.BlockSpec((1, n, tm, LANES), lambda bi, i: (bi, 0, i, 0)),
                jax.ShapeDtypeStruct((b, n, s, LANES), dtype))

    outs = [out("rows", A_HEADS, BF16), out("rows", 1, F32), out("rows", 1, BF16), out("rows", 1, BF16),
            out("rows", 1, BF16), out("slots", len(B_GROUPS), F32), out("slots", B_HEADS, F32),
            out("rows", C_HEADS, BF16), out("rows", C_HEADS, BF16), out("rows", C_HEADS, BF16),
            out("rows", D_HEADS, BF16), out("rows", D_HEADS, BF16), out("rows", D_HEADS, BF16),
            out("rows", 1, F32)]
    consts = (gain, w_ext)
    consts2 = (fbias, qn_g, kvn_g, wq, wkv)
    return pl.pallas_call(
        _inproj_kernel,
        grid=(b, s // tm),
        in_specs=[pl.BlockSpec((1, tm, d), lambda bi, i: (bi, i, 0))]
        + [_const_spec(c.shape) for c in consts]
        + [pl.BlockSpec((tm, 4 * LANES), lambda bi, i: (i, 0))]
        + [_const_spec(c.shape) for c in consts2],
        out_specs=[o[0] for o in outs],
        out_shape=[o[1] for o in outs],
        scratch_shapes=[pltpu.VMEM((8, LANES), F32)],
        name="inproj",
        compiler_params=_cparams(2),
    )(x, *consts, tab, *consts2)


def _gelu_tanh(x):
    return 0.5 * x * (1.0 + jnp.tanh(math.sqrt(2.0 / math.pi) * (x + 0.044715 * (x * x * x))))


def _compress_kernel(kc_ref, vc_ref, pos_ref, w1k_ref, w1v_ref, w2k_ref, w2v_ref, out_ref):
    nc = kc_ref.shape[1]
    half = A_CMP_STRIDE * HEAD_DIM
    pos_a = pos_ref[0:1, :]
    pos_b = pos_ref[1:2, :]

    def phi(chunks, w1_ref, w2_ref):
        u = _dot((chunks + pos_a).astype(BF16), w1_ref[0:half, :])
        v = _dot((chunks + pos_b).astype(BF16), w1_ref[half:2 * half, :])
        hid = u + pltpu.roll(v, nc - 1, 0)
        return _dot(_gelu_tanh(hid).astype(BF16), w2_ref[...])

    out_ref[0] = (phi(kc_ref[0], w1k_ref, w2k_ref) + phi(vc_ref[0], w1v_ref, w2v_ref)).astype(BF16)


def _compress(kc_in, vc_in, pos2, w1k, w1v, w2k_pad, w2v_pad):
    b, nc, width = kc_in.shape
    blk = pl.BlockSpec((1, nc, width), lambda bi: (bi, 0, 0))
    return pl.pallas_call(
        _compress_kernel,
        grid=(b,),
        in_specs=[blk, blk, _const_spec(pos2.shape), _const_spec(w1k.shape), _const_spec(w1v.shape),
                  _const_spec(w2k_pad.shape), _const_spec(w2v_pad.shape)],
        out_specs=pl.BlockSpec((1, nc, LANES), lambda bi: (bi, 0, 0)),
        out_shape=jax.ShapeDtypeStruct((b, nc, LANES), BF16),
        name="compress",
        compiler_params=_cparams(1),
    )(kc_in, vc_in, pos2, w1k, w1v, w2k_pad, w2v_pad)


def _nsa_cmp_win_kernel(q_ref, kvc_ref, kvw_ref, small_ref, ovl_ref, bias_ref,
                        part_ref, sel_ref, *, n_top):
    tq = min(TQ_NSA, q_ref.shape[1])
    for u in range(q_ref.shape[1] // tq):
        _nsa_cmp_win_tile(u * tq, tq, q_ref, kvc_ref, kvw_ref, small_ref, ovl_ref, bias_ref,
                          part_ref, sel_ref, n_top)


def _nsa_cmp_win_tile(r0, tq, q_ref, kvc_ref, kvw_ref, small_ref, ovl_ref, bias_ref, part_ref, sel_ref, n_top):
    nc = kvc_ref.shape[1]
    n_sel = ovl_ref.shape[0]
    t0 = pl.multiple_of(pl.program_id(1) * q_ref.shape[1] + r0, tq)
    tile = slice(r0, r0 + tq)
    q4 = jnp.concatenate([q_ref[0, tile, _slot(h)] for h in range(A_HEADS)], axis=0)
    rows = A_HEADS * tq

    kvc = kvc_ref[0]
    s = _dot_nt(q4, kvc)
    t_row = t0 + lax.broadcasted_iota(jnp.int32, (tq, nc), 0)
    c_end = lax.broadcasted_iota(jnp.int32, (tq, nc), 1) * A_CMP_STRIDE + (A_CMP_LEN - 1)
    valid = jnp.concatenate([c_end <= t_row] * A_HEADS, axis=0)
    s = jnp.where(valid, s, NEG_INF)
    m = jnp.max(s, axis=1, keepdims=True)
    p = jnp.where(valid, jnp.exp2(s - m), 0.0)
    den = jnp.maximum(jnp.sum(p, axis=1, keepdims=True), 1e-30)
    p = p / den
    o_c = _dot(p.astype(BF16), kvc)

    p_sum = p[0:tq] + p[tq:2 * tq] + p[2 * tq:3 * tq] + p[3 * tq:4 * tq]
    p_hi = p_sum.astype(BF16)
    p_lo = (p_sum - p_hi.astype(F32)).astype(BF16)
    imp_t = _dot_nt(ovl_ref[...], p_hi) + _dot_nt(ovl_ref[...], p_lo)

    j_idx = lax.broadcasted_iota(jnp.int32, (n_sel, tq), 0)
    cur = lax.shift_right_logical(t0 + lax.broadcasted_iota(jnp.int32, (n_sel, tq), 1), 6)
    forced = (j_idx == 0) | (j_idx == cur) | (j_idx == cur - 1)
    imp_t = jnp.where(forced, FORCE_SCORE, jnp.where(j_idx > cur, -FORCE_SCORE, imp_t))
    sub8 = 8
    groups = [imp_t[v * sub8:(v + 1) * sub8] for v in range(n_sel // sub8)]
    ranks = [jnp.zeros((sub8, tq), F32) for _ in groups]
    in_group = lax.broadcasted_iota(jnp.int32, (sub8, tq), 0)
    for c in range(n_sel):
        row_c = imp_t[c:c + 1, :]
        for v, grp in enumerate(groups):
            if v > c // sub8:
                beats = row_c >= grp
            elif v < c // sub8:
                beats = row_c > grp
            else:
                beats = (row_c > grp) | ((row_c == grp) & (in_group > c % sub8))
            ranks[v] = ranks[v] + jnp.where(beats, 1.0, 0.0)
    rank = jnp.concatenate(ranks, axis=0)
    sel_t = jnp.where(rank < float(n_top), 1.0, 0.0).astype(BF16)
    sel_ref[0, :, tile] = sel_t

    span = A_WINDOW + tq
    kvw = kvw_ref[0, pl.ds(t0, span), :]
    kpos = t0 - A_WINDOW + lax.broadcasted_iota(jnp.int32, (1, span), 1)
    sw = _dot_nt(q4, kvw) + bias_ref[...] + jnp.where(kpos >= 0, 0.0, NEG_INF)
    mw = jnp.max(sw, axis=1, keepdims=True)
    pw = jnp.exp2(sw - mw)
    o_w = _dot(pw.astype(BF16), kvw) / jnp.sum(pw, axis=1, keepdims=True)

    g = _sigmoid(small_ref[0, tile])
    gated = []
    for h in range(A_HEADS):
        g_c = g[:, LANE_GATE + 3 * h:LANE_GATE + 3 * h + 1]
        g_w = g[:, LANE_GATE + 3 * h + 2:LANE_GATE + 3 * h + 3]
        gated.append(g_c * o_c[h * tq:(h + 1) * tq] + g_w * o_w[h * tq:(h + 1) * tq])
    low = lax.broadcasted_iota(jnp.int32, (tq, LANES), 1) < HEAD_DIM
    for pair in range(A_HEADS // 2):
        part_ref[0, tile, _slot(pair)] = jnp.where(low, pltpu.roll(gated[2 * pair], HEAD_DIM, 1),
                                                   gated[2 * pair + 1])


def _nsa_cmp_win(aq, kvc, kvw_pad, small, ovl, bias_w):
    b, s, _ = aq.shape
    tq = min(TQ_NSA_STEP, s)
    nc = kvc.shape[1]
    n_sel = ovl.shape[0]
    n_top = min(A_SEL_TOPK, n_sel)
    return pl.pallas_call(
        functools.partial(_nsa_cmp_win_kernel, n_top=n_top),
        grid=(b, s // tq),
        in_specs=[pl.BlockSpec((1, tq, A_HEADS * LANES), lambda bi, i: (bi, i, 0)),
                  pl.BlockSpec((1, nc, LANES), lambda bi, i: (bi, 0, 0)),
                  pl.BlockSpec((1, s + A_WINDOW, LANES), lambda bi, i: (bi, 0, 0)),
                  pl.BlockSpec((1, tq, LANES), lambda bi, i: (bi, i, 0)),
                  _const_spec(ovl.shape),
                  _const_spec(bias_w.shape)],
        out_specs=[pl.BlockSpec((1, tq, A_HEADS * HEAD_DIM), lambda bi, i: (bi, i, 0)),
                   pl.BlockSpec((1, n_sel, tq), lambda bi, i: (bi, 0, i))],
        out_shape=[jax.ShapeDtypeStruct((b, s, A_HEADS * HEAD_DIM), F32),
                   jax.ShapeDtypeStruct((b, n_sel, s), BF16)],
        name="nsa_cmp_win",
        compiler_params=_cparams(2),
    )(aq, kvc, kvw_pad, small, ovl, bias_w)


def _flash_kernel(*refs, mode, nh):
    if mode == "sel":
        q_ref, k_ref, v_ref, selt_ref, e_ref, bias_ref, part_ref, small_ref, o_ref, m_ref, acc_ref = refs
    else:
        q_ref, k_ref, v_ref, o_ref, m_ref, acc_ref = refs
    i = pl.program_id(1)
    tq = q_ref.shape[1]
    tk = tq
    shared_kv = mode == "sel"

    m_ref[...] = jnp.full(m_ref.shape, NEG_INF, F32)
    acc_ref[...] = jnp.zeros(acc_ref.shape, F32)

    lag = lax.broadcasted_iota(jnp.int32, (tq, tk), 0) - lax.broadcasted_iota(jnp.int32, (tq, tk), 1)

    def step(j, diag):
        k0 = pl.multiple_of(j * tk, tk)
        keep = (lag >= 0) if diag else None
        if mode == "sel":
            sel_keys = lax.dot_general(selt_ref[0], e_ref[:, pl.ds(k0, tk)], (((0,), (0,)), ((), ())),
                                       preferred_element_type=F32) > 0.5
            keep = (sel_keys & keep) if diag else sel_keys
        penalty = None if keep is None else jnp.where(keep, 0.0, NEG_INF)
        if mode == "sel":
            n_off, bt = bias_ref.shape[0], bias_ref.shape[2]

            def bias_tile(h):
                strips = []
                for a in range(tq // bt):
                    offs = [jnp.clip(i * (tq // bt) + a - j * (tk // bt) - c, 0, n_off - 1)
                            for c in range(tk // bt)]
                    strips.append(jnp.concatenate([bias_ref[o, h] for o in offs], axis=1))
                return jnp.concatenate(strips, axis=0)
        logits = []
        for h in range(nh):
            s = _dot_nt(q_ref[0, :, _slot(h)], k_ref[0, pl.ds(k0, tk), _slot(0 if shared_kv else h)])
            if mode == "sel":
                s = s + bias_tile(h)
            if penalty is not None:
                s = s + penalty
            logits.append(s)
        probs = []
        for h in range(nh):
            m_prev = m_ref[h]
            m_new = jnp.maximum(m_prev, jnp.max(logits[h], axis=1, keepdims=True))
            p = jnp.exp2(logits[h] - jnp.concatenate([m_new] * (tk // LANES), axis=1))
            alpha = jnp.exp2(m_prev - m_new)
            m_ref[h] = m_new
            probs.append((p.astype(BF16), alpha))
        for h in range(nh):
            p, alpha = probs[h]
            acc_ref[h] = alpha * acc_ref[h] + _dot(p, v_ref[0, pl.ds(k0, tk), _slot(0 if shared_kv else h)])

    def full_tile(j, carry):
        step(j, False)
        return carry

    lax.fori_loop(0, i, full_tile, 0)
    step(i, True)

    low = lax.broadcasted_iota(jnp.int32, (tq, LANES), 1) < HEAD_DIM
    if mode == "sel":
        g = _sigmoid(small_ref[0])
    for pair in range(nh // 2):
        outs = []
        for h in (2 * pair, 2 * pair + 1):
            acc = acc_ref[h]
            outs.append(acc / acc[:, 0:1])
        o_pair = jnp.where(low, pltpu.roll(outs[0], HEAD_DIM, 1), outs[1])
        if mode == "sel":
            gate = [g[:, LANE_GATE + 3 * h + 1:LANE_GATE + 3 * h + 2] for h in (2 * pair, 2 * pair + 1)]
            o_pair = part_ref[0, :, _slot(pair)] + jnp.where(low, gate[0], gate[1]) * o_pair
        o_ref[0, :, _slot(pair)] = o_pair.astype(BF16)


def _flash(mode, q, k, v, sel_t=None, expand=None, bias=None, part=None, small=None):
    b, s, qw = q.shape
    nh = qw // LANES
    tq = min(TQ_FLASH, s)
    ow = nh * HEAD_DIM

    def rows(width):
        return pl.BlockSpec((1, tq, width), lambda bi, i: (bi, i, 0))

    def seq_spec(arr):
        return pl.BlockSpec((1,) + arr.shape[1:], lambda bi, i: (bi, 0, 0))

    args = (q, k, v)
    in_specs = [rows(qw), seq_spec(k), seq_spec(v)]
    if mode == "sel":
        args += (sel_t, expand, bias, part, small)
        in_specs += [pl.BlockSpec((1, sel_t.shape[1], tq), lambda bi, i: (bi, 0, i)),
                     _const_spec(expand.shape), _const_spec(bias.shape), rows(ow), rows(LANES)]
    return pl.pallas_call(
        functools.partial(_flash_kernel, mode=mode, nh=nh),
        grid=(b, s // tq),
        in_specs=in_specs,
        out_specs=rows(ow),
        out_shape=jax.ShapeDtypeStruct((b, s, ow), BF16),
        scratch_shapes=[pltpu.VMEM((nh, tq, LANES), F32)] * 2,
        name="flash_" + mode,
        compiler_params=_cparams(2),
    )(*args)


def _dilated_kernel(q_ref, kvp_ref, kvc_ref, bias_ref, o_ref, o_sc, l_sc):
    i = pl.program_id(1)
    rows = q_ref.shape[2]
    tb = TQ_DIL
    low = lax.broadcasted_iota(jnp.int32, (tb, LANES), 1) < HEAD_DIM
    col = lax.broadcasted_iota(jnp.int32, (tb, 2 * tb), 1)
    in_seq = (col >= tb) | (i > 0)
    for g, (_, dil) in enumerate(B_GROUPS):
        span = dil * tb
        for r in range(dil):
            for c in range(rows // span):
                cur = pl.ds(r + c * span, tb, stride=dil)
                if c > 0:
                    prev_ref, prev = kvc_ref, pl.ds(r + (c - 1) * span, tb, stride=dil)
                else:
                    prev_ref, prev = kvp_ref, pl.ds(rows - span + r, tb, stride=dil)
                q2 = q_ref[0, g, cur, :]
                outs, lses = [], []
                for h in range(B_HPG):
                    q_h = jnp.where(low if h == 0 else ~low, q2, 0.0).astype(BF16)
                    head = B_HPG * g + h
                    kv = jnp.concatenate([prev_ref[0, head, prev, :], kvc_ref[0, head, cur, :]],
                                         axis=0).astype(BF16)
                    s = _dot_nt(q_h, kv) + bias_ref[g, h]
                    if c == 0:
                        s = jnp.where(in_seq, s, NEG_INF)
                    m = jnp.max(s, axis=1, keepdims=True)
                    p = jnp.exp2(s - m)
                    den = jnp.sum(p, axis=1, keepdims=True)
                    outs.append(_dot(p.astype(BF16), kv) / den)
                    lses.append(m + jnp.log2(den))
                o_sc[g, cur, :] = jnp.where(low, outs[1], outs[0])
                l_sc[g, cur, :] = jnp.where(low, lses[1], lses[0])
    step = 2 * tb
    for t in range(rows // step):
        rs = slice(t * step, (t + 1) * step)
        l0, l1, l2 = l_sc[0, rs, :], l_sc[1, rs, :], l_sc[2, rs, :]
        mx = jnp.maximum(jnp.maximum(l0, l1), l2)
        e0, e1, e2 = jnp.exp2(l0 - mx), jnp.exp2(l1 - mx), jnp.exp2(l2 - mx)
        mixed = (e0 * o_sc[0, rs, :] + e1 * o_sc[1, rs, :] + e2 * o_sc[2, rs, :]) / (e0 + e1 + e2)
        o_ref[0, rs, :] = mixed.astype(BF16)


def _dilated(bq, bkv, bias):
    b, _, s, _ = bq.shape
    rows = TQ_DIL * B_GROUPS[-1][1]
    n_groups = len(B_GROUPS)

    def tile(n_slots, prev):
        if prev:
            return pl.BlockSpec((1, n_slots, rows, LANES), lambda bi, i: (bi, 0, jnp.maximum(i - 1, 0), 0))
        return pl.BlockSpec((1, n_slots, rows, LANES), lambda bi, i: (bi, 0, i, 0))

    return pl.pallas_call(
        _dilated_kernel,
        grid=(b, s // rows),
        in_specs=[tile(bq.shape[1], False), tile(bkv.shape[1], True), tile(bkv.shape[1], False),
                  _const_spec(bias.shape)],
        out_specs=pl.BlockSpec((1, rows, LANES), lambda bi, i: (bi, i, 0)),
        out_shape=jax.ShapeDtypeStruct((b, s, LANES), BF16),
        scratch_shapes=[pltpu.VMEM((n_groups, rows, LANES), F32)] * 2,
        name="dilated",
        compiler_params=_cparams(2),
    )(bq, bkv, bkv, bias)


def _merge_kernel(x_ref, oa_ref, ob_ref, oc_ref, od_ref,
                  gpre_ref, gpost_ref, wg_ref, wba_ref, wbb_ref, wbc_ref, wbd_ref, wo_ref, out_ref):
    x = x_ref[0]
    h = _rms(x, gpre_ref[...]).astype(BF16)
    branches = ((oa_ref[0], wba_ref), (ob_ref[0], wbb_ref), (oc_ref[0], wbc_ref), (od_ref[0], wbd_ref))
    merged = None
    for n, (o_n, wb_ref) in enumerate(branches):
        term = _sigmoid(_dot(h, wg_ref[n])) * _dot(o_n, wb_ref[...])
        merged = term if merged is None else merged + term
    z = _dot(merged.astype(BF16), wo_ref[...])
    out_ref[0] = x + _rms(z, gpost_ref[...])


def _merge(x, oa, ob, oc, od, gpre, gpost, wg, wba, wbb, wbc, wbd, wo):
    b, s, d = x.shape
    tm = min(TM_PROJ, s)

    def rows(arr):
        return pl.BlockSpec((1, tm, arr.shape[2]), lambda bi, i: (bi, i, 0))

    acts = (x, oa, ob, oc, od)
    consts = (gpre, gpost, wg, wba, wbb, wbc, wbd, wo)
    return pl.pallas_call(
        _merge_kernel,
        grid=(b, s // tm),
        in_specs=[rows(a) for a in acts] + [_const_spec(c.shape) for c in consts],
        out_specs=rows(x),
        out_shape=jax.ShapeDtypeStruct(x.shape, F32),
        name="merge",
        compiler_params=_cparams(2),
    )(*acts, *consts)


def _ffn_kernel(x_ref, gpre_ref, gpost_ref, wup_ref, cw_ref, cb_ref, wdn_ref, out_ref, tail_ref, *, cuts):
    i = pl.program_id(1)
    tm = x_ref.shape[1]
    d_ff = wdn_ref.shape[0]
    x = x_ref[0]
    h = _rms(x, gpre_ref[...]).astype(BF16)

    @pl.when(i == 0)
    def _():
        tail_ref[...] = jnp.zeros_like(tail_ref)

    def conv(c0, chunk):
        u = _dot(h, wup_ref[:, c0:c0 + chunk])
        ue = jnp.concatenate([tail_ref[:, c0:c0 + chunk], u], axis=0)
        tail_ref[:, c0:c0 + chunk] = u[tm - CONV_HALO:]
        w = cw_ref[:, c0:c0 + chunk]
        out = cb_ref[:, c0:c0 + chunk] + w[0:1] * pltpu.roll(ue, 2, 0)[CONV_HALO:]
        out = out + w[1:2] * pltpu.roll(ue, 1, 0)[CONV_HALO:]
        return out + w[2:3] * u

    y = jnp.zeros((tm, x.shape[1]), F32)
    for c0, c1 in zip(cuts[:-1], cuts[1:]):
        gate = conv(c0, c1 - c0)
        val = conv(d_ff + c0, c1 - c0)
        act = gate * _sigmoid(gate) * val
        y = y + _dot(act.astype(BF16), wdn_ref[c0:c1, :])
    out_ref[0] = x + _rms(y, gpost_ref[...])


def _ffn(x, gpre, gpost, w_up, conv_w, conv_b, w_down):
    b, s, d = x.shape
    tm = min(TM_PROJ, s)
    d_ff = w_down.shape[0]
    n_tiles = d_ff // MXU_WIDTH
    cuts = (0, (n_tiles // 2) * MXU_WIDTH, d_ff) if d_ff % MXU_WIDTH == 0 and n_tiles > 1 else (0, d_ff)
    return pl.pallas_call(
        functools.partial(_ffn_kernel, cuts=cuts),
        grid=(b, s // tm),
        in_specs=[pl.BlockSpec((1, tm, d), lambda bi, i: (bi, i, 0)),
                  _const_spec(gpre.shape), _const_spec(gpost.shape), _const_spec(w_up.shape),
                  _const_spec(conv_w.shape), _const_spec(conv_b.shape), _const_spec(w_down.shape)],
        out_specs=pl.BlockSpec((1, tm, d), lambda bi, i: (bi, i, 0)),
        out_shape=jax.ShapeDtypeStruct(x.shape, F32),
        scratch_shapes=[pltpu.VMEM((CONV_HALO, w_up.shape[1]), F32)],
        name="ffn",
        compiler_params=_cparams(2),
    )(x, gpre, gpost, w_up, conv_w, conv_b, w_down)


def _rel_bucket(dist):
    dist = jnp.maximum(dist, 0)
    d = jnp.maximum(dist, 1).astype(F32)
    large = BUCKET_EXACT + (jnp.log(d / BUCKET_EXACT) / math.log(BUCKET_MAX_DIST / BUCKET_EXACT)
                            * (N_BUCKETS - BUCKET_EXACT)).astype(jnp.int32)
    large = jnp.minimum(large, N_BUCKETS - 1)
    return jnp.where(dist < BUCKET_EXACT, dist, large)


def _in_proj_columns():
    src = np.full((N_SLOTS * LANES,), -1, np.int64)
    scl = np.ones((N_SLOTS * LANES,), np.float32)
    o_aq = 0
    o_akv = o_aq + A_HEADS * HEAD_DIM
    o_ag = o_akv + 6 * HEAD_DIM
    o_b = o_ag + 3 * A_HEADS
    o_c = o_b + 3 * B_HEADS * HEAD_DIM
    o_cf = o_c + 3 * C_HEADS * HEAD_DIM
    o_dq = o_cf + C_HEADS
    o_dkv = o_dq + D_Q_LORA
    o_dkr = o_dkv + D_KV_LORA
    e = np.arange(HEAD_DIM)
    q_scale = HEAD_DIM ** -0.5 * LOG2E

    def put(slot, lane0, cols, scale=1.0):
        dst = slot * LANES + lane0 + np.arange(len(cols))
        src[dst] = cols
        scl[dst] = scale

    for h in range(A_HEADS):
        put(G_AQ + h // 2, (h % 2) * HEAD_DIM, o_aq + h * HEAD_DIM + e, q_scale)
    for n, slot in enumerate((G_CMP, G_SLC, G_WIN)):
        put(slot, 0, o_akv + (2 * n) * HEAD_DIM + e)
        put(slot, HEAD_DIM, o_akv + (2 * n + 1) * HEAD_DIM + e)
    for h in range(B_HEADS):
        odd = h % B_HPG
        put(G_BQ + h // B_HPG, odd * HEAD_DIM, o_b + h * HEAD_DIM + e, q_scale)
        put(G_BKV + h, odd * HEAD_DIM, o_b + (B_HEADS + h) * HEAD_DIM + e)
        put(G_BKV + h, (1 - odd) * HEAD_DIM, o_b + (2 * B_HEADS + h) * HEAD_DIM + e)
    for h in range(C_HEADS):
        put(G_CQ + h // 2, (h % 2) * HEAD_DIM, o_c + h * HEAD_DIM + e, q_scale)
        put(G_CKV + h, 0, o_c + (C_HEADS + h) * HEAD_DIM + e)
        put(G_CKV + h, HEAD_DIM, o_c + (2 * C_HEADS + h) * HEAD_DIM + e)
    put(G_DQL, 0, o_dq + np.arange(D_Q_LORA))
    put(G_DKVL, 0, o_dkv + np.arange(D_KV_LORA))
    put(G_S1, LANE_GATE, o_ag + np.arange(3 * A_HEADS))
    put(G_S1, LANE_FORGET, o_cf + np.repeat(np.arange(C_HEADS), N_TERMS))
    put(G_S1, LANE_ROPE, o_dkr + np.arange(D_ROPE))
    half = D_ROPE // 2
    put(G_S2, LANE_ROPE, o_dkr + half + np.arange(half), -1.0)
    put(G_S2, LANE_ROPE + half, o_dkr + np.arange(half))
    return src, scl


def _gather_cols(w, src, scl):
    pieces = []
    n = len(src)
    a = 0
    while a < n:
        e = a + 1
        if src[a] < 0:
            while e < n and src[e] < 0:
                e += 1
            pieces.append(jnp.zeros((w.shape[0], e - a), w.dtype))
        else:
            while e < n and src[e] == src[e - 1] + 1 and scl[e] == scl[a]:
                e += 1
            run = w[:, int(src[a]):int(src[a]) + (e - a)]
            pieces.append(run if scl[a] == 1.0 else run * float(scl[a]))
        a = e
    return jnp.concatenate(pieces, axis=1)


def _mla_weight_columns():
    per_q = D_NOPE + D_ROPE
    half = D_ROPE // 2
    qa = np.full((D_HEADS * LANES,), -1, np.int64)
    qb = np.full((D_HEADS * LANES,), -1, np.int64)
    qb_s = np.ones((D_HEADS * LANES,), np.float32)
    for h in range(D_HEADS):
        qa[h * LANES + np.arange(per_q)] = h * per_q + np.arange(per_q)
        rot = h * per_q + D_NOPE
        qb[h * LANES + D_NOPE + np.arange(half)] = rot + half + np.arange(half)
        qb_s[h * LANES + D_NOPE + np.arange(half)] = -1.0
        qb[h * LANES + D_NOPE + half + np.arange(half)] = rot + np.arange(half)
    ones = np.ones_like(qb_s)
    return (qa, ones), (qb, qb_s)


def _rope_tables(s):
    inv_freq = ROPE_THETA ** (-jnp.arange(0, D_ROPE, 2, dtype=F32) / D_ROPE)
    ang = jnp.arange(s, dtype=F32)[:, None] * inv_freq[None, :]
    cos2 = jnp.concatenate([jnp.cos(ang)] * 2, axis=1)
    sin2 = jnp.concatenate([jnp.sin(ang)] * 2, axis=1)
    scale = (D_NOPE + D_ROPE) ** -0.5 * LOG2E
    z_lo = jnp.zeros((s, D_NOPE), F32)
    z_hi = jnp.zeros((s, LANES - D_NOPE - D_ROPE), F32)
    c_q = jnp.concatenate([jnp.full((s, D_NOPE), scale, F32), scale * cos2, z_hi], axis=1)
    s_q = jnp.concatenate([z_lo, scale * sin2, z_hi], axis=1)
    c_k = jnp.concatenate([z_lo, cos2, z_hi], axis=1)
    s_k = jnp.concatenate([z_lo, sin2, z_hi], axis=1)
    return jnp.concatenate([c_q, s_q, c_k, s_k], axis=1)


def _toeplitz(vec, n_rows, n_cols, off, step=1):
    base = vec[:, ::step]
    last = base.shape[1] - 1
    lo, hi = off - (n_cols - 1), off + n_rows - 1
    core = base[:, max(lo, 0):min(hi, last) + 1]
    front = jnp.repeat(base[:, :1], max(0, -lo), axis=1)
    back = jnp.repeat(base[:, last:], max(0, hi - last), axis=1)
    g = jnp.concatenate([front, core, back], axis=1)
    lg = n_rows + n_cols - 1
    t = jnp.tile(g, (1, n_rows + 1))[:, :n_rows * (lg + 1)].reshape(-1, n_rows, lg + 1)
    return t[:, :, :n_cols][:, :, ::-1]


def _bias_tables(rel_table, s):
    bucket = _rel_bucket(jnp.arange(s))
    hit = bucket[None, :, None] == jnp.arange(N_BUCKETS)[None, None, :]
    by_dist = jnp.sum(jnp.where(hit, LOG2E * rel_table.T[:, None, :], 0.0), axis=-1)
    tqa = min(TQ_NSA, s)
    dist = jnp.arange(tqa)[:, None] + A_WINDOW - jnp.arange(A_WINDOW + tqa)[None, :]
    ok = (dist >= 0) & (dist < A_WINDOW)
    bias_w = jnp.where(ok[None], _toeplitz(by_dist[:A_HEADS], tqa, A_WINDOW + tqa, A_WINDOW), NEG_INF)
    bias_w = bias_w.reshape(A_HEADS * tqa, A_WINDOW + tqa)
    tf = min(BIAS_TILE, s)
    n_off = min(s // tf, -(-(BUCKET_MAX_DIST + tf - 1) // tf) + 1)
    bias_s = jnp.stack([_toeplitz(by_dist[:A_HEADS], tf, tf, n * tf) for n in range(n_off)])
    m = jnp.arange(TQ_DIL)[:, None] + TQ_DIL - jnp.arange(2 * TQ_DIL)[None, :]
    bias_d = []
    for g, (window, dil) in enumerate(B_GROUPS):
        ok = (m >= 0) & (m <= window // dil)
        heads = by_dist[A_HEADS + g * B_HPG:A_HEADS + (g + 1) * B_HPG]
        bias_d.append(jnp.where(ok[None], _toeplitz(heads, TQ_DIL, 2 * TQ_DIL, TQ_DIL, dil), NEG_INF))
    return bias_w, bias_s, jnp.stack(bias_d)


def _selection_constants(s):
    nc = s // A_CMP_STRIDE
    n_cmp = (s - A_CMP_LEN) // A_CMP_STRIDE + 1
    n_sel = s // A_SEL_BLOCK
    c = np.arange(nc)[:, None]
    j = np.arange(n_sel)[None, :]
    c_start = c * A_CMP_STRIDE
    overlap = ((c_start <= j * A_SEL_BLOCK + A_SEL_BLOCK - 1) & (c_start + A_CMP_LEN - 1 >= j * A_SEL_BLOCK)
               & (c < n_cmp))
    expand = (np.arange(s)[None, :] // A_SEL_BLOCK) == np.arange(n_sel)[:, None]
    return jnp.asarray(overlap.T, BF16), jnp.asarray(expand, BF16)


def kernel(x, rel_bias_table, norm_attn_pre, norm_attn_post, norm_ffn_pre, norm_ffn_post, w_in, nsa_cmp_pos, nsa_phi_k_w1, nsa_phi_k_w2, nsa_phi_v_w1, nsa_phi_v_w2, fox_forget_bias, mla_q_norm, mla_kv_norm, mla_w_uq, mla_w_ukv, w_branch_a, w_branch_b, w_branch_c, w_branch_d, w_merge_gate, w_o, ffn_w_up, ffn_conv_w, ffn_conv_b, ffn_w_down):
    b, s, d = x.shape
    depth = w_in.shape[0]
    assert s % (TQ_DIL * B_GROUPS[-1][1]) == 0, "every dilation class needs whole 128-row tiles"

    in_src, in_scl = _in_proj_columns()
    assert D_NOPE + D_VDIM == LANES, "an MLA head's [k_nope | v] up-projection fills one slot"
    (qa_i, qa_s), (qb_i, qb_s) = _mla_weight_columns()
    rope_tab = _rope_tables(s)
    bias_w, bias_s, bias_d = _bias_tables(rel_bias_table, s)
    overlap, expand = _selection_constants(s)
    half = A_CMP_STRIDE * HEAD_DIM

    for l in range(depth):
        w_ext = _gather_cols(w_in[l], in_src, in_scl).astype(BF16)
        fbias = jnp.zeros((1, LANES), F32).at[0, LANE_FORGET:LANE_FORGET + N_TERMS * C_HEADS].set(
            jnp.repeat(fox_forget_bias[l], N_TERMS))
        wq = jnp.concatenate([_gather_cols(mla_w_uq[l], qa_i, qa_s), _gather_cols(mla_w_uq[l], qb_i, qb_s)],
                             axis=1).astype(BF16)
        (aq, cmp, slc, slc_v, win, bq, bkv, cq, ck, c_v, dq, dk, d_v, small) = _inproj(
            x, norm_attn_pre[l][None], w_ext, rope_tab, fbias,
            mla_q_norm[l][None], mla_kv_norm[l][None], wq, mla_w_ukv[l].astype(BF16))

        nc = s // A_CMP_STRIDE
        w2k_pad = jnp.pad(nsa_phi_k_w2[l], ((0, 0), (0, LANES - HEAD_DIM))).astype(BF16)
        w2v_pad = jnp.pad(nsa_phi_v_w2[l], ((0, 0), (LANES - HEAD_DIM, 0))).astype(BF16)
        kvc = _compress(cmp[..., :HEAD_DIM].reshape(b, nc, half), cmp[..., HEAD_DIM:].reshape(b, nc, half),
                        nsa_cmp_pos[l].reshape(2, half),
                        nsa_phi_k_w1[l].astype(BF16), nsa_phi_v_w1[l].astype(BF16), w2k_pad, w2v_pad)
        kvw_pad = jnp.pad(win, ((0, 0), (A_WINDOW, 0), (0, 0)))
        part, sel_t = _nsa_cmp_win(aq, kvc, kvw_pad, small, overlap, bias_w)
        o_a = _flash("sel", aq, slc, slc_v, sel_t=sel_t, expand=expand, bias=bias_s, part=part, small=small)

        o_b = _dilated(bq, bkv, bias_d)

        o_c = _flash("fox", cq, ck, c_v)
        o_d = _flash("mla", dq, dk, d_v)

        x = _merge(x, o_a, o_b, o_c, o_d,
                   norm_attn_pre[l][None], norm_attn_post[l][None],
                   w_merge_gate[l].astype(BF16),
                   w_branch_a[l].astype(BF16),
                   jnp.concatenate([w_branch_b[l][HEAD_DIM:], w_branch_b[l][:HEAD_DIM]]).astype(BF16),
                   w_branch_c[l].astype(BF16),
                   w_branch_d[l].astype(BF16),
                   w_o[l].astype(BF16))
        x = _ffn(x, norm_ffn_pre[l][None], norm_ffn_post[l][None], ffn_w_up[l].astype(BF16),
                 ffn_conv_w[l], ffn_conv_b[l][None], ffn_w_down[l].astype(BF16))
    return x
```
